```python
import math
import jax
import jax.numpy as jnp
from jax import lax
import numpy as np

D_MODEL = 2048
BATCH = 2
SEQ = 4096
DEPTH = 2
DEC_BATCH = 128
DEC_SEQ = 1
PAST_LEN = 2048
PAGE_SIZE = 128

N_MIXERS = 4
GROUP_WIDTH = D_MODEL // N_MIXERS
HEAD_DIM = 64
N_HEADS = GROUP_WIDTH // HEAD_DIM
KV_HEADS = 2
Q_PER_KV = N_HEADS // KV_HEADS
KV_ROW = KV_HEADS * 2 * HEAD_DIM
IDX_HEADS = 4
IDX_DIM = 64
DSA_TOPK = 256
S5_CH = 16
S5_GROUPS = GROUP_WIDTH // S5_CH
S5_STATE = 64
S5_DT_MIN = 1e-3
S5_DT_MAX = 1e-1
CMP_BLOCK = 32
CMP_HIDDEN = 2 * HEAD_DIM
SLC_BLOCK = 64
SLC_TOPN = 16
WINDOW = 512
QBLOCK = 128
GLA_HEADS = 4
GLA_DK = GROUP_WIDTH // (2 * GLA_HEADS)
GLA_DV = GROUP_WIDTH // GLA_HEADS
GLA_LOWRANK = 16
GLA_TAU = 16.0
GLA_CHUNK = 64
FFN_HIDDEN = -(-8 * D_MODEL // (3 * 256)) * 256
DEEPNORM_ALPHA = (2 * DEPTH) ** 0.25
DEEPNORM_BETA = (8 * DEPTH) ** -0.25
LN_EPS = 1e-5

IN_SIZES = (
    GROUP_WIDTH,
    KV_ROW,
    IDX_HEADS * IDX_DIM,
    IDX_DIM,
    IDX_HEADS,
    GROUP_WIDTH,
    GROUP_WIDTH,
    KV_ROW,
    KV_ROW,
    KV_ROW,
    3 * N_HEADS,
    GLA_HEADS * GLA_DK,
    GLA_HEADS * GLA_DK,
    GROUP_WIDTH,
    GLA_LOWRANK,
    GROUP_WIDTH,
)
D_IN = sum(IN_SIZES)

kernel_name = 'hybrid_dsa_s5_nsa_gla_decoder_step'


def layer_norm(z, g, b):
    z = z.astype(jnp.float32)
    mu = jnp.mean(z, -1, keepdims=True)
    var = jnp.mean(jnp.square(z - mu), -1, keepdims=True)
    return (z - mu) * lax.rsqrt(var + LN_EPS) * g + b


def masked_softmax(s, mask):
    s = jnp.where(mask, s.astype(jnp.float32), -jnp.inf)
    m = jnp.max(s, axis=-1, keepdims=True)
    m = jnp.where(jnp.isfinite(m), m, 0.0)
    p = jnp.exp(s - m)
    return p / jnp.maximum(p.sum(-1, keepdims=True), jnp.finfo(jnp.float32).tiny)


def to_blocks(a):
    n, t = a.shape[:2]
    return jnp.swapaxes(a.reshape((n, t // QBLOCK, QBLOCK) + a.shape[2:]), 0, 1)


def from_blocks(a):
    a = jnp.swapaxes(a, 0, 1)
    return a.reshape((a.shape[0], -1) + a.shape[3:])


def gather_seq(rows, pos, *extra):
    n_idx = jnp.arange(rows.shape[0]).reshape((-1,) + (1,) * (pos.ndim - 1))
    return rows[(n_idx, pos) + tuple(extra)]


def gather_pages(pool, page_table):
    g = pool[page_table]
    return g.reshape((g.shape[0], -1) + g.shape[3:])


def gather_paged_rows(pool, page_table, pos, *extra):
    n_idx = jnp.arange(page_table.shape[0]).reshape((-1,) + (1,) * (pos.ndim - 1))
    phys = page_table[n_idx, pos // PAGE_SIZE]
    return pool[(phys, pos % PAGE_SIZE) + tuple(extra)]


def gather_past_and_new(pool, page_table, new_rows, pos, *extra):
    past = gather_paged_rows(pool, page_table, jnp.minimum(pos, PAST_LEN - 1), *extra)
    new = gather_seq(new_rows, jnp.clip(pos - PAST_LEN, 0, new_rows.shape[1] - 1), *extra)
    in_past = (pos < PAST_LEN).reshape(pos.shape + (1,) * (past.ndim - pos.ndim))
    return jnp.where(in_past, past, new)


def split_projection(h, w_in):
    n, t = h.shape[:2]
    offsets = np.cumsum(IN_SIZES)[:-1].tolist()
    (dq, dkv, iq, ik, iw, u, nq, ckv, skv, wkv, ng,
     gq, gk, gv, ga, gr) = jnp.split(h @ w_in, offsets, axis=-1)
    kv_shape = (n, t, KV_HEADS, 2, HEAD_DIM)
    return {
        'dsa_q': dq.reshape(n, t, N_HEADS, HEAD_DIM),
        'dsa_kv': dkv.reshape(kv_shape),
        'idx_q': iq.reshape(n, t, IDX_HEADS, IDX_DIM),
        'idx_k': ik,
        'idx_w': iw,
        's5_u': u,
        'nsa_q': nq.reshape(n, t, N_HEADS, HEAD_DIM),
        'nsa_cmp': ckv.reshape(kv_shape),
        'nsa_slc': skv.reshape(kv_shape),
        'nsa_win': wkv.reshape(kv_shape),
        'nsa_gate': jax.nn.sigmoid(ng.astype(jnp.float32)).reshape(n, t, N_HEADS, 3),
        'gla_q': gq.reshape(n, t, GLA_HEADS, GLA_DK),
        'gla_k': gk.reshape(n, t, GLA_HEADS, GLA_DK),
        'gla_v': gv.reshape(n, t, GLA_HEADS, GLA_DV),
        'gla_a': ga,
        'gla_r': gr,
    }


def indexer_scores(iq, iw, ik):
    dots = jnp.einsum('nqhd,nld->nqhl', iq, ik) * IDX_DIM ** -0.5
    return jnp.einsum('nqh,nqhl->nql', iw * IDX_HEADS ** -0.5, jax.nn.relu(dots)).astype(jnp.float32)


def dsa_select(scores, q_pos, n_keys):
    causal = jnp.arange(n_keys)[None, :] <= q_pos[:, None]
    k_sel = min(DSA_TOPK, n_keys // 4)
    _, sel = lax.top_k(jnp.where(causal, scores, -jnp.inf), k_sel)
    return sel, sel <= q_pos[None, :, None]


def dsa_attend(q, kv_sel, valid):
    n, nq = q.shape[:2]
    qg = q.reshape(n, nq, KV_HEADS, Q_PER_KV, HEAD_DIM)
    s = jnp.einsum('nqgrd,nqkgd->nqgrk', qg, kv_sel[..., 0, :]) * HEAD_DIM ** -0.5
    p = masked_softmax(s, valid[:, :, None, None, :])
    o = jnp.einsum('nqgrk,nqkgd->nqgrd', p, kv_sel[..., 1, :])
    return o.reshape(n, nq, GROUP_WIDTH)


def dsa_prompt(p):
    kv, ik = p['dsa_kv'], p['idx_k']
    t = kv.shape[1]

    def block(args):
        i, qb, iqb, iwb = args
        q_pos = i * QBLOCK + jnp.arange(QBLOCK)
        sel, valid = dsa_select(indexer_scores(iqb, iwb, ik), q_pos, t)
        return dsa_attend(qb, gather_seq(kv, sel), valid)

    out = lax.map(block, (jnp.arange(t // QBLOCK), to_blocks(p['dsa_q']),
                          to_blocks(p['idx_q']), to_blocks(p['idx_w'])))
    return from_blocks(out)


def dsa_sample(p, pool_kv, pool_idx, page_table):
    s_new = p['dsa_q'].shape[1]
    n_keys = PAST_LEN + s_new
    ik_all = jnp.concatenate([gather_pages(pool_idx, page_table), p['idx_k']], axis=1)
    q_pos = PAST_LEN + jnp.arange(s_new)
    sel, valid = dsa_select(indexer_scores(p['idx_q'], p['idx_w'], ik_all), q_pos, n_keys)
    kv_sel = gather_past_and_new(pool_kv, page_table, p['dsa_kv'], sel)
    return dsa_attend(p['dsa_q'], kv_sel, valid)


def complex_linear_combine(e1, e2):
    a1r, a1i, b1r, b1i = e1
    a2r, a2i, b2r, b2i = e2
    return (a2r * a1r - a2i * a1i,
            a2r * a1i + a2i * a1r,
            a2r * b1r - a2i * b1i + b2r,
            a2r * b1i + a2i * b1r + b2i)


def s5_mixer(u, h0, w):
    f32 = jnp.float32
    n, t = u.shape[:2]
    uf = u.astype(f32).reshape(n, t, S5_GROUPS, S5_CH)
    a_re, a_im = w['a_re'].astype(f32), w['a_im'].astype(f32)
    dt = jnp.exp(w['log_dt'].astype(f32))[:, None]
    mag = jnp.exp(a_re * dt)
    ab_re, ab_im = mag * jnp.cos(a_im * dt), mag * jnp.sin(a_im * dt)
    den = a_re * a_re + a_im * a_im
    nr, ni = ab_re - 1.0, ab_im
    f_re = (nr * a_re + ni * a_im) / den
    f_im = (ni * a_re - nr * a_im) / den
    b_re, b_im = w['b_re'].astype(f32), w['b_im'].astype(f32)
    bb_re = f_re[..., None] * b_re - f_im[..., None] * b_im
    bb_im = f_re[..., None] * b_im + f_im[..., None] * b_re
    bu_re = jnp.einsum('gsc,btgc->btgs', bb_re, uf)
    bu_im = jnp.einsum('gsc,btgc->btgs', bb_im, uf)
    cum_re, cum_im, sc_re, sc_im = lax.associative_scan(
        complex_linear_combine,
        (jnp.broadcast_to(ab_re, bu_re.shape), jnp.broadcast_to(ab_im, bu_im.shape), bu_re, bu_im),
        axis=1)
    h0 = h0.astype(f32)
    h0r, h0i = h0[:, 0][:, None], h0[:, 1][:, None]
    h_re = cum_re * h0r - cum_im * h0i + sc_re
    h_im = cum_re * h0i + cum_im * h0r + sc_im
    y = (jnp.einsum('gcs,btgs->btgc', w['c_re'].astype(f32), h_re)
         - jnp.einsum('gcs,btgs->btgc', w['c_im'].astype(f32), h_im)
         + w['d'].astype(f32) * uf)
    y = jax.nn.gelu(y.reshape(n, t, GROUP_WIDTH))
    y = y * jax.nn.sigmoid(y @ w['w_glu'].astype(f32) + w['b_glu'].astype(f32))
    h_last = jnp.stack([h_re[:, -1], h_im[:, -1]], axis=1)
    return y, h_last


def nsa_compress(rows, w):
    n, length = rows.shape[:2]
    nc = length // CMP_BLOCK
    blk = rows[:, :nc * CMP_BLOCK].reshape(n, nc, CMP_BLOCK, KV_HEADS, 2, HEAD_DIM)
    blk = blk + w['cmp_pe'][:, None]
    flat = jnp.moveaxis(blk, 2, 4).reshape(n, nc, KV_HEADS, 2, CMP_BLOCK * HEAD_DIM)
    hid = jax.nn.gelu(jnp.einsum('ncgjf,jfe->ncgje', flat, w['cmp_w1']))
    return jnp.einsum('ncgje,jed->ncgjd', hid, w['cmp_w2'])


def nsa_core(q, gate, q_pos, cmp_kv, n_slc_blocks, slc_gather, win_kv, win_pos):
    n, nq = q.shape[:2]
    qg = q.reshape(n, nq, KV_HEADS, Q_PER_KV, HEAD_DIM)
    scale = HEAD_DIM ** -0.5
    nc = cmp_kv.shape[1]
    c_vis = (jnp.arange(nc) + 1) * CMP_BLOCK - 1 <= q_pos[:, None]
    s_c = jnp.einsum('nqgrd,ncgd->nqgrc', qg, cmp_kv[..., 0, :]) * scale
    p_c = masked_softmax(s_c, c_vis[None, :, None, None, :])
    o_cmp = jnp.einsum('nqgrc,ncgd->nqgrd', p_c, cmp_kv[..., 1, :])
    ratio = SLC_BLOCK // CMP_BLOCK
    imp = p_c.sum(3)
    imp = jnp.pad(imp, ((0, 0), (0, 0), (0, 0), (0, n_slc_blocks * ratio - nc)))
    imp = imp.reshape(n, nq, KV_HEADS, n_slc_blocks, ratio).sum(-1)
    blk = jnp.arange(n_slc_blocks)[None, :]
    cur = (q_pos // SLC_BLOCK)[:, None]
    forced = (blk == 0) | (blk == cur) | (blk == cur - 1)
    future = blk > cur
    score = jnp.where(future[None, :, None, :], -jnp.inf,
                      jnp.where(forced[None, :, None, :], jnp.inf, imp))
    n_sel = min(SLC_TOPN, n_slc_blocks)
    _, sel = lax.top_k(score, n_sel)
    pos = (sel[..., None] * SLC_BLOCK + jnp.arange(SLC_BLOCK)).reshape(n, nq, KV_HEADS, n_sel * SLC_BLOCK)
    valid = pos <= q_pos[None, :, None, None]
    kv_s = slc_gather(pos)
    s_s = jnp.einsum('nqgrd,nqgkd->nqgrk', qg, kv_s[..., 0, :]) * scale
    p_s = masked_softmax(s_s, valid[:, :, :, None, :])
    o_slc = jnp.einsum('nqgrk,nqgkd->nqgrd', p_s, kv_s[..., 1, :])
    w_vis = ((win_pos[None, :] <= q_pos[:, None]) & (win_pos[None, :] > q_pos[:, None] - WINDOW)
             & (win_pos[None, :] >= 0))
    s_w = jnp.einsum('nqgrd,nwgd->nqgrw', qg, win_kv[..., 0, :]) * scale
    p_w = masked_softmax(s_w, w_vis[None, :, None, None, :])
    o_win = jnp.einsum('nqgrw,nwgd->nqgrd', p_w, win_kv[..., 1, :])
    g = gate.reshape(n, nq, KV_HEADS, Q_PER_KV, 3)
    o = g[..., 0:1] * o_cmp + g[..., 1:2] * o_slc + g[..., 2:3] * o_win
    return o.reshape(n, nq, GROUP_WIDTH)


def nsa_prompt(p, w):
    slc_rows, win_rows = p['nsa_slc'], p['nsa_win']
    t = slc_rows.shape[1]
    cmp_kv = nsa_compress(p['nsa_cmp'], w)
    win_pad = jnp.pad(win_rows, ((0, 0), (WINDOW, 0), (0, 0), (0, 0), (0, 0)))
    g_idx = jnp.arange(KV_HEADS)[None, None, :, None]

    def block(args):
        i, qb, gb = args
        q_pos = i * QBLOCK + jnp.arange(QBLOCK)
        win_kv = lax.dynamic_slice_in_dim(win_pad, i * QBLOCK, WINDOW + QBLOCK, axis=1)
        win_pos = i * QBLOCK - WINDOW + jnp.arange(WINDOW + QBLOCK)
        return nsa_core(qb, gb, q_pos, cmp_kv, t // SLC_BLOCK,
                        lambda pos: gather_seq(slc_rows, pos, g_idx), win_kv, win_pos)

    out = lax.map(block, (jnp.arange(t // QBLOCK), to_blocks(p['nsa_q']), to_blocks(p['nsa_gate'])))
    return from_blocks(out), win_rows[:, -min(WINDOW, t):]


def nsa_sample(p, w, pool_cmp, pool_slc, win_buf, page_table):
    s_new = p['nsa_q'].shape[1]
    n_keys = PAST_LEN + s_new
    cmp_all = jnp.concatenate([gather_pages(pool_cmp, page_table), p['nsa_cmp']], axis=1)
    cmp_kv = nsa_compress(cmp_all, w)
    wbuf = win_buf.shape[1]
    win_all = jnp.concatenate([win_buf, p['nsa_win']], axis=1)
    win_pos = PAST_LEN - wbuf + jnp.arange(wbuf + s_new)
    q_pos = PAST_LEN + jnp.arange(s_new)
    g_idx = jnp.arange(KV_HEADS)[None, None, :, None]
    out = nsa_core(p['nsa_q'], p['nsa_gate'], q_pos, cmp_kv, -(-n_keys // SLC_BLOCK),
                   lambda pos: gather_past_and_new(pool_slc, page_table, p['nsa_slc'], pos, g_idx),
                   win_all, win_pos)
    return out, win_all[:, -wbuf:]


def gla_recurrence(q, k, v, log_a, s0):
    n, t = q.shape[:2]
    c = min(GLA_CHUNK, t)
    tp = -(-t // c) * c

    def chunks(a):
        a = jnp.pad(a, ((0, 0), (0, tp - t)) + ((0, 0),) * (a.ndim - 2))
        return jnp.moveaxis(a.reshape((n, tp // c, c) + a.shape[2:]), 1, 0)

    tril = jnp.tril(jnp.ones((c, c), dtype=bool))[None, :, :, None, None]

    def step(s, inp):
        qc, kc, vc, gc = inp
        b = jnp.cumsum(gc, axis=1)
        o_inter = jnp.einsum('nchk,nhkv->nchv', qc * jnp.exp(b), s)
        decay = jnp.exp(jnp.where(tril, b[:, :, None] - b[:, None, :], -jnp.inf))
        att = jnp.einsum('nthk,nshk,ntshk->nths', qc, kc, decay)
        o_intra = jnp.einsum('nths,nshv->nthv', att, vc)
        b_end = b[:, -1]
        s_new = (jnp.exp(b_end)[..., None] * s
                 + jnp.einsum('nshk,nshv->nhkv', kc * jnp.exp(b_end[:, None] - b), vc))
        return s_new, o_inter + o_intra

    s_last, o = lax.scan(step, s0, (chunks(q), chunks(k), chunks(v), chunks(log_a)))
    o = jnp.moveaxis(o, 0, 1).reshape((n, tp) + o.shape[3:])[:, :t]
    return o, s_last


def gla_mixer(p, s0, w):
    f32 = jnp.float32
    n, t = p['gla_q'].shape[:2]
    q = p['gla_q'].astype(f32) * GLA_DK ** -0.5
    k = p['gla_k'].astype(f32)
    v = p['gla_v'].astype(f32)
    z = (p['gla_a'] @ w['gla_w_gate'] + w['gla_b_gate']).astype(f32)
    log_a = (jax.nn.log_sigmoid(z) / GLA_TAU).reshape(n, t, GLA_HEADS, GLA_DK)
    o, s_last = gla_recurrence(q, k, v, log_a, s0.astype(f32))
    o = o * lax.rsqrt(jnp.mean(o * o, -1, keepdims=True) + LN_EPS) * w['gla_norm_g'].astype(f32)
    o = o.reshape(n, t, GROUP_WIDTH) * jax.nn.silu(p['gla_r'].astype(f32))
    return o, s_last


def mix_prompt(p, w):
    n = p['s5_u'].shape[0]
    o_dsa = dsa_prompt(p)
    o_s5, h_s5 = s5_mixer(p['s5_u'], jnp.zeros((n, 2, S5_GROUPS, S5_STATE), jnp.float32), w)
    o_nsa, win_state = nsa_prompt(p, w)
    o_gla, s_gla = gla_mixer(p, jnp.zeros((n, GLA_HEADS, GLA_DK, GLA_DV), jnp.float32), w)
    mixed = jnp.concatenate([o_dsa, o_s5, o_nsa, o_gla], axis=-1)
    return mixed, (p['dsa_kv'], p['idx_k'], p['nsa_cmp'], p['nsa_slc'], win_state, h_s5, s_gla)


def mix_sample(p, w, pool_dsa_kv, pool_dsa_idx, pool_nsa_cmp, pool_nsa_slc, win_buf, h_s5, s_gla, page_table):
    o_dsa = dsa_sample(p, pool_dsa_kv, pool_dsa_idx, page_table)
    o_s5, h_new = s5_mixer(p['s5_u'], h_s5, w)
    o_nsa, win_new = nsa_sample(p, w, pool_nsa_cmp, pool_nsa_slc, win_buf, page_table)
    o_gla, s_new = gla_mixer(p, s_gla, w)
    mixed = jnp.concatenate([o_dsa, o_s5, o_nsa, o_gla], axis=-1)
    return mixed, (p['dsa_kv'], p['idx_k'], p['nsa_cmp'], p['nsa_slc'], win_new, h_new, s_new)


def trunk_layer(x, mixed, w_out, ln1_g, ln1_b, w_gate, w_up, w_down, ln2_g, ln2_b):
    h = layer_norm(DEEPNORM_ALPHA * x + mixed.astype(x.dtype) @ w_out, ln1_g, ln1_b).astype(x.dtype)
    f = (jax.nn.silu(h @ w_gate) * (h @ w_up)) @ w_down
    return layer_norm(DEEPNORM_ALPHA * h + f, ln2_g, ln2_b).astype(x.dtype)


def setup_inputs(seed: int = 0) -> dict:
    key = jax.random.key(seed)
    ks = iter(jax.random.split(key, 48))
    f32 = jnp.float32

    def nrm(shape, scale):
        return jax.random.normal(next(ks), shape, f32) * scale

    n_pages = PAST_LEN // PAGE_SIZE
    pool_pages = (5 * DEC_BATCH * n_pages + 3) // 4
    wbuf = min(WINDOW, PAST_LEN)
    kv_row = (KV_HEADS, 2, HEAD_DIM)
    page_table = jax.random.permutation(next(ks), pool_pages)[:DEC_BATCH * n_pages]
    page_table = page_table.reshape(DEC_BATCH, n_pages).astype(jnp.int32)
    s5_n = jnp.arange(S5_STATE, dtype=f32)
    return {
        'x_prompt': nrm((BATCH, SEQ, D_MODEL), 1.0),
        'x_sample': nrm((DEC_BATCH, DEC_SEQ, D_MODEL), 1.0),
        'cache_dsa_kv': nrm((DEPTH, pool_pages, PAGE_SIZE) + kv_row, 1.0),
        'cache_dsa_idx': nrm((DEPTH, pool_pages, PAGE_SIZE, IDX_DIM), 1.0),
        'cache_nsa_cmp': nrm((DEPTH, pool_pages, PAGE_SIZE) + kv_row, 1.0),
        'cache_nsa_slc': nrm((DEPTH, pool_pages, PAGE_SIZE) + kv_row, 1.0),
        'cache_nsa_win': nrm((DEPTH, DEC_BATCH, wbuf) + kv_row, 1.0),
        'state_s5': nrm((DEPTH, DEC_BATCH, 2, S5_GROUPS, S5_STATE), 0.5),
        'state_gla': nrm((DEPTH, DEC_BATCH, GLA_HEADS, GLA_DK, GLA_DV), 0.5),
        'page_table': page_table,
        'w_in': nrm((DEPTH, D_MODEL, D_IN), D_MODEL ** -0.5),
        's5_a_re': -0.5 + nrm((DEPTH, S5_GROUPS, S5_STATE), 0.01),
        's5_a_im': math.pi * s5_n + nrm((DEPTH, S5_GROUPS, S5_STATE), 0.01),
        's5_b_re': nrm((DEPTH, S5_GROUPS, S5_STATE, S5_CH), (2 * S5_CH) ** -0.5),
        's5_b_im': nrm((DEPTH, S5_GROUPS, S5_STATE, S5_CH), (2 * S5_CH) ** -0.5),
        's5_c_re': nrm((DEPTH, S5_GROUPS, S5_CH, S5_STATE), (2 * S5_STATE) ** -0.5),
        's5_c_im': nrm((DEPTH, S5_GROUPS, S5_CH, S5_STATE), (2 * S5_STATE) ** -0.5),
        's5_d': nrm((DEPTH, S5_GROUPS, S5_CH), 1.0),
        's5_log_dt': jax.random.uniform(next(ks), (DEPTH, S5_GROUPS), f32,
                                        math.log(S5_DT_MIN), math.log(S5_DT_MAX)),
        's5_w_glu': nrm((DEPTH, GROUP_WIDTH, GROUP_WIDTH), GROUP_WIDTH ** -0.5),
        's5_b_glu': nrm((DEPTH, GROUP_WIDTH), 0.02),
        'nsa_cmp_pe': nrm((DEPTH, CMP_BLOCK, 2, HEAD_DIM), 0.1),
        'nsa_cmp_w1': nrm((DEPTH, 2, CMP_BLOCK * HEAD_DIM, CMP_HIDDEN), (CMP_BLOCK * HEAD_DIM) ** -0.5),
        'nsa_cmp_w2': nrm((DEPTH, 2, CMP_HIDDEN, HEAD_DIM), CMP_HIDDEN ** -0.5),
        'gla_w_gate': nrm((DEPTH, GLA_LOWRANK, GLA_HEADS * GLA_DK), GLA_LOWRANK ** -0.5),
        'gla_b_gate': 1.0 + nrm((DEPTH, GLA_HEADS * GLA_DK), 0.1),
        'gla_norm_g': 1.0 + nrm((DEPTH, GLA_DV), 0.02),
        'w_out': nrm((DEPTH, D_MODEL, D_MODEL), D_MODEL ** -0.5 * DEEPNORM_BETA),
        'ln1_g': 1.0 + nrm((DEPTH, D_MODEL), 0.02),
        'ln1_b': nrm((DEPTH, D_MODEL), 0.02),
        'ffn_w_gate': nrm((DEPTH, D_MODEL, FFN_HIDDEN), D_MODEL ** -0.5),
        'ffn_w_up': nrm((DEPTH, D_MODEL, FFN_HIDDEN), D_MODEL ** -0.5),
        'ffn_w_down': nrm((DEPTH, FFN_HIDDEN, D_MODEL), FFN_HIDDEN ** -0.5 * DEEPNORM_BETA),
        'ln2_g': 1.0 + nrm((DEPTH, D_MODEL), 0.02),
        'ln2_b': nrm((DEPTH, D_MODEL), 0.02),
    }


def reference(x_prompt, x_sample, cache_dsa_kv, cache_dsa_idx, cache_nsa_cmp, cache_nsa_slc,
              cache_nsa_win, state_s5, state_gla, page_table, w_in,
              s5_a_re, s5_a_im, s5_b_re, s5_b_im, s5_c_re, s5_c_im, s5_d, s5_log_dt,
              s5_w_glu, s5_b_glu, nsa_cmp_pe, nsa_cmp_w1, nsa_cmp_w2,
              gla_w_gate, gla_b_gate, gla_norm_g, w_out, ln1_g, ln1_b,
              ffn_w_gate, ffn_w_up, ffn_w_down, ln2_g, ln2_b):
    hp, hs = x_prompt, x_sample
    outs_p = [[] for _ in range(7)]
    outs_s = [[] for _ in range(7)]
    for l in range(DEPTH):
        w = {'a_re': s5_a_re[l], 'a_im': s5_a_im[l], 'b_re': s5_b_re[l], 'b_im': s5_b_im[l],
             'c_re': s5_c_re[l], 'c_im': s5_c_im[l], 'd': s5_d[l], 'log_dt': s5_log_dt[l],
             'w_glu': s5_w_glu[l], 'b_glu': s5_b_glu[l],
             'cmp_pe': nsa_cmp_pe[l], 'cmp_w1': nsa_cmp_w1[l], 'cmp_w2': nsa_cmp_w2[l],
             'gla_w_gate': gla_w_gate[l], 'gla_b_gate': gla_b_gate[l], 'gla_norm_g': gla_norm_g[l]}
        mixed_p, st_p = mix_prompt(split_projection(hp, w_in[l]), w)
        mixed_s, st_s = mix_sample(split_projection(hs, w_in[l]), w,
                                   cache_dsa_kv[l], cache_dsa_idx[l], cache_nsa_cmp[l], cache_nsa_slc[l],
                                   cache_nsa_win[l], state_s5[l], state_gla[l], page_table)
        hp = trunk_layer(hp, mixed_p, w_out[l], ln1_g[l], ln1_b[l],
                         ffn_w_gate[l], ffn_w_up[l], ffn_w_down[l], ln2_g[l], ln2_b[l])
        hs = trunk_layer(hs, mixed_s, w_out[l], ln1_g[l], ln1_b[l],
                         ffn_w_gate[l], ffn_w_up[l], ffn_w_down[l], ln2_g[l], ln2_b[l])
        for lst, st in zip(outs_p, st_p):
            lst.append(st)
        for lst, st in zip(outs_s, st_s):
            lst.append(st)
    dsa_kv_p, dsa_idx_p, nsa_cmp_p, nsa_slc_p, nsa_win_p, s5_p, gla_p = [jnp.stack(a) for a in outs_p]
    dsa_kv_s, dsa_idx_s, nsa_cmp_s, nsa_slc_s, nsa_win_s, s5_s, gla_s = [jnp.stack(a) for a in outs_s]
    return (hp, hs, dsa_kv_p, dsa_kv_s, dsa_idx_p, dsa_idx_s, nsa_cmp_p, nsa_cmp_s,
            nsa_slc_p, nsa_slc_s, nsa_win_p, nsa_win_s, s5_p, s5_s, gla_p, gla_s)
```

```python
import functools
import math

import jax
import jax.numpy as jnp
from jax import lax
import numpy as np
from jax.experimental import pallas as pl
from jax.experimental.pallas import tpu as pltpu

D_MODEL = 2048
DEPTH = 2
PAST_LEN = 2048
PAGE_SIZE = 128
GROUP_WIDTH = 512
HEAD_DIM = 64
N_HEADS = 8
KV_HEADS = 2
Q_PER_KV = 4
KV_ROW = 256
IDX_HEADS = 4
IDX_DIM = 64
DSA_TOPK = 256
S5_CH = 16
S5_GROUPS = 32
S5_STATE = 64
CMP_BLOCK = 32
CMP_HIDDEN = 128
SLC_BLOCK = 64
SLC_TOPN = 16
WINDOW = 512
QBLOCK = 128
GLA_HEADS = 4
GLA_DK = 64
GLA_DV = 128
GLA_LOWRANK = 16
GLA_TAU = 16.0
GLA_CHUNK = 64
FFN_HIDDEN = 5632
DEEPNORM_ALPHA = (2 * DEPTH) ** 0.25
LN_EPS = 1e-5

IN_SIZES = (512, 256, 256, 64, 4, 512, 512, 256, 256, 256, 24, 256, 256, 512, 16, 512)
D_IN = sum(IN_SIZES)
D_IN_PAD = 4608

VMEM_LIMIT = 56 * 1024 * 1024


def _matmul_kernel(x_ref, w_ref, o_ref):
    o_ref[...] = jnp.dot(x_ref[...].astype(jnp.bfloat16), w_ref[...],
                         preferred_element_type=jnp.float32)


def dense_matmul(x, w_bf16, tn=512):
    m, k = x.shape
    n = w_bf16.shape[1]
    tm = min(512, m)
    return pl.pallas_call(
        _matmul_kernel,
        out_shape=jax.ShapeDtypeStruct((m, n), jnp.float32),
        grid=(m // tm, n // tn),
        in_specs=[pl.BlockSpec((tm, k), lambda i, j: (i, 0)),
                  pl.BlockSpec((k, tn), lambda i, j: (0, j))],
        out_specs=pl.BlockSpec((tm, tn), lambda i, j: (i, j)),
        compiler_params=pltpu.CompilerParams(
            dimension_semantics=("parallel", "parallel"), vmem_limit_bytes=VMEM_LIMIT),
        name="dense_matmul",
    )(x, w_bf16)


def _layer_norm_rows(z, g, b):
    mu = jnp.mean(z, axis=-1, keepdims=True)
    zc = z - mu
    var = jnp.mean(zc * zc, axis=-1, keepdims=True)
    return zc * lax.rsqrt(var + LN_EPS) * g + b


def _outproj_ln_kernel(x_ref, m_ref, w_ref, g_ref, b_ref, o_ref):
    y = jnp.dot(m_ref[...].astype(jnp.bfloat16), w_ref[...], preferred_element_type=jnp.float32)
    z = DEEPNORM_ALPHA * x_ref[...] + y
    o_ref[...] = _layer_norm_rows(z, g_ref[...], b_ref[...])


def outproj_ln(x, mixed, w_bf16, g, b):
    m, d = x.shape
    tm = min(256, m)
    return pl.pallas_call(
        _outproj_ln_kernel,
        out_shape=jax.ShapeDtypeStruct((m, d), jnp.float32),
        grid=(m // tm,),
        in_specs=[pl.BlockSpec((tm, d), lambda i: (i, 0)),
                  pl.BlockSpec((tm, d), lambda i: (i, 0)),
                  pl.BlockSpec((d, d), lambda i: (0, 0)),
                  pl.BlockSpec((1, d), lambda i: (0, 0)),
                  pl.BlockSpec((1, d), lambda i: (0, 0))],
        out_specs=pl.BlockSpec((tm, d), lambda i: (i, 0)),
        compiler_params=pltpu.CompilerParams(
            dimension_semantics=("parallel",), vmem_limit_bytes=VMEM_LIMIT),
        name="outproj_ln",
    )(x, mixed, w_bf16, g.reshape(1, d), b.reshape(1, d))


def _ffn_ln_kernel(h_ref, wg_ref, wu_ref, wd_ref, g_ref, b_ref, o_ref, acc_ref):
    f = pl.program_id(1)

    @pl.when(f == 0)
    def _():
        acc_ref[...] = jnp.zeros_like(acc_ref)

    hb = h_ref[...].astype(jnp.bfloat16)
    a = jnp.dot(hb, wg_ref[...], preferred_element_type=jnp.float32)
    u = jnp.dot(hb, wu_ref[...], preferred_element_type=jnp.float32)
    act = (a * jax.nn.sigmoid(a) * u).astype(jnp.bfloat16)
    acc_ref[...] += jnp.dot(act, wd_ref[...], preferred_element_type=jnp.float32)

    @pl.when(f == pl.num_programs(1) - 1)
    def _():
        z = DEEPNORM_ALPHA * h_ref[...] + acc_ref[...]
        o_ref[...] = _layer_norm_rows(z, g_ref[...], b_ref[...])


def ffn_ln(h, wg, wu, wd, g, b, tf=512):
    m, d = h.shape
    fh = wg.shape[1]
    tm = min(512, m)
    return pl.pallas_call(
        _ffn_ln_kernel,
        out_shape=jax.ShapeDtypeStruct((m, d), jnp.float32),
        grid=(m // tm, fh // tf),
        in_specs=[pl.BlockSpec((tm, d), lambda i, f: (i, 0)),
                  pl.BlockSpec((d, tf), lambda i, f: (0, f)),
                  pl.BlockSpec((d, tf), lambda i, f: (0, f)),
                  pl.BlockSpec((tf, d), lambda i, f: (f, 0)),
                  pl.BlockSpec((1, d), lambda i, f: (0, 0)),
                  pl.BlockSpec((1, d), lambda i, f: (0, 0))],
        out_specs=pl.BlockSpec((tm, d), lambda i, f: (i, 0)),
        scratch_shapes=[pltpu.VMEM((tm, d), jnp.float32)],
        compiler_params=pltpu.CompilerParams(
            dimension_semantics=("parallel", "arbitrary"), vmem_limit_bytes=VMEM_LIMIT),
        name="ffn_ln",
    )(h, wg, wu, wd, g.reshape(1, d), b.reshape(1, d))


def masked_softmax(s, mask):
    s = jnp.where(mask, s.astype(jnp.float32), -jnp.inf)
    m = jnp.max(s, axis=-1, keepdims=True)
    m = jnp.where(jnp.isfinite(m), m, 0.0)
    p = jnp.exp(s - m)
    return p / jnp.maximum(p.sum(-1, keepdims=True), jnp.finfo(jnp.float32).tiny)


def to_blocks(a):
    n, t = a.shape[:2]
    return jnp.swapaxes(a.reshape((n, t // QBLOCK, QBLOCK) + a.shape[2:]), 0, 1)


def from_blocks(a):
    a = jnp.swapaxes(a, 0, 1)
    return a.reshape((a.shape[0], -1) + a.shape[3:])


def gather_seq(rows, pos, *extra):
    n_idx = jnp.arange(rows.shape[0]).reshape((-1,) + (1,) * (pos.ndim - 1))
    return rows[(n_idx, pos) + tuple(extra)]


def gather_pages(pool, page_table):
    g = pool[page_table]
    return g.reshape((g.shape[0], -1) + g.shape[3:])


def gather_paged_rows(pool, page_table, pos, *extra):
    n_idx = jnp.arange(page_table.shape[0]).reshape((-1,) + (1,) * (pos.ndim - 1))
    phys = page_table[n_idx, pos // PAGE_SIZE]
    return pool[(phys, pos % PAGE_SIZE) + tuple(extra)]


def gather_past_and_new(pool, page_table, new_rows, pos, *extra):
    past = gather_paged_rows(pool, page_table, jnp.minimum(pos, PAST_LEN - 1), *extra)
    new = gather_seq(new_rows, jnp.clip(pos - PAST_LEN, 0, new_rows.shape[1] - 1), *extra)
    in_past = (pos < PAST_LEN).reshape(pos.shape + (1,) * (past.ndim - pos.ndim))
    return jnp.where(in_past, past, new)


def split_projection(proj, n, t):
    offsets = np.cumsum(IN_SIZES)[:-1].tolist()
    (dq, dkv, iq, ik, iw, u, nq, ckv, skv, wkv, ng,
     gq, gk, gv, ga, gr) = jnp.split(proj[:, :D_IN].reshape(n, t, D_IN), offsets, axis=-1)
    kv_shape = (n, t, KV_HEADS, 2, HEAD_DIM)
    return {
        'dsa_q': dq.reshape(n, t, N_HEADS, HEAD_DIM),
        'dsa_kv': dkv.reshape(kv_shape),
        'idx_q': iq.reshape(n, t, IDX_HEADS, IDX_DIM),
        'idx_k': ik,
        'idx_w': iw,
        's5_u': u,
        'nsa_q': nq.reshape(n, t, N_HEADS, HEAD_DIM),
        'nsa_cmp': ckv.reshape(kv_shape),
        'nsa_slc': skv.reshape(kv_shape),
        'nsa_win': wkv.reshape(kv_shape),
        'nsa_gate': jax.nn.sigmoid(ng).reshape(n, t, N_HEADS, 3),
        'gla_q': gq.reshape(n, t, GLA_HEADS, GLA_DK),
        'gla_k': gk.reshape(n, t, GLA_HEADS, GLA_DK),
        'gla_v': gv.reshape(n, t, GLA_HEADS, GLA_DV),
        'gla_a': ga,
        'gla_r': gr,
    }


def indexer_scores(iq, iw, ik):
    dots = jnp.einsum('nqhd,nld->nqhl', iq, ik) * IDX_DIM ** -0.5
    return jnp.einsum('nqh,nqhl->nql', iw * IDX_HEADS ** -0.5, jax.nn.relu(dots)).astype(jnp.float32)


def dsa_select(scores, q_pos, n_keys):
    causal = jnp.arange(n_keys)[None, :] <= q_pos[:, None]
    k_sel = min(DSA_TOPK, n_keys // 4)
    _, sel = lax.top_k(jnp.where(causal, scores, -jnp.inf), k_sel)
    return sel, sel <= q_pos[None, :, None]


def dsa_attend(q, kv_sel, valid):
    n, nq = q.shape[:2]
    qg = q.reshape(n, nq, KV_HEADS, Q_PER_KV, HEAD_DIM)
    s = jnp.einsum('nqgrd,nqkgd->nqgrk', qg, kv_sel[..., 0, :]) * HEAD_DIM ** -0.5
    p = masked_softmax(s, valid[:, :, None, None, :])
    o = jnp.einsum('nqgrk,nqkgd->nqgrd', p, kv_sel[..., 1, :])
    return o.reshape(n, nq, GROUP_WIDTH)


def dsa_prompt(p):
    kv, ik = p['dsa_kv'], p['idx_k']
    t = kv.shape[1]

    def block(args):
        i, qb, iqb, iwb = args
        q_pos = i * QBLOCK + jnp.arange(QBLOCK)
        sel, valid = dsa_select(indexer_scores(iqb, iwb, ik), q_pos, t)
        return dsa_attend(qb, gather_seq(kv, sel), valid)

    out = lax.map(block, (jnp.arange(t // QBLOCK), to_blocks(p['dsa_q']),
                          to_blocks(p['idx_q']), to_blocks(p['idx_w'])))
    return from_blocks(out)


def dsa_sample(p, pool_kv, pool_idx, page_table):
    s_new = p['dsa_q'].shape[1]
    n_keys = PAST_LEN + s_new
    ik_all = jnp.concatenate([gather_pages(pool_idx, page_table), p['idx_k']], axis=1)
    q_pos = PAST_LEN + jnp.arange(s_new)
    sel, valid = dsa_select(indexer_scores(p['idx_q'], p['idx_w'], ik_all), q_pos, n_keys)
    kv_sel = gather_past_and_new(pool_kv, page_table, p['dsa_kv'], sel)
    return dsa_attend(p['dsa_q'], kv_sel, valid)


def complex_linear_combine(e1, e2):
    a1r, a1i, b1r, b1i = e1
    a2r, a2i, b2r, b2i = e2
    return (a2r * a1r - a2i * a1i,
            a2r * a1i + a2i * a1r,
            a2r * b1r - a2i * b1i + b2r,
            a2r * b1i + a2i * b1r + b2i)


def s5_mixer(u, h0, w):
    f32 = jnp.float32
    n, t = u.shape[:2]
    uf = u.reshape(n, t, S5_GROUPS, S5_CH)
    a_re, a_im = w['a_re'], w['a_im']
    dt = jnp.exp(w['log_dt'])[:, None]
    mag = jnp.exp(a_re * dt)
    ab_re, ab_im = mag * jnp.cos(a_im * dt), mag * jnp.sin(a_im * dt)
    den = a_re * a_re + a_im * a_im
    nr, ni = ab_re - 1.0, ab_im
    f_re = (nr * a_re + ni * a_im) / den
    f_im = (ni * a_re - nr * a_im) / den
    b_re, b_im = w['b_re'], w['b_im']
    bb_re = f_re[..., None] * b_re - f_im[..., None] * b_im
    bb_im = f_re[..., None] * b_im + f_im[..., None] * b_re
    bu_re = jnp.einsum('gsc,btgc->btgs', bb_re, uf)
    bu_im = jnp.einsum('gsc,btgc->btgs', bb_im, uf)
    cum_re, cum_im, sc_re, sc_im = lax.associative_scan(
        complex_linear_combine,
        (jnp.broadcast_to(ab_re, bu_re.shape), jnp.broadcast_to(ab_im, bu_im.shape), bu_re, bu_im),
        axis=1)
    h0r, h0i = h0[:, 0][:, None], h0[:, 1][:, None]
    h_re = cum_re * h0r - cum_im * h0i + sc_re
    h_im = cum_re * h0i + cum_im * h0r + sc_im
    y = (jnp.einsum('gcs,btgs->btgc', w['c_re'], h_re)
         - jnp.einsum('gcs,btgs->btgc', w['c_im'], h_im)
         + w['d'] * uf)
    y = jax.nn.gelu(y.reshape(n, t, GROUP_WIDTH))
    y = y * jax.nn.sigmoid(y @ w['w_glu'] + w['b_glu'])
    h_last = jnp.stack([h_re[:, -1], h_im[:, -1]], axis=1)
    return y, h_last


def nsa_compress(rows, w):
    n, length = rows.shape[:2]
    nc = length // CMP_BLOCK
    blk = rows[:, :nc * CMP_BLOCK].reshape(n, nc, CMP_BLOCK, KV_HEADS, 2, HEAD_DIM)
    blk = blk + w['cmp_pe'][:, None]
    flat = jnp.moveaxis(blk, 2, 4).reshape(n, nc, KV_HEADS, 2, CMP_BLOCK * HEAD_DIM)
    hid = jax.nn.gelu(jnp.einsum('ncgjf,jfe->ncgje', flat, w['cmp_w1']))
    return jnp.einsum('ncgje,jed->ncgjd', hid, w['cmp_w2'])


def nsa_core(q, gate, q_pos, cmp_kv, n_slc_blocks, slc_gather, win_kv, win_pos):
    n, nq = q.shape[:2]
    qg = q.reshape(n, nq, KV_HEADS, Q_PER_KV, HEAD_DIM)
    scale = HEAD_DIM ** -0.5
    nc = cmp_kv.shape[1]
    c_vis = (jnp.arange(nc) + 1) * CMP_BLOCK - 1 <= q_pos[:, None]
    s_c = jnp.einsum('nqgrd,ncgd->nqgrc', qg, cmp_kv[..., 0, :]) * scale
    p_c = masked_softmax(s_c, c_vis[None, :, None, None, :])
    o_cmp = jnp.einsum('nqgrc,ncgd->nqgrd', p_c, cmp_kv[..., 1, :])
    ratio = SLC_BLOCK // CMP_BLOCK
    imp = p_c.sum(3)
    imp = jnp.pad(imp, ((0, 0), (0, 0), (0, 0), (0, n_slc_blocks * ratio - nc)))
    imp = imp.reshape(n, nq, KV_HEADS, n_slc_blocks, ratio).sum(-1)
    blk = jnp.arange(n_slc_blocks)[None, :]
    cur = (q_pos // SLC_BLOCK)[:, None]
    forced = (blk == 0) | (blk == cur) | (blk == cur - 1)
    future = blk > cur
    score = jnp.where(future[None, :, None, :], -jnp.inf,
                      jnp.where(forced[None, :, None, :], jnp.inf, imp))
    n_sel = min(SLC_TOPN, n_slc_blocks)
    _, sel = lax.top_k(score, n_sel)
    pos = (sel[..., None] * SLC_BLOCK + jnp.arange(SLC_BLOCK)).reshape(n, nq, KV_HEADS, n_sel * SLC_BLOCK)
    valid = pos <= q_pos[None, :, None, None]
    kv_s = slc_gather(pos)
    s_s = jnp.einsum('nqgrd,nqgkd->nqgrk', qg, kv_s[..., 0, :]) * scale
    p_s = masked_softmax(s_s, valid[:, :, :, None, :])
    o_slc = jnp.einsum('nqgrk,nqgkd->nqgrd', p_s, kv_s[..., 1, :])
    w_vis = ((win_pos[None, :] <= q_pos[:, None]) & (win_pos[None, :] > q_pos[:, None] - WINDOW)
             & (win_pos[None, :] >= 0))
    s_w = jnp.einsum('nqgrd,nwgd->nqgrw', qg, win_kv[..., 0, :]) * scale
    p_w = masked_softmax(s_w, w_vis[None, :, None, None, :])
    o_win = jnp.einsum('nqgrw,nwgd->nqgrd', p_w, win_kv[..., 1, :])
    g = gate.reshape(n, nq, KV_HEADS, Q_PER_KV, 3)
    o = g[..., 0:1] * o_cmp + g[..., 1:2] * o_slc + g[..., 2:3] * o_win
    return o.reshape(n, nq, GROUP_WIDTH)


def nsa_prompt(p, w):
    slc_rows, win_rows = p['nsa_slc'], p['nsa_win']
    t = slc_rows.shape[1]
    cmp_kv = nsa_compress(p['nsa_cmp'], w)
    win_pad = jnp.pad(win_rows, ((0, 0), (WINDOW, 0), (0, 0), (0, 0), (0, 0)))
    g_idx = jnp.arange(KV_HEADS)[None, None, :, None]

    def block(args):
        i, qb, gb = args
        q_pos = i * QBLOCK + jnp.arange(QBLOCK)
        win_kv = lax.dynamic_slice_in_dim(win_pad, i * QBLOCK, WINDOW + QBLOCK, axis=1)
        win_pos = i * QBLOCK - WINDOW + jnp.arange(WINDOW + QBLOCK)
        return nsa_core(qb, gb, q_pos, cmp_kv, t // SLC_BLOCK,
                        lambda pos: gather_seq(slc_rows, pos, g_idx), win_kv, win_pos)

    out = lax.map(block, (jnp.arange(t // QBLOCK), to_blocks(p['nsa_q']), to_blocks(p['nsa_gate'])))
    return from_blocks(out), win_rows[:, -min(WINDOW, t):]


def nsa_sample(p, w, pool_cmp, pool_slc, win_buf, page_table):
    s_new = p['nsa_q'].shape[1]
    n_keys = PAST_LEN + s_new
    cmp_all = jnp.concatenate([gather_pages(pool_cmp, page_table), p['nsa_cmp']], axis=1)
    cmp_kv = nsa_compress(cmp_all, w)
    wbuf = win_buf.shape[1]
    win_all = jnp.concatenate([win_buf, p['nsa_win']], axis=1)
    win_pos = PAST_LEN - wbuf + jnp.arange(wbuf + s_new)
    q_pos = PAST_LEN + jnp.arange(s_new)
    g_idx = jnp.arange(KV_HEADS)[None, None, :, None]
    out = nsa_core(p['nsa_q'], p['nsa_gate'], q_pos, cmp_kv, -(-n_keys // SLC_BLOCK),
                   lambda pos: gather_past_and_new(pool_slc, page_table, p['nsa_slc'], pos, g_idx),
                   win_all, win_pos)
    return out, win_all[:, -wbuf:]


def gla_recurrence(q, k, v, log_a, s0):
    n, t = q.shape[:2]
    c = min(GLA_CHUNK, t)
    tp = -(-t // c) * c

    def chunks(a):
        a = jnp.pad(a, ((0, 0), (0, tp - t)) + ((0, 0),) * (a.ndim - 2))
        return jnp.moveaxis(a.reshape((n, tp // c, c) + a.shape[2:]), 1, 0)

    tril = jnp.tril(jnp.ones((c, c), dtype=bool))[None, :, :, None, None]

    def step(s, inp):
        qc, kc, vc, gc = inp
        b = jnp.cumsum(gc, axis=1)
        o_inter = jnp.einsum('nchk,nhkv->nchv', qc * jnp.exp(b), s)
        decay = jnp.exp(jnp.where(tril, b[:, :, None] - b[:, None, :], -jnp.inf))
        att = jnp.einsum('nthk,nshk,ntshk->nths', qc, kc, decay)
        o_intra = jnp.einsum('nths,nshv->nthv', att, vc)
        b_end = b[:, -1]
        s_new = (jnp.exp(b_end)[..., None] * s
                 + jnp.einsum('nshk,nshv->nhkv', kc * jnp.exp(b_end[:, None] - b), vc))
        return s_new, o_inter + o_intra

    s_last, o = lax.scan(step, s0, (chunks(q), chunks(k), chunks(v), chunks(log_a)))
    o = jnp.moveaxis(o, 0, 1).reshape((n, tp) + o.shape[3:])[:, :t]
    return o, s_last


def gla_mixer(p, s0, w):
    n, t = p['gla_q'].shape[:2]
    q = p['gla_q'] * GLA_DK ** -0.5
    k = p['gla_k']
    v = p['gla_v']
    z = p['gla_a'] @ w['gla_w_gate'] + w['gla_b_gate']
    log_a = (jax.nn.log_sigmoid(z) / GLA_TAU).reshape(n, t, GLA_HEADS, GLA_DK)
    o, s_last = gla_recurrence(q, k, v, log_a, s0)
    o = o * lax.rsqrt(jnp.mean(o * o, -1, keepdims=True) + LN_EPS) * w['gla_norm_g']
    o = o.reshape(n, t, GROUP_WIDTH) * jax.nn.silu(p['gla_r'])
    return o, s_last


def mix_prompt(p, w):
    n = p['s5_u'].shape[0]
    o_dsa = dsa_prompt(p)
    o_s5, h_s5 = s5_mixer(p['s5_u'], jnp.zeros((n, 2, S5_GROUPS, S5_STATE), jnp.float32), w)
    o_nsa, win_state = nsa_prompt(p, w)
    o_gla, s_gla = gla_mixer(p, jnp.zeros((n, GLA_HEADS, GLA_DK, GLA_DV), jnp.float32), w)
    mixed = jnp.concatenate([o_dsa, o_s5, o_nsa, o_gla], axis=-1)
    return mixed, (p['dsa_kv'], p['idx_k'], p['nsa_cmp'], p['nsa_slc'], win_state, h_s5, s_gla)


def mix_sample(p, w, pool_dsa_kv, pool_dsa_idx, pool_nsa_cmp, pool_nsa_slc, win_buf, h_s5, s_gla, page_table):
    o_dsa = dsa_sample(p, pool_dsa_kv, pool_dsa_idx, page_table)
    o_s5, h_new = s5_mixer(p['s5_u'], h_s5, w)
    o_nsa, win_new = nsa_sample(p, w, pool_nsa_cmp, pool_nsa_slc, win_buf, page_table)
    o_gla, s_new = gla_mixer(p, s_gla, w)
    mixed = jnp.concatenate([o_dsa, o_s5, o_nsa, o_gla], axis=-1)
    return mixed, (p['dsa_kv'], p['idx_k'], p['nsa_cmp'], p['nsa_slc'], win_new, h_new, s_new)


def kernel(x_prompt, x_sample, cache_dsa_kv, cache_dsa_idx, cache_nsa_cmp, cache_nsa_slc, cache_nsa_win, state_s5, state_gla, page_table, w_in, s5_a_re, s5_a_im, s5_b_re, s5_b_im, s5_c_re, s5_c_im, s5_d, s5_log_dt, s5_w_glu, s5_b_glu, nsa_cmp_pe, nsa_cmp_w1, nsa_cmp_w2, gla_w_gate, gla_b_gate, gla_norm_g, w_out, ln1_g, ln1_b, ffn_w_gate, ffn_w_up, ffn_w_down, ln2_g, ln2_b):
    bf16 = jnp.bfloat16
    np_, tp_ = x_prompt.shape[:2]
    ns_, ts_ = x_sample.shape[:2]
    hp = x_prompt.reshape(np_ * tp_, D_MODEL)
    hs = x_sample.reshape(ns_ * ts_, D_MODEL)
    outs_p = [[] for _ in range(7)]
    outs_s = [[] for _ in range(7)]
    for l in range(DEPTH):
        w = {'a_re': s5_a_re[l], 'a_im': s5_a_im[l], 'b_re': s5_b_re[l], 'b_im': s5_b_im[l],
             'c_re': s5_c_re[l], 'c_im': s5_c_im[l], 'd': s5_d[l], 'log_dt': s5_log_dt[l],
             'w_glu': s5_w_glu[l], 'b_glu': s5_b_glu[l],
             'cmp_pe': nsa_cmp_pe[l], 'cmp_w1': nsa_cmp_w1[l], 'cmp_w2': nsa_cmp_w2[l],
             'gla_w_gate': gla_w_gate[l], 'gla_b_gate': gla_b_gate[l], 'gla_norm_g': gla_norm_g[l]}
        w_in_l = jnp.pad(w_in[l], ((0, 0), (0, D_IN_PAD - D_IN))).astype(bf16)
        w_out_l = w_out[l].astype(bf16)
        wg_l, wu_l, wd_l = ffn_w_gate[l].astype(bf16), ffn_w_up[l].astype(bf16), ffn_w_down[l].astype(bf16)
        pp = split_projection(dense_matmul(hp, w_in_l), np_, tp_)
        ps = split_projection(dense_matmul(hs, w_in_l), ns_, ts_)
        mixed_p, st_p = mix_prompt(pp, w)
        mixed_s, st_s = mix_sample(ps, w, cache_dsa_kv[l], cache_dsa_idx[l], cache_nsa_cmp[l],
                                   cache_nsa_slc[l], cache_nsa_win[l], state_s5[l], state_gla[l], page_table)
        hp = outproj_ln(hp, mixed_p.reshape(np_ * tp_, D_MODEL), w_out_l, ln1_g[l], ln1_b[l])
        hs = outproj_ln(hs, mixed_s.reshape(ns_ * ts_, D_MODEL), w_out_l, ln1_g[l], ln1_b[l])
        hp = ffn_ln(hp, wg_l, wu_l, wd_l, ln2_g[l], ln2_b[l])
        hs = ffn_ln(hs, wg_l, wu_l, wd_l, ln2_g[l], ln2_b[l])
        for lst, st in zip(outs_p, st_p):
            lst.append(st)
        for lst, st in zip(outs_s, st_s):
            lst.append(st)
    dsa_kv_p, dsa_idx_p, nsa_cmp_p, nsa_slc_p, nsa_win_p, s5_p, gla_p = [jnp.stack(a) for a in outs_p]
    dsa_kv_s, dsa_idx_s, nsa_cmp_s, nsa_slc_s, nsa_win_s, s5_s, gla_s = [jnp.stack(a) for a in outs_s]
    return (hp.reshape(np_, tp_, D_MODEL), hs.reshape(ns_, ts_, D_MODEL),
            dsa_kv_p, dsa_kv_s, dsa_idx_p, dsa_idx_s, nsa_cmp_p, nsa_cmp_s,
            nsa_slc_p, nsa_slc_s, nsa_win_p, nsa_win_s, s5_p, s5_s, gla_p, gla_s)
```

```python
import functools
import math

import jax
import jax.numpy as jnp
from jax import lax
import numpy as np
from jax.experimental import pallas as pl
from jax.experimental.pallas import tpu as pltpu

D_MODEL = 2048
DEPTH = 2
PAST_LEN = 2048
PAGE_SIZE = 128
GROUP_WIDTH = 512
HEAD_DIM = 64
N_HEADS = 8
KV_HEADS = 2
Q_PER_KV = 4
KV_ROW = 256
IDX_HEADS = 4
IDX_DIM = 64
DSA_TOPK = 256
S5_CH = 16
S5_GROUPS = 32
S5_STATE = 64
S5_WIDTH = S5_GROUPS * S5_STATE
CMP_BLOCK = 32
CMP_HIDDEN = 128
SLC_BLOCK = 64
SLC_TOPN = 16
WINDOW = 512
QBLOCK = 128
GLA_HEADS = 4
GLA_DK = 64
GLA_DV = 128
GLA_LOWRANK = 16
GLA_TAU = 16.0
GLA_CHUNK = 64
FFN_HIDDEN = 5632
DEEPNORM_ALPHA = (2 * DEPTH) ** 0.25
LN_EPS = 1e-5

IN_SIZES = (512, 256, 256, 64, 4, 512, 512, 256, 256, 256, 24, 256, 256, 512, 16, 512)
D_IN = sum(IN_SIZES)
D_IN_PAD = 4608

VMEM_LIMIT = 56 * 1024 * 1024
LANES = 128
KEY_CHUNK = 512
CMP_LANES = 128
S5_TIME_CHUNK = 256
ATT_SCALE = HEAD_DIM ** -0.5
MASKED = -1e30
INT_MIN = -2 ** 31
F32_TINY = float(np.finfo(np.float32).tiny)

bf16 = jnp.bfloat16
f32 = jnp.float32


def _matmul_kernel(x_ref, w_ref, o_ref):
    o_ref[...] = jnp.dot(x_ref[...].astype(bf16), w_ref[...], preferred_element_type=f32)


def dense_matmul(x, w_bf16, tn=512):
    m, k = x.shape
    n = w_bf16.shape[1]
    tm = min(512, m)
    return pl.pallas_call(
        _matmul_kernel,
        out_shape=jax.ShapeDtypeStruct((m, n), f32),
        grid=(m // tm, n // tn),
        in_specs=[pl.BlockSpec((tm, k), lambda i, j: (i, 0)),
                  pl.BlockSpec((k, tn), lambda i, j: (0, j))],
        out_specs=pl.BlockSpec((tm, tn), lambda i, j: (i, j)),
        compiler_params=pltpu.CompilerParams(
            dimension_semantics=("parallel", "parallel"), vmem_limit_bytes=VMEM_LIMIT),
        name="dense_matmul",
    )(x, w_bf16)


def _layer_norm_rows(z, g, b):
    mu = jnp.mean(z, axis=-1, keepdims=True)
    zc = z - mu
    var = jnp.mean(zc * zc, axis=-1, keepdims=True)
    return zc * lax.rsqrt(var + LN_EPS) * g + b


def _outproj_ln_kernel(x_ref, m0_ref, m1_ref, m2_ref, m3_ref, w_ref, g_ref, b_ref, o_ref):
    y = DEEPNORM_ALPHA * x_ref[...]
    for j, m_ref in enumerate((m0_ref, m1_ref, m2_ref, m3_ref)):
        y = y + jnp.dot(m_ref[...].astype(bf16), w_ref[j * GROUP_WIDTH:(j + 1) * GROUP_WIDTH, :],
                        preferred_element_type=f32)
    o_ref[...] = _layer_norm_rows(y, g_ref[...], b_ref[...])


def outproj_ln(x, mixers, w_bf16, g, b):
    m, d = x.shape
    tm = min(256, m)
    mix_spec = pl.BlockSpec((tm, GROUP_WIDTH), lambda i: (i, 0))
    return pl.pallas_call(
        _outproj_ln_kernel,
        out_shape=jax.ShapeDtypeStruct((m, d), f32),
        grid=(m // tm,),
        in_specs=[pl.BlockSpec((tm, d), lambda i: (i, 0)),
                  mix_spec, mix_spec, mix_spec, mix_spec,
                  pl.BlockSpec((d, d), lambda i: (0, 0)),
                  pl.BlockSpec((1, d), lambda i: (0, 0)),
                  pl.BlockSpec((1, d), lambda i: (0, 0))],
        out_specs=pl.BlockSpec((tm, d), lambda i: (i, 0)),
        compiler_params=pltpu.CompilerParams(
            dimension_semantics=("parallel",), vmem_limit_bytes=VMEM_LIMIT),
        name="outproj_ln",
    )(x, *mixers, w_bf16, g.reshape(1, d), b.reshape(1, d))


def _ffn_ln_kernel(h_ref, wg_ref, wu_ref, wd_ref, g_ref, b_ref, o_ref, acc_ref):
    f = pl.program_id(1)

    @pl.when(f == 0)
    def _():
        acc_ref[...] = jnp.zeros_like(acc_ref)

    hb = h_ref[...].astype(bf16)
    a = jnp.dot(hb, wg_ref[...], preferred_element_type=f32)
    u = jnp.dot(hb, wu_ref[...], preferred_element_type=f32)
    act = (a * jax.nn.sigmoid(a) * u).astype(bf16)
    acc_ref[...] += jnp.dot(act, wd_ref[...], preferred_element_type=f32)

    @pl.when(f == pl.num_programs(1) - 1)
    def _():
        z = DEEPNORM_ALPHA * h_ref[...] + acc_ref[...]
        o_ref[...] = _layer_norm_rows(z, g_ref[...], b_ref[...])


def ffn_ln(h, wg, wu, wd, g, b, tf=512):
    m, d = h.shape
    fh = wg.shape[1]
    tm = min(512, m)
    return pl.pallas_call(
        _ffn_ln_kernel,
        out_shape=jax.ShapeDtypeStruct((m, d), f32),
        grid=(m // tm, fh // tf),
        in_specs=[pl.BlockSpec((tm, d), lambda i, f: (i, 0)),
                  pl.BlockSpec((d, tf), lambda i, f: (0, f)),
                  pl.BlockSpec((d, tf), lambda i, f: (0, f)),
                  pl.BlockSpec((tf, d), lambda i, f: (f, 0)),
                  pl.BlockSpec((1, d), lambda i, f: (0, 0)),
                  pl.BlockSpec((1, d), lambda i, f: (0, 0))],
        out_specs=pl.BlockSpec((tm, d), lambda i, f: (i, 0)),
        scratch_shapes=[pltpu.VMEM((tm, d), f32)],
        compiler_params=pltpu.CompilerParams(
            dimension_semantics=("parallel", "arbitrary"), vmem_limit_bytes=VMEM_LIMIT),
        name="ffn_ln",
    )(h, wg, wu, wd, g.reshape(1, d), b.reshape(1, d))


def _tile_rows(x, reps):
    return jnp.concatenate([x] * reps, axis=0)


def _masked_flash(qg, kt_at, v_at, c_lo, c_hi, mask_at):
    rows = qg.shape[0]

    def body(c, carry):
        m, l, acc = carry
        s = jnp.dot(qg, kt_at(c), preferred_element_type=f32) * ATT_SCALE
        keep = _tile_rows(mask_at(c), rows // QBLOCK) > 0.5
        s = jnp.where(keep, s, MASKED)
        m_new = jnp.maximum(m, jnp.max(s, axis=1, keepdims=True))
        p = jnp.where(keep, jnp.exp(s - m_new), 0.0)
        alpha = jnp.exp(m - m_new)
        l = alpha * l + jnp.sum(p, axis=1, keepdims=True)
        acc = alpha * acc + jnp.dot(p.astype(bf16), v_at(c), preferred_element_type=f32)
        return m_new, l, acc

    init = (jnp.full((rows, 1), MASKED, f32), jnp.zeros((rows, 1), f32),
            jnp.zeros((rows, HEAD_DIM), f32))
    _, l, acc = lax.fori_loop(c_lo, c_hi, body, init)
    return acc / l


def _sortable_key(x):
    bits = pltpu.bitcast(x, jnp.int32)
    return bits ^ (jnp.right_shift(bits, 31) & jnp.int32(0x7FFFFFFF))


def _dsa_prompt_kernel(iq_ref, iw_ref, ikt_ref, q_ref, kt_ref, v_ref, o_ref, keys_ref, sel_ref,
                       *, idx_bits):
    i = pl.program_id(1)
    nch = (i * QBLOCK + QBLOCK + KEY_CHUNK - 1) // KEY_CHUNK
    q_pos = i * QBLOCK + lax.broadcasted_iota(jnp.int32, (QBLOCK, 1), 0)
    lane = lax.broadcasted_iota(jnp.int32, (QBLOCK, KEY_CHUNK), 1)

    iq = iq_ref[...]
    iq_stack = jnp.concatenate(
        [iq[:, h * IDX_DIM:(h + 1) * IDX_DIM] for h in range(IDX_HEADS)], axis=0).astype(bf16)
    iw = iw_ref[...] * IDX_HEADS ** -0.5

    def score_body(c, carry):
        dots = jnp.dot(iq_stack, ikt_ref[c], preferred_element_type=f32) * IDX_DIM ** -0.5
        acc = jnp.zeros((QBLOCK, KEY_CHUNK), f32)
        for h in range(IDX_HEADS):
            acc = acc + iw[:, h:h + 1] * jnp.maximum(dots[h * QBLOCK:(h + 1) * QBLOCK], 0.0)
        causal = c * KEY_CHUNK + lane <= q_pos
        keys_ref[c] = jnp.where(causal, _sortable_key(acc), INT_MIN)
        return carry

    lax.fori_loop(0, nch, score_body, 0)

    def count(pred_at):
        def body(c, acc):
            hit = jnp.where(pred_at(c), 1.0, 0.0)
            part = hit[:, :LANES]
            for j in range(1, KEY_CHUNK // LANES):
                part = part + hit[:, j * LANES:(j + 1) * LANES]
            return acc + part
        acc = lax.fori_loop(0, nch, body, jnp.zeros((QBLOCK, LANES), f32))
        return jnp.sum(acc, axis=1, keepdims=True)

    topk = float(DSA_TOPK)
    cnt_nonneg = count(lambda c: keys_ref[c] >= 0)
    thr0 = jnp.where(cnt_nonneg >= topk, 0, INT_MIN).astype(jnp.int32)

    def bit_body(b, thr):
        cand = thr + jnp.left_shift(jnp.int32(1), 30 - b)
        cnt = count(lambda c: keys_ref[c] >= cand)
        return jnp.where(cnt >= topk, cand, thr)

    thr = lax.fori_loop(0, 31, bit_body, thr0)
    need = topk - count(lambda c: keys_ref[c] > thr)

    def cut_body(b, cut):
        cand = cut + jnp.left_shift(jnp.int32(1), idx_bits - 1 - b)
        cnt = count(lambda c: (keys_ref[c] == thr) & (c * KEY_CHUNK + lane < cand))
        return jnp.where(cnt < need, cand, cut)

    cut = lax.fori_loop(0, idx_bits, cut_body, jnp.zeros((QBLOCK, 1), jnp.int32))

    def sel_body(c, carry):
        k = keys_ref[c]
        kidx = c * KEY_CHUNK + lane
        chosen = ((k > thr) | ((k == thr) & (kidx <= cut))) & (kidx <= q_pos)
        sel_ref[c] = jnp.where(chosen, 1.0, 0.0)
        return carry

    lax.fori_loop(0, nch, sel_body, 0)

    for g in range(KV_HEADS):
        o = _masked_flash(q_ref[g].astype(bf16), lambda c: kt_ref[g, c], lambda c: v_ref[g, c],
                          0, nch, lambda c: sel_ref[c])
        for r in range(Q_PER_KV):
            h = g * Q_PER_KV + r
            o_ref[:, h * HEAD_DIM:(h + 1) * HEAD_DIM] = o[r * QBLOCK:(r + 1) * QBLOCK]


def _stack_query_heads(q, n, t):
    q = q.reshape(n, t // QBLOCK, QBLOCK, KV_HEADS, Q_PER_KV, HEAD_DIM)
    q = q.transpose(0, 3, 1, 4, 2, 5)
    return q.reshape(n, KV_HEADS, t // QBLOCK, Q_PER_KV * QBLOCK, HEAD_DIM)


def _chunked_kv(kv, n, t, chunk):
    kv = kv.reshape(n, t // chunk, chunk, KV_HEADS, 2, HEAD_DIM).astype(bf16)
    kt = kv[:, :, :, :, 0, :].transpose(0, 3, 1, 4, 2)
    v = kv[:, :, :, :, 1, :].transpose(0, 3, 1, 2, 4)
    return kt, v


def dsa_prompt(dq, dkv, iq, ik, iw):
    n, t = dq.shape[:2]
    nchunks = t // KEY_CHUNK
    idx_bits = int(math.log2(t))
    assert 2 ** idx_bits == t and t % KEY_CHUNK == 0 and min(DSA_TOPK, t // 4) == DSA_TOPK
    ikt = ik.reshape(n, nchunks, KEY_CHUNK, IDX_DIM).transpose(0, 1, 3, 2).astype(bf16)
    qs = _stack_query_heads(dq, n, t)
    kt, v = _chunked_kv(dkv, n, t, KEY_CHUNK)
    return pl.pallas_call(
        functools.partial(_dsa_prompt_kernel, idx_bits=idx_bits),
        out_shape=jax.ShapeDtypeStruct((n, t, GROUP_WIDTH), f32),
        grid=(n, t // QBLOCK),
        in_specs=[
            pl.BlockSpec((None, QBLOCK, IDX_HEADS * IDX_DIM), lambda b, i: (b, i, 0)),
            pl.BlockSpec((None, QBLOCK, IDX_HEADS), lambda b, i: (b, i, 0)),
            pl.BlockSpec((None, nchunks, IDX_DIM, KEY_CHUNK), lambda b, i: (b, 0, 0, 0)),
            pl.BlockSpec((None, KV_HEADS, None, Q_PER_KV * QBLOCK, HEAD_DIM), lambda b, i: (b, 0, i, 0, 0)),
            pl.BlockSpec((None, KV_HEADS, nchunks, HEAD_DIM, KEY_CHUNK), lambda b, i: (b, 0, 0, 0, 0)),
            pl.BlockSpec((None, KV_HEADS, nchunks, KEY_CHUNK, HEAD_DIM), lambda b, i: (b, 0, 0, 0, 0)),
        ],
        out_specs=pl.BlockSpec((None, QBLOCK, GROUP_WIDTH), lambda b, i: (b, i, 0)),
        scratch_shapes=[pltpu.VMEM((nchunks, QBLOCK, KEY_CHUNK), jnp.int32),
                        pltpu.VMEM((nchunks, QBLOCK, KEY_CHUNK), f32)],
        compiler_params=pltpu.CompilerParams(
            dimension_semantics=("parallel", "arbitrary"), vmem_limit_bytes=VMEM_LIMIT),
        name="dsa_prompt",
    )(iq, iw, ikt, qs, kt, v)


def _nsa_compress_kernel(x_ref, pe_ref, w1_ref, w2_ref, o_ref, hid_ref):
    k = pl.program_id(0)

    @pl.when(k == 0)
    def _():
        hid_ref[...] = jnp.zeros_like(hid_ref)

    hid_ref[...] += jnp.dot((x_ref[...] + pe_ref[...]).astype(bf16), w1_ref[...],
                            preferred_element_type=f32)

    @pl.when(k == pl.num_programs(0) - 1)
    def _():
        hid = jax.nn.gelu(hid_ref[...]).astype(bf16)
        o_ref[...] = jnp.dot(hid, w2_ref[...], preferred_element_type=f32)


def nsa_compress_weights(cmp_pe, cmp_w1, cmp_w2):
    w1 = cmp_w1.reshape(2, CMP_BLOCK, HEAD_DIM, CMP_HIDDEN)
    eye_g = jnp.eye(KV_HEADS, dtype=cmp_w1.dtype)
    eye_j = jnp.eye(2, dtype=cmp_w1.dtype)
    w1_big = jnp.einsum('jtde,gh,jk->tgjdhke', w1, eye_g, eye_j)
    w1_big = w1_big.reshape(CMP_BLOCK * KV_ROW, KV_HEADS * 2 * CMP_HIDDEN).astype(bf16)
    w2_big = jnp.einsum('jed,gh,jk->gjehkd', cmp_w2, eye_g, eye_j)
    w2_big = w2_big.reshape(KV_HEADS * 2 * CMP_HIDDEN, KV_ROW).astype(bf16)
    pe_row = jnp.broadcast_to(cmp_pe[:, None], (CMP_BLOCK, KV_HEADS, 2, HEAD_DIM))
    return pe_row.reshape(1, CMP_BLOCK * KV_ROW), w1_big, w2_big


def nsa_compress_rows(x, pe_row, w1_big, w2_big, tk=2048):
    m, kdim = x.shape
    hw = w1_big.shape[1]
    return pl.pallas_call(
        _nsa_compress_kernel,
        out_shape=jax.ShapeDtypeStruct((m, KV_ROW), f32),
        grid=(kdim // tk,),
        in_specs=[pl.BlockSpec((m, tk), lambda k: (0, k)),
                  pl.BlockSpec((1, tk), lambda k: (0, k)),
                  pl.BlockSpec((tk, hw), lambda k: (k, 0)),
                  pl.BlockSpec((hw, KV_ROW), lambda k: (0, 0))],
        out_specs=pl.BlockSpec((m, KV_ROW), lambda k: (0, 0)),
        scratch_shapes=[pltpu.VMEM((m, hw), f32)],
        compiler_params=pltpu.CompilerParams(
            dimension_semantics=("arbitrary",), vmem_limit_bytes=VMEM_LIMIT),
        name="nsa_compress",
    )(x, pe_row, w1_big, w2_big)


def _nsa_prompt_kernel(gate_ref, q_ref, ckt_ref, cv_ref, skt_ref, sv_ref, wkt_ref, wv_ref, o_ref):
    i = pl.program_id(1)
    nch = (i * QBLOCK + QBLOCK + KEY_CHUNK - 1) // KEY_CHUNK
    q_pos = i * QBLOCK + lax.broadcasted_iota(jnp.int32, (QBLOCK, 1), 0)
    lane = lax.broadcasted_iota(jnp.int32, (QBLOCK, CMP_LANES), 1)
    lane_k = lax.broadcasted_iota(jnp.int32, (QBLOCK, KEY_CHUNK), 1)
    expand_row = lax.broadcasted_iota(jnp.int32, (CMP_LANES, KEY_CHUNK), 0)
    expand_col = lax.broadcasted_iota(jnp.int32, (CMP_LANES, KEY_CHUNK), 1)
    gate = jax.nn.sigmoid(gate_ref[...])
    heads_rows = Q_PER_KV * QBLOCK
    ratio = SLC_BLOCK // CMP_BLOCK
    on_block_lane = (lane & (ratio - 1)) == 0
    blk = jnp.right_shift(lane, 1)
    cur = jnp.right_shift(q_pos, 6)
    future = blk > cur
    forced = (blk == 0) | (blk == cur) | (blk == cur - 1)
    c_vis = _tile_rows(jnp.where((lane + 1) * CMP_BLOCK - 1 <= q_pos, 1.0, 0.0), Q_PER_KV) > 0.5

    for g in range(KV_HEADS):
        qg = q_ref[g].astype(bf16)
        s = jnp.dot(qg, ckt_ref[g], preferred_element_type=f32) * ATT_SCALE
        s = jnp.where(c_vis, s, -jnp.inf)
        m = jnp.max(s, axis=1, keepdims=True)
        m = jnp.where(m == -jnp.inf, 0.0, m)
        p = jnp.exp(s - m)
        p = p / jnp.maximum(jnp.sum(p, axis=1, keepdims=True), F32_TINY)
        o_cmp = jnp.dot(p.astype(bf16), cv_ref[g], preferred_element_type=f32)
        imp = p[0:QBLOCK]
        for r in range(1, Q_PER_KV):
            imp = imp + p[r * QBLOCK:(r + 1) * QBLOCK]
        imp = imp + pltpu.roll(imp, CMP_LANES - 1, 1)
        score = jnp.where(future, -jnp.inf, jnp.where(forced, jnp.inf, imp))
        score = jnp.where(on_block_lane, score, -jnp.inf)
        rank = jnp.zeros((QBLOCK, CMP_LANES), f32)
        for b in range(CMP_LANES // ratio):
            col = score[:, ratio * b:ratio * b + 1]
            ahead = (col > score) | ((col == score) & (ratio * b < lane))
            rank = rank + jnp.where(ahead, 1.0, 0.0)
        sel_blk = jnp.where((rank < float(SLC_TOPN)) & on_block_lane, 1.0, 0.0).astype(bf16)

        def slc_mask(c):
            kidx = c * KEY_CHUNK + lane_k
            expand = jnp.where(
                expand_row == ratio * jnp.right_shift(c * KEY_CHUNK + expand_col, 6), 1.0, 0.0).astype(bf16)
            picked = jnp.dot(sel_blk, expand, preferred_element_type=f32)
            return jnp.where((picked > 0.5) & (kidx <= q_pos), 1.0, 0.0)

        o_slc = _masked_flash(qg, lambda c: skt_ref[g, c], lambda c: sv_ref[g, c], 0, nch, slc_mask)

        def win_mask(c):
            kidx = c * QBLOCK + lane
            return jnp.where((kidx <= q_pos) & (kidx > q_pos - WINDOW), 1.0, 0.0)

        o_win = _masked_flash(qg, lambda c: wkt_ref[g, c], lambda c: wv_ref[g, c],
                              jnp.maximum(i - WINDOW // QBLOCK, 0), i + 1, win_mask)
        for r in range(Q_PER_KV):
            h = g * Q_PER_KV + r
            rows = slice(r * QBLOCK, (r + 1) * QBLOCK)
            o_ref[:, h * HEAD_DIM:(h + 1) * HEAD_DIM] = (
                gate[:, 3 * h:3 * h + 1] * o_cmp[rows]
                + gate[:, 3 * h + 1:3 * h + 2] * o_slc[rows]
                + gate[:, 3 * h + 2:3 * h + 3] * o_win[rows])


def nsa_prompt(nq, ng, ckv, skv, wkv, cmp_weights):
    n, t = nq.shape[:2]
    nc = t // CMP_BLOCK
    assert nc <= CMP_LANES and t % KEY_CHUNK == 0 and SLC_BLOCK == 64 and SLC_BLOCK // CMP_BLOCK == 2
    cmp_rows = nsa_compress_rows(ckv.reshape(n * nc, CMP_BLOCK * KV_ROW), *cmp_weights)
    cmp_rows = cmp_rows.reshape(n, nc, KV_HEADS, 2, HEAD_DIM)
    cmp_rows = jnp.pad(cmp_rows, ((0, 0), (0, CMP_LANES - nc), (0, 0), (0, 0), (0, 0))).astype(bf16)
    ckt = cmp_rows[:, :, :, 0, :].transpose(0, 2, 3, 1)
    cv = cmp_rows[:, :, :, 1, :].transpose(0, 2, 1, 3)
    qs = _stack_query_heads(nq, n, t)
    skt, sv = _chunked_kv(skv, n, t, KEY_CHUNK)
    wkt, wv = _chunked_kv(wkv, n, t, QBLOCK)
    nk, nw = t // KEY_CHUNK, t // QBLOCK
    return pl.pallas_call(
        _nsa_prompt_kernel,
        out_shape=jax.ShapeDtypeStruct((n, t, GROUP_WIDTH), f32),
        grid=(n, t // QBLOCK),
        in_specs=[
            pl.BlockSpec((None, QBLOCK, 3 * N_HEADS), lambda b, i: (b, i, 0)),
            pl.BlockSpec((None, KV_HEADS, None, Q_PER_KV * QBLOCK, HEAD_DIM), lambda b, i: (b, 0, i, 0, 0)),
            pl.BlockSpec((None, KV_HEADS, HEAD_DIM, CMP_LANES), lambda b, i: (b, 0, 0, 0)),
            pl.BlockSpec((None, KV_HEADS, CMP_LANES, HEAD_DIM), lambda b, i: (b, 0, 0, 0)),
            pl.BlockSpec((None, KV_HEADS, nk, HEAD_DIM, KEY_CHUNK), lambda b, i: (b, 0, 0, 0, 0)),
            pl.BlockSpec((None, KV_HEADS, nk, KEY_CHUNK, HEAD_DIM), lambda b, i: (b, 0, 0, 0, 0)),
            pl.BlockSpec((None, KV_HEADS, nw, HEAD_DIM, QBLOCK), lambda b, i: (b, 0, 0, 0, 0)),
            pl.BlockSpec((None, KV_HEADS, nw, QBLOCK, HEAD_DIM), lambda b, i: (b, 0, 0, 0, 0)),
        ],
        out_specs=pl.BlockSpec((None, QBLOCK, GROUP_WIDTH), lambda b, i: (b, i, 0)),
        compiler_params=pltpu.CompilerParams(
            dimension_semantics=("parallel", "arbitrary"), vmem_limit_bytes=VMEM_LIMIT),
        name="nsa_prompt",
    )(ng, qs, ckt, cv, skt, sv, wkt, wv)


def s5_discretize(w):
    a_re, a_im = w['a_re'], w['a_im']
    dt = jnp.exp(w['log_dt'])[:, None]
    mag = jnp.exp(a_re * dt)
    ab_re, ab_im = mag * jnp.cos(a_im * dt), mag * jnp.sin(a_im * dt)
    den = a_re * a_re + a_im * a_im
    nr, ni = ab_re - 1.0, ab_im
    f_re = (nr * a_re + ni * a_im) / den
    f_im = (ni * a_re - nr * a_im) / den
    b_re, b_im = w['b_re'], w['b_im']
    bb_re = f_re[..., None] * b_re - f_im[..., None] * b_im
    bb_im = f_re[..., None] * b_im + f_im[..., None] * b_re
    eye = jnp.eye(S5_GROUPS, dtype=f32)

    def in_map(bb):
        return jnp.einsum('gsc,gh->gchs', bb, eye).reshape(GROUP_WIDTH, S5_WIDTH).astype(bf16)

    def out_map(cc):
        return jnp.einsum('gcs,gh->gshc', cc, eye).reshape(S5_WIDTH, GROUP_WIDTH).astype(bf16)

    return dict(a_re=ab_re.reshape(1, S5_WIDTH), a_im=ab_im.reshape(1, S5_WIDTH),
                b_re=in_map(bb_re), b_im=in_map(bb_im),
                c_re=out_map(w['c_re']), c_im=out_map(w['c_im']),
                d=w['d'].reshape(1, GROUP_WIDTH), w_glu=w['w_glu'].astype(bf16),
                b_glu=w['b_glu'].reshape(1, GROUP_WIDTH))


def _s5_prompt_kernel(u_ref, h0_ref, are_ref, aim_ref, bre_ref, bim_ref, cre_ref, cim_ref, d_ref,
                      wglu_ref, bglu_ref, o_ref, hlast_ref, state_ref, bure_ref, buim_ref, hre_ref, him_ref):
    j = pl.program_id(1)

    @pl.when(j == 0)
    def _():
        state_ref[...] = h0_ref[...]

    u = u_ref[...]
    ub = u.astype(bf16)
    bure_ref[...] = jnp.dot(ub, bre_ref[...], preferred_element_type=f32)
    buim_ref[...] = jnp.dot(ub, bim_ref[...], preferred_element_type=f32)
    a_re, a_im = are_ref[...], aim_ref[...]

    def step(t, carry):
        h_re, h_im = carry
        row = pl.ds(t, 1)
        n_re = a_re * h_re - a_im * h_im + bure_ref[row, :]
        n_im = a_re * h_im + a_im * h_re + buim_ref[row, :]
        hre_ref[row, :] = n_re
        him_ref[row, :] = n_im
        return n_re, n_im

    h_re, h_im = lax.fori_loop(0, u.shape[0], step, (state_ref[0:1, :], state_ref[1:2, :]), unroll=8)
    state_ref[0:1, :] = h_re
    state_ref[1:2, :] = h_im
    hlast_ref[...] = state_ref[...]
    y = (jnp.dot(hre_ref[...].astype(bf16), cre_ref[...], preferred_element_type=f32)
         - jnp.dot(him_ref[...].astype(bf16), cim_ref[...], preferred_element_type=f32)
         + d_ref[...] * u)
    y = jax.nn.gelu(y)
    o_ref[...] = y * jax.nn.sigmoid(
        jnp.dot(y.astype(bf16), wglu_ref[...], preferred_element_type=f32) + bglu_ref[...])


def s5_prompt(u, h0, sw):
    n, t = u.shape[:2]
    tc = min(S5_TIME_CHUNK, t)
    const = lambda shape: pl.BlockSpec(shape, lambda b, j: (0,) * len(shape))
    return pl.pallas_call(
        _s5_prompt_kernel,
        out_shape=(jax.ShapeDtypeStruct((n, t, GROUP_WIDTH), f32),
                   jax.ShapeDtypeStruct((n, 2, S5_WIDTH), f32)),
        grid=(n, t // tc),
        in_specs=[pl.BlockSpec((None, tc, GROUP_WIDTH), lambda b, j: (b, j, 0)),
                  pl.BlockSpec((None, 2, S5_WIDTH), lambda b, j: (b, 0, 0)),
                  const((1, S5_WIDTH)), const((1, S5_WIDTH)),
                  const((GROUP_WIDTH, S5_WIDTH)), const((GROUP_WIDTH, S5_WIDTH)),
                  const((S5_WIDTH, GROUP_WIDTH)), const((S5_WIDTH, GROUP_WIDTH)),
                  const((1, GROUP_WIDTH)), const((GROUP_WIDTH, GROUP_WIDTH)), const((1, GROUP_WIDTH))],
        out_specs=(pl.BlockSpec((None, tc, GROUP_WIDTH), lambda b, j: (b, j, 0)),
                   pl.BlockSpec((None, 2, S5_WIDTH), lambda b, j: (b, 0, 0))),
        scratch_shapes=[pltpu.VMEM((2, S5_WIDTH), f32)] + [pltpu.VMEM((tc, S5_WIDTH), f32)] * 4,
        compiler_params=pltpu.CompilerParams(
            dimension_semantics=("parallel", "arbitrary"), vmem_limit_bytes=VMEM_LIMIT),
        name="s5_prompt",
    )(u, h0, sw['a_re'], sw['a_im'], sw['b_re'], sw['b_im'], sw['c_re'], sw['c_im'], sw['d'],
      sw['w_glu'], sw['b_glu'])


def masked_softmax(s, mask):
    s = jnp.where(mask, s.astype(jnp.float32), -jnp.inf)
    m = jnp.max(s, axis=-1, keepdims=True)
    m = jnp.where(jnp.isfinite(m), m, 0.0)
    p = jnp.exp(s - m)
    return p / jnp.maximum(p.sum(-1, keepdims=True), jnp.finfo(jnp.float32).tiny)


def gather_seq(rows, pos, *extra):
    n_idx = jnp.arange(rows.shape[0]).reshape((-1,) + (1,) * (pos.ndim - 1))
    return rows[(n_idx, pos) + tuple(extra)]


def gather_pages(pool, page_table):
    g = pool[page_table]
    return g.reshape((g.shape[0], -1) + g.shape[3:])


def gather_paged_rows(pool, page_table, pos, *extra):
    n_idx = jnp.arange(page_table.shape[0]).reshape((-1,) + (1,) * (pos.ndim - 1))
    phys = page_table[n_idx, pos // PAGE_SIZE]
    return pool[(phys, pos % PAGE_SIZE) + tuple(extra)]


def gather_past_and_new(pool, page_table, new_rows, pos, *extra):
    past = gather_paged_rows(pool, page_table, jnp.minimum(pos, PAST_LEN - 1), *extra)
    new = gather_seq(new_rows, jnp.clip(pos - PAST_LEN, 0, new_rows.shape[1] - 1), *extra)
    in_past = (pos < PAST_LEN).reshape(pos.shape + (1,) * (past.ndim - pos.ndim))
    return jnp.where(in_past, past, new)


def split_projection(cols, n, t):
    (dq, dkv, iq, ik, iw, u, nq, ckv, skv, wkv, ng, gq, gk, gv, ga, gr) = cols
    kv_shape = (n, t, KV_HEADS, 2, HEAD_DIM)
    return {
        'dsa_q': dq.reshape(n, t, N_HEADS, HEAD_DIM),
        'dsa_kv': dkv.reshape(kv_shape),
        'idx_q': iq.reshape(n, t, IDX_HEADS, IDX_DIM),
        'idx_k': ik,
        'idx_w': iw,
        's5_u': u,
        'nsa_q': nq.reshape(n, t, N_HEADS, HEAD_DIM),
        'nsa_cmp': ckv.reshape(kv_shape),
        'nsa_slc': skv.reshape(kv_shape),
        'nsa_win': wkv.reshape(kv_shape),
        'nsa_gate': jax.nn.sigmoid(ng).reshape(n, t, N_HEADS, 3),
        'gla_q': gq.reshape(n, t, GLA_HEADS, GLA_DK),
        'gla_k': gk.reshape(n, t, GLA_HEADS, GLA_DK),
        'gla_v': gv.reshape(n, t, GLA_HEADS, GLA_DV),
        'gla_a': ga,
        'gla_r': gr,
    }


def indexer_scores(iq, iw, ik):
    dots = jnp.einsum('nqhd,nld->nqhl', iq, ik) * IDX_DIM ** -0.5
    return jnp.einsum('nqh,nqhl->nql', iw * IDX_HEADS ** -0.5, jax.nn.relu(dots)).astype(jnp.float32)


def dsa_select(scores, q_pos, n_keys):
    causal = jnp.arange(n_keys)[None, :] <= q_pos[:, None]
    k_sel = min(DSA_TOPK, n_keys // 4)
    _, sel = lax.top_k(jnp.where(causal, scores, -jnp.inf), k_sel)
    return sel, sel <= q_pos[None, :, None]


def dsa_attend(q, kv_sel, valid):
    n, nq = q.shape[:2]
    qg = q.reshape(n, nq, KV_HEADS, Q_PER_KV, HEAD_DIM)
    s = jnp.einsum('nqgrd,nqkgd->nqgrk', qg, kv_sel[..., 0, :]) * HEAD_DIM ** -0.5
    p = masked_softmax(s, valid[:, :, None, None, :])
    o = jnp.einsum('nqgrk,nqkgd->nqgrd', p, kv_sel[..., 1, :])
    return o.reshape(n, nq, GROUP_WIDTH)


def dsa_sample(p, pool_kv, pool_idx, page_table):
    s_new = p['dsa_q'].shape[1]
    n_keys = PAST_LEN + s_new
    ik_all = jnp.concatenate([gather_pages(pool_idx, page_table), p['idx_k']], axis=1)
    q_pos = PAST_LEN + jnp.arange(s_new)
    sel, valid = dsa_select(indexer_scores(p['idx_q'], p['idx_w'], ik_all), q_pos, n_keys)
    kv_sel = gather_past_and_new(pool_kv, page_table, p['dsa_kv'], sel)
    return dsa_attend(p['dsa_q'], kv_sel, valid)


def complex_linear_combine(e1, e2):
    a1r, a1i, b1r, b1i = e1
    a2r, a2i, b2r, b2i = e2
    return (a2r * a1r - a2i * a1i,
            a2r * a1i + a2i * a1r,
            a2r * b1r - a2i * b1i + b2r,
            a2r * b1i + a2i * b1r + b2i)


def s5_mixer(u, h0, w):
    n, t = u.shape[:2]
    uf = u.reshape(n, t, S5_GROUPS, S5_CH)
    a_re, a_im = w['a_re'], w['a_im']
    dt = jnp.exp(w['log_dt'])[:, None]
    mag = jnp.exp(a_re * dt)
    ab_re, ab_im = mag * jnp.cos(a_im * dt), mag * jnp.sin(a_im * dt)
    den = a_re * a_re + a_im * a_im
    nr, ni = ab_re - 1.0, ab_im
    f_re = (nr * a_re + ni * a_im) / den
    f_im = (ni * a_re - nr * a_im) / den
    b_re, b_im = w['b_re'], w['b_im']
    bb_re = f_re[..., None] * b_re - f_im[..., None] * b_im
    bb_im = f_re[..., None] * b_im + f_im[..., None] * b_re
    bu_re = jnp.einsum('gsc,btgc->btgs', bb_re, uf)
    bu_im = jnp.einsum('gsc,btgc->btgs', bb_im, uf)
    cum_re, cum_im, sc_re, sc_im = lax.associative_scan(
        complex_linear_combine,
        (jnp.broadcast_to(ab_re, bu_re.shape), jnp.broadcast_to(ab_im, bu_im.shape), bu_re, bu_im),
        axis=1)
    h0r, h0i = h0[:, 0][:, None], h0[:, 1][:, None]
    h_re = cum_re * h0r - cum_im * h0i + sc_re
    h_im = cum_re * h0i + cum_im * h0r + sc_im
    y = (jnp.einsum('gcs,btgs->btgc', w['c_re'], h_re)
         - jnp.einsum('gcs,btgs->btgc', w['c_im'], h_im)
         + w['d'] * uf)
    y = jax.nn.gelu(y.reshape(n, t, GROUP_WIDTH))
    y = y * jax.nn.sigmoid(y @ w['w_glu'] + w['b_glu'])
    h_last = jnp.stack([h_re[:, -1], h_im[:, -1]], axis=1)
    return y, h_last


def nsa_compress(rows, w):
    n, length = rows.shape[:2]
    nc = length // CMP_BLOCK
    blk = rows[:, :nc * CMP_BLOCK].reshape(n, nc, CMP_BLOCK, KV_HEADS, 2, HEAD_DIM)
    blk = blk + w['cmp_pe'][:, None]
    flat = jnp.moveaxis(blk, 2, 4).reshape(n, nc, KV_HEADS, 2, CMP_BLOCK * HEAD_DIM)
    hid = jax.nn.gelu(jnp.einsum('ncgjf,jfe->ncgje', flat, w['cmp_w1']))
    return jnp.einsum('ncgje,jed->ncgjd', hid, w['cmp_w2'])


def nsa_core(q, gate, q_pos, cmp_kv, n_slc_blocks, slc_gather, win_kv, win_pos):
    n, nq = q.shape[:2]
    qg = q.reshape(n, nq, KV_HEADS, Q_PER_KV, HEAD_DIM)
    scale = HEAD_DIM ** -0.5
    nc = cmp_kv.shape[1]
    c_vis = (jnp.arange(nc) + 1) * CMP_BLOCK - 1 <= q_pos[:, None]
    s_c = jnp.einsum('nqgrd,ncgd->nqgrc', qg, cmp_kv[..., 0, :]) * scale
    p_c = masked_softmax(s_c, c_vis[None, :, None, None, :])
    o_cmp = jnp.einsum('nqgrc,ncgd->nqgrd', p_c, cmp_kv[..., 1, :])
    ratio = SLC_BLOCK // CMP_BLOCK
    imp = p_c.sum(3)
    imp = jnp.pad(imp, ((0, 0), (0, 0), (0, 0), (0, n_slc_blocks * ratio - nc)))
    imp = imp.reshape(n, nq, KV_HEADS, n_slc_blocks, ratio).sum(-1)
    blk = jnp.arange(n_slc_blocks)[None, :]
    cur = (q_pos // SLC_BLOCK)[:, None]
    forced = (blk == 0) | (blk == cur) | (blk == cur - 1)
    future = blk > cur
    score = jnp.where(future[None, :, None, :], -jnp.inf,
                      jnp.where(forced[None, :, None, :], jnp.inf, imp))
    n_sel = min(SLC_TOPN, n_slc_blocks)
    _, sel = lax.top_k(score, n_sel)
    pos = (sel[..., None] * SLC_BLOCK + jnp.arange(SLC_BLOCK)).reshape(n, nq, KV_HEADS, n_sel * SLC_BLOCK)
    valid = pos <= q_pos[None, :, None, None]
    kv_s = slc_gather(pos)
    s_s = jnp.einsum('nqgrd,nqgkd->nqgrk', qg, kv_s[..., 0, :]) * scale
    p_s = masked_softmax(s_s, valid[:, :, :, None, :])
    o_slc = jnp.einsum('nqgrk,nqgkd->nqgrd', p_s, kv_s[..., 1, :])
    w_vis = ((win_pos[None, :] <= q_pos[:, None]) & (win_pos[None, :] > q_pos[:, None] - WINDOW)
             & (win_pos[None, :] >= 0))
    s_w = jnp.einsum('nqgrd,nwgd->nqgrw', qg, win_kv[..., 0, :]) * scale
    p_w = masked_softmax(s_w, w_vis[None, :, None, None, :])
    o_win = jnp.einsum('nqgrw,nwgd->nqgrd', p_w, win_kv[..., 1, :])
    g = gate.reshape(n, nq, KV_HEADS, Q_PER_KV, 3)
    o = g[..., 0:1] * o_cmp + g[..., 1:2] * o_slc + g[..., 2:3] * o_win
    return o.reshape(n, nq, GROUP_WIDTH)


def nsa_sample(p, w, pool_cmp, pool_slc, win_buf, page_table):
    s_new = p['nsa_q'].shape[1]
    n_keys = PAST_LEN + s_new
    cmp_all = jnp.concatenate([gather_pages(pool_cmp, page_table), p['nsa_cmp']], axis=1)
    cmp_kv = nsa_compress(cmp_all, w)
    wbuf = win_buf.shape[1]
    win_all = jnp.concatenate([win_buf, p['nsa_win']], axis=1)
    win_pos = PAST_LEN - wbuf + jnp.arange(wbuf + s_new)
    q_pos = PAST_LEN + jnp.arange(s_new)
    g_idx = jnp.arange(KV_HEADS)[None, None, :, None]
    out = nsa_core(p['nsa_q'], p['nsa_gate'], q_pos, cmp_kv, -(-n_keys // SLC_BLOCK),
                   lambda pos: gather_past_and_new(pool_slc, page_table, p['nsa_slc'], pos, g_idx),
                   win_all, win_pos)
    return out, win_all[:, -wbuf:]


def gla_recurrence(q, k, v, log_a, s0):
    n, t = q.shape[:2]
    c = min(GLA_CHUNK, t)
    tp = -(-t // c) * c

    def chunks(a):
        a = jnp.pad(a, ((0, 0), (0, tp - t)) + ((0, 0),) * (a.ndim - 2))
        return jnp.moveaxis(a.reshape((n, tp // c, c) + a.shape[2:]), 1, 0)

    tril = jnp.tril(jnp.ones((c, c), dtype=bool))[None, :, :, None, None]

    def step(s, inp):
        qc, kc, vc, gc = inp
        b = jnp.cumsum(gc, axis=1)
        o_inter = jnp.einsum('nchk,nhkv->nchv', qc * jnp.exp(b), s)
        decay = jnp.exp(jnp.where(tril, b[:, :, None] - b[:, None, :], -jnp.inf))
        att = jnp.einsum('nthk,nshk,ntshk->nths', qc, kc, decay)
        o_intra = jnp.einsum('nths,nshv->nthv', att, vc)
        b_end = b[:, -1]
        s_new = (jnp.exp(b_end)[..., None] * s
                 + jnp.einsum('nshk,nshv->nhkv', kc * jnp.exp(b_end[:, None] - b), vc))
        return s_new, o_inter + o_intra

    s_last, o = lax.scan(step, s0, (chunks(q), chunks(k), chunks(v), chunks(log_a)))
    o = jnp.moveaxis(o, 0, 1).reshape((n, tp) + o.shape[3:])[:, :t]
    return o, s_last


def gla_mixer(p, s0, w):
    n, t = p['gla_q'].shape[:2]
    q = p['gla_q'] * GLA_DK ** -0.5
    k = p['gla_k']
    v = p['gla_v']
    z = p['gla_a'] @ w['gla_w_gate'] + w['gla_b_gate']
    log_a = (jax.nn.log_sigmoid(z) / GLA_TAU).reshape(n, t, GLA_HEADS, GLA_DK)
    o, s_last = gla_recurrence(q, k, v, log_a, s0)
    o = o * lax.rsqrt(jnp.mean(o * o, -1, keepdims=True) + LN_EPS) * w['gla_norm_g']
    o = o.reshape(n, t, GROUP_WIDTH) * jax.nn.silu(p['gla_r'])
    return o, s_last


def split_columns(proj, n, t):
    offsets = np.cumsum(IN_SIZES)[:-1].tolist()
    return jnp.split(proj[:, :D_IN].reshape(n, t, D_IN), offsets, axis=-1)


def mix_prompt(cols, w, sw, cmp_weights):
    (dq, dkv, iq, ik, iw, u, nq, ckv, skv, wkv, ng, gq, gk, gv, ga, gr) = cols
    n, t = dq.shape[:2]
    kv_shape = (n, t, KV_HEADS, 2, HEAD_DIM)
    o_dsa = dsa_prompt(dq, dkv, iq, ik, iw)
    o_s5, h_s5 = s5_prompt(u, jnp.zeros((n, 2, S5_WIDTH), f32), sw)
    o_nsa = nsa_prompt(nq, ng, ckv, skv, wkv, cmp_weights)
    p = {'gla_q': gq.reshape(n, t, GLA_HEADS, GLA_DK), 'gla_k': gk.reshape(n, t, GLA_HEADS, GLA_DK),
         'gla_v': gv.reshape(n, t, GLA_HEADS, GLA_DV), 'gla_a': ga, 'gla_r': gr}
    o_gla, s_gla = gla_mixer(p, jnp.zeros((n, GLA_HEADS, GLA_DK, GLA_DV), f32), w)
    win_state = wkv[:, -min(WINDOW, t):].reshape((n, min(WINDOW, t)) + kv_shape[2:])
    mixers = [o.reshape(n * t, GROUP_WIDTH) for o in (o_dsa, o_s5, o_nsa, o_gla)]
    return mixers, (dkv.reshape(kv_shape), ik, ckv.reshape(kv_shape), skv.reshape(kv_shape), win_state,
                    h_s5.reshape(n, 2, S5_GROUPS, S5_STATE), s_gla)


def mix_sample(cols, w, pool_dsa_kv, pool_dsa_idx, pool_nsa_cmp, pool_nsa_slc, win_buf, h_s5, s_gla, page_table):
    n, t = cols[0].shape[:2]
    p = split_projection(cols, n, t)
    o_dsa = dsa_sample(p, pool_dsa_kv, pool_dsa_idx, page_table)
    o_s5, h_new = s5_mixer(p['s5_u'], h_s5, w)
    o_nsa, win_new = nsa_sample(p, w, pool_nsa_cmp, pool_nsa_slc, win_buf, page_table)
    o_gla, s_new = gla_mixer(p, s_gla, w)
    mixers = [o.reshape(n * t, GROUP_WIDTH) for o in (o_dsa, o_s5, o_nsa, o_gla)]
    return mixers, (p['dsa_kv'], p['idx_k'], p['nsa_cmp'], p['nsa_slc'], win_new, h_new, s_new)


def kernel(x_prompt, x_sample, cache_dsa_kv, cache_dsa_idx, cache_nsa_cmp, cache_nsa_slc, cache_nsa_win, state_s5, state_gla, page_table, w_in, s5_a_re, s5_a_im, s5_b_re, s5_b_im, s5_c_re, s5_c_im, s5_d, s5_log_dt, s5_w_glu, s5_b_glu, nsa_cmp_pe, nsa_cmp_w1, nsa_cmp_w2, gla_w_gate, gla_b_gate, gla_norm_g, w_out, ln1_g, ln1_b, ffn_w_gate, ffn_w_up, ffn_w_down, ln2_g, ln2_b):
    np_, tp_ = x_prompt.shape[:2]
    ns_, ts_ = x_sample.shape[:2]
    hp = x_prompt.reshape(np_ * tp_, D_MODEL)
    hs = x_sample.reshape(ns_ * ts_, D_MODEL)
    outs_p = [[] for _ in range(7)]
    outs_s = [[] for _ in range(7)]
    for l in range(DEPTH):
        w = {'a_re': s5_a_re[l], 'a_im': s5_a_im[l], 'b_re': s5_b_re[l], 'b_im': s5_b_im[l],
             'c_re': s5_c_re[l], 'c_im': s5_c_im[l], 'd': s5_d[l], 'log_dt': s5_log_dt[l],
             'w_glu': s5_w_glu[l], 'b_glu': s5_b_glu[l],
             'cmp_pe': nsa_cmp_pe[l], 'cmp_w1': nsa_cmp_w1[l], 'cmp_w2': nsa_cmp_w2[l],
             'gla_w_gate': gla_w_gate[l], 'gla_b_gate': gla_b_gate[l], 'gla_norm_g': gla_norm_g[l]}
        sw = s5_discretize(w)
        cmp_weights = nsa_compress_weights(nsa_cmp_pe[l], nsa_cmp_w1[l], nsa_cmp_w2[l])
        w_in_l = jnp.pad(w_in[l], ((0, 0), (0, D_IN_PAD - D_IN))).astype(bf16)
        w_out_l = w_out[l].astype(bf16)
        wg_l, wu_l, wd_l = ffn_w_gate[l].astype(bf16), ffn_w_up[l].astype(bf16), ffn_w_down[l].astype(bf16)
        cols_p = split_columns(dense_matmul(hp, w_in_l), np_, tp_)
        cols_s = split_columns(dense_matmul(hs, w_in_l), ns_, ts_)
        mixers_p, st_p = mix_prompt(cols_p, w, sw, cmp_weights)
        mixers_s, st_s = mix_sample(cols_s, w, cache_dsa_kv[l], cache_dsa_idx[l], cache_nsa_cmp[l],
                                    cache_nsa_slc[l], cache_nsa_win[l], state_s5[l], state_gla[l], page_table)
        hp = outproj_ln(hp, mixers_p, w_out_l, ln1_g[l], ln1_b[l])
        hs = outproj_ln(hs, mixers_s, w_out_l, ln1_g[l], ln1_b[l])
        hp = ffn_ln(hp, wg_l, wu_l, wd_l, ln2_g[l], ln2_b[l])
        hs = ffn_ln(hs, wg_l, wu_l, wd_l, ln2_g[l], ln2_b[l])
        for lst, st in zip(outs_p, st_p):
            lst.append(st)
        for lst, st in zip(outs_s, st_s):
            lst.append(st)
    dsa_kv_p, dsa_idx_p, nsa_cmp_p, nsa_slc_p, nsa_win_p, s5_p, gla_p = [jnp.stack(a) for a in outs_p]
    dsa_kv_s, dsa_idx_s, nsa_cmp_s, nsa_slc_s, nsa_win_s, s5_s, gla_s = [jnp.stack(a) for a in outs_s]
    return (hp.reshape(np_, tp_, D_MODEL), hs.reshape(ns_, ts_, D_MODEL),
            dsa_kv_p, dsa_kv_s, dsa_idx_p, dsa_idx_s, nsa_cmp_p, nsa_cmp_s,
            nsa_slc_p, nsa_slc_s, nsa_win_p, nsa_win_s, s5_p, s5_s, gla_p, gla_s)
```

```python
import functools
import math

import jax
import jax.numpy as jnp
from jax import lax
import numpy as np
from jax.experimental import pallas as pl
from jax.experimental.pallas import tpu as pltpu

D_MODEL = 2048
DEPTH = 2
PAST_LEN = 2048
PAGE_SIZE = 128
GROUP_WIDTH = 512
HEAD_DIM = 64
N_HEADS = 8
KV_HEADS = 2
Q_PER_KV = 4
KV_ROW = 256
IDX_HEADS = 4
IDX_DIM = 64
DSA_TOPK = 256
S5_CH = 16
S5_GROUPS = 32
S5_STATE = 64
S5_WIDTH = S5_GROUPS * S5_STATE
CMP_BLOCK = 32
CMP_HIDDEN = 128
SLC_BLOCK = 64
SLC_TOPN = 16
WINDOW = 512
QBLOCK = 128
GLA_HEADS = 4
GLA_DK = 64
GLA_DV = 128
GLA_LOWRANK = 16
GLA_TAU = 16.0
GLA_CHUNK = 64
FFN_HIDDEN = 5632
DEEPNORM_ALPHA = (2 * DEPTH) ** 0.25
LN_EPS = 1e-5

IN_SIZES = (512, 256, 256, 64, 4, 512, 512, 256, 256, 256, 24, 256, 256, 512, 16, 512)
D_IN = sum(IN_SIZES)
D_IN_PAD = 4608

VMEM_LIMIT = 56 * 1024 * 1024
LANES = 128
KEY_CHUNK = 512
CMP_LANES = 128
S5_TIME_CHUNK = 256
ATT_SCALE = HEAD_DIM ** -0.5
MASKED = -1e30
INT_MIN = -2 ** 31
F32_TINY = float(np.finfo(np.float32).tiny)

bf16 = jnp.bfloat16
f32 = jnp.float32


def _matmul_kernel(x_ref, w_ref, o_ref):
    o_ref[...] = jnp.dot(x_ref[...].astype(bf16), w_ref[...], preferred_element_type=f32)


def dense_matmul(x, w_bf16, tn=512):
    m, k = x.shape
    n = w_bf16.shape[1]
    tm = min(512, m)
    return pl.pallas_call(
        _matmul_kernel,
        out_shape=jax.ShapeDtypeStruct((m, n), f32),
        grid=(m // tm, n // tn),
        in_specs=[pl.BlockSpec((tm, k), lambda i, j: (i, 0)),
                  pl.BlockSpec((k, tn), lambda i, j: (0, j))],
        out_specs=pl.BlockSpec((tm, tn), lambda i, j: (i, j)),
        compiler_params=pltpu.CompilerParams(
            dimension_semantics=("parallel", "parallel"), vmem_limit_bytes=VMEM_LIMIT),
        name="dense_matmul",
    )(x, w_bf16)


def _layer_norm_rows(z, g, b):
    mu = jnp.mean(z, axis=-1, keepdims=True)
    zc = z - mu
    var = jnp.mean(zc * zc, axis=-1, keepdims=True)
    return zc * lax.rsqrt(var + LN_EPS) * g + b


def _outproj_ln_kernel(x_ref, m0_ref, m1_ref, m2_ref, m3_ref, w_ref, g_ref, b_ref, o_ref):
    y = DEEPNORM_ALPHA * x_ref[...]
    for j, m_ref in enumerate((m0_ref, m1_ref, m2_ref, m3_ref)):
        y = y + jnp.dot(m_ref[...].astype(bf16), w_ref[j * GROUP_WIDTH:(j + 1) * GROUP_WIDTH, :],
                        preferred_element_type=f32)
    o_ref[...] = _layer_norm_rows(y, g_ref[...], b_ref[...])


def outproj_ln(x, mixers, w_bf16, g, b):
    m, d = x.shape
    tm = min(256, m)
    mix_spec = pl.BlockSpec((tm, GROUP_WIDTH), lambda i: (i, 0))
    return pl.pallas_call(
        _outproj_ln_kernel,
        out_shape=jax.ShapeDtypeStruct((m, d), f32),
        grid=(m // tm,),
        in_specs=[pl.BlockSpec((tm, d), lambda i: (i, 0)),
                  mix_spec, mix_spec, mix_spec, mix_spec,
                  pl.BlockSpec((d, d), lambda i: (0, 0)),
                  pl.BlockSpec((1, d), lambda i: (0, 0)),
                  pl.BlockSpec((1, d), lambda i: (0, 0))],
        out_specs=pl.BlockSpec((tm, d), lambda i: (i, 0)),
        compiler_params=pltpu.CompilerParams(
            dimension_semantics=("parallel",), vmem_limit_bytes=VMEM_LIMIT),
        name="outproj_ln",
    )(x, *mixers, w_bf16, g.reshape(1, d), b.reshape(1, d))


def _ffn_ln_kernel(h_ref, wg_ref, wu_ref, wd_ref, g_ref, b_ref, o_ref, acc_ref):
    f = pl.program_id(1)

    @pl.when(f == 0)
    def _():
        acc_ref[...] = jnp.zeros_like(acc_ref)

    hb = h_ref[...].astype(bf16)
    a = jnp.dot(hb, wg_ref[...], preferred_element_type=f32)
    u = jnp.dot(hb, wu_ref[...], preferred_element_type=f32)
    act = (a * jax.nn.sigmoid(a) * u).astype(bf16)
    acc_ref[...] += jnp.dot(act, wd_ref[...], preferred_element_type=f32)

    @pl.when(f == pl.num_programs(1) - 1)
    def _():
        z = DEEPNORM_ALPHA * h_ref[...] + acc_ref[...]
        o_ref[...] = _layer_norm_rows(z, g_ref[...], b_ref[...])


def ffn_ln(h, wg, wu, wd, g, b, tf=512):
    m, d = h.shape
    fh = wg.shape[1]
    tm = min(512, m)
    return pl.pallas_call(
        _ffn_ln_kernel,
        out_shape=jax.ShapeDtypeStruct((m, d), f32),
        grid=(m // tm, fh // tf),
        in_specs=[pl.BlockSpec((tm, d), lambda i, f: (i, 0)),
                  pl.BlockSpec((d, tf), lambda i, f: (0, f)),
                  pl.BlockSpec((d, tf), lambda i, f: (0, f)),
                  pl.BlockSpec((tf, d), lambda i, f: (f, 0)),
                  pl.BlockSpec((1, d), lambda i, f: (0, 0)),
                  pl.BlockSpec((1, d), lambda i, f: (0, 0))],
        out_specs=pl.BlockSpec((tm, d), lambda i, f: (i, 0)),
        scratch_shapes=[pltpu.VMEM((tm, d), f32)],
        compiler_params=pltpu.CompilerParams(
            dimension_semantics=("parallel", "arbitrary"), vmem_limit_bytes=VMEM_LIMIT),
        name="ffn_ln",
    )(h, wg, wu, wd, g.reshape(1, d), b.reshape(1, d))


def _tile_rows(x, reps):
    return jnp.concatenate([x] * reps, axis=0)


def _masked_flash(qg, kt_at, v_at, c_lo, c_hi, mask_at):
    rows = qg.shape[0]

    def body(c, carry):
        m, l, acc = carry
        s = jnp.dot(qg, kt_at(c), preferred_element_type=f32) * ATT_SCALE
        keep = _tile_rows(mask_at(c), rows // QBLOCK) > 0.5
        s = jnp.where(keep, s, MASKED)
        m_new = jnp.maximum(m, jnp.max(s, axis=1, keepdims=True))
        p = jnp.where(keep, jnp.exp(s - m_new), 0.0)
        alpha = jnp.exp(m - m_new)
        l = alpha * l + jnp.sum(p, axis=1, keepdims=True)
        acc = alpha * acc + jnp.dot(p.astype(bf16), v_at(c), preferred_element_type=f32)
        return m_new, l, acc

    init = (jnp.full((rows, 1), MASKED, f32), jnp.zeros((rows, 1), f32),
            jnp.zeros((rows, HEAD_DIM), f32))
    _, l, acc = lax.fori_loop(c_lo, c_hi, body, init)
    return acc / l


def _sortable_key(x):
    bits = pltpu.bitcast(x, jnp.int32)
    return bits ^ (jnp.right_shift(bits, 31) & jnp.int32(0x7FFFFFFF))


def _topk_threshold(count, rows, topk, idx_bits):
    topk = float(topk)
    cnt_nonneg = count(lambda k, i: k >= 0)
    thr0 = jnp.where(cnt_nonneg >= topk, 0, INT_MIN).astype(jnp.int32)

    def bit_body(b, thr):
        cand = thr + jnp.left_shift(jnp.int32(1), 30 - b)
        return jnp.where(count(lambda k, i: k >= cand) >= topk, cand, thr)

    thr = lax.fori_loop(0, 31, bit_body, thr0)
    need = topk - count(lambda k, i: k > thr)

    def cut_body(b, cut):
        cand = cut + jnp.left_shift(jnp.int32(1), idx_bits - 1 - b)
        return jnp.where(count(lambda k, i: (k == thr) & (i < cand)) < need, cand, cut)

    cut = lax.fori_loop(0, idx_bits, cut_body, jnp.zeros((rows, 1), jnp.int32))
    return thr, cut


def _select_blocks(imp, q_pos, lane):
    ratio = SLC_BLOCK // CMP_BLOCK
    on_block_lane = (lane & (ratio - 1)) == 0
    blk = jnp.right_shift(lane, 1)
    cur = jnp.right_shift(q_pos, 6)
    future = blk > cur
    forced = (blk == 0) | (blk == cur) | (blk == cur - 1)
    imp = imp + pltpu.roll(imp, CMP_LANES - 1, 1)
    score = jnp.where(future, -jnp.inf, jnp.where(forced, jnp.inf, imp))
    score = jnp.where(on_block_lane, score, -jnp.inf)
    rank = jnp.zeros(imp.shape, f32)
    for b in range(CMP_LANES // ratio):
        col = score[:, ratio * b:ratio * b + 1]
        ahead = (col > score) | ((col == score) & (ratio * b < lane))
        rank = rank + jnp.where(ahead, 1.0, 0.0)
    return jnp.where((rank < float(SLC_TOPN)) & on_block_lane, 1.0, 0.0)


def _dsa_prompt_kernel(iq_ref, iw_ref, ikt_ref, q_ref, kt_ref, v_ref, o_ref, keys_ref, sel_ref,
                       *, idx_bits):
    i = pl.program_id(1)
    nch = (i * QBLOCK + QBLOCK + KEY_CHUNK - 1) // KEY_CHUNK
    q_pos = i * QBLOCK + lax.broadcasted_iota(jnp.int32, (QBLOCK, 1), 0)
    lane = lax.broadcasted_iota(jnp.int32, (QBLOCK, KEY_CHUNK), 1)

    iq = iq_ref[...]
    iq_stack = jnp.concatenate(
        [iq[:, h * IDX_DIM:(h + 1) * IDX_DIM] for h in range(IDX_HEADS)], axis=0).astype(bf16)
    iw = iw_ref[...] * IDX_HEADS ** -0.5

    def score_body(c, carry):
        dots = jnp.dot(iq_stack, ikt_ref[c], preferred_element_type=f32) * IDX_DIM ** -0.5
        acc = jnp.zeros((QBLOCK, KEY_CHUNK), f32)
        for h in range(IDX_HEADS):
            acc = acc + iw[:, h:h + 1] * jnp.maximum(dots[h * QBLOCK:(h + 1) * QBLOCK], 0.0)
        causal = c * KEY_CHUNK + lane <= q_pos
        keys_ref[c] = jnp.where(causal, _sortable_key(acc), INT_MIN)
        return carry

    lax.fori_loop(0, nch, score_body, 0)

    def count(pred):
        def body(c, acc):
            hit = jnp.where(pred(keys_ref[c], c * KEY_CHUNK + lane), 1.0, 0.0)
            part = hit[:, :LANES]
            for j in range(1, KEY_CHUNK // LANES):
                part = part + hit[:, j * LANES:(j + 1) * LANES]
            return acc + part
        acc = lax.fori_loop(0, nch, body, jnp.zeros((QBLOCK, LANES), f32))
        return jnp.sum(acc, axis=1, keepdims=True)

    thr, cut = _topk_threshold(count, QBLOCK, DSA_TOPK, idx_bits)

    def sel_body(c, carry):
        k = keys_ref[c]
        kidx = c * KEY_CHUNK + lane
        chosen = ((k > thr) | ((k == thr) & (kidx <= cut))) & (kidx <= q_pos)
        sel_ref[c] = jnp.where(chosen, 1.0, 0.0)
        return carry

    lax.fori_loop(0, nch, sel_body, 0)

    for g in range(KV_HEADS):
        o = _masked_flash(q_ref[g].astype(bf16), lambda c: kt_ref[g, c], lambda c: v_ref[g, c],
                          0, nch, lambda c: sel_ref[c])
        for r in range(Q_PER_KV):
            h = g * Q_PER_KV + r
            o_ref[:, h * HEAD_DIM:(h + 1) * HEAD_DIM] = o[r * QBLOCK:(r + 1) * QBLOCK]


def _stack_query_heads(q, n, t):
    q = q.reshape(n, t // QBLOCK, QBLOCK, KV_HEADS, Q_PER_KV, HEAD_DIM)
    q = q.transpose(0, 3, 1, 4, 2, 5)
    return q.reshape(n, KV_HEADS, t // QBLOCK, Q_PER_KV * QBLOCK, HEAD_DIM)


def _chunked_kv(kv, n, t, chunk):
    kv = kv.reshape(n, t // chunk, chunk, KV_HEADS, 2, HEAD_DIM).astype(bf16)
    kt = kv[:, :, :, :, 0, :].transpose(0, 3, 1, 4, 2)
    v = kv[:, :, :, :, 1, :].transpose(0, 3, 1, 2, 4)
    return kt, v


def dsa_prompt(dq, dkv, iq, ik, iw):
    n, t = dq.shape[:2]
    nchunks = t // KEY_CHUNK
    idx_bits = int(math.log2(t))
    assert 2 ** idx_bits == t and t % KEY_CHUNK == 0 and min(DSA_TOPK, t // 4) == DSA_TOPK
    ikt = ik.reshape(n, nchunks, KEY_CHUNK, IDX_DIM).transpose(0, 1, 3, 2).astype(bf16)
    qs = _stack_query_heads(dq, n, t)
    kt, v = _chunked_kv(dkv, n, t, KEY_CHUNK)
    return pl.pallas_call(
        functools.partial(_dsa_prompt_kernel, idx_bits=idx_bits),
        out_shape=jax.ShapeDtypeStruct((n, t, GROUP_WIDTH), f32),
        grid=(n, t // QBLOCK),
        in_specs=[
            pl.BlockSpec((None, QBLOCK, IDX_HEADS * IDX_DIM), lambda b, i: (b, i, 0)),
            pl.BlockSpec((None, QBLOCK, IDX_HEADS), lambda b, i: (b, i, 0)),
            pl.BlockSpec((None, nchunks, IDX_DIM, KEY_CHUNK), lambda b, i: (b, 0, 0, 0)),
            pl.BlockSpec((None, KV_HEADS, None, Q_PER_KV * QBLOCK, HEAD_DIM), lambda b, i: (b, 0, i, 0, 0)),
            pl.BlockSpec((None, KV_HEADS, nchunks, HEAD_DIM, KEY_CHUNK), lambda b, i: (b, 0, 0, 0, 0)),
            pl.BlockSpec((None, KV_HEADS, nchunks, KEY_CHUNK, HEAD_DIM), lambda b, i: (b, 0, 0, 0, 0)),
        ],
        out_specs=pl.BlockSpec((None, QBLOCK, GROUP_WIDTH), lambda b, i: (b, i, 0)),
        scratch_shapes=[pltpu.VMEM((nchunks, QBLOCK, KEY_CHUNK), jnp.int32),
                        pltpu.VMEM((nchunks, QBLOCK, KEY_CHUNK), f32)],
        compiler_params=pltpu.CompilerParams(
            dimension_semantics=("parallel", "arbitrary"), vmem_limit_bytes=VMEM_LIMIT),
        name="dsa_prompt",
    )(iq, iw, ikt, qs, kt, v)


def _nsa_compress_kernel(x_ref, pe_ref, w1_ref, w2_ref, o_ref, hid_ref):
    k = pl.program_id(0)

    @pl.when(k == 0)
    def _():
        hid_ref[...] = jnp.zeros_like(hid_ref)

    hid_ref[...] += jnp.dot((x_ref[...] + pe_ref[...]).astype(bf16), w1_ref[...],
                            preferred_element_type=f32)

    @pl.when(k == pl.num_programs(0) - 1)
    def _():
        hid = jax.nn.gelu(hid_ref[...]).astype(bf16)
        o_ref[...] = jnp.dot(hid, w2_ref[...], preferred_element_type=f32)


def nsa_compress_weights(cmp_pe, cmp_w1, cmp_w2):
    w1 = cmp_w1.reshape(2, CMP_BLOCK, HEAD_DIM, CMP_HIDDEN)
    eye_g = jnp.eye(KV_HEADS, dtype=cmp_w1.dtype)
    eye_j = jnp.eye(2, dtype=cmp_w1.dtype)
    w1_big = jnp.einsum('jtde,gh,jk->tgjdhke', w1, eye_g, eye_j)
    w1_big = w1_big.reshape(CMP_BLOCK * KV_ROW, KV_HEADS * 2 * CMP_HIDDEN).astype(bf16)
    w2_big = jnp.einsum('jed,gh,jk->gjehkd', cmp_w2, eye_g, eye_j)
    w2_big = w2_big.reshape(KV_HEADS * 2 * CMP_HIDDEN, KV_ROW).astype(bf16)
    pe_row = jnp.broadcast_to(cmp_pe[:, None], (CMP_BLOCK, KV_HEADS, 2, HEAD_DIM))
    return pe_row.reshape(1, CMP_BLOCK * KV_ROW), w1_big, w2_big


def nsa_compress_rows(x, pe_row, w1_big, w2_big, tk=2048):
    m, kdim = x.shape
    hw = w1_big.shape[1]
    return pl.pallas_call(
        _nsa_compress_kernel,
        out_shape=jax.ShapeDtypeStruct((m, KV_ROW), f32),
        grid=(kdim // tk,),
        in_specs=[pl.BlockSpec((m, tk), lambda k: (0, k)),
                  pl.BlockSpec((1, tk), lambda k: (0, k)),
                  pl.BlockSpec((tk, hw), lambda k: (k, 0)),
                  pl.BlockSpec((hw, KV_ROW), lambda k: (0, 0))],
        out_specs=pl.BlockSpec((m, KV_ROW), lambda k: (0, 0)),
        scratch_shapes=[pltpu.VMEM((m, hw), f32)],
        compiler_params=pltpu.CompilerParams(
            dimension_semantics=("arbitrary",), vmem_limit_bytes=VMEM_LIMIT),
        name="nsa_compress",
    )(x, pe_row, w1_big, w2_big)


def _nsa_prompt_kernel(gate_ref, q_ref, ckt_ref, cv_ref, skt_ref, sv_ref, wkt_ref, wv_ref, o_ref):
    i = pl.program_id(1)
    nch = (i * QBLOCK + QBLOCK + KEY_CHUNK - 1) // KEY_CHUNK
    q_pos = i * QBLOCK + lax.broadcasted_iota(jnp.int32, (QBLOCK, 1), 0)
    lane = lax.broadcasted_iota(jnp.int32, (QBLOCK, CMP_LANES), 1)
    lane_k = lax.broadcasted_iota(jnp.int32, (QBLOCK, KEY_CHUNK), 1)
    expand_row = lax.broadcasted_iota(jnp.int32, (CMP_LANES, KEY_CHUNK), 0)
    expand_col = lax.broadcasted_iota(jnp.int32, (CMP_LANES, KEY_CHUNK), 1)
    gate = jax.nn.sigmoid(gate_ref[...])
    ratio = SLC_BLOCK // CMP_BLOCK
    c_vis =_tile_rows(jnp.where((lane + 1) * CMP_BLOCK - 1 <= q_pos, 1.0, 0.0), Q_PER_KV) > 0.5

    for g in range(KV_HEADS):
        qg = q_ref[g].astype(bf16)
        s = jnp.dot(qg, ckt_ref[g], preferred_element_type=f32) * ATT_SCALE
        s = jnp.where(c_vis, s, -jnp.inf)
        m = jnp.max(s, axis=1, keepdims=True)
        m = jnp.where(m == -jnp.inf, 0.0, m)
        p = jnp.exp(s - m)
        p = p / jnp.maximum(jnp.sum(p, axis=1, keepdims=True), F32_TINY)
        o_cmp = jnp.dot(p.astype(bf16), cv_ref[g], preferred_element_type=f32)
        imp = p[0:QBLOCK]
        for r in range(1, Q_PER_KV):
            imp = imp + p[r * QBLOCK:(r + 1) * QBLOCK]
        sel_blk = _select_blocks(imp, q_pos, lane).astype(bf16)

        def slc_mask(c):
            kidx = c * KEY_CHUNK + lane_k
            expand = jnp.where(
                expand_row == ratio * jnp.right_shift(c * KEY_CHUNK + expand_col, 6), 1.0, 0.0).astype(bf16)
            picked = jnp.dot(sel_blk, expand, preferred_element_type=f32)
            return jnp.where((picked > 0.5) & (kidx <= q_pos), 1.0, 0.0)

        o_slc = _masked_flash(qg, lambda c: skt_ref[g, c], lambda c: sv_ref[g, c], 0, nch, slc_mask)

        def win_mask(c):
            kidx = c * QBLOCK + lane
            return jnp.where((kidx <= q_pos) & (kidx > q_pos - WINDOW), 1.0, 0.0)

        o_win = _masked_flash(qg, lambda c: wkt_ref[g, c], lambda c: wv_ref[g, c],
                              jnp.maximum(i - WINDOW // QBLOCK, 0), i + 1, win_mask)
        for r in range(Q_PER_KV):
            h = g * Q_PER_KV + r
            rows = slice(r * QBLOCK, (r + 1) * QBLOCK)
            o_ref[:, h * HEAD_DIM:(h + 1) * HEAD_DIM] = (
                gate[:, 3 * h:3 * h + 1] * o_cmp[rows]
                + gate[:, 3 * h + 1:3 * h + 2] * o_slc[rows]
                + gate[:, 3 * h + 2:3 * h + 3] * o_win[rows])


def nsa_prompt(nq, ng, ckv, skv, wkv, cmp_weights):
    n, t = nq.shape[:2]
    nc = t // CMP_BLOCK
    assert nc <= CMP_LANES and t % KEY_CHUNK == 0 and SLC_BLOCK == 64 and SLC_BLOCK // CMP_BLOCK == 2
    cmp_rows = nsa_compress_rows(ckv.reshape(n * nc, CMP_BLOCK * KV_ROW), *cmp_weights)
    cmp_rows = cmp_rows.reshape(n, nc, KV_HEADS, 2, HEAD_DIM)
    cmp_rows = jnp.pad(cmp_rows, ((0, 0), (0, CMP_LANES - nc), (0, 0), (0, 0), (0, 0))).astype(bf16)
    ckt = cmp_rows[:, :, :, 0, :].transpose(0, 2, 3, 1)
    cv = cmp_rows[:, :, :, 1, :].transpose(0, 2, 1, 3)
    qs = _stack_query_heads(nq, n, t)
    skt, sv = _chunked_kv(skv, n, t, KEY_CHUNK)
    wkt, wv = _chunked_kv(wkv, n, t, QBLOCK)
    nk, nw = t // KEY_CHUNK, t // QBLOCK
    return pl.pallas_call(
        _nsa_prompt_kernel,
        out_shape=jax.ShapeDtypeStruct((n, t, GROUP_WIDTH), f32),
        grid=(n, t // QBLOCK),
        in_specs=[
            pl.BlockSpec((None, QBLOCK, 3 * N_HEADS), lambda b, i: (b, i, 0)),
            pl.BlockSpec((None, KV_HEADS, None, Q_PER_KV * QBLOCK, HEAD_DIM), lambda b, i: (b, 0, i, 0, 0)),
            pl.BlockSpec((None, KV_HEADS, HEAD_DIM, CMP_LANES), lambda b, i: (b, 0, 0, 0)),
            pl.BlockSpec((None, KV_HEADS, CMP_LANES, HEAD_DIM), lambda b, i: (b, 0, 0, 0)),
            pl.BlockSpec((None, KV_HEADS, nk, HEAD_DIM, KEY_CHUNK), lambda b, i: (b, 0, 0, 0, 0)),
            pl.BlockSpec((None, KV_HEADS, nk, KEY_CHUNK, HEAD_DIM), lambda b, i: (b, 0, 0, 0, 0)),
            pl.BlockSpec((None, KV_HEADS, nw, HEAD_DIM, QBLOCK), lambda b, i: (b, 0, 0, 0, 0)),
            pl.BlockSpec((None, KV_HEADS, nw, QBLOCK, HEAD_DIM), lambda b, i: (b, 0, 0, 0, 0)),
        ],
        out_specs=pl.BlockSpec((None, QBLOCK, GROUP_WIDTH), lambda b, i: (b, i, 0)),
        compiler_params=pltpu.CompilerParams(
            dimension_semantics=("parallel", "arbitrary"), vmem_limit_bytes=VMEM_LIMIT),
        name="nsa_prompt",
    )(ng, qs, ckt, cv, skt, sv, wkt, wv)


def s5_discretize(w):
    a_re, a_im = w['a_re'], w['a_im']
    dt = jnp.exp(w['log_dt'])[:, None]
    mag = jnp.exp(a_re * dt)
    ab_re, ab_im = mag * jnp.cos(a_im * dt), mag * jnp.sin(a_im * dt)
    den = a_re * a_re + a_im * a_im
    nr, ni = ab_re - 1.0, ab_im
    f_re = (nr * a_re + ni * a_im) / den
    f_im = (ni * a_re - nr * a_im) / den
    b_re, b_im = w['b_re'], w['b_im']
    bb_re = f_re[..., None] * b_re - f_im[..., None] * b_im
    bb_im = f_re[..., None] * b_im + f_im[..., None] * b_re
    eye = jnp.eye(S5_GROUPS, dtype=f32)

    def in_map(bb):
        return jnp.einsum('gsc,gh->gchs', bb, eye).reshape(GROUP_WIDTH, S5_WIDTH).astype(bf16)

    def out_map(cc):
        return jnp.einsum('gcs,gh->gshc', cc, eye).reshape(S5_WIDTH, GROUP_WIDTH).astype(bf16)

    return dict(a_re=ab_re.reshape(1, S5_WIDTH), a_im=ab_im.reshape(1, S5_WIDTH),
                b_re=in_map(bb_re), b_im=in_map(bb_im),
                c_re=out_map(w['c_re']), c_im=out_map(w['c_im']),
                d=w['d'].reshape(1, GROUP_WIDTH), w_glu=w['w_glu'].astype(bf16),
                b_glu=w['b_glu'].reshape(1, GROUP_WIDTH))


def _s5_prompt_kernel(u_ref, h0_ref, are_ref, aim_ref, bre_ref, bim_ref, cre_ref, cim_ref, d_ref,
                      wglu_ref, bglu_ref, o_ref, hlast_ref, state_ref, bure_ref, buim_ref, hre_ref, him_ref):
    j = pl.program_id(1)

    @pl.when(j == 0)
    def _():
        state_ref[...] = h0_ref[...]

    u = u_ref[...]
    ub = u.astype(bf16)
    bure_ref[...] = jnp.dot(ub, bre_ref[...], preferred_element_type=f32)
    buim_ref[...] = jnp.dot(ub, bim_ref[...], preferred_element_type=f32)
    a_re, a_im = are_ref[...], aim_ref[...]

    def step(t, carry):
        h_re, h_im = carry
        row = pl.ds(t, 1)
        n_re = a_re * h_re - a_im * h_im + bure_ref[row, :]
        n_im = a_re * h_im + a_im * h_re + buim_ref[row, :]
        hre_ref[row, :] = n_re
        him_ref[row, :] = n_im
        return n_re, n_im

    h_re, h_im = lax.fori_loop(0, u.shape[0], step, (state_ref[0:1, :], state_ref[1:2, :]), unroll=8)
    state_ref[0:1, :] = h_re
    state_ref[1:2, :] = h_im
    hlast_ref[...] = state_ref[...]
    y = (jnp.dot(hre_ref[...].astype(bf16), cre_ref[...], preferred_element_type=f32)
         - jnp.dot(him_ref[...].astype(bf16), cim_ref[...], preferred_element_type=f32)
         + d_ref[...] * u)
    y = jax.nn.gelu(y)
    o_ref[...] = y * jax.nn.sigmoid(
        jnp.dot(y.astype(bf16), wglu_ref[...], preferred_element_type=f32) + bglu_ref[...])


def s5_prompt(u, h0, sw):
    n, t = u.shape[:2]
    tc = min(S5_TIME_CHUNK, t)
    const = lambda shape: pl.BlockSpec(shape, lambda b, j: (0,) * len(shape))
    return pl.pallas_call(
        _s5_prompt_kernel,
        out_shape=(jax.ShapeDtypeStruct((n, t, GROUP_WIDTH), f32),
                   jax.ShapeDtypeStruct((n, 2, S5_WIDTH), f32)),
        grid=(n, t // tc),
        in_specs=[pl.BlockSpec((None, tc, GROUP_WIDTH), lambda b, j: (b, j, 0)),
                  pl.BlockSpec((None, 2, S5_WIDTH), lambda b, j: (b, 0, 0)),
                  const((1, S5_WIDTH)), const((1, S5_WIDTH)),
                  const((GROUP_WIDTH, S5_WIDTH)), const((GROUP_WIDTH, S5_WIDTH)),
                  const((S5_WIDTH, GROUP_WIDTH)), const((S5_WIDTH, GROUP_WIDTH)),
                  const((1, GROUP_WIDTH)), const((GROUP_WIDTH, GROUP_WIDTH)), const((1, GROUP_WIDTH))],
        out_specs=(pl.BlockSpec((None, tc, GROUP_WIDTH), lambda b, j: (b, j, 0)),
                   pl.BlockSpec((None, 2, S5_WIDTH), lambda b, j: (b, 0, 0))),
        scratch_shapes=[pltpu.VMEM((2, S5_WIDTH), f32)] + [pltpu.VMEM((tc, S5_WIDTH), f32)] * 4,
        compiler_params=pltpu.CompilerParams(
            dimension_semantics=("parallel", "arbitrary"), vmem_limit_bytes=VMEM_LIMIT),
        name="s5_prompt",
    )(u, h0, sw['a_re'], sw['a_im'], sw['b_re'], sw['b_im'], sw['c_re'], sw['c_im'], sw['d'],
      sw['w_glu'], sw['b_glu'])


def _s5_step_kernel(u_ref, h0_ref, are_ref, aim_ref, bre_ref, bim_ref, cre_ref, cim_ref, d_ref,
                    wglu_ref, bglu_ref, o_ref, h_ref):
    u = u_ref[...]
    ub = u.astype(bf16)
    h0_re, h0_im = h0_ref[:, :S5_WIDTH], h0_ref[:, S5_WIDTH:]
    a_re, a_im = are_ref[...], aim_ref[...]
    h_re = a_re * h0_re - a_im * h0_im + jnp.dot(ub, bre_ref[...], preferred_element_type=f32)
    h_im = a_re * h0_im + a_im * h0_re + jnp.dot(ub, bim_ref[...], preferred_element_type=f32)
    h_ref[:, :S5_WIDTH] = h_re
    h_ref[:, S5_WIDTH:] = h_im
    y = (jnp.dot(h_re.astype(bf16), cre_ref[...], preferred_element_type=f32)
         - jnp.dot(h_im.astype(bf16), cim_ref[...], preferred_element_type=f32)
         + d_ref[...] * u)
    y = jax.nn.gelu(y)
    o_ref[...] = y * jax.nn.sigmoid(
        jnp.dot(y.astype(bf16), wglu_ref[...], preferred_element_type=f32) + bglu_ref[...])


def s5_step(u, h0, sw):
    n = u.shape[0]
    return pl.pallas_call(
        _s5_step_kernel,
        out_shape=(jax.ShapeDtypeStruct((n, GROUP_WIDTH), f32),
                   jax.ShapeDtypeStruct((n, 2 * S5_WIDTH), f32)),
        compiler_params=pltpu.CompilerParams(vmem_limit_bytes=VMEM_LIMIT),
        name="s5_step",
    )(u, h0, sw['a_re'], sw['a_im'], sw['b_re'], sw['b_im'], sw['c_re'], sw['c_im'], sw['d'],
      sw['w_glu'], sw['b_glu'])


GLA_SUB = 16
GLA_QK = GLA_HEADS * GLA_DK
HIGHEST = lax.Precision.HIGHEST
_NT = (((1,), (1,)), ((), ()))
_TN = (((0,), (0,)), ((), ()))


def _log_decay(ga, wgate_ref, bgate_ref):
    z = jnp.dot(ga, wgate_ref[...], preferred_element_type=f32, precision=HIGHEST) + bgate_ref[...]
    return jax.nn.log_sigmoid(z) / GLA_TAU


def _head_sum_matrix():
    r = lax.broadcasted_iota(jnp.int32, (GLA_QK, GROUP_WIDTH), 0) // GLA_DK
    c = lax.broadcasted_iota(jnp.int32, (GLA_QK, GROUP_WIDTH), 1) // GLA_DV
    return jnp.where(r == c, 1.0, 0.0)


def _gla_finish(o, gr, normg_ref):
    outs = []
    for h in range(GLA_HEADS):
        oh = o[:, h * GLA_DV:(h + 1) * GLA_DV]
        outs.append(oh * lax.rsqrt(jnp.mean(oh * oh, axis=-1, keepdims=True) + LN_EPS) * normg_ref[...])
    return jnp.concatenate(outs, axis=1) * (gr * jax.nn.sigmoid(gr))


def _gla_prompt_kernel(q_ref, k_ref, v_ref, ga_ref, gr_ref, wgate_ref, bgate_ref, normg_ref, s0_ref,
                       o_ref, slast_ref, st_ref, oi_ref):
    j = pl.program_id(1)

    @pl.when(j == 0)
    def _():
        st_ref[...] = s0_ref[...]

    rows = q_ref.shape[0]
    nsub = rows // GLA_SUB
    q = q_ref[...] * GLA_DK ** -0.5
    k = k_ref[...]
    v = v_ref[...]
    log_a = _log_decay(ga_ref[...], wgate_ref, bgate_ref)
    ri = lax.broadcasted_iota(jnp.int32, (rows, rows), 0)
    ci = lax.broadcasted_iota(jnp.int32, (rows, rows), 1)
    tri = jnp.where((ri // GLA_SUB == ci // GLA_SUB) & (ci <= ri), 1.0, 0.0)
    b = jnp.dot(tri, log_a, preferred_element_type=f32, precision=HIGHEST)
    row_in_sub = lax.broadcasted_iota(jnp.int32, (rows, 1), 0) % GLA_SUB
    head_sum = _head_sum_matrix()

    def sub_row(x, jj):
        x3 = x.reshape(nsub, GLA_SUB, x.shape[1])
        return jnp.broadcast_to(x3[:, jj:jj + 1, :], x3.shape).reshape(x.shape)

    o = jnp.zeros((rows, GROUP_WIDTH), f32)
    for jj in range(GLA_SUB):
        decay = jnp.exp(jnp.minimum(b - sub_row(b, jj), 0.0))
        prod = jnp.where(row_in_sub >= jj, q * sub_row(k, jj) * decay, 0.0)
        att = jnp.dot(prod, head_sum, preferred_element_type=f32, precision=HIGHEST)
        o = o + att * sub_row(v, jj)

    for i in range(nsub):
        blk = slice(i * GLA_SUB, (i + 1) * GLA_SUB)
        b_blk = b[blk]
        b_end = b[(i + 1) * GLA_SUB - 1:(i + 1) * GLA_SUB]
        q_blk = (q[blk] * jnp.exp(b_blk)).astype(bf16)
        k_blk = (k[blk] * jnp.exp(b_end - b_blk)).astype(bf16)
        v_blk = v[blk].astype(bf16)
        a_end = jnp.exp(b_end)
        for h in range(GLA_HEADS):
            ks = slice(h * GLA_DK, (h + 1) * GLA_DK)
            vs = slice(h * GLA_DV, (h + 1) * GLA_DV)
            st = st_ref[h]
            oi_ref[blk, vs] = lax.dot_general(q_blk[:, ks], st.astype(bf16), _NT, preferred_element_type=f32)
            st_ref[h] = st * a_end[:, ks] + lax.dot_general(v_blk[:, vs], k_blk[:, ks], _TN,
                                                            preferred_element_type=f32)

    o_ref[...] = _gla_finish(o + oi_ref[...], gr_ref[...], normg_ref)
    slast_ref[...] = st_ref[...]


def gla_prompt(gq, gk, gv, ga, gr, w_gate, b_gate, norm_g):
    n, t = gq.shape[:2]
    tc = min(GLA_CHUNK, t)
    rows = lambda width: pl.BlockSpec((None, tc, width), lambda b, j: (b, j, 0))
    const = lambda shape: pl.BlockSpec(shape, lambda b, j: (0,) * len(shape))
    state = pl.BlockSpec((None, GLA_HEADS, GLA_DV, GLA_DK), lambda b, j: (b, 0, 0, 0))
    o, s_t = pl.pallas_call(
        _gla_prompt_kernel,
        out_shape=(jax.ShapeDtypeStruct((n, t, GROUP_WIDTH), f32),
                   jax.ShapeDtypeStruct((n, GLA_HEADS, GLA_DV, GLA_DK), f32)),
        grid=(n, t // tc),
        in_specs=[rows(GLA_QK), rows(GLA_QK), rows(GROUP_WIDTH), rows(GLA_LOWRANK), rows(GROUP_WIDTH),
                  const((GLA_LOWRANK, GLA_QK)), const((1, GLA_QK)), const((1, GLA_DV)), state],
        out_specs=(rows(GROUP_WIDTH), state),
        scratch_shapes=[pltpu.VMEM((GLA_HEADS, GLA_DV, GLA_DK), f32), pltpu.VMEM((tc, GROUP_WIDTH), f32)],
        compiler_params=pltpu.CompilerParams(
            dimension_semantics=("parallel", "arbitrary"), vmem_limit_bytes=VMEM_LIMIT),
        name="gla_prompt",
    )(gq, gk, gv, ga, gr, w_gate, b_gate.reshape(1, GLA_QK), norm_g.reshape(1, GLA_DV),
      jnp.zeros((n, GLA_HEADS, GLA_DV, GLA_DK), f32))
    return o, s_t.transpose(0, 1, 3, 2)


GLA_STEP_SEQS = 8


def _gla_step_kernel(q_ref, k_ref, v_ref, ga_ref, gr_ref, wgate_ref, bgate_ref, normg_ref, s_ref,
                     o_ref, snew_ref):
    q = q_ref[...] * GLA_DK ** -0.5
    k = k_ref[...]
    v = v_ref[...]
    a = jnp.exp(_log_decay(ga_ref[...], wgate_ref, bgate_ref))
    qa = q * a
    seq = lax.broadcasted_iota(jnp.int32, (GLA_STEP_SEQS, 1), 0)
    o = jnp.dot(q * k, _head_sum_matrix(), preferred_element_type=f32, precision=HIGHEST) * v
    o_heads = [jnp.zeros((GLA_STEP_SEQS, GLA_DV), f32) for _ in range(GLA_HEADS)]
    for i in range(GLA_STEP_SEQS):
        mine = seq == i
        pick = jnp.broadcast_to(jnp.where(mine, 1.0, 0.0), (GLA_STEP_SEQS, GLA_DV))
        qa_i = jnp.where(mine, qa, 0.0).astype(bf16)
        k_i = jnp.where(mine, k, 0.0).astype(bf16)
        vb = v.astype(bf16)
        for h in range(GLA_HEADS):
            ks = slice(h * GLA_DK, (h + 1) * GLA_DK)
            vs = slice(h * GLA_DV, (h + 1) * GLA_DV)
            s_old = s_ref[i, h]
            a_rows = lax.dot_general(a[:, ks], pick, _TN, preferred_element_type=f32, precision=HIGHEST)
            o_heads[h] = o_heads[h] + jnp.dot(qa_i[:, ks], s_old.astype(bf16), preferred_element_type=f32)
            snew_ref[i, h] = a_rows * s_old + lax.dot_general(k_i[:, ks], vb[:, vs], _TN,
                                                               preferred_element_type=f32)
    o_ref[...] = _gla_finish(o + jnp.concatenate(o_heads, axis=1), gr_ref[...], normg_ref)


def gla_step(gq, gk, gv, ga, gr, w_gate, b_gate, norm_g, s0):
    n = gq.shape[0]
    rows = lambda width: pl.BlockSpec((GLA_STEP_SEQS, width), lambda i: (i, 0))
    const = lambda shape: pl.BlockSpec(shape, lambda i: (0,) * len(shape))
    state = pl.BlockSpec((GLA_STEP_SEQS, GLA_HEADS, GLA_DK, GLA_DV), lambda i: (i, 0, 0, 0))
    return pl.pallas_call(
        _gla_step_kernel,
        out_shape=(jax.ShapeDtypeStruct((n, GROUP_WIDTH), f32),
                   jax.ShapeDtypeStruct((n, GLA_HEADS, GLA_DK, GLA_DV), f32)),
        grid=(n // GLA_STEP_SEQS,),
        in_specs=[rows(GLA_QK), rows(GLA_QK), rows(GROUP_WIDTH), rows(GLA_LOWRANK), rows(GROUP_WIDTH),
                  const((GLA_LOWRANK, GLA_QK)), const((1, GLA_QK)), const((1, GLA_DV)), state],
        out_specs=(rows(GROUP_WIDTH), state),
        compiler_params=pltpu.CompilerParams(
            dimension_semantics=("parallel",), vmem_limit_bytes=VMEM_LIMIT),
        name="gla_step",
    )(gq, gk, gv, ga, gr, w_gate, b_gate.reshape(1, GLA_QK), norm_g.reshape(1, GLA_DV), s0)


N_PAGES = PAST_LEN // PAGE_SIZE
KEY_PAD = PAST_LEN + LANES
STEP_ROWS = 8


def _page_specs(block, seqs_per_step=1):
    zeros = (0,) * (len(block) - 1)
    return [pl.BlockSpec(block, lambda i, pt, s=s, p=p: (pt[i * seqs_per_step + s, p],) + zeros)
            for s in range(seqs_per_step) for p in range(N_PAGES)]


def _seq_spec(*tail):
    zeros = (0,) * len(tail)
    return pl.BlockSpec((None,) + tail, lambda i, pt: (i,) + zeros)


def _heads_on_kv_lanes(q):
    n = q.shape[0]
    qh = q.reshape(n, N_HEADS, HEAD_DIM)
    z = jnp.zeros((n, Q_PER_KV, HEAD_DIM), q.dtype)
    top = jnp.concatenate([qh[:, :Q_PER_KV], z, z, z], axis=-1)
    bot = jnp.concatenate([z, z, qh[:, Q_PER_KV:], z], axis=-1)
    return jnp.concatenate([top, bot], axis=1)


def _heads_from_kv_lanes(o8):
    n = o8.shape[0]
    o = o8.reshape(n, N_HEADS, KV_HEADS, 2, HEAD_DIM)
    return jnp.concatenate([o[:, :Q_PER_KV, 0, 1], o[:, Q_PER_KV:, 1, 1]], axis=1).reshape(n, GROUP_WIDTH)


def _step_attention(q8, key_blocks, masks, kv_new, new_ok):
    qb = q8.astype(bf16)
    scores = []
    s_new = jnp.sum(q8 * kv_new, axis=1, keepdims=True) * ATT_SCALE
    if new_ok is not None:
        s_new = jnp.where(new_ok, s_new, MASKED)
    m = s_new
    for blk, msk in zip(key_blocks, masks):
        s = lax.dot_general(qb, blk[...].astype(bf16), _NT, preferred_element_type=f32) * ATT_SCALE
        s = jnp.where(msk > 0.5, s, MASKED)
        scores.append(s)
        m = jnp.maximum(m, jnp.max(s, axis=1, keepdims=True))
    p_new = jnp.exp(s_new - m)
    if new_ok is not None:
        p_new = jnp.where(new_ok, p_new, 0.0)
    l = p_new
    acc = p_new * kv_new
    for blk, msk, s in zip(key_blocks, masks, scores):
        p = jnp.where(msk > 0.5, jnp.exp(s - m), 0.0)
        l = l + jnp.sum(p, axis=1, keepdims=True)
        acc = acc + jnp.dot(p.astype(bf16), blk[...].astype(bf16), preferred_element_type=f32)
    return acc / l


def _dsa_step_scores_kernel(pt_ref, iq_ref, iw_ref, iknew_ref, *rest):
    pages, keys_ref, kall_ref = rest[:N_PAGES], rest[N_PAGES], rest[N_PAGES + 1]
    for p in range(N_PAGES):
        kall_ref[p * PAGE_SIZE:(p + 1) * PAGE_SIZE, :] = pages[p][...]
    kall_ref[PAST_LEN:PAST_LEN + 1, :] = iknew_ref[...]
    kall_ref[PAST_LEN + 1:, :] = jnp.zeros((KEY_PAD - PAST_LEN - 1, IDX_DIM), f32)
    dots = lax.dot_general(iq_ref[...].astype(bf16), kall_ref[...].astype(bf16), _NT,
                           preferred_element_type=f32) * IDX_DIM ** -0.5
    iw = iw_ref[...] * IDX_HEADS ** -0.5
    lane = lax.broadcasted_iota(jnp.int32, (1, LANES), 1)
    for c in range(KEY_PAD // LANES):
        tile = slice(c * LANES, (c + 1) * LANES)
        score = jnp.sum(iw * jnp.maximum(dots[:, tile], 0.0), axis=0, keepdims=True)
        key = _sortable_key(score)
        if c == N_PAGES:
            key = jnp.where(lane == 0, key, INT_MIN)
        keys_ref[:, tile] = key


def _topk_rows_kernel(keys_ref, sel_ref, *, topk, idx_bits, n_valid):
    keys = keys_ref[...]
    lane = lax.broadcasted_iota(jnp.int32, keys.shape, 1)
    count = lambda pred: jnp.sum(jnp.where(pred(keys, lane), 1.0, 0.0), axis=1, keepdims=True)
    thr, cut = _topk_threshold(count, keys.shape[0], topk, idx_bits)
    chosen = ((keys > thr) | ((keys == thr) & (lane <= cut))) & (lane < n_valid)
    sel_ref[...] = jnp.where(chosen, 1.0, 0.0)


def _dsa_step_attend_kernel(pt_ref, q_ref, sel_ref, kvnew_ref, *rest):
    pages, o_ref = rest[:N_PAGES], rest[N_PAGES]
    masks = [sel_ref[:, p * PAGE_SIZE:(p + 1) * PAGE_SIZE] for p in range(N_PAGES)]
    new_ok = sel_ref[:, PAST_LEN:PAST_LEN + 1] > 0.5
    o_ref[...] = _step_attention(q_ref[...], pages, masks, kvnew_ref[...], new_ok)


def dsa_step(dq, dkv, iq, ik, iw, pool_kv, pool_idx, page_table):
    n = dq.shape[0]
    assert min(DSA_TOPK, (PAST_LEN + 1) // 4) == DSA_TOPK and PAGE_SIZE == LANES
    pad_heads = ((0, 0), (0, STEP_ROWS - IDX_HEADS), (0, 0))
    iq8 = jnp.pad(iq.reshape(n, IDX_HEADS, IDX_DIM), pad_heads)
    iw8 = jnp.broadcast_to(jnp.pad(iw.reshape(n, IDX_HEADS, 1), pad_heads), (n, STEP_ROWS, LANES))
    keys = pl.pallas_call(
        _dsa_step_scores_kernel,
        out_shape=jax.ShapeDtypeStruct((n, 1, KEY_PAD), jnp.int32),
        grid_spec=pltpu.PrefetchScalarGridSpec(
            num_scalar_prefetch=1, grid=(n,),
            in_specs=[_seq_spec(STEP_ROWS, IDX_DIM), _seq_spec(STEP_ROWS, LANES), _seq_spec(1, IDX_DIM)]
            + _page_specs((None, PAGE_SIZE, IDX_DIM)),
            out_specs=_seq_spec(1, KEY_PAD),
            scratch_shapes=[pltpu.VMEM((KEY_PAD, IDX_DIM), f32)]),
        compiler_params=pltpu.CompilerParams(
            dimension_semantics=("parallel",), vmem_limit_bytes=VMEM_LIMIT),
        name="dsa_step_scores",
    )(page_table, iq8, iw8, ik.reshape(n, 1, IDX_DIM), *([pool_idx] * N_PAGES))
    sel = pl.pallas_call(
        functools.partial(_topk_rows_kernel, topk=DSA_TOPK, idx_bits=int(math.ceil(math.log2(KEY_PAD))),
                          n_valid=PAST_LEN + 1),
        out_shape=jax.ShapeDtypeStruct((n, KEY_PAD), f32),
        compiler_params=pltpu.CompilerParams(vmem_limit_bytes=VMEM_LIMIT),
        name="dsa_step_topk",
    )(keys.reshape(n, KEY_PAD))
    o8 = pl.pallas_call(
        _dsa_step_attend_kernel,
        out_shape=jax.ShapeDtypeStruct((n, N_HEADS, KV_ROW), f32),
        grid_spec=pltpu.PrefetchScalarGridSpec(
            num_scalar_prefetch=1, grid=(n,),
            in_specs=[_seq_spec(N_HEADS, KV_ROW), _seq_spec(1, KEY_PAD), _seq_spec(1, KV_ROW)]
            + _page_specs((None, PAGE_SIZE, KV_ROW)),
            out_specs=_seq_spec(N_HEADS, KV_ROW)),
        compiler_params=pltpu.CompilerParams(
            dimension_semantics=("parallel",), vmem_limit_bytes=VMEM_LIMIT),
        name="dsa_step_attend",
    )(page_table, _heads_on_kv_lanes(dq), sel.reshape(n, 1, KEY_PAD), dkv.reshape(n, 1, KV_ROW),
      *([pool_kv.reshape(-1, PAGE_SIZE, KV_ROW)] * N_PAGES))
    return _heads_from_kv_lanes(o8)


CMP_STEP_SEQS = 2
BLOCKS_PER_PAGE = PAGE_SIZE // CMP_BLOCK
CMP_ROW = CMP_BLOCK * KV_ROW


def _nsa_compress_paged_kernel(pt_ref, pe_ref, w1_ref, w2_ref, *rest):
    npg = CMP_STEP_SEQS * N_PAGES
    pages, o_ref, x_ref = rest[:npg], rest[npg], rest[npg + 1]
    for s in range(npg):
        x_ref[s * BLOCKS_PER_PAGE:(s + 1) * BLOCKS_PER_PAGE, :] = pages[s][...] + pe_ref[...]
    tk = 2048
    hid = jnp.zeros((npg * BLOCKS_PER_PAGE, w1_ref.shape[1]), f32)
    for kk in range(CMP_ROW // tk):
        cols = slice(kk * tk, (kk + 1) * tk)
        hid = hid + jnp.dot(x_ref[:, cols].astype(bf16), w1_ref[cols, :], preferred_element_type=f32)
    o_ref[...] = jnp.dot(jax.nn.gelu(hid).astype(bf16), w2_ref[...], preferred_element_type=f32)


def nsa_compress_paged(pool_cmp, page_table, pe_row, w1_big, w2_big):
    n = page_table.shape[0]
    blocks = N_PAGES * BLOCKS_PER_PAGE
    hw = w1_big.shape[1]
    rows = CMP_STEP_SEQS * blocks
    const = lambda shape: pl.BlockSpec(shape, lambda i, pt: (0,) * len(shape))
    out = pl.pallas_call(
        _nsa_compress_paged_kernel,
        out_shape=jax.ShapeDtypeStruct((n * blocks, KV_ROW), f32),
        grid_spec=pltpu.PrefetchScalarGridSpec(
            num_scalar_prefetch=1, grid=(n // CMP_STEP_SEQS,),
            in_specs=[const((1, CMP_ROW)), const((CMP_ROW, hw)), const((hw, KV_ROW))]
            + _page_specs((None, BLOCKS_PER_PAGE, CMP_ROW), CMP_STEP_SEQS),
            out_specs=pl.BlockSpec((rows, KV_ROW), lambda i, pt: (i, 0)),
            scratch_shapes=[pltpu.VMEM((rows, CMP_ROW), f32)]),
        compiler_params=pltpu.CompilerParams(
            dimension_semantics=("parallel",), vmem_limit_bytes=VMEM_LIMIT),
        name="nsa_compress_paged",
    )(page_table, pe_row, w1_big, w2_big,
      *([pool_cmp.reshape(-1, BLOCKS_PER_PAGE, CMP_ROW)] * (CMP_STEP_SEQS * N_PAGES)))
    return out.reshape(n, blocks, KV_ROW)


def _nsa_step_kernel(pt_ref, q_ref, gate_ref, cmp_ref, slcnew_ref, win_ref, winnew_ref, *rest):
    pages, o_ref = rest[:N_PAGES], rest[N_PAGES]
    q8 = q_ref[...]
    lane = lax.broadcasted_iota(jnp.int32, (N_HEADS, CMP_LANES), 1)
    head = lax.broadcasted_iota(jnp.int32, (N_HEADS, 1), 0)
    q_pos = jnp.full((N_HEADS, 1), PAST_LEN, jnp.int32)
    c_rows = cmp_ref[...].astype(bf16)
    s = lax.dot_general(q8.astype(bf16), c_rows, _NT, preferred_element_type=f32) * ATT_SCALE
    s = jnp.where((lane + 1) * CMP_BLOCK - 1 <= q_pos, s, -jnp.inf)
    m = jnp.max(s, axis=1, keepdims=True)
    m = jnp.where(m == -jnp.inf, 0.0, m)
    p = jnp.exp(s - m)
    p = p / jnp.maximum(jnp.sum(p, axis=1, keepdims=True), F32_TINY)
    o_cmp = jnp.dot(p.astype(bf16), c_rows, preferred_element_type=f32)
    imp = jnp.where(head < Q_PER_KV,
                    jnp.sum(p[:Q_PER_KV], axis=0, keepdims=True),
                    jnp.sum(p[Q_PER_KV:], axis=0, keepdims=True))
    sel_blk = _select_blocks(imp, q_pos, lane)
    ratio = SLC_BLOCK // CMP_BLOCK
    per_page = PAGE_SIZE // SLC_BLOCK
    masks = []
    for pg in range(N_PAGES):
        msk = sel_blk[:, ratio * per_page * pg:ratio * per_page * pg + 1]
        for j in range(1, per_page):
            b = per_page * pg + j
            msk = jnp.where(lane < j * SLC_BLOCK, msk, sel_blk[:, ratio * b:ratio * b + 1])
        masks.append(msk)
    o_slc = _step_attention(q8, pages, masks, slcnew_ref[...], None)
    wbuf = win_ref.shape[0]
    wchunks = wbuf // LANES
    wblocks = [win_ref.at[pl.ds(c * LANES, LANES), :] for c in range(wchunks)]
    wmasks = [jnp.where(PAST_LEN - wbuf + c * LANES + lane > q_pos - WINDOW, 1.0, 0.0) for c in range(wchunks)]
    o_win = _step_attention(q8, wblocks, wmasks, winnew_ref[...], None)
    gate = jax.nn.sigmoid(gate_ref[...])
    wide = lambda g: jnp.concatenate([g] * (KV_ROW // LANES), axis=1)
    o_ref[...] = wide(gate[0]) * o_cmp + wide(gate[1]) * o_slc + wide(gate[2]) * o_win


def nsa_step(nq, ng, skv, wkv, pool_cmp, pool_slc, win_buf, page_table, cmp_weights):
    n = nq.shape[0]
    blocks = N_PAGES * BLOCKS_PER_PAGE
    wbuf = win_buf.shape[1]
    assert (PAST_LEN + 1) // CMP_BLOCK == blocks and blocks <= CMP_LANES and wbuf % LANES == 0
    assert PAST_LEN // SLC_BLOCK + 1 <= CMP_LANES // 2 and PAGE_SIZE == LANES
    cmp_rows = nsa_compress_paged(pool_cmp, page_table, *cmp_weights)
    cmp_rows = jnp.pad(cmp_rows, ((0, 0), (0, CMP_LANES - blocks), (0, 0)))
    gates = jnp.broadcast_to(ng.reshape(n, N_HEADS, 3).transpose(0, 2, 1)[..., None], (n, 3, N_HEADS, LANES))
    o8 = pl.pallas_call(
        _nsa_step_kernel,
        out_shape=jax.ShapeDtypeStruct((n, N_HEADS, KV_ROW), f32),
        grid_spec=pltpu.PrefetchScalarGridSpec(
            num_scalar_prefetch=1, grid=(n,),
            in_specs=[_seq_spec(N_HEADS, KV_ROW), _seq_spec(3, N_HEADS, LANES), _seq_spec(CMP_LANES, KV_ROW),
                      _seq_spec(1, KV_ROW), _seq_spec(wbuf, KV_ROW), _seq_spec(1, KV_ROW)]
            + _page_specs((None, PAGE_SIZE, KV_ROW)),
            out_specs=_seq_spec(N_HEADS, KV_ROW)),
        compiler_params=pltpu.CompilerParams(
            dimension_semantics=("parallel",), vmem_limit_bytes=VMEM_LIMIT),
        name="nsa_step",
    )(page_table, _heads_on_kv_lanes(nq), gates, cmp_rows, skv.reshape(n, 1, KV_ROW),
      win_buf.reshape(n, wbuf, KV_ROW), wkv.reshape(n, 1, KV_ROW),
      *([pool_slc.reshape(-1, PAGE_SIZE, KV_ROW)] * N_PAGES))
    return _heads_from_kv_lanes(o8)


def masked_softmax(s, mask):
    s = jnp.where(mask, s.astype(jnp.float32), -jnp.inf)
    m = jnp.max(s, axis=-1, keepdims=True)
    m = jnp.where(jnp.isfinite(m), m, 0.0)
    p = jnp.exp(s - m)
    return p / jnp.maximum(p.sum(-1, keepdims=True), jnp.finfo(jnp.float32).tiny)


def gather_seq(rows, pos, *extra):
    n_idx = jnp.arange(rows.shape[0]).reshape((-1,) + (1,) * (pos.ndim - 1))
    return rows[(n_idx, pos) + tuple(extra)]


def gather_pages(pool, page_table):
    g = pool[page_table]
    return g.reshape((g.shape[0], -1) + g.shape[3:])


def gather_paged_rows(pool, page_table, pos, *extra):
    n_idx = jnp.arange(page_table.shape[0]).reshape((-1,) + (1,) * (pos.ndim - 1))
    phys = page_table[n_idx, pos // PAGE_SIZE]
    return pool[(phys, pos % PAGE_SIZE) + tuple(extra)]


def gather_past_and_new(pool, page_table, new_rows, pos, *extra):
    past = gather_paged_rows(pool, page_table, jnp.minimum(pos, PAST_LEN - 1), *extra)
    new = gather_seq(new_rows, jnp.clip(pos - PAST_LEN, 0, new_rows.shape[1] - 1), *extra)
    in_past = (pos < PAST_LEN).reshape(pos.shape + (1,) * (past.ndim - pos.ndim))
    return jnp.where(in_past, past, new)


def split_projection(cols, n, t):
    (dq, dkv, iq, ik, iw, u, nq, ckv, skv, wkv, ng, gq, gk, gv, ga, gr) = cols
    kv_shape = (n, t, KV_HEADS, 2, HEAD_DIM)
    return {
        'dsa_q': dq.reshape(n, t, N_HEADS, HEAD_DIM),
        'dsa_kv': dkv.reshape(kv_shape),
        'idx_q': iq.reshape(n, t, IDX_HEADS, IDX_DIM),
        'idx_k': ik,
        'idx_w': iw,
        's5_u': u,
        'nsa_q': nq.reshape(n, t, N_HEADS, HEAD_DIM),
        'nsa_cmp': ckv.reshape(kv_shape),
        'nsa_slc': skv.reshape(kv_shape),
        'nsa_win': wkv.reshape(kv_shape),
        'nsa_gate': jax.nn.sigmoid(ng).reshape(n, t, N_HEADS, 3),
        'gla_q': gq.reshape(n, t, GLA_HEADS, GLA_DK),
        'gla_k': gk.reshape(n, t, GLA_HEADS, GLA_DK),
        'gla_v': gv.reshape(n, t, GLA_HEADS, GLA_DV),
        'gla_a': ga,
        'gla_r': gr,
    }


def indexer_scores(iq, iw, ik):
    dots = jnp.einsum('nqhd,nld->nqhl', iq, ik) * IDX_DIM ** -0.5
    return jnp.einsum('nqh,nqhl->nql', iw * IDX_HEADS ** -0.5, jax.nn.relu(dots)).astype(jnp.float32)


def dsa_select(scores, q_pos, n_keys):
    causal = jnp.arange(n_keys)[None, :] <= q_pos[:, None]
    k_sel = min(DSA_TOPK, n_keys // 4)
    _, sel = lax.top_k(jnp.where(causal, scores, -jnp.inf), k_sel)
    return sel, sel <= q_pos[None, :, None]


def dsa_attend(q, kv_sel, valid):
    n, nq = q.shape[:2]
    qg = q.reshape(n, nq, KV_HEADS, Q_PER_KV, HEAD_DIM)
    s = jnp.einsum('nqgrd,nqkgd->nqgrk', qg, kv_sel[..., 0, :]) * HEAD_DIM ** -0.5
    p = masked_softmax(s, valid[:, :, None, None, :])
    o = jnp.einsum('nqgrk,nqkgd->nqgrd', p, kv_sel[..., 1, :])
    return o.reshape(n, nq, GROUP_WIDTH)


def dsa_sample(p, pool_kv, pool_idx, page_table):
    s_new = p['dsa_q'].shape[1]
    n_keys = PAST_LEN + s_new
    ik_all = jnp.concatenate([gather_pages(pool_idx, page_table), p['idx_k']], axis=1)
    q_pos = PAST_LEN + jnp.arange(s_new)
    sel, valid = dsa_select(indexer_scores(p['idx_q'], p['idx_w'], ik_all), q_pos, n_keys)
    kv_sel = gather_past_and_new(pool_kv, page_table, p['dsa_kv'], sel)
    return dsa_attend(p['dsa_q'], kv_sel, valid)


def complex_linear_combine(e1, e2):
    a1r, a1i, b1r, b1i = e1
    a2r, a2i, b2r, b2i = e2
    return (a2r * a1r - a2i * a1i,
            a2r * a1i + a2i * a1r,
            a2r * b1r - a2i * b1i + b2r,
            a2r * b1i + a2i * b1r + b2i)


def s5_mixer(u, h0, w):
    n, t = u.shape[:2]
    uf = u.reshape(n, t, S5_GROUPS, S5_CH)
    a_re, a_im = w['a_re'], w['a_im']
    dt = jnp.exp(w['log_dt'])[:, None]
    mag = jnp.exp(a_re * dt)
    ab_re, ab_im = mag * jnp.cos(a_im * dt), mag * jnp.sin(a_im * dt)
    den = a_re * a_re + a_im * a_im
    nr, ni = ab_re - 1.0, ab_im
    f_re = (nr * a_re + ni * a_im) / den
    f_im = (ni * a_re - nr * a_im) / den
    b_re, b_im = w['b_re'], w['b_im']
    bb_re = f_re[..., None] * b_re - f_im[..., None] * b_im
    bb_im = f_re[..., None] * b_im + f_im[..., None] * b_re
    bu_re = jnp.einsum('gsc,btgc->btgs', bb_re, uf)
    bu_im = jnp.einsum('gsc,btgc->btgs', bb_im, uf)
    cum_re, cum_im, sc_re, sc_im = lax.associative_scan(
        complex_linear_combine,
        (jnp.broadcast_to(ab_re, bu_re.shape), jnp.broadcast_to(ab_im, bu_im.shape), bu_re, bu_im),
        axis=1)
    h0r, h0i = h0[:, 0][:, None], h0[:, 1][:, None]
    h_re = cum_re * h0r - cum_im * h0i + sc_re
    h_im = cum_re * h0i + cum_im * h0r + sc_im
    y = (jnp.einsum('gcs,btgs->btgc', w['c_re'], h_re)
         - jnp.einsum('gcs,btgs->btgc', w['c_im'], h_im)
         + w['d'] * uf)
    y = jax.nn.gelu(y.reshape(n, t, GROUP_WIDTH))
    y = y * jax.nn.sigmoid(y @ w['w_glu'] + w['b_glu'])
    h_last = jnp.stack([h_re[:, -1], h_im[:, -1]], axis=1)
    return y, h_last


def nsa_compress(rows, w):
    n, length = rows.shape[:2]
    nc = length // CMP_BLOCK
    blk = rows[:, :nc * CMP_BLOCK].reshape(n, nc, CMP_BLOCK, KV_HEADS, 2, HEAD_DIM)
    blk = blk + w['cmp_pe'][:, None]
    flat = jnp.moveaxis(blk, 2, 4).reshape(n, nc, KV_HEADS, 2, CMP_BLOCK * HEAD_DIM)
    hid = jax.nn.gelu(jnp.einsum('ncgjf,jfe->ncgje', flat, w['cmp_w1']))
    return jnp.einsum('ncgje,jed->ncgjd', hid, w['cmp_w2'])


def nsa_core(q, gate, q_pos, cmp_kv, n_slc_blocks, slc_gather, win_kv, win_pos):
    n, nq = q.shape[:2]
    qg = q.reshape(n, nq, KV_HEADS, Q_PER_KV, HEAD_DIM)
    scale = HEAD_DIM ** -0.5
    nc = cmp_kv.shape[1]
    c_vis = (jnp.arange(nc) + 1) * CMP_BLOCK - 1 <= q_pos[:, None]
    s_c = jnp.einsum('nqgrd,ncgd->nqgrc', qg, cmp_kv[..., 0, :]) * scale
    p_c = masked_softmax(s_c, c_vis[None, :, None, None, :])
    o_cmp = jnp.einsum('nqgrc,ncgd->nqgrd', p_c, cmp_kv[..., 1, :])
    ratio = SLC_BLOCK // CMP_BLOCK
    imp = p_c.sum(3)
    imp = jnp.pad(imp, ((0, 0), (0, 0), (0, 0), (0, n_slc_blocks * ratio - nc)))
    imp = imp.reshape(n, nq, KV_HEADS, n_slc_blocks, ratio).sum(-1)
    blk = jnp.arange(n_slc_blocks)[None, :]
    cur = (q_pos // SLC_BLOCK)[:, None]
    forced = (blk == 0) | (blk == cur) | (blk == cur - 1)
    future = blk > cur
    score = jnp.where(future[None, :, None, :], -jnp.inf,
                      jnp.where(forced[None, :, None, :], jnp.inf, imp))
    n_sel = min(SLC_TOPN, n_slc_blocks)
    _, sel = lax.top_k(score, n_sel)
    pos = (sel[..., None] * SLC_BLOCK + jnp.arange(SLC_BLOCK)).reshape(n, nq, KV_HEADS, n_sel * SLC_BLOCK)
    valid = pos <= q_pos[None, :, None, None]
    kv_s = slc_gather(pos)
    s_s = jnp.einsum('nqgrd,nqgkd->nqgrk', qg, kv_s[..., 0, :]) * scale
    p_s = masked_softmax(s_s, valid[:, :, :, None, :])
    o_slc = jnp.einsum('nqgrk,nqgkd->nqgrd', p_s, kv_s[..., 1, :])
    w_vis = ((win_pos[None, :] <= q_pos[:, None]) & (win_pos[None, :] > q_pos[:, None] - WINDOW)
             & (win_pos[None, :] >= 0))
    s_w = jnp.einsum('nqgrd,nwgd->nqgrw', qg, win_kv[..., 0, :]) * scale
    p_w = masked_softmax(s_w, w_vis[None, :, None, None, :])
    o_win = jnp.einsum('nqgrw,nwgd->nqgrd', p_w, win_kv[..., 1, :])
    g = gate.reshape(n, nq, KV_HEADS, Q_PER_KV, 3)
    o = g[..., 0:1] * o_cmp + g[..., 1:2] * o_slc + g[..., 2:3] * o_win
    return o.reshape(n, nq, GROUP_WIDTH)


def nsa_sample(p, w, pool_cmp, pool_slc, win_buf, page_table):
    s_new = p['nsa_q'].shape[1]
    n_keys = PAST_LEN + s_new
    cmp_all = jnp.concatenate([gather_pages(pool_cmp, page_table), p['nsa_cmp']], axis=1)
    cmp_kv = nsa_compress(cmp_all, w)
    wbuf = win_buf.shape[1]
    win_all = jnp.concatenate([win_buf, p['nsa_win']], axis=1)
    win_pos = PAST_LEN - wbuf + jnp.arange(wbuf + s_new)
    q_pos = PAST_LEN + jnp.arange(s_new)
    g_idx = jnp.arange(KV_HEADS)[None, None, :, None]
    out = nsa_core(p['nsa_q'], p['nsa_gate'], q_pos, cmp_kv, -(-n_keys // SLC_BLOCK),
                   lambda pos: gather_past_and_new(pool_slc, page_table, p['nsa_slc'], pos, g_idx),
                   win_all, win_pos)
    return out, win_all[:, -wbuf:]


def gla_recurrence(q, k, v, log_a, s0):
    n, t = q.shape[:2]
    c = min(GLA_CHUNK, t)
    tp = -(-t // c) * c

    def chunks(a):
        a = jnp.pad(a, ((0, 0), (0, tp - t)) + ((0, 0),) * (a.ndim - 2))
        return jnp.moveaxis(a.reshape((n, tp // c, c) + a.shape[2:]), 1, 0)

    tril = jnp.tril(jnp.ones((c, c), dtype=bool))[None, :, :, None, None]

    def step(s, inp):
        qc, kc, vc, gc = inp
        b = jnp.cumsum(gc, axis=1)
        o_inter = jnp.einsum('nchk,nhkv->nchv', qc * jnp.exp(b), s)
        decay = jnp.exp(jnp.where(tril, b[:, :, None] - b[:, None, :], -jnp.inf))
        att = jnp.einsum('nthk,nshk,ntshk->nths', qc, kc, decay)
        o_intra = jnp.einsum('nths,nshv->nthv', att, vc)
        b_end = b[:, -1]
        s_new = (jnp.exp(b_end)[..., None] * s
                 + jnp.einsum('nshk,nshv->nhkv', kc * jnp.exp(b_end[:, None] - b), vc))
        return s_new, o_inter + o_intra

    s_last, o = lax.scan(step, s0, (chunks(q), chunks(k), chunks(v), chunks(log_a)))
    o = jnp.moveaxis(o, 0, 1).reshape((n, tp) + o.shape[3:])[:, :t]
    return o, s_last


def gla_mixer(p, s0, w):
    n, t = p['gla_q'].shape[:2]
    q = p['gla_q'] * GLA_DK ** -0.5
    k = p['gla_k']
    v = p['gla_v']
    z = p['gla_a'] @ w['gla_w_gate'] + w['gla_b_gate']
    log_a = (jax.nn.log_sigmoid(z) / GLA_TAU).reshape(n, t, GLA_HEADS, GLA_DK)
    o, s_last = gla_recurrence(q, k, v, log_a, s0)
    o = o * lax.rsqrt(jnp.mean(o * o, -1, keepdims=True) + LN_EPS) * w['gla_norm_g']
    o = o.reshape(n, t, GROUP_WIDTH) * jax.nn.silu(p['gla_r'])
    return o, s_last


def split_columns(proj, n, t):
    offsets = np.cumsum(IN_SIZES)[:-1].tolist()
    return jnp.split(proj[:, :D_IN].reshape(n, t, D_IN), offsets, axis=-1)


def mix_prompt(cols, w, sw, cmp_weights):
    (dq, dkv, iq, ik, iw, u, nq, ckv, skv, wkv, ng, gq, gk, gv, ga, gr) = cols
    n, t = dq.shape[:2]
    kv_shape = (n, t, KV_HEADS, 2, HEAD_DIM)
    o_dsa = dsa_prompt(dq, dkv, iq, ik, iw)
    o_s5, h_s5 = s5_prompt(u, jnp.zeros((n, 2, S5_WIDTH), f32), sw)
    o_nsa = nsa_prompt(nq, ng, ckv, skv, wkv, cmp_weights)
    o_gla, s_gla = gla_prompt(gq, gk, gv, ga, gr, w['gla_w_gate'], w['gla_b_gate'], w['gla_norm_g'])
    win_state = wkv[:, -min(WINDOW, t):].reshape((n, min(WINDOW, t)) + kv_shape[2:])
    mixers = [o.reshape(n * t, GROUP_WIDTH) for o in (o_dsa, o_s5, o_nsa, o_gla)]
    return mixers, (dkv.reshape(kv_shape), ik, ckv.reshape(kv_shape), skv.reshape(kv_shape), win_state,
                    h_s5.reshape(n, 2, S5_GROUPS, S5_STATE), s_gla)


def mix_sample(cols, w, sw, cmp_weights, pool_dsa_kv, pool_dsa_idx, pool_nsa_cmp, pool_nsa_slc, win_buf,
               h_s5, s_gla, page_table):
    n, t = cols[0].shape[:2]
    assert t == 1
    (dq, dkv, iq, ik, iw, u, nq, ckv, skv, wkv, ng, gq, gk, gv, ga, gr) = [c[:, 0] for c in cols]
    kv_shape = (n, t, KV_HEADS, 2, HEAD_DIM)
    o_dsa = dsa_step(dq, dkv, iq, ik, iw, pool_dsa_kv, pool_dsa_idx, page_table)
    o_s5, h_new = s5_step(u, h_s5.reshape(n, 2 * S5_WIDTH), sw)
    o_nsa = nsa_step(nq, ng, skv, wkv, pool_nsa_cmp, pool_nsa_slc, win_buf, page_table, cmp_weights)
    o_gla, s_new = gla_step(gq, gk, gv, ga, gr, w['gla_w_gate'], w['gla_b_gate'], w['gla_norm_g'], s_gla)
    win_new = jnp.concatenate([win_buf[:, 1:], wkv.reshape(kv_shape)], axis=1)
    return [o_dsa, o_s5, o_nsa, o_gla], (
        dkv.reshape(kv_shape), ik.reshape(n, t, IDX_DIM), ckv.reshape(kv_shape), skv.reshape(kv_shape),
        win_new, h_new.reshape(n, 2, S5_GROUPS, S5_STATE), s_new)


def kernel(x_prompt, x_sample, cache_dsa_kv, cache_dsa_idx, cache_nsa_cmp, cache_nsa_slc, cache_nsa_win, state_s5, state_gla, page_table, w_in, s5_a_re, s5_a_im, s5_b_re, s5_b_im, s5_c_re, s5_c_im, s5_d, s5_log_dt, s5_w_glu, s5_b_glu, nsa_cmp_pe, nsa_cmp_w1, nsa_cmp_w2, gla_w_gate, gla_b_gate, gla_norm_g, w_out, ln1_g, ln1_b, ffn_w_gate, ffn_w_up, ffn_w_down, ln2_g, ln2_b):
    np_, tp_ = x_prompt.shape[:2]
    ns_, ts_ = x_sample.shape[:2]
    hp = x_prompt.reshape(np_ * tp_, D_MODEL)
    hs = x_sample.reshape(ns_ * ts_, D_MODEL)
    outs_p = [[] for _ in range(7)]
    outs_s = [[] for _ in range(7)]
    for l in range(DEPTH):
        w = {'a_re': s5_a_re[l], 'a_im': s5_a_im[l], 'b_re': s5_b_re[l], 'b_im': s5_b_im[l],
             'c_re': s5_c_re[l], 'c_im': s5_c_im[l], 'd': s5_d[l], 'log_dt': s5_log_dt[l],
             'w_glu': s5_w_glu[l], 'b_glu': s5_b_glu[l],
             'cmp_pe': nsa_cmp_pe[l], 'cmp_w1': nsa_cmp_w1[l], 'cmp_w2': nsa_cmp_w2[l],
             'gla_w_gate': gla_w_gate[l], 'gla_b_gate': gla_b_gate[l], 'gla_norm_g': gla_norm_g[l]}
        sw = s5_discretize(w)
        cmp_weights = nsa_compress_weights(nsa_cmp_pe[l], nsa_cmp_w1[l], nsa_cmp_w2[l])
        w_in_l = jnp.pad(w_in[l], ((0, 0), (0, D_IN_PAD - D_IN))).astype(bf16)
        w_out_l = w_out[l].astype(bf16)
        wg_l, wu_l, wd_l = ffn_w_gate[l].astype(bf16), ffn_w_up[l].astype(bf16), ffn_w_down[l].astype(bf16)
        cols_p = split_columns(dense_matmul(hp, w_in_l), np_, tp_)
        cols_s = split_columns(dense_matmul(hs, w_in_l), ns_, ts_)
        mixers_p, st_p = mix_prompt(cols_p, w, sw, cmp_weights)
        mixers_s, st_s = mix_sample(cols_s, w, sw, cmp_weights, cache_dsa_kv[l], cache_dsa_idx[l], cache_nsa_cmp[l],
                                    cache_nsa_slc[l], cache_nsa_win[l], state_s5[l], state_gla[l], page_table)
        hp = outproj_ln(hp, mixers_p, w_out_l, ln1_g[l], ln1_b[l])
        hs = outproj_ln(hs, mixers_s, w_out_l, ln1_g[l], ln1_b[l])
        hp = ffn_ln(hp, wg_l, wu_l, wd_l, ln2_g[l], ln2_b[l])
        hs = ffn_ln(hs, wg_l, wu_l, wd_l, ln2_g[l], ln2_b[l])
        for lst, st in zip(outs_p, st_p):
            lst.append(st)
        for lst, st in zip(outs_s, st_s):
            lst.append(st)
    dsa_kv_p, dsa_idx_p, nsa_cmp_p, nsa_slc_p, nsa_win_p, s5_p, gla_p = [jnp.stack(a) for a in outs_p]
    dsa_kv_s, dsa_idx_s, nsa_cmp_s, nsa_slc_s, nsa_win_s, s5_s, gla_s = [jnp.stack(a) for a in outs_s]
    return (hp.reshape(np_, tp_, D_MODEL), hs.reshape(ns_, ts_, D_MODEL),
            dsa_kv_p, dsa_kv_s, dsa_idx_p, dsa_idx_s, nsa_cmp_p, nsa_cmp_s,
            nsa_slc_p, nsa_slc_s, nsa_win_p, nsa_win_s, s5_p, s5_s, gla_p, gla_s)
```

```python
import functools
import math

import jax
import jax.numpy as jnp
from jax import lax
import numpy as np
from jax.experimental import pallas as pl
from jax.experimental.pallas import tpu as pltpu

D_MODEL = 2048
DEPTH = 2
PAST_LEN = 2048
PAGE_SIZE = 128
GROUP_WIDTH = 512
HEAD_DIM = 64
N_HEADS = 8
KV_HEADS = 2
Q_PER_KV = 4
KV_ROW = 256
IDX_HEADS = 4
IDX_DIM = 64
DSA_TOPK = 256
S5_CH = 16
S5_GROUPS = 32
S5_STATE = 64
S5_WIDTH = S5_GROUPS * S5_STATE
CMP_BLOCK = 32
CMP_HIDDEN = 128
SLC_BLOCK = 64
SLC_TOPN = 16
WINDOW = 512
QBLOCK = 128
GLA_HEADS = 4
GLA_DK = 64
GLA_DV = 128
GLA_LOWRANK = 16
GLA_TAU = 16.0
GLA_CHUNK = 64
FFN_HIDDEN = 5632
DEEPNORM_ALPHA = (2 * DEPTH) ** 0.25
LN_EPS = 1e-5

IN_SIZES = (512, 256, 256, 64, 4, 512, 512, 256, 256, 256, 24, 256, 256, 512, 16, 512)
D_IN = sum(IN_SIZES)
D_IN_PAD = 4608

VMEM_LIMIT = 56 * 1024 * 1024
LANES = 128
KEY_CHUNK = 512
CMP_LANES = 128
S5_TIME_CHUNK = 256
ATT_SCALE = HEAD_DIM ** -0.5
MASKED = -1e30
INT_MIN = -2 ** 31
F32_TINY = float(np.finfo(np.float32).tiny)

bf16 = jnp.bfloat16
f32 = jnp.float32


IN_PAD_SIZES = tuple(-(-s // LANES) * LANES for s in IN_SIZES)
IN_PAD_OFFSETS = tuple(int(o) for o in np.cumsum((0,) + IN_PAD_SIZES[:-1]))


def pad_in_projection(w_in):
    parts, off = [], 0
    for size, size_pad in zip(IN_SIZES, IN_PAD_SIZES):
        parts.append(jnp.pad(w_in[:, :, off:off + size], ((0, 0), (0, 0), (0, size_pad - size))))
        off += size
    return jnp.concatenate(parts, axis=-1).astype(bf16)


def _in_projection_kernel(x_ref, w_ref, *o_refs):
    xb = x_ref[...].astype(bf16)
    for o_ref, off, size, size_pad in zip(o_refs, IN_PAD_OFFSETS, IN_SIZES, IN_PAD_SIZES):
        y = jnp.dot(xb, w_ref[:, off:off + size_pad], preferred_element_type=f32)
        o_ref[...] = y[:, :size]


def in_projection(x, w_pad, layer):
    m, k = x.shape
    tm = min(256, m)
    return pl.pallas_call(
        _in_projection_kernel,
        out_shape=tuple(jax.ShapeDtypeStruct((m, s), f32) for s in IN_SIZES),
        grid=(m // tm,),
        in_specs=[pl.BlockSpec((tm, k), lambda i: (i, 0)),
                  pl.BlockSpec((None, k, w_pad.shape[2]), lambda i: (layer, 0, 0),
                               pipeline_mode=pl.Buffered(1))],
        out_specs=tuple(pl.BlockSpec((tm, s), lambda i: (i, 0)) for s in IN_SIZES),
        compiler_params=pltpu.CompilerParams(
            dimension_semantics=("parallel",), vmem_limit_bytes=VMEM_LIMIT),
        name="in_projection",
    )(x, w_pad)


def _layer_norm_rows(z, g, b):
    mu = jnp.mean(z, axis=-1, keepdims=True)
    zc = z - mu
    var = jnp.mean(zc * zc, axis=-1, keepdims=True)
    return zc * lax.rsqrt(var + LN_EPS) * g + b


def _outproj_ln_kernel(x_ref, m0_ref, m1_ref, m2_ref, m3_ref, w_ref, g_ref, b_ref, o_ref):
    y = DEEPNORM_ALPHA * x_ref[...]
    for j, m_ref in enumerate((m0_ref, m1_ref, m2_ref, m3_ref)):
        y = y + jnp.dot(m_ref[...].astype(bf16), w_ref[j * GROUP_WIDTH:(j + 1) * GROUP_WIDTH, :],
                        preferred_element_type=f32)
    o_ref[...] = _layer_norm_rows(y, g_ref[...], b_ref[...])


def outproj_ln(x, mixers, w_bf16, layer, g, b):
    m, d = x.shape
    tm = min(256, m)
    mix_spec = pl.BlockSpec((tm, GROUP_WIDTH), lambda i: (i, 0))
    return pl.pallas_call(
        _outproj_ln_kernel,
        out_shape=jax.ShapeDtypeStruct((m, d), f32),
        grid=(m // tm,),
        in_specs=[pl.BlockSpec((tm, d), lambda i: (i, 0)),
                  mix_spec, mix_spec, mix_spec, mix_spec,
                  pl.BlockSpec((None, d, d), lambda i: (layer, 0, 0)),
                  pl.BlockSpec((1, d), lambda i: (0, 0)),
                  pl.BlockSpec((1, d), lambda i: (0, 0))],
        out_specs=pl.BlockSpec((tm, d), lambda i: (i, 0)),
        compiler_params=pltpu.CompilerParams(
            dimension_semantics=("parallel",), vmem_limit_bytes=VMEM_LIMIT),
        name="outproj_ln",
    )(x, *mixers, w_bf16, g.reshape(1, d), b.reshape(1, d))


def _ffn_ln_kernel(h_ref, wg_ref, wu_ref, wd_ref, g_ref, b_ref, o_ref, acc_ref):
    f = pl.program_id(1)

    @pl.when(f == 0)
    def _():
        acc_ref[...] = jnp.zeros_like(acc_ref)

    hb = h_ref[...].astype(bf16)
    a = jnp.dot(hb, wg_ref[...], preferred_element_type=f32)
    u = jnp.dot(hb, wu_ref[...], preferred_element_type=f32)
    act = (a * jax.nn.sigmoid(a) * u).astype(bf16)
    acc_ref[...] += jnp.dot(act, wd_ref[...], preferred_element_type=f32)

    @pl.when(f == pl.num_programs(1) - 1)
    def _():
        z = DEEPNORM_ALPHA * h_ref[...] + acc_ref[...]
        o_ref[...] = _layer_norm_rows(z, g_ref[...], b_ref[...])


def ffn_ln(h, wg, wu, wd, layer, g, b, tf=512):
    m, d = h.shape
    fh = wg.shape[2]
    tm = min(512, m)
    return pl.pallas_call(
        _ffn_ln_kernel,
        out_shape=jax.ShapeDtypeStruct((m, d), f32),
        grid=(m // tm, fh // tf),
        in_specs=[pl.BlockSpec((tm, d), lambda i, f: (i, 0)),
                  pl.BlockSpec((None, d, tf), lambda i, f: (layer, 0, f)),
                  pl.BlockSpec((None, d, tf), lambda i, f: (layer, 0, f)),
                  pl.BlockSpec((None, tf, d), lambda i, f: (layer, f, 0)),
                  pl.BlockSpec((1, d), lambda i, f: (0, 0)),
                  pl.BlockSpec((1, d), lambda i, f: (0, 0))],
        out_specs=pl.BlockSpec((tm, d), lambda i, f: (i, 0)),
        scratch_shapes=[pltpu.VMEM((tm, d), f32)],
        compiler_params=pltpu.CompilerParams(
            dimension_semantics=("parallel", "arbitrary"), vmem_limit_bytes=VMEM_LIMIT),
        name="ffn_ln",
    )(h, wg, wu, wd, g.reshape(1, d), b.reshape(1, d))


def _tile_rows(x, reps):
    return jnp.concatenate([x] * reps, axis=0)


def _masked_flash(qg, kt_at, v_at, c_lo, c_hi, mask_at):
    rows = qg.shape[0]

    def body(c, carry):
        m, l, acc = carry
        s = jnp.dot(qg, kt_at(c), preferred_element_type=f32) * ATT_SCALE
        keep = _tile_rows(mask_at(c), rows // QBLOCK) > 0.5
        s = jnp.where(keep, s, MASKED)
        m_new = jnp.maximum(m, jnp.max(s, axis=1, keepdims=True))
        p = jnp.where(keep, jnp.exp(s - m_new), 0.0)
        alpha = jnp.exp(m - m_new)
        l = alpha * l + jnp.sum(p, axis=1, keepdims=True)
        acc = alpha * acc + jnp.dot(p.astype(bf16), v_at(c), preferred_element_type=f32)
        return m_new, l, acc

    init = (jnp.full((rows, 1), MASKED, f32), jnp.zeros((rows, 1), f32),
            jnp.zeros((rows, HEAD_DIM), f32))
    _, l, acc = lax.fori_loop(c_lo, c_hi, body, init)
    return acc / l


def _sortable_key(x):
    bits = pltpu.bitcast(x, jnp.int32)
    return bits ^ (jnp.right_shift(bits, 31) & jnp.int32(0x7FFFFFFF))


def _topk_threshold(count, rows, topk, idx_bits):
    topk = float(topk)
    cnt_nonneg = count(lambda k, i: k >= 0)
    thr0 = jnp.where(cnt_nonneg >= topk, 0, INT_MIN).astype(jnp.int32)

    def bit_body(b, thr):
        cand = thr + jnp.left_shift(jnp.int32(1), 30 - b)
        return jnp.where(count(lambda k, i: k >= cand) >= topk, cand, thr)

    thr = lax.fori_loop(0, 31, bit_body, thr0)
    need = topk - count(lambda k, i: k > thr)

    def cut_body(b, cut):
        cand = cut + jnp.left_shift(jnp.int32(1), idx_bits - 1 - b)
        return jnp.where(count(lambda k, i: (k == thr) & (i < cand)) < need, cand, cut)

    cut = lax.fori_loop(0, idx_bits, cut_body, jnp.zeros((rows, 1), jnp.int32))
    return thr, cut


def _select_blocks(imp, q_pos, lane):
    ratio = SLC_BLOCK // CMP_BLOCK
    on_block_lane = (lane & (ratio - 1)) == 0
    blk = jnp.right_shift(lane, 1)
    cur = jnp.right_shift(q_pos, 6)
    future = blk > cur
    forced = (blk == 0) | (blk == cur) | (blk == cur - 1)
    imp = imp + pltpu.roll(imp, CMP_LANES - 1, 1)
    score = jnp.where(future, -jnp.inf, jnp.where(forced, jnp.inf, imp))
    score = jnp.where(on_block_lane, score, -jnp.inf)
    rank = jnp.zeros(imp.shape, f32)
    for b in range(CMP_LANES // ratio):
        col = score[:, ratio * b:ratio * b + 1]
        ahead = (col > score) | ((col == score) & (ratio * b < lane))
        rank = rank + jnp.where(ahead, 1.0, 0.0)
    return jnp.where((rank < float(SLC_TOPN)) & on_block_lane, 1.0, 0.0)


def _dsa_prompt_kernel(iq_ref, iw_ref, ikt_ref, q_ref, kt_ref, v_ref, o_ref, keys_ref, sel_ref,
                       *, idx_bits):
    i = pl.program_id(1)
    nch = (i * QBLOCK + QBLOCK + KEY_CHUNK - 1) // KEY_CHUNK
    q_pos = i * QBLOCK + lax.broadcasted_iota(jnp.int32, (QBLOCK, 1), 0)
    lane = lax.broadcasted_iota(jnp.int32, (QBLOCK, KEY_CHUNK), 1)

    iq = iq_ref[...]
    iq_stack = jnp.concatenate(
        [iq[:, h * IDX_DIM:(h + 1) * IDX_DIM] for h in range(IDX_HEADS)], axis=0).astype(bf16)
    iw = iw_ref[...] * IDX_HEADS ** -0.5

    def score_body(c, carry):
        dots = jnp.dot(iq_stack, ikt_ref[c], preferred_element_type=f32) * IDX_DIM ** -0.5
        acc = jnp.zeros((QBLOCK, KEY_CHUNK), f32)
        for h in range(IDX_HEADS):
            acc = acc + iw[:, h:h + 1] * jnp.maximum(dots[h * QBLOCK:(h + 1) * QBLOCK], 0.0)
        causal = c * KEY_CHUNK + lane <= q_pos
        keys_ref[c] = jnp.where(causal, _sortable_key(acc), INT_MIN)
        return carry

    lax.fori_loop(0, nch, score_body, 0)

    def count(pred):
        def body(c, acc):
            hit = jnp.where(pred(keys_ref[c], c * KEY_CHUNK + lane), 1.0, 0.0)
            part = hit[:, :LANES]
            for j in range(1, KEY_CHUNK // LANES):
                part = part + hit[:, j * LANES:(j + 1) * LANES]
            return acc + part
        acc = lax.fori_loop(0, nch, body, jnp.zeros((QBLOCK, LANES), f32))
        return jnp.sum(acc, axis=1, keepdims=True)

    thr, cut = _topk_threshold(count, QBLOCK, DSA_TOPK, idx_bits)

    def sel_body(c, carry):
        k = keys_ref[c]
        kidx = c * KEY_CHUNK + lane
        chosen = ((k > thr) | ((k == thr) & (kidx <= cut))) & (kidx <= q_pos)
        sel_ref[c] = jnp.where(chosen, 1.0, 0.0)
        return carry

    lax.fori_loop(0, nch, sel_body, 0)

    for g in range(KV_HEADS):
        o = _masked_flash(q_ref[g].astype(bf16), lambda c: kt_ref[g, c], lambda c: v_ref[g, c],
                          0, nch, lambda c: sel_ref[c])
        for r in range(Q_PER_KV):
            h = g * Q_PER_KV + r
            o_ref[:, h * HEAD_DIM:(h + 1) * HEAD_DIM] = o[r * QBLOCK:(r + 1) * QBLOCK]


def _stack_query_heads(q, n, t):
    q = q.reshape(n, t // QBLOCK, QBLOCK, KV_HEADS, Q_PER_KV, HEAD_DIM)
    q = q.transpose(0, 3, 1, 4, 2, 5)
    return q.reshape(n, KV_HEADS, t // QBLOCK, Q_PER_KV * QBLOCK, HEAD_DIM)


def _chunked_kv(kv, n, t, chunk):
    kv = kv.reshape(n, t // chunk, chunk, KV_HEADS, 2, HEAD_DIM).astype(bf16)
    kt = kv[:, :, :, :, 0, :].transpose(0, 3, 1, 4, 2)
    v = kv[:, :, :, :, 1, :].transpose(0, 3, 1, 2, 4)
    return kt, v


def dsa_prompt(dq, dkv, iq, ik, iw):
    n, t = dq.shape[:2]
    nchunks = t // KEY_CHUNK
    idx_bits = int(math.log2(t))
    assert 2 ** idx_bits == t and t % KEY_CHUNK == 0 and min(DSA_TOPK, t // 4) == DSA_TOPK
    ikt = ik.reshape(n, nchunks, KEY_CHUNK, IDX_DIM).transpose(0, 1, 3, 2).astype(bf16)
    qs = _stack_query_heads(dq, n, t)
    kt, v = _chunked_kv(dkv, n, t, KEY_CHUNK)
    return pl.pallas_call(
        functools.partial(_dsa_prompt_kernel, idx_bits=idx_bits),
        out_shape=jax.ShapeDtypeStruct((n, t, GROUP_WIDTH), f32),
        grid=(n, t // QBLOCK),
        in_specs=[
            pl.BlockSpec((None, QBLOCK, IDX_HEADS * IDX_DIM), lambda b, i: (b, i, 0)),
            pl.BlockSpec((None, QBLOCK, IDX_HEADS), lambda b, i: (b, i, 0)),
            pl.BlockSpec((None, nchunks, IDX_DIM, KEY_CHUNK), lambda b, i: (b, 0, 0, 0)),
            pl.BlockSpec((None, KV_HEADS, None, Q_PER_KV * QBLOCK, HEAD_DIM), lambda b, i: (b, 0, i, 0, 0)),
            pl.BlockSpec((None, KV_HEADS, nchunks, HEAD_DIM, KEY_CHUNK), lambda b, i: (b, 0, 0, 0, 0)),
            pl.BlockSpec((None, KV_HEADS, nchunks, KEY_CHUNK, HEAD_DIM), lambda b, i: (b, 0, 0, 0, 0)),
        ],
        out_specs=pl.BlockSpec((None, QBLOCK, GROUP_WIDTH), lambda b, i: (b, i, 0)),
        scratch_shapes=[pltpu.VMEM((nchunks, QBLOCK, KEY_CHUNK), jnp.int32),
                        pltpu.VMEM((nchunks, QBLOCK, KEY_CHUNK), f32)],
        compiler_params=pltpu.CompilerParams(
            dimension_semantics=("parallel", "arbitrary"), vmem_limit_bytes=VMEM_LIMIT),
        name="dsa_prompt",
    )(iq, iw, ikt, qs, kt, v)


def _nsa_compress_kernel(x_ref, pe_ref, w1_ref, w2_ref, o_ref, hid_ref):
    k = pl.program_id(0)

    @pl.when(k == 0)
    def _():
        hid_ref[...] = jnp.zeros_like(hid_ref)

    hid_ref[...] += jnp.dot((x_ref[...] + pe_ref[...]).astype(bf16), w1_ref[...],
                            preferred_element_type=f32)

    @pl.when(k == pl.num_programs(0) - 1)
    def _():
        hid = jax.nn.gelu(hid_ref[...]).astype(bf16)
        o_ref[...] = jnp.dot(hid, w2_ref[...], preferred_element_type=f32)


def nsa_compress_weights(cmp_pe, cmp_w1, cmp_w2):
    w1 = cmp_w1.reshape(2, CMP_BLOCK, HEAD_DIM, CMP_HIDDEN)
    eye_g = jnp.eye(KV_HEADS, dtype=cmp_w1.dtype)
    eye_j = jnp.eye(2, dtype=cmp_w1.dtype)
    w1_big = jnp.einsum('jtde,gh,jk->tgjdhke', w1, eye_g, eye_j)
    w1_big = w1_big.reshape(CMP_BLOCK * KV_ROW, KV_HEADS * 2 * CMP_HIDDEN).astype(bf16)
    w2_big = jnp.einsum('jed,gh,jk->gjehkd', cmp_w2, eye_g, eye_j)
    w2_big = w2_big.reshape(KV_HEADS * 2 * CMP_HIDDEN, KV_ROW).astype(bf16)
    pe_row = jnp.broadcast_to(cmp_pe[:, None], (CMP_BLOCK, KV_HEADS, 2, HEAD_DIM))
    return pe_row.reshape(1, CMP_BLOCK * KV_ROW), w1_big, w2_big


def nsa_compress_rows(x, pe_row, w1_big, w2_big, tk=2048):
    m, kdim = x.shape
    hw = w1_big.shape[1]
    return pl.pallas_call(
        _nsa_compress_kernel,
        out_shape=jax.ShapeDtypeStruct((m, KV_ROW), f32),
        grid=(kdim // tk,),
        in_specs=[pl.BlockSpec((m, tk), lambda k: (0, k)),
                  pl.BlockSpec((1, tk), lambda k: (0, k)),
                  pl.BlockSpec((tk, hw), lambda k: (k, 0)),
                  pl.BlockSpec((hw, KV_ROW), lambda k: (0, 0))],
        out_specs=pl.BlockSpec((m, KV_ROW), lambda k: (0, 0)),
        scratch_shapes=[pltpu.VMEM((m, hw), f32)],
        compiler_params=pltpu.CompilerParams(
            dimension_semantics=("arbitrary",), vmem_limit_bytes=VMEM_LIMIT),
        name="nsa_compress",
    )(x, pe_row, w1_big, w2_big)


def _nsa_prompt_kernel(gate_ref, q_ref, ckt_ref, cv_ref, skt_ref, sv_ref, wkt_ref, wv_ref, o_ref):
    i = pl.program_id(1)
    nch = (i * QBLOCK + QBLOCK + KEY_CHUNK - 1) // KEY_CHUNK
    q_pos = i * QBLOCK + lax.broadcasted_iota(jnp.int32, (QBLOCK, 1), 0)
    lane = lax.broadcasted_iota(jnp.int32, (QBLOCK, CMP_LANES), 1)
    lane_k = lax.broadcasted_iota(jnp.int32, (QBLOCK, KEY_CHUNK), 1)
    expand_row = lax.broadcasted_iota(jnp.int32, (CMP_LANES, KEY_CHUNK), 0)
    expand_col = lax.broadcasted_iota(jnp.int32, (CMP_LANES, KEY_CHUNK), 1)
    gate = jax.nn.sigmoid(gate_ref[...])
    ratio = SLC_BLOCK // CMP_BLOCK
    c_vis =_tile_rows(jnp.where((lane + 1) * CMP_BLOCK - 1 <= q_pos, 1.0, 0.0), Q_PER_KV) > 0.5

    for g in range(KV_HEADS):
        qg = q_ref[g].astype(bf16)
        s = jnp.dot(qg, ckt_ref[g], preferred_element_type=f32) * ATT_SCALE
        s = jnp.where(c_vis, s, -jnp.inf)
        m = jnp.max(s, axis=1, keepdims=True)
        m = jnp.where(m == -jnp.inf, 0.0, m)
        p = jnp.exp(s - m)
        p = p / jnp.maximum(jnp.sum(p, axis=1, keepdims=True), F32_TINY)
        o_cmp = jnp.dot(p.astype(bf16), cv_ref[g], preferred_element_type=f32)
        imp = p[0:QBLOCK]
        for r in range(1, Q_PER_KV):
            imp = imp + p[r * QBLOCK:(r + 1) * QBLOCK]
        sel_blk = _select_blocks(imp, q_pos, lane).astype(bf16)

        def slc_mask(c):
            kidx = c * KEY_CHUNK + lane_k
            expand = jnp.where(
                expand_row == ratio * jnp.right_shift(c * KEY_CHUNK + expand_col, 6), 1.0, 0.0).astype(bf16)
            picked = jnp.dot(sel_blk, expand, preferred_element_type=f32)
            return jnp.where((picked > 0.5) & (kidx <= q_pos), 1.0, 0.0)

        o_slc = _masked_flash(qg, lambda c: skt_ref[g, c], lambda c: sv_ref[g, c], 0, nch, slc_mask)

        def win_mask(c):
            kidx = c * QBLOCK + lane
            return jnp.where((kidx <= q_pos) & (kidx > q_pos - WINDOW), 1.0, 0.0)

        o_win = _masked_flash(qg, lambda c: wkt_ref[g, c], lambda c: wv_ref[g, c],
                              jnp.maximum(i - WINDOW // QBLOCK, 0), i + 1, win_mask)
        for r in range(Q_PER_KV):
            h = g * Q_PER_KV + r
            rows = slice(r * QBLOCK, (r + 1) * QBLOCK)
            o_ref[:, h * HEAD_DIM:(h + 1) * HEAD_DIM] = (
                gate[:, 3 * h:3 * h + 1] * o_cmp[rows]
                + gate[:, 3 * h + 1:3 * h + 2] * o_slc[rows]
                + gate[:, 3 * h + 2:3 * h + 3] * o_win[rows])


def nsa_prompt(nq, ng, ckv, skv, wkv, cmp_weights):
    n, t = nq.shape[:2]
    nc = t // CMP_BLOCK
    assert nc <= CMP_LANES and t % KEY_CHUNK == 0 and SLC_BLOCK == 64 and SLC_BLOCK // CMP_BLOCK == 2
    cmp_rows = nsa_compress_rows(ckv.reshape(n * nc, CMP_BLOCK * KV_ROW), *cmp_weights)
    cmp_rows = cmp_rows.reshape(n, nc, KV_HEADS, 2, HEAD_DIM)
    cmp_rows = jnp.pad(cmp_rows, ((0, 0), (0, CMP_LANES - nc), (0, 0), (0, 0), (0, 0))).astype(bf16)
    ckt = cmp_rows[:, :, :, 0, :].transpose(0, 2, 3, 1)
    cv = cmp_rows[:, :, :, 1, :].transpose(0, 2, 1, 3)
    qs = _stack_query_heads(nq, n, t)
    skt, sv = _chunked_kv(skv, n, t, KEY_CHUNK)
    wkt, wv = _chunked_kv(wkv, n, t, QBLOCK)
    nk, nw = t // KEY_CHUNK, t // QBLOCK
    return pl.pallas_call(
        _nsa_prompt_kernel,
        out_shape=jax.ShapeDtypeStruct((n, t, GROUP_WIDTH), f32),
        grid=(n, t // QBLOCK),
        in_specs=[
            pl.BlockSpec((None, QBLOCK, 3 * N_HEADS), lambda b, i: (b, i, 0)),
            pl.BlockSpec((None, KV_HEADS, None, Q_PER_KV * QBLOCK, HEAD_DIM), lambda b, i: (b, 0, i, 0, 0)),
            pl.BlockSpec((None, KV_HEADS, HEAD_DIM, CMP_LANES), lambda b, i: (b, 0, 0, 0)),
            pl.BlockSpec((None, KV_HEADS, CMP_LANES, HEAD_DIM), lambda b, i: (b, 0, 0, 0)),
            pl.BlockSpec((None, KV_HEADS, nk, HEAD_DIM, KEY_CHUNK), lambda b, i: (b, 0, 0, 0, 0)),
            pl.BlockSpec((None, KV_HEADS, nk, KEY_CHUNK, HEAD_DIM), lambda b, i: (b, 0, 0, 0, 0)),
            pl.BlockSpec((None, KV_HEADS, nw, HEAD_DIM, QBLOCK), lambda b, i: (b, 0, 0, 0, 0)),
            pl.BlockSpec((None, KV_HEADS, nw, QBLOCK, HEAD_DIM), lambda b, i: (b, 0, 0, 0, 0)),
        ],
        out_specs=pl.BlockSpec((None, QBLOCK, GROUP_WIDTH), lambda b, i: (b, i, 0)),
        compiler_params=pltpu.CompilerParams(
            dimension_semantics=("parallel", "arbitrary"), vmem_limit_bytes=VMEM_LIMIT),
        name="nsa_prompt",
    )(ng, qs, ckt, cv, skt, sv, wkt, wv)


def s5_discretize(w):
    a_re, a_im = w['a_re'], w['a_im']
    dt = jnp.exp(w['log_dt'])[:, None]
    mag = jnp.exp(a_re * dt)
    ab_re, ab_im = mag * jnp.cos(a_im * dt), mag * jnp.sin(a_im * dt)
    den = a_re * a_re + a_im * a_im
    nr, ni = ab_re - 1.0, ab_im
    f_re = (nr * a_re + ni * a_im) / den
    f_im = (ni * a_re - nr * a_im) / den
    b_re, b_im = w['b_re'], w['b_im']
    bb_re = f_re[..., None] * b_re - f_im[..., None] * b_im
    bb_im = f_re[..., None] * b_im + f_im[..., None] * b_re
    eye = jnp.eye(S5_GROUPS, dtype=f32)

    def in_map(bb):
        return jnp.einsum('gsc,gh->gchs', bb, eye).reshape(GROUP_WIDTH, S5_WIDTH).astype(bf16)

    def out_map(cc):
        return jnp.einsum('gcs,gh->gshc', cc, eye).reshape(S5_WIDTH, GROUP_WIDTH).astype(bf16)

    return dict(a_re=ab_re.reshape(1, S5_WIDTH), a_im=ab_im.reshape(1, S5_WIDTH),
                b_re=in_map(bb_re), b_im=in_map(bb_im),
                c_re=out_map(w['c_re']), c_im=out_map(w['c_im']),
                d=w['d'].reshape(1, GROUP_WIDTH), w_glu=w['w_glu'].astype(bf16),
                b_glu=w['b_glu'].reshape(1, GROUP_WIDTH))


def _s5_prompt_kernel(u_ref, h0_ref, are_ref, aim_ref, bre_ref, bim_ref, cre_ref, cim_ref, d_ref,
                      wglu_ref, bglu_ref, o_ref, hlast_ref, state_ref, bure_ref, buim_ref, hre_ref, him_ref):
    j = pl.program_id(1)

    @pl.when(j == 0)
    def _():
        state_ref[...] = h0_ref[...]

    u = u_ref[...]
    ub = u.astype(bf16)
    bure_ref[...] = jnp.dot(ub, bre_ref[...], preferred_element_type=f32)
    buim_ref[...] = jnp.dot(ub, bim_ref[...], preferred_element_type=f32)
    a_re, a_im = are_ref[...], aim_ref[...]

    def step(t, carry):
        h_re, h_im = carry
        row = pl.ds(t, 1)
        n_re = a_re * h_re - a_im * h_im + bure_ref[row, :]
        n_im = a_re * h_im + a_im * h_re + buim_ref[row, :]
        hre_ref[row, :] = n_re
        him_ref[row, :] = n_im
        return n_re, n_im

    h_re, h_im = lax.fori_loop(0, u.shape[0], step, (state_ref[0:1, :], state_ref[1:2, :]), unroll=8)
    state_ref[0:1, :] = h_re
    state_ref[1:2, :] = h_im
    hlast_ref[...] = state_ref[...]
    y = (jnp.dot(hre_ref[...].astype(bf16), cre_ref[...], preferred_element_type=f32)
         - jnp.dot(him_ref[...].astype(bf16), cim_ref[...], preferred_element_type=f32)
         + d_ref[...] * u)
    y = jax.nn.gelu(y)
    o_ref[...] = y * jax.nn.sigmoid(
        jnp.dot(y.astype(bf16), wglu_ref[...], preferred_element_type=f32) + bglu_ref[...])


def s5_prompt(u, h0, sw):
    n, t = u.shape[:2]
    tc = min(S5_TIME_CHUNK, t)
    const = lambda shape: pl.BlockSpec(shape, lambda b, j: (0,) * len(shape))
    return pl.pallas_call(
        _s5_prompt_kernel,
        out_shape=(jax.ShapeDtypeStruct((n, t, GROUP_WIDTH), f32),
                   jax.ShapeDtypeStruct((n, 2, S5_WIDTH), f32)),
        grid=(n, t // tc),
        in_specs=[pl.BlockSpec((None, tc, GROUP_WIDTH), lambda b, j: (b, j, 0)),
                  pl.BlockSpec((None, 2, S5_WIDTH), lambda b, j: (b, 0, 0)),
                  const((1, S5_WIDTH)), const((1, S5_WIDTH)),
                  const((GROUP_WIDTH, S5_WIDTH)), const((GROUP_WIDTH, S5_WIDTH)),
                  const((S5_WIDTH, GROUP_WIDTH)), const((S5_WIDTH, GROUP_WIDTH)),
                  const((1, GROUP_WIDTH)), const((GROUP_WIDTH, GROUP_WIDTH)), const((1, GROUP_WIDTH))],
        out_specs=(pl.BlockSpec((None, tc, GROUP_WIDTH), lambda b, j: (b, j, 0)),
                   pl.BlockSpec((None, 2, S5_WIDTH), lambda b, j: (b, 0, 0))),
        scratch_shapes=[pltpu.VMEM((2, S5_WIDTH), f32)] + [pltpu.VMEM((tc, S5_WIDTH), f32)] * 4,
        compiler_params=pltpu.CompilerParams(
            dimension_semantics=("parallel", "arbitrary"), vmem_limit_bytes=VMEM_LIMIT),
        name="s5_prompt",
    )(u, h0, sw['a_re'], sw['a_im'], sw['b_re'], sw['b_im'], sw['c_re'], sw['c_im'], sw['d'],
      sw['w_glu'], sw['b_glu'])


def _s5_step_kernel(u_ref, h0_ref, are_ref, aim_ref, bre_ref, bim_ref, cre_ref, cim_ref, d_ref,
                    wglu_ref, bglu_ref, o_ref, h_ref):
    u = u_ref[...]
    ub = u.astype(bf16)
    h0_re, h0_im = h0_ref[:, :S5_WIDTH], h0_ref[:, S5_WIDTH:]
    a_re, a_im = are_ref[...], aim_ref[...]
    h_re = a_re * h0_re - a_im * h0_im + jnp.dot(ub, bre_ref[...], preferred_element_type=f32)
    h_im = a_re * h0_im + a_im * h0_re + jnp.dot(ub, bim_ref[...], preferred_element_type=f32)
    h_ref[:, :S5_WIDTH] = h_re
    h_ref[:, S5_WIDTH:] = h_im
    y = (jnp.dot(h_re.astype(bf16), cre_ref[...], preferred_element_type=f32)
         - jnp.dot(h_im.astype(bf16), cim_ref[...], preferred_element_type=f32)
         + d_ref[...] * u)
    y = jax.nn.gelu(y)
    o_ref[...] = y * jax.nn.sigmoid(
        jnp.dot(y.astype(bf16), wglu_ref[...], preferred_element_type=f32) + bglu_ref[...])


def s5_step(u, h0, sw):
    n = u.shape[0]
    return pl.pallas_call(
        _s5_step_kernel,
        out_shape=(jax.ShapeDtypeStruct((n, GROUP_WIDTH), f32),
                   jax.ShapeDtypeStruct((n, 2 * S5_WIDTH), f32)),
        compiler_params=pltpu.CompilerParams(vmem_limit_bytes=VMEM_LIMIT),
        name="s5_step",
    )(u, h0, sw['a_re'], sw['a_im'], sw['b_re'], sw['b_im'], sw['c_re'], sw['c_im'], sw['d'],
      sw['w_glu'], sw['b_glu'])


GLA_SUB = 16
GLA_QK = GLA_HEADS * GLA_DK
HIGHEST = lax.Precision.HIGHEST
_NT = (((1,), (1,)), ((), ()))
_TN = (((0,), (0,)), ((), ()))


def _log_decay(ga, wgate_ref, bgate_ref):
    z = jnp.dot(ga, wgate_ref[...], preferred_element_type=f32, precision=HIGHEST) + bgate_ref[...]
    return jax.nn.log_sigmoid(z) / GLA_TAU


def _head_sum_matrix():
    r = lax.broadcasted_iota(jnp.int32, (GLA_QK, GROUP_WIDTH), 0) // GLA_DK
    c = lax.broadcasted_iota(jnp.int32, (GLA_QK, GROUP_WIDTH), 1) // GLA_DV
    return jnp.where(r == c, 1.0, 0.0)


def _gla_finish(o, gr, normg_ref):
    outs = []
    for h in range(GLA_HEADS):
        oh = o[:, h * GLA_DV:(h + 1) * GLA_DV]
        outs.append(oh * lax.rsqrt(jnp.mean(oh * oh, axis=-1, keepdims=True) + LN_EPS) * normg_ref[...])
    return jnp.concatenate(outs, axis=1) * (gr * jax.nn.sigmoid(gr))


def _gla_prompt_kernel(q_ref, k_ref, v_ref, ga_ref, gr_ref, wgate_ref, bgate_ref, normg_ref, s0_ref,
                       o_ref, slast_ref, st_ref, oi_ref):
    j = pl.program_id(1)

    @pl.when(j == 0)
    def _():
        st_ref[...] = s0_ref[...]

    rows = q_ref.shape[0]
    nsub = rows // GLA_SUB
    q = q_ref[...] * GLA_DK ** -0.5
    k = k_ref[...]
    v = v_ref[...]
    log_a = _log_decay(ga_ref[...], wgate_ref, bgate_ref)
    ri = lax.broadcasted_iota(jnp.int32, (rows, rows), 0)
    ci = lax.broadcasted_iota(jnp.int32, (rows, rows), 1)
    tri = jnp.where((ri // GLA_SUB == ci // GLA_SUB) & (ci <= ri), 1.0, 0.0)
    b = jnp.dot(tri, log_a, preferred_element_type=f32, precision=HIGHEST)
    row_in_sub = lax.broadcasted_iota(jnp.int32, (rows, 1), 0) % GLA_SUB
    head_sum = _head_sum_matrix()

    def sub_row(x, jj):
        x3 = x.reshape(nsub, GLA_SUB, x.shape[1])
        return jnp.broadcast_to(x3[:, jj:jj + 1, :], x3.shape).reshape(x.shape)

    o = jnp.zeros((rows, GROUP_WIDTH), f32)
    for jj in range(GLA_SUB):
        decay = jnp.exp(jnp.minimum(b - sub_row(b, jj), 0.0))
        prod = jnp.where(row_in_sub >= jj, q * sub_row(k, jj) * decay, 0.0)
        att = jnp.dot(prod, head_sum, preferred_element_type=f32, precision=HIGHEST)
        o = o + att * sub_row(v, jj)

    for i in range(nsub):
        blk = slice(i * GLA_SUB, (i + 1) * GLA_SUB)
        b_blk = b[blk]
        b_end = b[(i + 1) * GLA_SUB - 1:(i + 1) * GLA_SUB]
        q_blk = (q[blk] * jnp.exp(b_blk)).astype(bf16)
        k_blk = (k[blk] * jnp.exp(b_end - b_blk)).astype(bf16)
        v_blk = v[blk].astype(bf16)
        a_end = jnp.exp(b_end)
        for h in range(GLA_HEADS):
            ks = slice(h * GLA_DK, (h + 1) * GLA_DK)
            vs = slice(h * GLA_DV, (h + 1) * GLA_DV)
            st = st_ref[h]
            oi_ref[blk, vs] = lax.dot_general(q_blk[:, ks], st.astype(bf16), _NT, preferred_element_type=f32)
            st_ref[h] = st * a_end[:, ks] + lax.dot_general(v_blk[:, vs], k_blk[:, ks], _TN,
                                                            preferred_element_type=f32)

    o_ref[...] = _gla_finish(o + oi_ref[...], gr_ref[...], normg_ref)
    slast_ref[...] = st_ref[...]


def gla_prompt(gq, gk, gv, ga, gr, w_gate, b_gate, norm_g):
    n, t = gq.shape[:2]
    tc = min(GLA_CHUNK, t)
    rows = lambda width: pl.BlockSpec((None, tc, width), lambda b, j: (b, j, 0))
    const = lambda shape: pl.BlockSpec(shape, lambda b, j: (0,) * len(shape))
    state = pl.BlockSpec((None, GLA_HEADS, GLA_DV, GLA_DK), lambda b, j: (b, 0, 0, 0))
    o, s_t = pl.pallas_call(
        _gla_prompt_kernel,
        out_shape=(jax.ShapeDtypeStruct((n, t, GROUP_WIDTH), f32),
                   jax.ShapeDtypeStruct((n, GLA_HEADS, GLA_DV, GLA_DK), f32)),
        grid=(n, t // tc),
        in_specs=[rows(GLA_QK), rows(GLA_QK), rows(GROUP_WIDTH), rows(GLA_LOWRANK), rows(GROUP_WIDTH),
                  const((GLA_LOWRANK, GLA_QK)), const((1, GLA_QK)), const((1, GLA_DV)), state],
        out_specs=(rows(GROUP_WIDTH), state),
        scratch_shapes=[pltpu.VMEM((GLA_HEADS, GLA_DV, GLA_DK), f32), pltpu.VMEM((tc, GROUP_WIDTH), f32)],
        compiler_params=pltpu.CompilerParams(
            dimension_semantics=("parallel", "arbitrary"), vmem_limit_bytes=VMEM_LIMIT),
        name="gla_prompt",
    )(gq, gk, gv, ga, gr, w_gate, b_gate.reshape(1, GLA_QK), norm_g.reshape(1, GLA_DV),
      jnp.zeros((n, GLA_HEADS, GLA_DV, GLA_DK), f32))
    return o, s_t.transpose(0, 1, 3, 2)


GLA_STEP_SEQS = 8


def _gla_step_kernel(q_ref, k_ref, v_ref, ga_ref, gr_ref, wgate_ref, bgate_ref, normg_ref, s_ref,
                     o_ref, snew_ref):
    q = q_ref[...] * GLA_DK ** -0.5
    k = k_ref[...]
    v = v_ref[...]
    a = jnp.exp(_log_decay(ga_ref[...], wgate_ref, bgate_ref))
    qa = q * a
    seq = lax.broadcasted_iota(jnp.int32, (GLA_STEP_SEQS, 1), 0)
    o = jnp.dot(q * k, _head_sum_matrix(), preferred_element_type=f32, precision=HIGHEST) * v
    o_heads = [jnp.zeros((GLA_STEP_SEQS, GLA_DV), f32) for _ in range(GLA_HEADS)]
    for i in range(GLA_STEP_SEQS):
        mine = seq == i
        pick = jnp.broadcast_to(jnp.where(mine, 1.0, 0.0), (GLA_STEP_SEQS, GLA_DV))
        qa_i = jnp.where(mine, qa, 0.0).astype(bf16)
        k_i = jnp.where(mine, k, 0.0).astype(bf16)
        vb = v.astype(bf16)
        for h in range(GLA_HEADS):
            ks = slice(h * GLA_DK, (h + 1) * GLA_DK)
            vs = slice(h * GLA_DV, (h + 1) * GLA_DV)
            s_old = s_ref[i, h]
            a_rows = lax.dot_general(a[:, ks], pick, _TN, preferred_element_type=f32, precision=HIGHEST)
            o_heads[h] = o_heads[h] + jnp.dot(qa_i[:, ks], s_old.astype(bf16), preferred_element_type=f32)
            snew_ref[i, h] = a_rows * s_old + lax.dot_general(k_i[:, ks], vb[:, vs], _TN,
                                                               preferred_element_type=f32)
    o_ref[...] = _gla_finish(o + jnp.concatenate(o_heads, axis=1), gr_ref[...], normg_ref)


def gla_step(gq, gk, gv, ga, gr, w_gate, b_gate, norm_g, s0):
    n = gq.shape[0]
    rows = lambda width: pl.BlockSpec((GLA_STEP_SEQS, width), lambda i: (i, 0))
    const = lambda shape: pl.BlockSpec(shape, lambda i: (0,) * len(shape))
    state = pl.BlockSpec((GLA_STEP_SEQS, GLA_HEADS, GLA_DK, GLA_DV), lambda i: (i, 0, 0, 0))
    return pl.pallas_call(
        _gla_step_kernel,
        out_shape=(jax.ShapeDtypeStruct((n, GROUP_WIDTH), f32),
                   jax.ShapeDtypeStruct((n, GLA_HEADS, GLA_DK, GLA_DV), f32)),
        grid=(n // GLA_STEP_SEQS,),
        in_specs=[rows(GLA_QK), rows(GLA_QK), rows(GROUP_WIDTH), rows(GLA_LOWRANK), rows(GROUP_WIDTH),
                  const((GLA_LOWRANK, GLA_QK)), const((1, GLA_QK)), const((1, GLA_DV)), state],
        out_specs=(rows(GROUP_WIDTH), state),
        compiler_params=pltpu.CompilerParams(
            dimension_semantics=("parallel",), vmem_limit_bytes=VMEM_LIMIT),
        name="gla_step",
    )(gq, gk, gv, ga, gr, w_gate, b_gate.reshape(1, GLA_QK), norm_g.reshape(1, GLA_DV), s0)


N_PAGES = PAST_LEN // PAGE_SIZE
KEY_PAD = PAST_LEN + LANES
STEP_ROWS = 8


def _page_specs(block, seqs_per_step=1):
    zeros = (0,) * (len(block) - 1)
    return [pl.BlockSpec(block, lambda i, pt, s=s, p=p: (pt[i * seqs_per_step + s, p],) + zeros)
            for s in range(seqs_per_step) for p in range(N_PAGES)]


def _seq_spec(*tail):
    zeros = (0,) * len(tail)
    return pl.BlockSpec((None,) + tail, lambda i, pt: (i,) + zeros)


def _heads_on_kv_lanes(q):
    n = q.shape[0]
    qh = q.reshape(n, N_HEADS, HEAD_DIM)
    z = jnp.zeros((n, Q_PER_KV, HEAD_DIM), q.dtype)
    top = jnp.concatenate([qh[:, :Q_PER_KV], z, z, z], axis=-1)
    bot = jnp.concatenate([z, z, qh[:, Q_PER_KV:], z], axis=-1)
    return jnp.concatenate([top, bot], axis=1)


def _heads_from_kv_lanes(o8):
    n = o8.shape[0]
    o = o8.reshape(n, N_HEADS, KV_HEADS, 2, HEAD_DIM)
    return jnp.concatenate([o[:, :Q_PER_KV, 0, 1], o[:, Q_PER_KV:, 1, 1]], axis=1).reshape(n, GROUP_WIDTH)


def _step_attention(q8, key_blocks, masks, kv_new, new_ok):
    qb = q8.astype(bf16)
    scores = []
    s_new = jnp.sum(q8 * kv_new, axis=1, keepdims=True) * ATT_SCALE
    if new_ok is not None:
        s_new = jnp.where(new_ok, s_new, MASKED)
    m = s_new
    for blk, msk in zip(key_blocks, masks):
        s = lax.dot_general(qb, blk[...].astype(bf16), _NT, preferred_element_type=f32) * ATT_SCALE
        s = jnp.where(msk > 0.5, s, MASKED)
        scores.append(s)
        m = jnp.maximum(m, jnp.max(s, axis=1, keepdims=True))
    p_new = jnp.exp(s_new - m)
    if new_ok is not None:
        p_new = jnp.where(new_ok, p_new, 0.0)
    l = p_new
    acc = p_new * kv_new
    for blk, msk, s in zip(key_blocks, masks, scores):
        p = jnp.where(msk > 0.5, jnp.exp(s - m), 0.0)
        l = l + jnp.sum(p, axis=1, keepdims=True)
        acc = acc + jnp.dot(p.astype(bf16), blk[...].astype(bf16), preferred_element_type=f32)
    return acc / l


def _dsa_step_scores_kernel(pt_ref, iq_ref, iw_ref, iknew_ref, *rest):
    pages, keys_ref, kall_ref = rest[:N_PAGES], rest[N_PAGES], rest[N_PAGES + 1]
    for p in range(N_PAGES):
        kall_ref[p * PAGE_SIZE:(p + 1) * PAGE_SIZE, :] = pages[p][...]
    kall_ref[PAST_LEN:PAST_LEN + 1, :] = iknew_ref[...]
    kall_ref[PAST_LEN + 1:, :] = jnp.zeros((KEY_PAD - PAST_LEN - 1, IDX_DIM), f32)
    dots = lax.dot_general(iq_ref[...].astype(bf16), kall_ref[...].astype(bf16), _NT,
                           preferred_element_type=f32) * IDX_DIM ** -0.5
    iw = iw_ref[...] * IDX_HEADS ** -0.5
    lane = lax.broadcasted_iota(jnp.int32, (1, LANES), 1)
    for c in range(KEY_PAD // LANES):
        tile = slice(c * LANES, (c + 1) * LANES)
        score = jnp.sum(iw * jnp.maximum(dots[:, tile], 0.0), axis=0, keepdims=True)
        key = _sortable_key(score)
        if c == N_PAGES:
            key = jnp.where(lane == 0, key, INT_MIN)
        keys_ref[:, tile] = key


def _topk_rows_kernel(keys_ref, sel_ref, *, topk, idx_bits, n_valid):
    keys = keys_ref[...]
    lane = lax.broadcasted_iota(jnp.int32, keys.shape, 1)
    count = lambda pred: jnp.sum(jnp.where(pred(keys, lane), 1.0, 0.0), axis=1, keepdims=True)
    thr, cut = _topk_threshold(count, keys.shape[0], topk, idx_bits)
    chosen = ((keys > thr) | ((keys == thr) & (lane <= cut))) & (lane < n_valid)
    sel_ref[...] = jnp.where(chosen, 1.0, 0.0)


def _dsa_step_attend_kernel(pt_ref, q_ref, sel_ref, kvnew_ref, *rest):
    pages, o_ref = rest[:N_PAGES], rest[N_PAGES]
    masks = [sel_ref[:, p * PAGE_SIZE:(p + 1) * PAGE_SIZE] for p in range(N_PAGES)]
    new_ok = sel_ref[:, PAST_LEN:PAST_LEN + 1] > 0.5
    o_ref[...] = _step_attention(q_ref[...], pages, masks, kvnew_ref[...], new_ok)


def dsa_step(dq, dkv, iq, ik, iw, pool_kv, pool_idx, page_table):
    n = dq.shape[0]
    assert min(DSA_TOPK, (PAST_LEN + 1) // 4) == DSA_TOPK and PAGE_SIZE == LANES
    pad_heads = ((0, 0), (0, STEP_ROWS - IDX_HEADS), (0, 0))
    iq8 = jnp.pad(iq.reshape(n, IDX_HEADS, IDX_DIM), pad_heads)
    iw8 = jnp.broadcast_to(jnp.pad(iw.reshape(n, IDX_HEADS, 1), pad_heads), (n, STEP_ROWS, LANES))
    keys = pl.pallas_call(
        _dsa_step_scores_kernel,
        out_shape=jax.ShapeDtypeStruct((n, 1, KEY_PAD), jnp.int32),
        grid_spec=pltpu.PrefetchScalarGridSpec(
            num_scalar_prefetch=1, grid=(n,),
            in_specs=[_seq_spec(STEP_ROWS, IDX_DIM), _seq_spec(STEP_ROWS, LANES), _seq_spec(1, IDX_DIM)]
            + _page_specs((None, PAGE_SIZE, IDX_DIM)),
            out_specs=_seq_spec(1, KEY_PAD),
            scratch_shapes=[pltpu.VMEM((KEY_PAD, IDX_DIM), f32)]),
        compiler_params=pltpu.CompilerParams(
            dimension_semantics=("parallel",), vmem_limit_bytes=VMEM_LIMIT),
        name="dsa_step_scores",
    )(page_table, iq8, iw8, ik.reshape(n, 1, IDX_DIM), *([pool_idx] * N_PAGES))
    sel = pl.pallas_call(
        functools.partial(_topk_rows_kernel, topk=DSA_TOPK, idx_bits=int(math.ceil(math.log2(KEY_PAD))),
                          n_valid=PAST_LEN + 1),
        out_shape=jax.ShapeDtypeStruct((n, KEY_PAD), f32),
        compiler_params=pltpu.CompilerParams(vmem_limit_bytes=VMEM_LIMIT),
        name="dsa_step_topk",
    )(keys.reshape(n, KEY_PAD))
    o8 = pl.pallas_call(
        _dsa_step_attend_kernel,
        out_shape=jax.ShapeDtypeStruct((n, N_HEADS, KV_ROW), f32),
        grid_spec=pltpu.PrefetchScalarGridSpec(
            num_scalar_prefetch=1, grid=(n,),
            in_specs=[_seq_spec(N_HEADS, KV_ROW), _seq_spec(1, KEY_PAD), _seq_spec(1, KV_ROW)]
            + _page_specs((None, PAGE_SIZE, KV_ROW)),
            out_specs=_seq_spec(N_HEADS, KV_ROW)),
        compiler_params=pltpu.CompilerParams(
            dimension_semantics=("parallel",), vmem_limit_bytes=VMEM_LIMIT),
        name="dsa_step_attend",
    )(page_table, _heads_on_kv_lanes(dq), sel.reshape(n, 1, KEY_PAD), dkv.reshape(n, 1, KV_ROW),
      *([pool_kv.reshape(-1, PAGE_SIZE, KV_ROW)] * N_PAGES))
    return _heads_from_kv_lanes(o8)


CMP_STEP_SEQS = 2
BLOCKS_PER_PAGE = PAGE_SIZE // CMP_BLOCK
CMP_ROW = CMP_BLOCK * KV_ROW


def _nsa_compress_paged_kernel(pt_ref, pe_ref, w1_ref, w2_ref, *rest):
    npg = CMP_STEP_SEQS * N_PAGES
    pages, o_ref, x_ref = rest[:npg], rest[npg], rest[npg + 1]
    for s in range(npg):
        x_ref[s * BLOCKS_PER_PAGE:(s + 1) * BLOCKS_PER_PAGE, :] = pages[s][...] + pe_ref[...]
    tk = 2048
    hid = jnp.zeros((npg * BLOCKS_PER_PAGE, w1_ref.shape[1]), f32)
    for kk in range(CMP_ROW // tk):
        cols = slice(kk * tk, (kk + 1) * tk)
        hid = hid + jnp.dot(x_ref[:, cols].astype(bf16), w1_ref[cols, :], preferred_element_type=f32)
    o_ref[...] = jnp.dot(jax.nn.gelu(hid).astype(bf16), w2_ref[...], preferred_element_type=f32)


def nsa_compress_paged(pool_cmp, page_table, pe_row, w1_big, w2_big):
    n = page_table.shape[0]
    blocks = N_PAGES * BLOCKS_PER_PAGE
    hw = w1_big.shape[1]
    rows = CMP_STEP_SEQS * blocks
    const = lambda shape: pl.BlockSpec(shape, lambda i, pt: (0,) * len(shape))
    out = pl.pallas_call(
        _nsa_compress_paged_kernel,
        out_shape=jax.ShapeDtypeStruct((n * blocks, KV_ROW), f32),
        grid_spec=pltpu.PrefetchScalarGridSpec(
            num_scalar_prefetch=1, grid=(n // CMP_STEP_SEQS,),
            in_specs=[const((1, CMP_ROW)), const((CMP_ROW, hw)), const((hw, KV_ROW))]
            + _page_specs((None, BLOCKS_PER_PAGE, CMP_ROW), CMP_STEP_SEQS),
            out_specs=pl.BlockSpec((rows, KV_ROW), lambda i, pt: (i, 0)),
            scratch_shapes=[pltpu.VMEM((rows, CMP_ROW), f32)]),
        compiler_params=pltpu.CompilerParams(
            dimension_semantics=("parallel",), vmem_limit_bytes=VMEM_LIMIT),
        name="nsa_compress_paged",
    )(page_table, pe_row, w1_big, w2_big,
      *([pool_cmp.reshape(-1, BLOCKS_PER_PAGE, CMP_ROW)] * (CMP_STEP_SEQS * N_PAGES)))
    return out.reshape(n, blocks, KV_ROW)


def _nsa_step_kernel(pt_ref, q_ref, gate_ref, cmp_ref, slcnew_ref, win_ref, winnew_ref, *rest):
    pages, o_ref = rest[:N_PAGES], rest[N_PAGES]
    q8 = q_ref[...]
    lane = lax.broadcasted_iota(jnp.int32, (N_HEADS, CMP_LANES), 1)
    head = lax.broadcasted_iota(jnp.int32, (N_HEADS, 1), 0)
    q_pos = jnp.full((N_HEADS, 1), PAST_LEN, jnp.int32)
    c_rows = cmp_ref[...].astype(bf16)
    s = lax.dot_general(q8.astype(bf16), c_rows, _NT, preferred_element_type=f32) * ATT_SCALE
    s = jnp.where((lane + 1) * CMP_BLOCK - 1 <= q_pos, s, -jnp.inf)
    m = jnp.max(s, axis=1, keepdims=True)
    m = jnp.where(m == -jnp.inf, 0.0, m)
    p = jnp.exp(s - m)
    p = p / jnp.maximum(jnp.sum(p, axis=1, keepdims=True), F32_TINY)
    o_cmp = jnp.dot(p.astype(bf16), c_rows, preferred_element_type=f32)
    imp = jnp.where(head < Q_PER_KV,
                    jnp.sum(p[:Q_PER_KV], axis=0, keepdims=True),
                    jnp.sum(p[Q_PER_KV:], axis=0, keepdims=True))
    sel_blk = _select_blocks(imp, q_pos, lane)
    ratio = SLC_BLOCK // CMP_BLOCK
    per_page = PAGE_SIZE // SLC_BLOCK
    masks = []
    for pg in range(N_PAGES):
        msk = sel_blk[:, ratio * per_page * pg:ratio * per_page * pg + 1]
        for j in range(1, per_page):
            b = per_page * pg + j
            msk = jnp.where(lane < j * SLC_BLOCK, msk, sel_blk[:, ratio * b:ratio * b + 1])
        masks.append(msk)
    o_slc = _step_attention(q8, pages, masks, slcnew_ref[...], None)
    wbuf = win_ref.shape[0]
    wchunks = wbuf // LANES
    wblocks = [win_ref.at[pl.ds(c * LANES, LANES), :] for c in range(wchunks)]
    wmasks = [jnp.where(PAST_LEN - wbuf + c * LANES + lane > q_pos - WINDOW, 1.0, 0.0) for c in range(wchunks)]
    o_win = _step_attention(q8, wblocks, wmasks, winnew_ref[...], None)
    gate = jax.nn.sigmoid(gate_ref[...])
    wide = lambda g: jnp.concatenate([g] * (KV_ROW // LANES), axis=1)
    o_ref[...] = wide(gate[0]) * o_cmp + wide(gate[1]) * o_slc + wide(gate[2]) * o_win


def nsa_step(nq, ng, skv, wkv, pool_cmp, pool_slc, win_buf, page_table, cmp_weights):
    n = nq.shape[0]
    blocks = N_PAGES * BLOCKS_PER_PAGE
    wbuf = win_buf.shape[1]
    assert (PAST_LEN + 1) // CMP_BLOCK == blocks and blocks <= CMP_LANES and wbuf % LANES == 0
    assert PAST_LEN // SLC_BLOCK + 1 <= CMP_LANES // 2 and PAGE_SIZE == LANES
    cmp_rows = nsa_compress_paged(pool_cmp, page_table, *cmp_weights)
    cmp_rows = jnp.pad(cmp_rows, ((0, 0), (0, CMP_LANES - blocks), (0, 0)))
    gates = jnp.broadcast_to(ng.reshape(n, N_HEADS, 3).transpose(0, 2, 1)[..., None], (n, 3, N_HEADS, LANES))
    o8 = pl.pallas_call(
        _nsa_step_kernel,
        out_shape=jax.ShapeDtypeStruct((n, N_HEADS, KV_ROW), f32),
        grid_spec=pltpu.PrefetchScalarGridSpec(
            num_scalar_prefetch=1, grid=(n,),
            in_specs=[_seq_spec(N_HEADS, KV_ROW), _seq_spec(3, N_HEADS, LANES), _seq_spec(CMP_LANES, KV_ROW),
                      _seq_spec(1, KV_ROW), _seq_spec(wbuf, KV_ROW), _seq_spec(1, KV_ROW)]
            + _page_specs((None, PAGE_SIZE, KV_ROW)),
            out_specs=_seq_spec(N_HEADS, KV_ROW)),
        compiler_params=pltpu.CompilerParams(
            dimension_semantics=("parallel",), vmem_limit_bytes=VMEM_LIMIT),
        name="nsa_step",
    )(page_table, _heads_on_kv_lanes(nq), gates, cmp_rows, skv.reshape(n, 1, KV_ROW),
      win_buf.reshape(n, wbuf, KV_ROW), wkv.reshape(n, 1, KV_ROW),
      *([pool_slc.reshape(-1, PAGE_SIZE, KV_ROW)] * N_PAGES))
    return _heads_from_kv_lanes(o8)


def _tokens_minor(cache):
    token_axis = 2
    return jnp.moveaxis(cache, token_axis, -1)


def _layer_page_specs(block_tail, layer, seqs_per_step=1):
    zeros = (0,) * len(block_tail)
    return [pl.BlockSpec((None, None) + block_tail,
                         lambda i, pt, s=s, p=p: (layer, pt[i * seqs_per_step + s, p]) + zeros)
            for s in range(seqs_per_step) for p in range(N_PAGES)]


def _layer_seq_spec(tail, layer):
    zeros = (0,) * len(tail)
    return pl.BlockSpec((None, None) + tail, lambda i, pt: (layer, i) + zeros)


def _decode_attention(q8, blocks, masks, kv_new, new_ok, is_g0):
    qb = q8.astype(bf16)
    by_group = lambda fn: jnp.where(is_g0, fn(0), fn(1))
    s_new = by_group(lambda g: jnp.sum(q8 * kv_new[2 * g:2 * g + 1], axis=1, keepdims=True)) * ATT_SCALE
    if new_ok is not None:
        s_new = jnp.where(new_ok, s_new, MASKED)
    m = s_new
    scores = []
    for blk, msk in zip(blocks, masks):
        s = by_group(lambda g: jnp.dot(qb, blk[g, 0].astype(bf16), preferred_element_type=f32)) * ATT_SCALE
        s = jnp.where(msk > 0.5, s, MASKED)
        scores.append(s)
        m = jnp.maximum(m, jnp.max(s, axis=1, keepdims=True))
    p_new = jnp.exp(s_new - m)
    if new_ok is not None:
        p_new = jnp.where(new_ok, p_new, 0.0)
    l = p_new
    acc = p_new * by_group(lambda g: kv_new[2 * g + 1:2 * g + 2])
    for blk, msk, s in zip(blocks, masks, scores):
        p = jnp.where(msk > 0.5, jnp.exp(s - m), 0.0)
        l = l + jnp.sum(p, axis=1, keepdims=True)
        pb = p.astype(bf16)
        acc = acc + by_group(lambda g: lax.dot_general(pb, blk[g, 1].astype(bf16), _NT,
                                                       preferred_element_type=f32))
    return acc / l


def _dsa_decode_scores_kernel(pt_ref, iq_ref, iw_ref, iknew_ref, *rest):
    pages, keys_ref = rest[:N_PAGES], rest[N_PAGES]
    iq = iq_ref[...]
    iqb = iq.astype(bf16)
    iw = iw_ref[...] * IDX_HEADS ** -0.5
    weigh = lambda dots, w: jnp.sum(w * jnp.maximum(dots * IDX_DIM ** -0.5, 0.0), axis=0, keepdims=True)
    for p in range(N_PAGES):
        dots = jnp.dot(iqb, pages[p][...].astype(bf16), preferred_element_type=f32)
        keys_ref[:, p * PAGE_SIZE:(p + 1) * PAGE_SIZE] = _sortable_key(weigh(dots, iw))
    new_score = weigh(jnp.sum(iq * iknew_ref[...], axis=1, keepdims=True), iw[:, :1])
    lane = lax.broadcasted_iota(jnp.int32, (1, LANES), 1)
    keys_ref[:, PAST_LEN:] = jnp.where(lane == 0, _sortable_key(jnp.broadcast_to(new_score, (1, LANES))), INT_MIN)


def _dsa_decode_attend_kernel(pt_ref, q_ref, sel_ref, kvnew_ref, *rest):
    pages, o_ref = rest[:N_PAGES], rest[N_PAGES]
    masks = [sel_ref[:, p * PAGE_SIZE:(p + 1) * PAGE_SIZE] for p in range(N_PAGES)]
    new_ok = sel_ref[:, PAST_LEN:PAST_LEN + 1] > 0.5
    is_g0 = lax.broadcasted_iota(jnp.int32, (N_HEADS, 1), 0) < Q_PER_KV
    o_ref[...] = _decode_attention(q_ref[...], pages, masks, kvnew_ref[...], new_ok, is_g0)


def dsa_decode(dq, dkv, iq, ik, iw, pools_kv, pools_idx, layer, page_table):
    n = dq.shape[0]
    assert min(DSA_TOPK, (PAST_LEN + 1) // 4) == DSA_TOPK and PAGE_SIZE == LANES
    pad_heads = ((0, 0), (0, STEP_ROWS - IDX_HEADS), (0, 0))
    iq8 = jnp.pad(iq.reshape(n, IDX_HEADS, IDX_DIM), pad_heads)
    iw8 = jnp.broadcast_to(jnp.pad(iw.reshape(n, IDX_HEADS, 1), pad_heads), (n, STEP_ROWS, LANES))
    keys = pl.pallas_call(
        _dsa_decode_scores_kernel,
        out_shape=jax.ShapeDtypeStruct((n, 1, KEY_PAD), jnp.int32),
        grid_spec=pltpu.PrefetchScalarGridSpec(
            num_scalar_prefetch=1, grid=(n,),
            in_specs=[_seq_spec(STEP_ROWS, IDX_DIM), _seq_spec(STEP_ROWS, LANES), _seq_spec(1, IDX_DIM)]
            + _layer_page_specs((IDX_DIM, PAGE_SIZE), layer),
            out_specs=_seq_spec(1, KEY_PAD)),
        compiler_params=pltpu.CompilerParams(
            dimension_semantics=("parallel",), vmem_limit_bytes=VMEM_LIMIT),
        name="dsa_decode_scores",
    )(page_table, iq8, iw8, ik.reshape(n, 1, IDX_DIM), *([pools_idx] * N_PAGES))
    sel = pl.pallas_call(
        functools.partial(_topk_rows_kernel, topk=DSA_TOPK, idx_bits=int(math.ceil(math.log2(KEY_PAD))),
                          n_valid=PAST_LEN + 1),
        out_shape=jax.ShapeDtypeStruct((n, KEY_PAD), f32),
        compiler_params=pltpu.CompilerParams(vmem_limit_bytes=VMEM_LIMIT),
        name="dsa_decode_topk",
    )(keys.reshape(n, KEY_PAD))
    o8 = pl.pallas_call(
        _dsa_decode_attend_kernel,
        out_shape=jax.ShapeDtypeStruct((n, N_HEADS, HEAD_DIM), f32),
        grid_spec=pltpu.PrefetchScalarGridSpec(
            num_scalar_prefetch=1, grid=(n,),
            in_specs=[_seq_spec(N_HEADS, HEAD_DIM), _seq_spec(1, KEY_PAD), _seq_spec(2 * KV_HEADS, HEAD_DIM)]
            + _layer_page_specs((KV_HEADS, 2, HEAD_DIM, PAGE_SIZE), layer),
            out_specs=_seq_spec(N_HEADS, HEAD_DIM)),
        compiler_params=pltpu.CompilerParams(
            dimension_semantics=("parallel",), vmem_limit_bytes=VMEM_LIMIT),
        name="dsa_decode_attend",
    )(page_table, dq.reshape(n, N_HEADS, HEAD_DIM), sel.reshape(n, 1, KEY_PAD),
      dkv.reshape(n, 2 * KV_HEADS, HEAD_DIM), *([pools_kv] * N_PAGES))
    return o8.reshape(n, GROUP_WIDTH)


def _nsa_compress_decode_kernel(pt_ref, pe_ref, w1_ref, w2_ref, *rest):
    npg = CMP_STEP_SEQS * N_PAGES
    pages, o_ref, rows_ref = rest[:npg], rest[npg], rest[npg + 1]
    nblk = npg * BLOCKS_PER_PAGE
    blocks_per_seq = N_PAGES * BLOCKS_PER_PAGE
    for s in range(npg):
        for gj in range(2 * KV_HEADS):
            rows_ref[gj, s * PAGE_SIZE:(s + 1) * PAGE_SIZE, :] = pages[s][gj // 2, gj % 2].T
    for gj in range(2 * KV_HEADS):
        j = gj % 2

        def token_body(t, hid):
            x = rows_ref[gj, pl.ds(t, nblk, stride=CMP_BLOCK), :] + pe_ref[t, j:j + 1, :]
            w = w1_ref[j, pl.ds(pl.multiple_of(t * HEAD_DIM, HEAD_DIM), HEAD_DIM), :]
            return hid + jnp.dot(x.astype(bf16), w, preferred_element_type=f32)

        hid = lax.fori_loop(0, CMP_BLOCK, token_body, jnp.zeros((nblk, CMP_HIDDEN), f32))
        out = jnp.dot(jax.nn.gelu(hid).astype(bf16), w2_ref[j], preferred_element_type=f32)
        for s in range(CMP_STEP_SEQS):
            o_ref[s, gj] = out[s * blocks_per_seq:(s + 1) * blocks_per_seq]


def nsa_compress_decode(pools_cmp, layer, page_table, cmp_pe, cmp_w1, cmp_w2):
    n = page_table.shape[0]
    blocks = N_PAGES * BLOCKS_PER_PAGE
    const = lambda shape: pl.BlockSpec(shape, lambda i, pt: (0,) * len(shape))
    return pl.pallas_call(
        _nsa_compress_decode_kernel,
        out_shape=jax.ShapeDtypeStruct((n, 2 * KV_HEADS, blocks, HEAD_DIM), f32),
        grid_spec=pltpu.PrefetchScalarGridSpec(
            num_scalar_prefetch=1, grid=(n // CMP_STEP_SEQS,),
            in_specs=[const((CMP_BLOCK, 2, HEAD_DIM)), const((2, CMP_BLOCK * HEAD_DIM, CMP_HIDDEN)),
                      const((2, CMP_HIDDEN, HEAD_DIM))]
            + _layer_page_specs((KV_HEADS, 2, HEAD_DIM, PAGE_SIZE), layer, CMP_STEP_SEQS),
            out_specs=pl.BlockSpec((CMP_STEP_SEQS, 2 * KV_HEADS, blocks, HEAD_DIM), lambda i, pt: (i, 0, 0, 0)),
            scratch_shapes=[pltpu.VMEM((2 * KV_HEADS, CMP_STEP_SEQS * N_PAGES * PAGE_SIZE, HEAD_DIM), f32)]),
        compiler_params=pltpu.CompilerParams(
            dimension_semantics=("parallel",), vmem_limit_bytes=VMEM_LIMIT),
        name="nsa_compress_decode",
    )(page_table, cmp_pe, cmp_w1.astype(bf16), cmp_w2.astype(bf16),
      *([pools_cmp] * (CMP_STEP_SEQS * N_PAGES)))


def _nsa_decode_kernel(pt_ref, q_ref, gate_ref, cmp_ref, slcnew_ref, win_ref, winnew_ref, *rest):
    pages, o_ref = rest[:N_PAGES], rest[N_PAGES]
    q8 = q_ref[...]
    qb = q8.astype(bf16)
    lane = lax.broadcasted_iota(jnp.int32, (N_HEADS, CMP_LANES), 1)
    head = lax.broadcasted_iota(jnp.int32, (N_HEADS, 1), 0)
    is_g0 = head < Q_PER_KV
    by_group = lambda fn: jnp.where(is_g0, fn(0), fn(1))
    q_pos = jnp.full((N_HEADS, 1), PAST_LEN, jnp.int32)
    blocks = cmp_ref.shape[1]
    s = by_group(lambda g: lax.dot_general(qb, cmp_ref[2 * g].astype(bf16), _NT,
                                           preferred_element_type=f32)) * ATT_SCALE
    lane_c = lane[:, :blocks]
    s = jnp.where((lane_c + 1) * CMP_BLOCK - 1 <= q_pos, s, -jnp.inf)
    m = jnp.max(s, axis=1, keepdims=True)
    m = jnp.where(m == -jnp.inf, 0.0, m)
    p = jnp.exp(s - m)
    p = p / jnp.maximum(jnp.sum(p, axis=1, keepdims=True), F32_TINY)
    pb = p.astype(bf16)
    o_cmp = by_group(lambda g: jnp.dot(pb, cmp_ref[2 * g + 1].astype(bf16), preferred_element_type=f32))
    imp = jnp.where(is_g0, jnp.sum(p[:Q_PER_KV], axis=0, keepdims=True),
                    jnp.sum(p[Q_PER_KV:], axis=0, keepdims=True))
    imp = jnp.concatenate([imp, jnp.zeros((N_HEADS, CMP_LANES - blocks), f32)], axis=1)
    sel_blk = _select_blocks(imp, q_pos, lane)
    ratio = SLC_BLOCK // CMP_BLOCK
    per_page = PAGE_SIZE // SLC_BLOCK
    masks = []
    for pg in range(N_PAGES):
        msk = sel_blk[:, ratio * per_page * pg:ratio * per_page * pg + 1]
        for j in range(1, per_page):
            b = per_page * pg + j
            msk = jnp.where(lane < j * SLC_BLOCK, msk, sel_blk[:, ratio * b:ratio * b + 1])
        masks.append(msk)
    o_slc = _decode_attention(q8, pages, masks, slcnew_ref[...], None, is_g0)
    wbuf = win_ref.shape[-1]
    slot = lax.broadcasted_iota(jnp.int32, (1, wbuf), 1)
    wmask = jnp.where(PAST_LEN - wbuf + slot > PAST_LEN - WINDOW, 1.0, 0.0)
    o_win = _decode_attention(q8, [win_ref], [wmask], winnew_ref[...], None, is_g0)
    gate = jax.nn.sigmoid(gate_ref[...])
    o_ref[...] = (gate[0][:, :HEAD_DIM] * o_cmp + gate[1][:, :HEAD_DIM] * o_slc
                  + gate[2][:, :HEAD_DIM] * o_win)


def nsa_decode(nq, ng, skv, wkv, pool_cmp, pools_slc, wins, layer, page_table, cmp_weights):
    n = nq.shape[0]
    blocks = N_PAGES * BLOCKS_PER_PAGE
    wbuf = wins.shape[-1]
    assert (PAST_LEN + 1) // CMP_BLOCK == blocks and blocks <= CMP_LANES
    assert PAST_LEN // SLC_BLOCK + 1 <= CMP_LANES // 2 and PAGE_SIZE == LANES
    cmp_rows = nsa_compress_paged(pool_cmp, page_table, *cmp_weights)
    cmp_rows = cmp_rows.reshape(n, blocks, 2 * KV_HEADS, HEAD_DIM).transpose(0, 2, 1, 3)
    gates = jnp.broadcast_to(ng.reshape(n, N_HEADS, 3).transpose(0, 2, 1)[..., None], (n, 3, N_HEADS, LANES))
    kv_tile = (KV_HEADS, 2, HEAD_DIM)
    o8 = pl.pallas_call(
        _nsa_decode_kernel,
        out_shape=jax.ShapeDtypeStruct((n, N_HEADS, HEAD_DIM), f32),
        grid_spec=pltpu.PrefetchScalarGridSpec(
            num_scalar_prefetch=1, grid=(n,),
            in_specs=[_seq_spec(N_HEADS, HEAD_DIM), _seq_spec(3, N_HEADS, LANES),
                      _seq_spec(2 * KV_HEADS, blocks, HEAD_DIM), _seq_spec(2 * KV_HEADS, HEAD_DIM),
                      _layer_seq_spec(kv_tile + (wbuf,), layer), _seq_spec(2 * KV_HEADS, HEAD_DIM)]
            + _layer_page_specs(kv_tile + (PAGE_SIZE,), layer),
            out_specs=_seq_spec(N_HEADS, HEAD_DIM)),
        compiler_params=pltpu.CompilerParams(
            dimension_semantics=("parallel",), vmem_limit_bytes=VMEM_LIMIT),
        name="nsa_decode",
    )(page_table, nq.reshape(n, N_HEADS, HEAD_DIM), gates, cmp_rows, skv.reshape(n, 2 * KV_HEADS, HEAD_DIM),
      wins, wkv.reshape(n, 2 * KV_HEADS, HEAD_DIM), *([pools_slc] * N_PAGES))
    return o8.reshape(n, GROUP_WIDTH)


def masked_softmax(s, mask):
    s = jnp.where(mask, s.astype(jnp.float32), -jnp.inf)
    m = jnp.max(s, axis=-1, keepdims=True)
    m = jnp.where(jnp.isfinite(m), m, 0.0)
    p = jnp.exp(s - m)
    return p / jnp.maximum(p.sum(-1, keepdims=True), jnp.finfo(jnp.float32).tiny)


def gather_seq(rows, pos, *extra):
    n_idx = jnp.arange(rows.shape[0]).reshape((-1,) + (1,) * (pos.ndim - 1))
    return rows[(n_idx, pos) + tuple(extra)]


def gather_pages(pool, page_table):
    g = pool[page_table]
    return g.reshape((g.shape[0], -1) + g.shape[3:])


def gather_paged_rows(pool, page_table, pos, *extra):
    n_idx = jnp.arange(page_table.shape[0]).reshape((-1,) + (1,) * (pos.ndim - 1))
    phys = page_table[n_idx, pos // PAGE_SIZE]
    return pool[(phys, pos % PAGE_SIZE) + tuple(extra)]


def gather_past_and_new(pool, page_table, new_rows, pos, *extra):
    past = gather_paged_rows(pool, page_table, jnp.minimum(pos, PAST_LEN - 1), *extra)
    new = gather_seq(new_rows, jnp.clip(pos - PAST_LEN, 0, new_rows.shape[1] - 1), *extra)
    in_past = (pos < PAST_LEN).reshape(pos.shape + (1,) * (past.ndim - pos.ndim))
    return jnp.where(in_past, past, new)


def split_projection(cols, n, t):
    (dq, dkv, iq, ik, iw, u, nq, ckv, skv, wkv, ng, gq, gk, gv, ga, gr) = cols
    kv_shape = (n, t, KV_HEADS, 2, HEAD_DIM)
    return {
        'dsa_q': dq.reshape(n, t, N_HEADS, HEAD_DIM),
        'dsa_kv': dkv.reshape(kv_shape),
        'idx_q': iq.reshape(n, t, IDX_HEADS, IDX_DIM),
        'idx_k': ik,
        'idx_w': iw,
        's5_u': u,
        'nsa_q': nq.reshape(n, t, N_HEADS, HEAD_DIM),
        'nsa_cmp': ckv.reshape(kv_shape),
        'nsa_slc': skv.reshape(kv_shape),
        'nsa_win': wkv.reshape(kv_shape),
        'nsa_gate': jax.nn.sigmoid(ng).reshape(n, t, N_HEADS, 3),
        'gla_q': gq.reshape(n, t, GLA_HEADS, GLA_DK),
        'gla_k': gk.reshape(n, t, GLA_HEADS, GLA_DK),
        'gla_v': gv.reshape(n, t, GLA_HEADS, GLA_DV),
        'gla_a': ga,
        'gla_r': gr,
    }


def indexer_scores(iq, iw, ik):
    dots = jnp.einsum('nqhd,nld->nqhl', iq, ik) * IDX_DIM ** -0.5
    return jnp.einsum('nqh,nqhl->nql', iw * IDX_HEADS ** -0.5, jax.nn.relu(dots)).astype(jnp.float32)


def dsa_select(scores, q_pos, n_keys):
    causal = jnp.arange(n_keys)[None, :] <= q_pos[:, None]
    k_sel = min(DSA_TOPK, n_keys // 4)
    _, sel = lax.top_k(jnp.where(causal, scores, -jnp.inf), k_sel)
    return sel, sel <= q_pos[None, :, None]


def dsa_attend(q, kv_sel, valid):
    n, nq = q.shape[:2]
    qg = q.reshape(n, nq, KV_HEADS, Q_PER_KV, HEAD_DIM)
    s = jnp.einsum('nqgrd,nqkgd->nqgrk', qg, kv_sel[..., 0, :]) * HEAD_DIM ** -0.5
    p = masked_softmax(s, valid[:, :, None, None, :])
    o = jnp.einsum('nqgrk,nqkgd->nqgrd', p, kv_sel[..., 1, :])
    return o.reshape(n, nq, GROUP_WIDTH)


def dsa_sample(p, pool_kv, pool_idx, page_table):
    s_new = p['dsa_q'].shape[1]
    n_keys = PAST_LEN + s_new
    ik_all = jnp.concatenate([gather_pages(pool_idx, page_table), p['idx_k']], axis=1)
    q_pos = PAST_LEN + jnp.arange(s_new)
    sel, valid = dsa_select(indexer_scores(p['idx_q'], p['idx_w'], ik_all), q_pos, n_keys)
    kv_sel = gather_past_and_new(pool_kv, page_table, p['dsa_kv'], sel)
    return dsa_attend(p['dsa_q'], kv_sel, valid)


def complex_linear_combine(e1, e2):
    a1r, a1i, b1r, b1i = e1
    a2r, a2i, b2r, b2i = e2
    return (a2r * a1r - a2i * a1i,
            a2r * a1i + a2i * a1r,
            a2r * b1r - a2i * b1i + b2r,
            a2r * b1i + a2i * b1r + b2i)


def s5_mixer(u, h0, w):
    n, t = u.shape[:2]
    uf = u.reshape(n, t, S5_GROUPS, S5_CH)
    a_re, a_im = w['a_re'], w['a_im']
    dt = jnp.exp(w['log_dt'])[:, None]
    mag = jnp.exp(a_re * dt)
    ab_re, ab_im = mag * jnp.cos(a_im * dt), mag * jnp.sin(a_im * dt)
    den = a_re * a_re + a_im * a_im
    nr, ni = ab_re - 1.0, ab_im
    f_re = (nr * a_re + ni * a_im) / den
    f_im = (ni * a_re - nr * a_im) / den
    b_re, b_im = w['b_re'], w['b_im']
    bb_re = f_re[..., None] * b_re - f_im[..., None] * b_im
    bb_im = f_re[..., None] * b_im + f_im[..., None] * b_re
    bu_re = jnp.einsum('gsc,btgc->btgs', bb_re, uf)
    bu_im = jnp.einsum('gsc,btgc->btgs', bb_im, uf)
    cum_re, cum_im, sc_re, sc_im = lax.associative_scan(
        complex_linear_combine,
        (jnp.broadcast_to(ab_re, bu_re.shape), jnp.broadcast_to(ab_im, bu_im.shape), bu_re, bu_im),
        axis=1)
    h0r, h0i = h0[:, 0][:, None], h0[:, 1][:, None]
    h_re = cum_re * h0r - cum_im * h0i + sc_re
    h_im = cum_re * h0i + cum_im * h0r + sc_im
    y = (jnp.einsum('gcs,btgs->btgc', w['c_re'], h_re)
         - jnp.einsum('gcs,btgs->btgc', w['c_im'], h_im)
         + w['d'] * uf)
    y = jax.nn.gelu(y.reshape(n, t, GROUP_WIDTH))
    y = y * jax.nn.sigmoid(y @ w['w_glu'] + w['b_glu'])
    h_last = jnp.stack([h_re[:, -1], h_im[:, -1]], axis=1)
    return y, h_last


def nsa_compress(rows, w):
    n, length = rows.shape[:2]
    nc = length // CMP_BLOCK
    blk = rows[:, :nc * CMP_BLOCK].reshape(n, nc, CMP_BLOCK, KV_HEADS, 2, HEAD_DIM)
    blk = blk + w['cmp_pe'][:, None]
    flat = jnp.moveaxis(blk, 2, 4).reshape(n, nc, KV_HEADS, 2, CMP_BLOCK * HEAD_DIM)
    hid = jax.nn.gelu(jnp.einsum('ncgjf,jfe->ncgje', flat, w['cmp_w1']))
    return jnp.einsum('ncgje,jed->ncgjd', hid, w['cmp_w2'])


def nsa_core(q, gate, q_pos, cmp_kv, n_slc_blocks, slc_gather, win_kv, win_pos):
    n, nq = q.shape[:2]
    qg = q.reshape(n, nq, KV_HEADS, Q_PER_KV, HEAD_DIM)
    scale = HEAD_DIM ** -0.5
    nc = cmp_kv.shape[1]
    c_vis = (jnp.arange(nc) + 1) * CMP_BLOCK - 1 <= q_pos[:, None]
    s_c = jnp.einsum('nqgrd,ncgd->nqgrc', qg, cmp_kv[..., 0, :]) * scale
    p_c = masked_softmax(s_c, c_vis[None, :, None, None, :])
    o_cmp = jnp.einsum('nqgrc,ncgd->nqgrd', p_c, cmp_kv[..., 1, :])
    ratio = SLC_BLOCK // CMP_BLOCK
    imp = p_c.sum(3)
    imp = jnp.pad(imp, ((0, 0), (0, 0), (0, 0), (0, n_slc_blocks * ratio - nc)))
    imp = imp.reshape(n, nq, KV_HEADS, n_slc_blocks, ratio).sum(-1)
    blk = jnp.arange(n_slc_blocks)[None, :]
    cur = (q_pos // SLC_BLOCK)[:, None]
    forced = (blk == 0) | (blk == cur) | (blk == cur - 1)
    future = blk > cur
    score = jnp.where(future[None, :, None, :], -jnp.inf,
                      jnp.where(forced[None, :, None, :], jnp.inf, imp))
    n_sel = min(SLC_TOPN, n_slc_blocks)
    _, sel = lax.top_k(score, n_sel)
    pos = (sel[..., None] * SLC_BLOCK + jnp.arange(SLC_BLOCK)).reshape(n, nq, KV_HEADS, n_sel * SLC_BLOCK)
    valid = pos <= q_pos[None, :, None, None]
    kv_s = slc_gather(pos)
    s_s = jnp.einsum('nqgrd,nqgkd->nqgrk', qg, kv_s[..., 0, :]) * scale
    p_s = masked_softmax(s_s, valid[:, :, :, None, :])
    o_slc = jnp.einsum('nqgrk,nqgkd->nqgrd', p_s, kv_s[..., 1, :])
    w_vis = ((win_pos[None, :] <= q_pos[:, None]) & (win_pos[None, :] > q_pos[:, None] - WINDOW)
             & (win_pos[None, :] >= 0))
    s_w = jnp.einsum('nqgrd,nwgd->nqgrw', qg, win_kv[..., 0, :]) * scale
    p_w = masked_softmax(s_w, w_vis[None, :, None, None, :])
    o_win = jnp.einsum('nqgrw,nwgd->nqgrd', p_w, win_kv[..., 1, :])
    g = gate.reshape(n, nq, KV_HEADS, Q_PER_KV, 3)
    o = g[..., 0:1] * o_cmp + g[..., 1:2] * o_slc + g[..., 2:3] * o_win
    return o.reshape(n, nq, GROUP_WIDTH)


def nsa_sample(p, w, pool_cmp, pool_slc, win_buf, page_table):
    s_new = p['nsa_q'].shape[1]
    n_keys = PAST_LEN + s_new
    cmp_all = jnp.concatenate([gather_pages(pool_cmp, page_table), p['nsa_cmp']], axis=1)
    cmp_kv = nsa_compress(cmp_all, w)
    wbuf = win_buf.shape[1]
    win_all = jnp.concatenate([win_buf, p['nsa_win']], axis=1)
    win_pos = PAST_LEN - wbuf + jnp.arange(wbuf + s_new)
    q_pos = PAST_LEN + jnp.arange(s_new)
    g_idx = jnp.arange(KV_HEADS)[None, None, :, None]
    out = nsa_core(p['nsa_q'], p['nsa_gate'], q_pos, cmp_kv, -(-n_keys // SLC_BLOCK),
                   lambda pos: gather_past_and_new(pool_slc, page_table, p['nsa_slc'], pos, g_idx),
                   win_all, win_pos)
    return out, win_all[:, -wbuf:]


def gla_recurrence(q, k, v, log_a, s0):
    n, t = q.shape[:2]
    c = min(GLA_CHUNK, t)
    tp = -(-t // c) * c

    def chunks(a):
        a = jnp.pad(a, ((0, 0), (0, tp - t)) + ((0, 0),) * (a.ndim - 2))
        return jnp.moveaxis(a.reshape((n, tp // c, c) + a.shape[2:]), 1, 0)

    tril = jnp.tril(jnp.ones((c, c), dtype=bool))[None, :, :, None, None]

    def step(s, inp):
        qc, kc, vc, gc = inp
        b = jnp.cumsum(gc, axis=1)
        o_inter = jnp.einsum('nchk,nhkv->nchv', qc * jnp.exp(b), s)
        decay = jnp.exp(jnp.where(tril, b[:, :, None] - b[:, None, :], -jnp.inf))
        att = jnp.einsum('nthk,nshk,ntshk->nths', qc, kc, decay)
        o_intra = jnp.einsum('nths,nshv->nthv', att, vc)
        b_end = b[:, -1]
        s_new = (jnp.exp(b_end)[..., None] * s
                 + jnp.einsum('nshk,nshv->nhkv', kc * jnp.exp(b_end[:, None] - b), vc))
        return s_new, o_inter + o_intra

    s_last, o = lax.scan(step, s0, (chunks(q), chunks(k), chunks(v), chunks(log_a)))
    o = jnp.moveaxis(o, 0, 1).reshape((n, tp) + o.shape[3:])[:, :t]
    return o, s_last


def gla_mixer(p, s0, w):
    n, t = p['gla_q'].shape[:2]
    q = p['gla_q'] * GLA_DK ** -0.5
    k = p['gla_k']
    v = p['gla_v']
    z = p['gla_a'] @ w['gla_w_gate'] + w['gla_b_gate']
    log_a = (jax.nn.log_sigmoid(z) / GLA_TAU).reshape(n, t, GLA_HEADS, GLA_DK)
    o, s_last = gla_recurrence(q, k, v, log_a, s0)
    o = o * lax.rsqrt(jnp.mean(o * o, -1, keepdims=True) + LN_EPS) * w['gla_norm_g']
    o = o.reshape(n, t, GROUP_WIDTH) * jax.nn.silu(p['gla_r'])
    return o, s_last


def mix_prompt(cols, n, t, w, sw, cmp_weights):
    (dq, dkv, iq, ik, iw, u, nq, ckv, skv, wkv, ng, gq, gk, gv, ga, gr) = [
        c.reshape(n, t, c.shape[-1]) for c in cols]
    kv_shape = (n, t, KV_HEADS, 2, HEAD_DIM)
    o_dsa = dsa_prompt(dq, dkv, iq, ik, iw)
    o_s5, h_s5 = s5_prompt(u, jnp.zeros((n, 2, S5_WIDTH), f32), sw)
    o_nsa = nsa_prompt(nq, ng, ckv, skv, wkv, cmp_weights)
    o_gla, s_gla = gla_prompt(gq, gk, gv, ga, gr, w['gla_w_gate'], w['gla_b_gate'], w['gla_norm_g'])
    win_state = wkv[:, -min(WINDOW, t):].reshape((n, min(WINDOW, t)) + kv_shape[2:])
    mixers = [o.reshape(n * t, GROUP_WIDTH) for o in (o_dsa, o_s5, o_nsa, o_gla)]
    return mixers, (dkv.reshape(kv_shape), ik, ckv.reshape(kv_shape), skv.reshape(kv_shape), win_state,
                    h_s5.reshape(n, 2, S5_GROUPS, S5_STATE), s_gla)


def mix_sample(cols, w, sw, cmp_weights, layer, pools, win_buf, h_s5, s_gla, page_table):
    (dq, dkv, iq, ik, iw, u, nq, ckv, skv, wkv, ng, gq, gk, gv, ga, gr) = cols
    n, t = dq.shape[0], 1
    kv_shape = (n, t, KV_HEADS, 2, HEAD_DIM)
    o_dsa = dsa_decode(dq, dkv, iq, ik, iw, pools['dsa_kv'], pools['dsa_idx'], layer, page_table)
    o_s5, h_new = s5_step(u, h_s5.reshape(n, 2 * S5_WIDTH), sw)
    o_nsa = nsa_decode(nq, ng, skv, wkv, pools['nsa_cmp'][layer], pools['nsa_slc'], pools['nsa_win'], layer,
                       page_table, cmp_weights)
    o_gla, s_new = gla_step(gq, gk, gv, ga, gr, w['gla_w_gate'], w['gla_b_gate'], w['gla_norm_g'], s_gla)
    win_new = jnp.concatenate([win_buf[:, 1:], wkv.reshape(kv_shape)], axis=1)
    return [o_dsa, o_s5, o_nsa, o_gla], (
        dkv.reshape(kv_shape), ik.reshape(n, t, IDX_DIM), ckv.reshape(kv_shape), skv.reshape(kv_shape),
        win_new, h_new.reshape(n, 2, S5_GROUPS, S5_STATE), s_new)


def kernel(x_prompt, x_sample, cache_dsa_kv, cache_dsa_idx, cache_nsa_cmp, cache_nsa_slc, cache_nsa_win, state_s5, state_gla, page_table, w_in, s5_a_re, s5_a_im, s5_b_re, s5_b_im, s5_c_re, s5_c_im, s5_d, s5_log_dt, s5_w_glu, s5_b_glu, nsa_cmp_pe, nsa_cmp_w1, nsa_cmp_w2, gla_w_gate, gla_b_gate, gla_norm_g, w_out, ln1_g, ln1_b, ffn_w_gate, ffn_w_up, ffn_w_down, ln2_g, ln2_b):
    np_, tp_ = x_prompt.shape[:2]
    ns_, ts_ = x_sample.shape[:2]
    assert ts_ == 1
    w_in_pad = pad_in_projection(w_in)
    w_out_b = w_out.astype(bf16)
    wg_b, wu_b, wd_b = ffn_w_gate.astype(bf16), ffn_w_up.astype(bf16), ffn_w_down.astype(bf16)
    hp = x_prompt.reshape(np_ * tp_, D_MODEL)
    hs = x_sample.reshape(ns_ * ts_, D_MODEL)
    outs_p = [[] for _ in range(7)]
    outs_s = [[] for _ in range(7)]
    pools = {'dsa_kv': _tokens_minor(cache_dsa_kv), 'dsa_idx': _tokens_minor(cache_dsa_idx),
             'nsa_cmp': cache_nsa_cmp, 'nsa_slc': _tokens_minor(cache_nsa_slc),
             'nsa_win': _tokens_minor(cache_nsa_win)}
    for l in range(DEPTH):
        w = {'a_re': s5_a_re[l], 'a_im': s5_a_im[l], 'b_re': s5_b_re[l], 'b_im': s5_b_im[l],
             'c_re': s5_c_re[l], 'c_im': s5_c_im[l], 'd': s5_d[l], 'log_dt': s5_log_dt[l],
             'w_glu': s5_w_glu[l], 'b_glu': s5_b_glu[l],
             'cmp_pe': nsa_cmp_pe[l], 'cmp_w1': nsa_cmp_w1[l], 'cmp_w2': nsa_cmp_w2[l],
             'gla_w_gate': gla_w_gate[l], 'gla_b_gate': gla_b_gate[l], 'gla_norm_g': gla_norm_g[l]}
        sw = s5_discretize(w)
        cmp_weights = nsa_compress_weights(nsa_cmp_pe[l], nsa_cmp_w1[l], nsa_cmp_w2[l])
        mixers_p, st_p = mix_prompt(in_projection(hp, w_in_pad, l), np_, tp_, w, sw, cmp_weights)
        mixers_s, st_s = mix_sample(in_projection(hs, w_in_pad, l), w, sw, cmp_weights, l, pools, cache_nsa_win[l],
                                    state_s5[l], state_gla[l], page_table)
        hp = outproj_ln(hp, mixers_p, w_out_b, l, ln1_g[l], ln1_b[l])
        hs = outproj_ln(hs, mixers_s, w_out_b, l, ln1_g[l], ln1_b[l])
        hp = ffn_ln(hp, wg_b, wu_b, wd_b, l, ln2_g[l], ln2_b[l])
        hs = ffn_ln(hs, wg_b, wu_b, wd_b, l, ln2_g[l], ln2_b[l])
        for lst, st in zip(outs_p, st_p):
            lst.append(st)
        for lst, st in zip(outs_s, st_s):
            lst.append(st)
    dsa_kv_p, dsa_idx_p, nsa_cmp_p, nsa_slc_p, nsa_win_p, s5_p, gla_p = [jnp.stack(a) for a in outs_p]
    dsa_kv_s, dsa_idx_s, nsa_cmp_s, nsa_slc_s, nsa_win_s, s5_s, gla_s = [jnp.stack(a) for a in outs_s]
    return (hp.reshape(np_, tp_, D_MODEL), hs.reshape(ns_, ts_, D_MODEL),
            dsa_kv_p, dsa_kv_s, dsa_idx_p, dsa_idx_s, nsa_cmp_p, nsa_cmp_s,
            nsa_slc_p, nsa_slc_s, nsa_win_p, nsa_win_s, s5_p, s5_s, gla_p, gla_s)
```

```python
import functools
import math

import jax
import jax.numpy as jnp
from jax import lax
import numpy as np
from jax.experimental import pallas as pl
from jax.experimental.pallas import tpu as pltpu

D_MODEL = 2048
DEPTH = 2
PAST_LEN = 2048
PAGE_SIZE = 128
GROUP_WIDTH = 512
HEAD_DIM = 64
N_HEADS = 8
KV_HEADS = 2
Q_PER_KV = 4
KV_ROW = 256
IDX_HEADS = 4
IDX_DIM = 64
DSA_TOPK = 256
S5_CH = 16
S5_GROUPS = 32
S5_STATE = 64
S5_WIDTH = S5_GROUPS * S5_STATE
CMP_BLOCK = 32
CMP_HIDDEN = 128
SLC_BLOCK = 64
SLC_TOPN = 16
WINDOW = 512
QBLOCK = 128
GLA_HEADS = 4
GLA_DK = 64
GLA_DV = 128
GLA_LOWRANK = 16
GLA_TAU = 16.0
GLA_CHUNK = 64
FFN_HIDDEN = 5632
DEEPNORM_ALPHA = (2 * DEPTH) ** 0.25
LN_EPS = 1e-5

IN_SIZES = (512, 256, 256, 64, 4, 512, 512, 256, 256, 256, 24, 256, 256, 512, 16, 512)
D_IN = sum(IN_SIZES)
D_IN_PAD = 4608

VMEM_LIMIT = 56 * 1024 * 1024
LANES = 128
KEY_CHUNK = 512
CMP_LANES = 128
S5_TIME_CHUNK = 256
ATT_SCALE = HEAD_DIM ** -0.5
MASKED = -1e30
INT_MIN = -2 ** 31
F32_TINY = float(np.finfo(np.float32).tiny)

bf16 = jnp.bfloat16
f32 = jnp.float32


IN_PAD_SIZES = tuple(-(-s // LANES) * LANES for s in IN_SIZES)
IN_PAD_OFFSETS = tuple(int(o) for o in np.cumsum((0,) + IN_PAD_SIZES[:-1]))


def pad_in_projection(w_in):
    parts, off = [], 0
    for size, size_pad in zip(IN_SIZES, IN_PAD_SIZES):
        parts.append(jnp.pad(w_in[:, :, off:off + size], ((0, 0), (0, 0), (0, size_pad - size))))
        off += size
    return jnp.concatenate(parts, axis=-1).astype(bf16)


def _in_projection_kernel(x_ref, w_ref, *o_refs):
    xb = x_ref[...].astype(bf16)
    for o_ref, off, size, size_pad in zip(o_refs, IN_PAD_OFFSETS, IN_SIZES, IN_PAD_SIZES):
        y = jnp.dot(xb, w_ref[:, off:off + size_pad], preferred_element_type=f32)
        o_ref[...] = y[:, :size]


def in_projection(x, w_pad, layer):
    m, k = x.shape
    tm = min(256, m)
    return pl.pallas_call(
        _in_projection_kernel,
        out_shape=tuple(jax.ShapeDtypeStruct((m, s), f32) for s in IN_SIZES),
        grid=(m // tm,),
        in_specs=[pl.BlockSpec((tm, k), lambda i: (i, 0)),
                  pl.BlockSpec((None, k, w_pad.shape[2]), lambda i: (layer, 0, 0),
                               pipeline_mode=pl.Buffered(1))],
        out_specs=tuple(pl.BlockSpec((tm, s), lambda i: (i, 0)) for s in IN_SIZES),
        compiler_params=pltpu.CompilerParams(
            dimension_semantics=("parallel",), vmem_limit_bytes=VMEM_LIMIT),
        name="in_projection",
    )(x, w_pad)


def _layer_norm_rows(z, g, b):
    mu = jnp.mean(z, axis=-1, keepdims=True)
    zc = z - mu
    var = jnp.mean(zc * zc, axis=-1, keepdims=True)
    return zc * lax.rsqrt(var + LN_EPS) * g + b


def _outproj_ln_kernel(x_ref, m0_ref, m1_ref, m2_ref, m3_ref, w_ref, g_ref, b_ref, o_ref):
    y = DEEPNORM_ALPHA * x_ref[...]
    for j, m_ref in enumerate((m0_ref, m1_ref, m2_ref, m3_ref)):
        y = y + jnp.dot(m_ref[...].astype(bf16), w_ref[j * GROUP_WIDTH:(j + 1) * GROUP_WIDTH, :],
                        preferred_element_type=f32)
    o_ref[...] = _layer_norm_rows(y, g_ref[...], b_ref[...])


def outproj_ln(x, mixers, w_bf16, layer, g, b):
    m, d = x.shape
    tm = min(256, m)
    mix_spec = pl.BlockSpec((tm, GROUP_WIDTH), lambda i: (i, 0))
    return pl.pallas_call(
        _outproj_ln_kernel,
        out_shape=jax.ShapeDtypeStruct((m, d), f32),
        grid=(m // tm,),
        in_specs=[pl.BlockSpec((tm, d), lambda i: (i, 0)),
                  mix_spec, mix_spec, mix_spec, mix_spec,
                  pl.BlockSpec((None, d, d), lambda i: (layer, 0, 0)),
                  pl.BlockSpec((1, d), lambda i: (0, 0)),
                  pl.BlockSpec((1, d), lambda i: (0, 0))],
        out_specs=pl.BlockSpec((tm, d), lambda i: (i, 0)),
        compiler_params=pltpu.CompilerParams(
            dimension_semantics=("parallel",), vmem_limit_bytes=VMEM_LIMIT),
        name="outproj_ln",
    )(x, *mixers, w_bf16, g.reshape(1, d), b.reshape(1, d))


def _ffn_ln_kernel(h_ref, wg_ref, wu_ref, wd_ref, g_ref, b_ref, o_ref, acc_ref):
    f = pl.program_id(1)

    @pl.when(f == 0)
    def _():
        acc_ref[...] = jnp.zeros_like(acc_ref)

    hb = h_ref[...].astype(bf16)
    a = jnp.dot(hb, wg_ref[...], preferred_element_type=f32)
    u = jnp.dot(hb, wu_ref[...], preferred_element_type=f32)
    act = (a * jax.nn.sigmoid(a) * u).astype(bf16)
    acc_ref[...] += jnp.dot(act, wd_ref[...], preferred_element_type=f32)

    @pl.when(f == pl.num_programs(1) - 1)
    def _():
        z = DEEPNORM_ALPHA * h_ref[...] + acc_ref[...]
        o_ref[...] = _layer_norm_rows(z, g_ref[...], b_ref[...])


def ffn_ln(h, wg, wu, wd, layer, g, b, tf=512):
    m, d = h.shape
    fh = wg.shape[2]
    tm = min(512, m)
    return pl.pallas_call(
        _ffn_ln_kernel,
        out_shape=jax.ShapeDtypeStruct((m, d), f32),
        grid=(m // tm, fh // tf),
        in_specs=[pl.BlockSpec((tm, d), lambda i, f: (i, 0)),
                  pl.BlockSpec((None, d, tf), lambda i, f: (layer, 0, f)),
                  pl.BlockSpec((None, d, tf), lambda i, f: (layer, 0, f)),
                  pl.BlockSpec((None, tf, d), lambda i, f: (layer, f, 0)),
                  pl.BlockSpec((1, d), lambda i, f: (0, 0)),
                  pl.BlockSpec((1, d), lambda i, f: (0, 0))],
        out_specs=pl.BlockSpec((tm, d), lambda i, f: (i, 0)),
        scratch_shapes=[pltpu.VMEM((tm, d), f32)],
        compiler_params=pltpu.CompilerParams(
            dimension_semantics=("parallel", "arbitrary"), vmem_limit_bytes=VMEM_LIMIT),
        name="ffn_ln",
    )(h, wg, wu, wd, g.reshape(1, d), b.reshape(1, d))


def _tile_rows(x, reps):
    return jnp.concatenate([x] * reps, axis=0)


def _scaled_queries(q):
    assert math.log2(HEAD_DIM) % 2 == 0
    return (q * ATT_SCALE).astype(bf16)


def _masked_flash(qg, kt_at, v_at, c_lo, c_hi, bias_at):
    rows = qg.shape[0]

    def body(c, carry):
        m, l, acc = carry
        s = jnp.dot(qg, kt_at(c), preferred_element_type=f32)
        bias = bias_at(c)
        s = jnp.concatenate([s[r * QBLOCK:(r + 1) * QBLOCK] + bias for r in range(rows // QBLOCK)], axis=0)
        m_new = jnp.maximum(m, jnp.max(s, axis=1, keepdims=True))
        p = jnp.exp(s - m_new)
        alpha = jnp.exp(m - m_new)
        l = alpha * l + jnp.sum(p, axis=1, keepdims=True)
        acc = alpha * acc + jnp.dot(p.astype(bf16), v_at(c), preferred_element_type=f32)
        return m_new, l, acc

    init = (jnp.full((rows, 1), MASKED, f32), jnp.zeros((rows, 1), f32),
            jnp.zeros((rows, HEAD_DIM), f32))
    _, l, acc = lax.fori_loop(c_lo, c_hi, body, init)
    return acc / l


def _sortable_key(x):
    bits = pltpu.bitcast(x, jnp.int32)
    return bits ^ (jnp.right_shift(bits, 31) & jnp.int32(0x7FFFFFFF))


def _topk_threshold(count, rows, topk, idx_bits):
    topk = float(topk)
    cnt_nonneg = count(lambda k, i: k >= 0)
    thr0 = jnp.where(cnt_nonneg >= topk, 0, INT_MIN).astype(jnp.int32)

    def bit_body(b, thr):
        cand = thr + jnp.left_shift(jnp.int32(1), 30 - b)
        return jnp.where(count(lambda k, i: k >= cand) >= topk, cand, thr)

    thr = lax.fori_loop(0, 31, bit_body, thr0)
    need = topk - count(lambda k, i: k > thr)
    overshoot = (count(lambda k, i: k >= thr) > topk) & (thr > INT_MIN)
    any_overshoot = jnp.max(jnp.where(overshoot, 1.0, 0.0)) > 0.5

    def search_cut():
        def cut_body(b, cut):
            cand = cut + jnp.left_shift(jnp.int32(1), idx_bits - 1 - b)
            return jnp.where(count(lambda k, i: (k == thr) & (i < cand)) < need, cand, cut)
        return lax.fori_loop(0, idx_bits, cut_body, jnp.zeros((rows, 1), jnp.int32))

    cut = lax.cond(any_overshoot, search_cut, lambda: jnp.full((rows, 1), 2 ** idx_bits - 1, jnp.int32))
    return thr, cut


def _select_blocks(imp, q_pos, lane):
    ratio = SLC_BLOCK // CMP_BLOCK
    on_block_lane = (lane & (ratio - 1)) == 0
    blk = jnp.right_shift(lane, 1)
    cur = jnp.right_shift(q_pos, 6)
    future = blk > cur
    forced = (blk == 0) | (blk == cur) | (blk == cur - 1)
    imp = imp + pltpu.roll(imp, CMP_LANES - 1, 1)
    score = jnp.where(future, -jnp.inf, jnp.where(forced, jnp.inf, imp))
    score = jnp.where(on_block_lane, score, -jnp.inf)
    if imp.shape[0] == CMP_LANES:
        score_t = score.T
        blk_t = lax.broadcasted_iota(jnp.int32, score_t.shape, 0)
        rank_t = jnp.zeros(score_t.shape, f32)
        for b in range(CMP_LANES // ratio):
            other = score_t[ratio * b:ratio * b + 1, :]
            ahead = (other > score_t) | ((other == score_t) & (ratio * b < blk_t))
            rank_t = rank_t + jnp.where(ahead, 1.0, 0.0)
        rank = rank_t.T
    else:
        rank = jnp.zeros(imp.shape, f32)
        for b in range(CMP_LANES // ratio):
            col = score[:, ratio * b:ratio * b + 1]
            ahead = (col > score) | ((col == score) & (ratio * b < lane))
            rank = rank + jnp.where(ahead, 1.0, 0.0)
    return jnp.where((rank < float(SLC_TOPN)) & on_block_lane, 1.0, 0.0)


def _dsa_prompt_kernel(iq_ref, iw_ref, ikt_ref, q_ref, kt_ref, v_ref, o_ref, keys_ref, sel_ref,
                       *, idx_bits):
    i = pl.program_id(1)
    nch = (i * QBLOCK + QBLOCK + KEY_CHUNK - 1) // KEY_CHUNK
    q_pos = i * QBLOCK + lax.broadcasted_iota(jnp.int32, (QBLOCK, 1), 0)
    lane = lax.broadcasted_iota(jnp.int32, (QBLOCK, KEY_CHUNK), 1)

    iq = iq_ref[...]
    iq_stack = jnp.concatenate(
        [iq[:, h * IDX_DIM:(h + 1) * IDX_DIM] for h in range(IDX_HEADS)], axis=0).astype(bf16)
    iw = iw_ref[...] * IDX_HEADS ** -0.5

    def score_body(c, carry):
        dots = jnp.dot(iq_stack, ikt_ref[c], preferred_element_type=f32) * IDX_DIM ** -0.5
        acc = jnp.zeros((QBLOCK, KEY_CHUNK), f32)
        for h in range(IDX_HEADS):
            acc = acc + iw[:, h:h + 1] * jnp.maximum(dots[h * QBLOCK:(h + 1) * QBLOCK], 0.0)
        causal = c * KEY_CHUNK + lane <= q_pos
        keys_ref[c] = jnp.where(causal, _sortable_key(acc), INT_MIN)
        return carry

    lax.fori_loop(0, nch, score_body, 0)

    def count(pred):
        def body(c, acc):
            hit = jnp.where(pred(keys_ref[c], c * KEY_CHUNK + lane), 1.0, 0.0)
            part = hit[:, :LANES]
            for j in range(1, KEY_CHUNK // LANES):
                part = part + hit[:, j * LANES:(j + 1) * LANES]
            return acc + part
        acc = lax.fori_loop(0, nch, body, jnp.zeros((QBLOCK, LANES), f32))
        return jnp.sum(acc, axis=1, keepdims=True)

    thr, cut = _topk_threshold(count, QBLOCK, DSA_TOPK, idx_bits)

    def sel_body(c, carry):
        k = keys_ref[c]
        kidx = c * KEY_CHUNK + lane
        chosen = ((k > thr) | ((k == thr) & (kidx <= cut))) & (kidx <= q_pos)
        sel_ref[c] = jnp.where(chosen, 0.0, MASKED)
        return carry

    lax.fori_loop(0, nch, sel_body, 0)

    for g in range(KV_HEADS):
        o = _masked_flash(_scaled_queries(q_ref[g]), lambda c: kt_ref[g, c], lambda c: v_ref[g, c],
                          0, nch, lambda c: sel_ref[c])
        for r in range(Q_PER_KV):
            h = g * Q_PER_KV + r
            o_ref[:, h * HEAD_DIM:(h + 1) * HEAD_DIM] = o[r * QBLOCK:(r + 1) * QBLOCK]


def _stack_query_heads(q, n, t):
    q = q.reshape(n, t // QBLOCK, QBLOCK, KV_HEADS, Q_PER_KV, HEAD_DIM)
    q = q.transpose(0, 3, 1, 4, 2, 5)
    return q.reshape(n, KV_HEADS, t // QBLOCK, Q_PER_KV * QBLOCK, HEAD_DIM)


def _chunked_kv(kv, n, t, chunk):
    kv = kv.reshape(n, t // chunk, chunk, KV_HEADS, 2, HEAD_DIM).astype(bf16)
    kt = kv[:, :, :, :, 0, :].transpose(0, 3, 1, 4, 2)
    v = kv[:, :, :, :, 1, :].transpose(0, 3, 1, 2, 4)
    return kt, v


def dsa_prompt(dq, dkv, iq, ik, iw):
    n, t = dq.shape[:2]
    nchunks = t // KEY_CHUNK
    idx_bits = int(math.log2(t))
    assert 2 ** idx_bits == t and t % KEY_CHUNK == 0 and min(DSA_TOPK, t // 4) == DSA_TOPK
    ikt = ik.reshape(n, nchunks, KEY_CHUNK, IDX_DIM).transpose(0, 1, 3, 2).astype(bf16)
    qs = _stack_query_heads(dq, n, t)
    kt, v = _chunked_kv(dkv, n, t, KEY_CHUNK)
    return pl.pallas_call(
        functools.partial(_dsa_prompt_kernel, idx_bits=idx_bits),
        out_shape=jax.ShapeDtypeStruct((n, t, GROUP_WIDTH), f32),
        grid=(n, t // QBLOCK),
        in_specs=[
            pl.BlockSpec((None, QBLOCK, IDX_HEADS * IDX_DIM), lambda b, i: (b, i, 0)),
            pl.BlockSpec((None, QBLOCK, IDX_HEADS), lambda b, i: (b, i, 0)),
            pl.BlockSpec((None, nchunks, IDX_DIM, KEY_CHUNK), lambda b, i: (b, 0, 0, 0)),
            pl.BlockSpec((None, KV_HEADS, None, Q_PER_KV * QBLOCK, HEAD_DIM), lambda b, i: (b, 0, i, 0, 0)),
            pl.BlockSpec((None, KV_HEADS, nchunks, HEAD_DIM, KEY_CHUNK), lambda b, i: (b, 0, 0, 0, 0)),
            pl.BlockSpec((None, KV_HEADS, nchunks, KEY_CHUNK, HEAD_DIM), lambda b, i: (b, 0, 0, 0, 0)),
        ],
        out_specs=pl.BlockSpec((None, QBLOCK, GROUP_WIDTH), lambda b, i: (b, i, 0)),
        scratch_shapes=[pltpu.VMEM((nchunks, QBLOCK, KEY_CHUNK), jnp.int32),
                        pltpu.VMEM((nchunks, QBLOCK, KEY_CHUNK), f32)],
        compiler_params=pltpu.CompilerParams(
            dimension_semantics=("parallel", "arbitrary"), vmem_limit_bytes=VMEM_LIMIT),
        name="dsa_prompt",
    )(iq, iw, ikt, qs, kt, v)


def _nsa_compress_kernel(x_ref, pe_ref, w1_ref, w2_ref, o_ref, hid_ref):
    k = pl.program_id(0)

    @pl.when(k == 0)
    def _():
        hid_ref[...] = jnp.zeros_like(hid_ref)

    hid_ref[...] += jnp.dot((x_ref[...] + pe_ref[...]).astype(bf16), w1_ref[...],
                            preferred_element_type=f32)

    @pl.when(k == pl.num_programs(0) - 1)
    def _():
        hid = jax.nn.gelu(hid_ref[...]).astype(bf16)
        o_ref[...] = jnp.dot(hid, w2_ref[...], preferred_element_type=f32)


def nsa_compress_weights(cmp_pe, cmp_w1, cmp_w2):
    w1 = cmp_w1.reshape(2, CMP_BLOCK, HEAD_DIM, CMP_HIDDEN)
    eye_g = jnp.eye(KV_HEADS, dtype=cmp_w1.dtype)
    eye_j = jnp.eye(2, dtype=cmp_w1.dtype)
    w1_big = jnp.einsum('jtde,gh,jk->tgjdhke', w1, eye_g, eye_j)
    w1_big = w1_big.reshape(CMP_BLOCK * KV_ROW, KV_HEADS * 2 * CMP_HIDDEN).astype(bf16)
    w2_big = jnp.einsum('jed,gh,jk->gjehkd', cmp_w2, eye_g, eye_j)
    w2_big = w2_big.reshape(KV_HEADS * 2 * CMP_HIDDEN, KV_ROW).astype(bf16)
    pe_row = jnp.broadcast_to(cmp_pe[:, None], (CMP_BLOCK, KV_HEADS, 2, HEAD_DIM))
    return pe_row.reshape(1, CMP_BLOCK * KV_ROW), w1_big, w2_big


def nsa_compress_rows(x, pe_row, w1_big, w2_big, tk=2048):
    m, kdim = x.shape
    hw = w1_big.shape[1]
    return pl.pallas_call(
        _nsa_compress_kernel,
        out_shape=jax.ShapeDtypeStruct((m, KV_ROW), f32),
        grid=(kdim // tk,),
        in_specs=[pl.BlockSpec((m, tk), lambda k: (0, k)),
                  pl.BlockSpec((1, tk), lambda k: (0, k)),
                  pl.BlockSpec((tk, hw), lambda k: (k, 0)),
                  pl.BlockSpec((hw, KV_ROW), lambda k: (0, 0))],
        out_specs=pl.BlockSpec((m, KV_ROW), lambda k: (0, 0)),
        scratch_shapes=[pltpu.VMEM((m, hw), f32)],
        compiler_params=pltpu.CompilerParams(
            dimension_semantics=("arbitrary",), vmem_limit_bytes=VMEM_LIMIT),
        name="nsa_compress",
    )(x, pe_row, w1_big, w2_big)


def _nsa_prompt_kernel(gate_ref, q_ref, ckt_ref, cv_ref, skt_ref, sv_ref, wkt_ref, wv_ref, o_ref):
    i = pl.program_id(1)
    nch = (i * QBLOCK + QBLOCK + KEY_CHUNK - 1) // KEY_CHUNK
    q_pos = i * QBLOCK + lax.broadcasted_iota(jnp.int32, (QBLOCK, 1), 0)
    lane = lax.broadcasted_iota(jnp.int32, (QBLOCK, CMP_LANES), 1)
    lane_k = lax.broadcasted_iota(jnp.int32, (QBLOCK, KEY_CHUNK), 1)
    expand_row = lax.broadcasted_iota(jnp.int32, (CMP_LANES, KEY_CHUNK), 0)
    expand_col = lax.broadcasted_iota(jnp.int32, (CMP_LANES, KEY_CHUNK), 1)
    gate = jax.nn.sigmoid(gate_ref[...])
    ratio = SLC_BLOCK // CMP_BLOCK
    c_vis = _tile_rows(jnp.where((lane + 1) * CMP_BLOCK - 1 <= q_pos, 1.0, 0.0), Q_PER_KV) > 0.5
    w_tiles = WINDOW // QBLOCK + 1
    w_lo = jnp.maximum(i - WINDOW // QBLOCK, 0)
    w_idx = w_lo * QBLOCK + lax.broadcasted_iota(jnp.int32, (QBLOCK, w_tiles * QBLOCK), 1)
    w_bias = _tile_rows(jnp.where((w_idx <= q_pos) & (w_idx > q_pos - WINDOW), 0.0, MASKED), Q_PER_KV)

    for g in range(KV_HEADS):
        qg = _scaled_queries(q_ref[g])
        s = jnp.dot(qg, ckt_ref[g], preferred_element_type=f32)
        s = jnp.where(c_vis, s, -jnp.inf)
        m = jnp.max(s, axis=1, keepdims=True)
        m = jnp.where(m == -jnp.inf, 0.0, m)
        p = jnp.exp(s - m)
        p = p / jnp.maximum(jnp.sum(p, axis=1, keepdims=True), F32_TINY)
        o_cmp = jnp.dot(p.astype(bf16), cv_ref[g], preferred_element_type=f32)
        imp = p[0:QBLOCK]
        for r in range(1, Q_PER_KV):
            imp = imp + p[r * QBLOCK:(r + 1) * QBLOCK]
        sel_blk = _select_blocks(imp, q_pos, lane).astype(bf16)

        def slc_mask(c):
            kidx = c * KEY_CHUNK + lane_k
            expand = jnp.where(
                expand_row == ratio * jnp.right_shift(c * KEY_CHUNK + expand_col, 6), 1.0, 0.0).astype(bf16)
            picked = jnp.dot(sel_blk, expand, preferred_element_type=f32)
            return jnp.where((picked > 0.5) & (kidx <= q_pos), 0.0, MASKED)

        o_slc = _masked_flash(qg, lambda c: skt_ref[g, c], lambda c: sv_ref[g, c], 0, nch, slc_mask)

        w_kt = jnp.concatenate([wkt_ref[g, w_lo + j] for j in range(w_tiles)], axis=1)
        w_v = jnp.concatenate([wv_ref[g, w_lo + j] for j in range(w_tiles)], axis=0)
        s = jnp.dot(qg, w_kt, preferred_element_type=f32) + w_bias
        pw = jnp.exp(s - jnp.max(s, axis=1, keepdims=True))
        o_win = (jnp.dot(pw.astype(bf16), w_v, preferred_element_type=f32)
                 / jnp.sum(pw, axis=1, keepdims=True))
        for r in range(Q_PER_KV):
            h = g * Q_PER_KV + r
            rows = slice(r * QBLOCK, (r + 1) * QBLOCK)
            o_ref[:, h * HEAD_DIM:(h + 1) * HEAD_DIM] = (
                gate[:, 3 * h:3 * h + 1] * o_cmp[rows]
                + gate[:, 3 * h + 1:3 * h + 2] * o_slc[rows]
                + gate[:, 3 * h + 2:3 * h + 3] * o_win[rows])


def nsa_prompt(nq, ng, ckv, skv, wkv, cmp_weights):
    n, t = nq.shape[:2]
    nc = t // CMP_BLOCK
    assert nc <= CMP_LANES and t % KEY_CHUNK == 0 and SLC_BLOCK == 64 and SLC_BLOCK // CMP_BLOCK == 2
    assert t >= WINDOW + QBLOCK and WINDOW % QBLOCK == 0
    cmp_rows = nsa_compress_rows(ckv.reshape(n * nc, CMP_BLOCK * KV_ROW), *cmp_weights)
    cmp_rows = cmp_rows.reshape(n, nc, KV_HEADS, 2, HEAD_DIM)
    cmp_rows = jnp.pad(cmp_rows, ((0, 0), (0, CMP_LANES - nc), (0, 0), (0, 0), (0, 0))).astype(bf16)
    ckt = cmp_rows[:, :, :, 0, :].transpose(0, 2, 3, 1)
    cv = cmp_rows[:, :, :, 1, :].transpose(0, 2, 1, 3)
    qs = _stack_query_heads(nq, n, t)
    skt, sv = _chunked_kv(skv, n, t, KEY_CHUNK)
    wkt, wv = _chunked_kv(wkv, n, t, QBLOCK)
    nk, nw = t // KEY_CHUNK, t // QBLOCK
    return pl.pallas_call(
        _nsa_prompt_kernel,
        out_shape=jax.ShapeDtypeStruct((n, t, GROUP_WIDTH), f32),
        grid=(n, t // QBLOCK),
        in_specs=[
            pl.BlockSpec((None, QBLOCK, 3 * N_HEADS), lambda b, i: (b, i, 0)),
            pl.BlockSpec((None, KV_HEADS, None, Q_PER_KV * QBLOCK, HEAD_DIM), lambda b, i: (b, 0, i, 0, 0)),
            pl.BlockSpec((None, KV_HEADS, HEAD_DIM, CMP_LANES), lambda b, i: (b, 0, 0, 0)),
            pl.BlockSpec((None, KV_HEADS, CMP_LANES, HEAD_DIM), lambda b, i: (b, 0, 0, 0)),
            pl.BlockSpec((None, KV_HEADS, nk, HEAD_DIM, KEY_CHUNK), lambda b, i: (b, 0, 0, 0, 0)),
            pl.BlockSpec((None, KV_HEADS, nk, KEY_CHUNK, HEAD_DIM), lambda b, i: (b, 0, 0, 0, 0)),
            pl.BlockSpec((None, KV_HEADS, nw, HEAD_DIM, QBLOCK), lambda b, i: (b, 0, 0, 0, 0)),
            pl.BlockSpec((None, KV_HEADS, nw, QBLOCK, HEAD_DIM), lambda b, i: (b, 0, 0, 0, 0)),
        ],
        out_specs=pl.BlockSpec((None, QBLOCK, GROUP_WIDTH), lambda b, i: (b, i, 0)),
        compiler_params=pltpu.CompilerParams(
            dimension_semantics=("parallel", "arbitrary"), vmem_limit_bytes=VMEM_LIMIT),
        name="nsa_prompt",
    )(ng, qs, ckt, cv, skt, sv, wkt, wv)


def s5_discretize(w):
    a_re, a_im = w['a_re'], w['a_im']
    dt = jnp.exp(w['log_dt'])[:, None]
    mag = jnp.exp(a_re * dt)
    ab_re, ab_im = mag * jnp.cos(a_im * dt), mag * jnp.sin(a_im * dt)
    den = a_re * a_re + a_im * a_im
    nr, ni = ab_re - 1.0, ab_im
    f_re = (nr * a_re + ni * a_im) / den
    f_im = (ni * a_re - nr * a_im) / den
    b_re, b_im = w['b_re'], w['b_im']
    bb_re = f_re[..., None] * b_re - f_im[..., None] * b_im
    bb_im = f_re[..., None] * b_im + f_im[..., None] * b_re
    eye = jnp.eye(S5_GROUPS, dtype=f32)

    def in_map(bb):
        return jnp.einsum('gsc,gh->gchs', bb, eye).reshape(GROUP_WIDTH, S5_WIDTH).astype(bf16)

    def out_map(cc):
        return jnp.einsum('gcs,gh->gshc', cc, eye).reshape(S5_WIDTH, GROUP_WIDTH).astype(bf16)

    return dict(a_re=ab_re.reshape(1, S5_WIDTH), a_im=ab_im.reshape(1, S5_WIDTH),
                b_re=in_map(bb_re), b_im=in_map(bb_im),
                c_re=out_map(w['c_re']), c_im=out_map(w['c_im']),
                d=w['d'].reshape(1, GROUP_WIDTH), w_glu=w['w_glu'].astype(bf16),
                b_glu=w['b_glu'].reshape(1, GROUP_WIDTH))


def _s5_prompt_kernel(u_ref, h0_ref, are_ref, aim_ref, bre_ref, bim_ref, cre_ref, cim_ref, d_ref,
                      wglu_ref, bglu_ref, o_ref, hlast_ref, state_ref, bure_ref, buim_ref, hre_ref, him_ref):
    j = pl.program_id(1)

    @pl.when(j == 0)
    def _():
        state_ref[...] = h0_ref[...]

    u = u_ref[...]
    ub = u.astype(bf16)
    bure_ref[...] = jnp.dot(ub, bre_ref[...], preferred_element_type=f32)
    buim_ref[...] = jnp.dot(ub, bim_ref[...], preferred_element_type=f32)
    a_re, a_im = are_ref[...], aim_ref[...]

    def step(t, carry):
        h_re, h_im = carry
        row = pl.ds(t, 1)
        n_re = a_re * h_re - a_im * h_im + bure_ref[row, :]
        n_im = a_re * h_im + a_im * h_re + buim_ref[row, :]
        hre_ref[row, :] = n_re
        him_ref[row, :] = n_im
        return n_re, n_im

    h_re, h_im = lax.fori_loop(0, u.shape[0], step, (state_ref[0:1, :], state_ref[1:2, :]), unroll=8)
    state_ref[0:1, :] = h_re
    state_ref[1:2, :] = h_im
    hlast_ref[...] = state_ref[...]
    y = (jnp.dot(hre_ref[...].astype(bf16), cre_ref[...], preferred_element_type=f32)
         - jnp.dot(him_ref[...].astype(bf16), cim_ref[...], preferred_element_type=f32)
         + d_ref[...] * u)
    y = jax.nn.gelu(y)
    o_ref[...] = y * jax.nn.sigmoid(
        jnp.dot(y.astype(bf16), wglu_ref[...], preferred_element_type=f32) + bglu_ref[...])


def s5_prompt(u, h0, sw):
    n, t = u.shape[:2]
    tc = min(S5_TIME_CHUNK, t)
    const = lambda shape: pl.BlockSpec(shape, lambda b, j: (0,) * len(shape))
    return pl.pallas_call(
        _s5_prompt_kernel,
        out_shape=(jax.ShapeDtypeStruct((n, t, GROUP_WIDTH), f32),
                   jax.ShapeDtypeStruct((n, 2, S5_WIDTH), f32)),
        grid=(n, t // tc),
        in_specs=[pl.BlockSpec((None, tc, GROUP_WIDTH), lambda b, j: (b, j, 0)),
                  pl.BlockSpec((None, 2, S5_WIDTH), lambda b, j: (b, 0, 0)),
                  const((1, S5_WIDTH)), const((1, S5_WIDTH)),
                  const((GROUP_WIDTH, S5_WIDTH)), const((GROUP_WIDTH, S5_WIDTH)),
                  const((S5_WIDTH, GROUP_WIDTH)), const((S5_WIDTH, GROUP_WIDTH)),
                  const((1, GROUP_WIDTH)), const((GROUP_WIDTH, GROUP_WIDTH)), const((1, GROUP_WIDTH))],
        out_specs=(pl.BlockSpec((None, tc, GROUP_WIDTH), lambda b, j: (b, j, 0)),
                   pl.BlockSpec((None, 2, S5_WIDTH), lambda b, j: (b, 0, 0))),
        scratch_shapes=[pltpu.VMEM((2, S5_WIDTH), f32)] + [pltpu.VMEM((tc, S5_WIDTH), f32)] * 4,
        compiler_params=pltpu.CompilerParams(
            dimension_semantics=("parallel", "arbitrary"), vmem_limit_bytes=VMEM_LIMIT),
        name="s5_prompt",
    )(u, h0, sw['a_re'], sw['a_im'], sw['b_re'], sw['b_im'], sw['c_re'], sw['c_im'], sw['d'],
      sw['w_glu'], sw['b_glu'])


def _s5_step_kernel(u_ref, h0_ref, are_ref, aim_ref, bre_ref, bim_ref, cre_ref, cim_ref, d_ref,
                    wglu_ref, bglu_ref, o_ref, h_ref):
    u = u_ref[...]
    ub = u.astype(bf16)
    h0_re, h0_im = h0_ref[:, :S5_WIDTH], h0_ref[:, S5_WIDTH:]
    a_re, a_im = are_ref[...], aim_ref[...]
    h_re = a_re * h0_re - a_im * h0_im + jnp.dot(ub, bre_ref[...], preferred_element_type=f32)
    h_im = a_re * h0_im + a_im * h0_re + jnp.dot(ub, bim_ref[...], preferred_element_type=f32)
    h_ref[:, :S5_WIDTH] = h_re
    h_ref[:, S5_WIDTH:] = h_im
    y = (jnp.dot(h_re.astype(bf16), cre_ref[...], preferred_element_type=f32)
         - jnp.dot(h_im.astype(bf16), cim_ref[...], preferred_element_type=f32)
         + d_ref[...] * u)
    y = jax.nn.gelu(y)
    o_ref[...] = y * jax.nn.sigmoid(
        jnp.dot(y.astype(bf16), wglu_ref[...], preferred_element_type=f32) + bglu_ref[...])


def s5_step(u, h0, sw):
    n = u.shape[0]
    return pl.pallas_call(
        _s5_step_kernel,
        out_shape=(jax.ShapeDtypeStruct((n, GROUP_WIDTH), f32),
                   jax.ShapeDtypeStruct((n, 2 * S5_WIDTH), f32)),
        compiler_params=pltpu.CompilerParams(vmem_limit_bytes=VMEM_LIMIT),
        name="s5_step",
    )(u, h0, sw['a_re'], sw['a_im'], sw['b_re'], sw['b_im'], sw['c_re'], sw['c_im'], sw['d'],
      sw['w_glu'], sw['b_glu'])


GLA_SUB = 16
GLA_QK = GLA_HEADS * GLA_DK
HIGHEST = lax.Precision.HIGHEST
_NT = (((1,), (1,)), ((), ()))
_TN = (((0,), (0,)), ((), ()))


def _log_decay(ga, wgate_ref, bgate_ref):
    z = jnp.dot(ga, wgate_ref[...], preferred_element_type=f32, precision=HIGHEST) + bgate_ref[...]
    return jax.nn.log_sigmoid(z) / GLA_TAU


def _head_sum_matrix():
    r = lax.broadcasted_iota(jnp.int32, (GLA_QK, GROUP_WIDTH), 0) // GLA_DK
    c = lax.broadcasted_iota(jnp.int32, (GLA_QK, GROUP_WIDTH), 1) // GLA_DV
    return jnp.where(r == c, 1.0, 0.0)


def _gla_finish(o, gr, normg_ref):
    outs = []
    for h in range(GLA_HEADS):
        oh = o[:, h * GLA_DV:(h + 1) * GLA_DV]
        outs.append(oh * lax.rsqrt(jnp.mean(oh * oh, axis=-1, keepdims=True) + LN_EPS) * normg_ref[...])
    return jnp.concatenate(outs, axis=1) * (gr * jax.nn.sigmoid(gr))


def _gla_prompt_kernel(q_ref, k_ref, v_ref, ga_ref, gr_ref, wgate_ref, bgate_ref, normg_ref, s0_ref,
                       o_ref, slast_ref, st_ref, oi_ref):
    j = pl.program_id(1)

    @pl.when(j == 0)
    def _():
        st_ref[...] = s0_ref[...]

    rows = q_ref.shape[0]
    nsub = rows // GLA_SUB
    q = q_ref[...] * GLA_DK ** -0.5
    k = k_ref[...]
    v = v_ref[...]
    log_a = _log_decay(ga_ref[...], wgate_ref, bgate_ref)
    ri = lax.broadcasted_iota(jnp.int32, (rows, rows), 0)
    ci = lax.broadcasted_iota(jnp.int32, (rows, rows), 1)
    tri = jnp.where((ri // GLA_SUB == ci // GLA_SUB) & (ci <= ri), 1.0, 0.0)
    b = jnp.dot(tri, log_a, preferred_element_type=f32, precision=HIGHEST)
    row_in_sub = lax.broadcasted_iota(jnp.int32, (rows, 1), 0) % GLA_SUB
    head_sum = _head_sum_matrix()

    def sub_row(x, jj):
        x3 = x.reshape(nsub, GLA_SUB, x.shape[1])
        return jnp.broadcast_to(x3[:, jj:jj + 1, :], x3.shape).reshape(x.shape)

    head_sum = head_sum.astype(bf16)
    o = jnp.zeros((rows, GROUP_WIDTH), f32)
    for jj in range(GLA_SUB):
        decay = jnp.exp(jnp.minimum(b - sub_row(b, jj), 0.0))
        prod = jnp.where(row_in_sub >= jj, q * sub_row(k, jj) * decay, 0.0)
        prod_hi = prod.astype(bf16)
        prod_lo = (prod - prod_hi.astype(f32)).astype(bf16)
        att = (jnp.dot(prod_hi, head_sum, preferred_element_type=f32)
               + jnp.dot(prod_lo, head_sum, preferred_element_type=f32))
        o = o + att * sub_row(v, jj)

    for i in range(nsub):
        blk = slice(i * GLA_SUB, (i + 1) * GLA_SUB)
        b_blk = b[blk]
        b_end = b[(i + 1) * GLA_SUB - 1:(i + 1) * GLA_SUB]
        q_blk = (q[blk] * jnp.exp(b_blk)).astype(bf16)
        k_blk = (k[blk] * jnp.exp(b_end - b_blk)).astype(bf16)
        v_blk = v[blk].astype(bf16)
        a_end = jnp.exp(b_end)
        for h in range(GLA_HEADS):
            ks = slice(h * GLA_DK, (h + 1) * GLA_DK)
            vs = slice(h * GLA_DV, (h + 1) * GLA_DV)
            st = st_ref[h]
            oi_ref[blk, vs] = lax.dot_general(q_blk[:, ks], st.astype(bf16), _NT, preferred_element_type=f32)
            st_ref[h] = st * a_end[:, ks] + lax.dot_general(v_blk[:, vs], k_blk[:, ks], _TN,
                                                            preferred_element_type=f32)

    o_ref[...] = _gla_finish(o + oi_ref[...], gr_ref[...], normg_ref)
    slast_ref[...] = st_ref[...]


def gla_prompt(gq, gk, gv, ga, gr, w_gate, b_gate, norm_g):
    n, t = gq.shape[:2]
    tc = min(GLA_CHUNK, t)
    rows = lambda width: pl.BlockSpec((None, tc, width), lambda b, j: (b, j, 0))
    const = lambda shape: pl.BlockSpec(shape, lambda b, j: (0,) * len(shape))
    state = pl.BlockSpec((None, GLA_HEADS, GLA_DV, GLA_DK), lambda b, j: (b, 0, 0, 0))
    o, s_t = pl.pallas_call(
        _gla_prompt_kernel,
        out_shape=(jax.ShapeDtypeStruct((n, t, GROUP_WIDTH), f32),
                   jax.ShapeDtypeStruct((n, GLA_HEADS, GLA_DV, GLA_DK), f32)),
        grid=(n, t // tc),
        in_specs=[rows(GLA_QK), rows(GLA_QK), rows(GROUP_WIDTH), rows(GLA_LOWRANK), rows(GROUP_WIDTH),
                  const((GLA_LOWRANK, GLA_QK)), const((1, GLA_QK)), const((1, GLA_DV)), state],
        out_specs=(rows(GROUP_WIDTH), state),
        scratch_shapes=[pltpu.VMEM((GLA_HEADS, GLA_DV, GLA_DK), f32), pltpu.VMEM((tc, GROUP_WIDTH), f32)],
        compiler_params=pltpu.CompilerParams(
            dimension_semantics=("parallel", "arbitrary"), vmem_limit_bytes=VMEM_LIMIT),
        name="gla_prompt",
    )(gq, gk, gv, ga, gr, w_gate, b_gate.reshape(1, GLA_QK), norm_g.reshape(1, GLA_DV),
      jnp.zeros((n, GLA_HEADS, GLA_DV, GLA_DK), f32))
    return o, s_t.transpose(0, 1, 3, 2)


GLA_STEP_SEQS = 8


def _gla_step_kernel(q_ref, k_ref, v_ref, ga_ref, gr_ref, wgate_ref, bgate_ref, normg_ref, s_ref,
                     o_ref, snew_ref):
    q = q_ref[...] * GLA_DK ** -0.5
    k = k_ref[...]
    v = v_ref[...]
    a = jnp.exp(_log_decay(ga_ref[...], wgate_ref, bgate_ref))
    qa = q * a
    seq = lax.broadcasted_iota(jnp.int32, (GLA_STEP_SEQS, 1), 0)
    o = jnp.dot(q * k, _head_sum_matrix(), preferred_element_type=f32, precision=HIGHEST) * v
    o_heads = [jnp.zeros((GLA_STEP_SEQS, GLA_DV), f32) for _ in range(GLA_HEADS)]
    for i in range(GLA_STEP_SEQS):
        mine = seq == i
        pick = jnp.broadcast_to(jnp.where(mine, 1.0, 0.0), (GLA_STEP_SEQS, GLA_DV))
        qa_i = jnp.where(mine, qa, 0.0).astype(bf16)
        k_i = jnp.where(mine, k, 0.0).astype(bf16)
        vb = v.astype(bf16)
        for h in range(GLA_HEADS):
            ks = slice(h * GLA_DK, (h + 1) * GLA_DK)
            vs = slice(h * GLA_DV, (h + 1) * GLA_DV)
            s_old = s_ref[i, h]
            a_rows = lax.dot_general(a[:, ks], pick, _TN, preferred_element_type=f32, precision=HIGHEST)
            o_heads[h] = o_heads[h] + jnp.dot(qa_i[:, ks], s_old.astype(bf16), preferred_element_type=f32)
            snew_ref[i, h] = a_rows * s_old + lax.dot_general(k_i[:, ks], vb[:, vs], _TN,
                                                               preferred_element_type=f32)
    o_ref[...] = _gla_finish(o + jnp.concatenate(o_heads, axis=1), gr_ref[...], normg_ref)


def gla_step(gq, gk, gv, ga, gr, w_gate, b_gate, norm_g, s0):
    n = gq.shape[0]
    rows = lambda width: pl.BlockSpec((GLA_STEP_SEQS, width), lambda i: (i, 0))
    const = lambda shape: pl.BlockSpec(shape, lambda i: (0,) * len(shape))
    state = pl.BlockSpec((GLA_STEP_SEQS, GLA_HEADS, GLA_DK, GLA_DV), lambda i: (i, 0, 0, 0))
    return pl.pallas_call(
        _gla_step_kernel,
        out_shape=(jax.ShapeDtypeStruct((n, GROUP_WIDTH), f32),
                   jax.ShapeDtypeStruct((n, GLA_HEADS, GLA_DK, GLA_DV), f32)),
        grid=(n // GLA_STEP_SEQS,),
        in_specs=[rows(GLA_QK), rows(GLA_QK), rows(GROUP_WIDTH), rows(GLA_LOWRANK), rows(GROUP_WIDTH),
                  const((GLA_LOWRANK, GLA_QK)), const((1, GLA_QK)), const((1, GLA_DV)), state],
        out_specs=(rows(GROUP_WIDTH), state),
        compiler_params=pltpu.CompilerParams(
            dimension_semantics=("parallel",), vmem_limit_bytes=VMEM_LIMIT),
        name="gla_step",
    )(gq, gk, gv, ga, gr, w_gate, b_gate.reshape(1, GLA_QK), norm_g.reshape(1, GLA_DV), s0)


N_PAGES = PAST_LEN // PAGE_SIZE
KEY_PAD = PAST_LEN + LANES
STEP_ROWS = 8


def _page_specs(block, seqs_per_step=1):
    zeros = (0,) * (len(block) - 1)
    return [pl.BlockSpec(block, lambda i, pt, s=s, p=p: (pt[i * seqs_per_step + s, p],) + zeros)
            for s in range(seqs_per_step) for p in range(N_PAGES)]


def _seq_spec(*tail):
    zeros = (0,) * len(tail)
    return pl.BlockSpec((None,) + tail, lambda i, pt: (i,) + zeros)


def _heads_on_kv_lanes(q):
    n = q.shape[0]
    qh = q.reshape(n, N_HEADS, HEAD_DIM)
    z = jnp.zeros((n, Q_PER_KV, HEAD_DIM), q.dtype)
    top = jnp.concatenate([qh[:, :Q_PER_KV], z, z, z], axis=-1)
    bot = jnp.concatenate([z, z, qh[:, Q_PER_KV:], z], axis=-1)
    return jnp.concatenate([top, bot], axis=1)


def _heads_from_kv_lanes(o8):
    n = o8.shape[0]
    o = o8.reshape(n, N_HEADS, KV_HEADS, 2, HEAD_DIM)
    return jnp.concatenate([o[:, :Q_PER_KV, 0, 1], o[:, Q_PER_KV:, 1, 1]], axis=1).reshape(n, GROUP_WIDTH)


def _step_attention(q8, key_blocks, masks, kv_new, new_ok):
    qb = q8.astype(bf16)
    scores = []
    s_new = jnp.sum(q8 * kv_new, axis=1, keepdims=True) * ATT_SCALE
    if new_ok is not None:
        s_new = jnp.where(new_ok, s_new, MASKED)
    m = s_new
    for blk, msk in zip(key_blocks, masks):
        s = lax.dot_general(qb, blk[...].astype(bf16), _NT, preferred_element_type=f32) * ATT_SCALE
        s = jnp.where(msk > 0.5, s, MASKED)
        scores.append(s)
        m = jnp.maximum(m, jnp.max(s, axis=1, keepdims=True))
    p_new = jnp.exp(s_new - m)
    if new_ok is not None:
        p_new = jnp.where(new_ok, p_new, 0.0)
    l = p_new
    acc = p_new * kv_new
    for blk, msk, s in zip(key_blocks, masks, scores):
        p = jnp.where(msk > 0.5, jnp.exp(s - m), 0.0)
        l = l + jnp.sum(p, axis=1, keepdims=True)
        acc = acc + jnp.dot(p.astype(bf16), blk[...].astype(bf16), preferred_element_type=f32)
    return acc / l


def _dsa_step_scores_kernel(pt_ref, iq_ref, iw_ref, iknew_ref, *rest):
    pages, keys_ref, kall_ref = rest[:N_PAGES], rest[N_PAGES], rest[N_PAGES + 1]
    for p in range(N_PAGES):
        kall_ref[p * PAGE_SIZE:(p + 1) * PAGE_SIZE, :] = pages[p][...]
    kall_ref[PAST_LEN:PAST_LEN + 1, :] = iknew_ref[...]
    kall_ref[PAST_LEN + 1:, :] = jnp.zeros((KEY_PAD - PAST_LEN - 1, IDX_DIM), f32)
    dots = lax.dot_general(iq_ref[...].astype(bf16), kall_ref[...].astype(bf16), _NT,
                           preferred_element_type=f32) * IDX_DIM ** -0.5
    iw = iw_ref[...] * IDX_HEADS ** -0.5
    lane = lax.broadcasted_iota(jnp.int32, (1, LANES), 1)
    for c in range(KEY_PAD // LANES):
        tile = slice(c * LANES, (c + 1) * LANES)
        score = jnp.sum(iw * jnp.maximum(dots[:, tile], 0.0), axis=0, keepdims=True)
        key = _sortable_key(score)
        if c == N_PAGES:
            key = jnp.where(lane == 0, key, INT_MIN)
        keys_ref[:, tile] = key


def _topk_rows_kernel(keys_ref, sel_ref, *, topk, idx_bits, n_valid):
    keys = keys_ref[...]
    lane = lax.broadcasted_iota(jnp.int32, keys.shape, 1)
    count = lambda pred: jnp.sum(jnp.where(pred(keys, lane), 1.0, 0.0), axis=1, keepdims=True)
    thr, cut = _topk_threshold(count, keys.shape[0], topk, idx_bits)
    chosen = ((keys > thr) | ((keys == thr) & (lane <= cut))) & (lane < n_valid)
    sel_ref[...] = jnp.where(chosen, 1.0, 0.0)


def _dsa_step_attend_kernel(pt_ref, q_ref, sel_ref, kvnew_ref, *rest):
    pages, o_ref = rest[:N_PAGES], rest[N_PAGES]
    masks = [sel_ref[:, p * PAGE_SIZE:(p + 1) * PAGE_SIZE] for p in range(N_PAGES)]
    new_ok = sel_ref[:, PAST_LEN:PAST_LEN + 1] > 0.5
    o_ref[...] = _step_attention(q_ref[...], pages, masks, kvnew_ref[...], new_ok)


def dsa_step(dq, dkv, iq, ik, iw, pool_kv, pool_idx, page_table):
    n = dq.shape[0]
    assert min(DSA_TOPK, (PAST_LEN + 1) // 4) == DSA_TOPK and PAGE_SIZE == LANES
    pad_heads = ((0, 0), (0, STEP_ROWS - IDX_HEADS), (0, 0))
    iq8 = jnp.pad(iq.reshape(n, IDX_HEADS, IDX_DIM), pad_heads)
    iw8 = jnp.broadcast_to(jnp.pad(iw.reshape(n, IDX_HEADS, 1), pad_heads), (n, STEP_ROWS, LANES))
    keys = pl.pallas_call(
        _dsa_step_scores_kernel,
        out_shape=jax.ShapeDtypeStruct((n, 1, KEY_PAD), jnp.int32),
        grid_spec=pltpu.PrefetchScalarGridSpec(
            num_scalar_prefetch=1, grid=(n,),
            in_specs=[_seq_spec(STEP_ROWS, IDX_DIM), _seq_spec(STEP_ROWS, LANES), _seq_spec(1, IDX_DIM)]
            + _page_specs((None, PAGE_SIZE, IDX_DIM)),
            out_specs=_seq_spec(1, KEY_PAD),
            scratch_shapes=[pltpu.VMEM((KEY_PAD, IDX_DIM), f32)]),
        compiler_params=pltpu.CompilerParams(
            dimension_semantics=("parallel",), vmem_limit_bytes=VMEM_LIMIT),
        name="dsa_step_scores",
    )(page_table, iq8, iw8, ik.reshape(n, 1, IDX_DIM), *([pool_idx] * N_PAGES))
    sel = pl.pallas_call(
        functools.partial(_topk_rows_kernel, topk=DSA_TOPK, idx_bits=int(math.ceil(math.log2(KEY_PAD))),
                          n_valid=PAST_LEN + 1),
        out_shape=jax.ShapeDtypeStruct((n, KEY_PAD), f32),
        compiler_params=pltpu.CompilerParams(vmem_limit_bytes=VMEM_LIMIT),
        name="dsa_step_topk",
    )(keys.reshape(n, KEY_PAD))
    o8 = pl.pallas_call(
        _dsa_step_attend_kernel,
        out_shape=jax.ShapeDtypeStruct((n, N_HEADS, KV_ROW), f32),
        grid_spec=pltpu.PrefetchScalarGridSpec(
            num_scalar_prefetch=1, grid=(n,),
            in_specs=[_seq_spec(N_HEADS, KV_ROW), _seq_spec(1, KEY_PAD), _seq_spec(1, KV_ROW)]
            + _page_specs((None, PAGE_SIZE, KV_ROW)),
            out_specs=_seq_spec(N_HEADS, KV_ROW)),
        compiler_params=pltpu.CompilerParams(
            dimension_semantics=("parallel",), vmem_limit_bytes=VMEM_LIMIT),
        name="dsa_step_attend",
    )(page_table, _heads_on_kv_lanes(dq), sel.reshape(n, 1, KEY_PAD), dkv.reshape(n, 1, KV_ROW),
      *([pool_kv.reshape(-1, PAGE_SIZE, KV_ROW)] * N_PAGES))
    return _heads_from_kv_lanes(o8)


CMP_STEP_SEQS = 2
BLOCKS_PER_PAGE = PAGE_SIZE // CMP_BLOCK
CMP_ROW = CMP_BLOCK * KV_ROW


def _nsa_compress_paged_kernel(pt_ref, pe_ref, w1_ref, w2_ref, *rest):
    npg = CMP_STEP_SEQS * N_PAGES
    pages, o_ref, x_ref = rest[:npg], rest[npg], rest[npg + 1]
    for s in range(npg):
        x_ref[s * BLOCKS_PER_PAGE:(s + 1) * BLOCKS_PER_PAGE, :] = pages[s][...] + pe_ref[...]
    tk = 2048
    hid = jnp.zeros((npg * BLOCKS_PER_PAGE, w1_ref.shape[1]), f32)
    for kk in range(CMP_ROW // tk):
        cols = slice(kk * tk, (kk + 1) * tk)
        hid = hid + jnp.dot(x_ref[:, cols].astype(bf16), w1_ref[cols, :], preferred_element_type=f32)
    o_ref[...] = jnp.dot(jax.nn.gelu(hid).astype(bf16), w2_ref[...], preferred_element_type=f32)


def nsa_compress_paged(pool_cmp, page_table, pe_row, w1_big, w2_big):
    n = page_table.shape[0]
    blocks = N_PAGES * BLOCKS_PER_PAGE
    hw = w1_big.shape[1]
    rows = CMP_STEP_SEQS * blocks
    const = lambda shape: pl.BlockSpec(shape, lambda i, pt: (0,) * len(shape))
    out = pl.pallas_call(
        _nsa_compress_paged_kernel,
        out_shape=jax.ShapeDtypeStruct((n * blocks, KV_ROW), f32),
        grid_spec=pltpu.PrefetchScalarGridSpec(
            num_scalar_prefetch=1, grid=(n // CMP_STEP_SEQS,),
            in_specs=[const((1, CMP_ROW)), const((CMP_ROW, hw)), const((hw, KV_ROW))]
            + _page_specs((None, BLOCKS_PER_PAGE, CMP_ROW), CMP_STEP_SEQS),
            out_specs=pl.BlockSpec((rows, KV_ROW), lambda i, pt: (i, 0)),
            scratch_shapes=[pltpu.VMEM((rows, CMP_ROW), f32)]),
        compiler_params=pltpu.CompilerParams(
            dimension_semantics=("parallel",), vmem_limit_bytes=VMEM_LIMIT),
        name="nsa_compress_paged",
    )(page_table, pe_row, w1_big, w2_big,
      *([pool_cmp.reshape(-1, BLOCKS_PER_PAGE, CMP_ROW)] * (CMP_STEP_SEQS * N_PAGES)))
    return out.reshape(n, blocks, KV_ROW)


def _nsa_step_kernel(pt_ref, q_ref, gate_ref, cmp_ref, slcnew_ref, win_ref, winnew_ref, *rest):
    pages, o_ref = rest[:N_PAGES], rest[N_PAGES]
    q8 = q_ref[...]
    lane = lax.broadcasted_iota(jnp.int32, (N_HEADS, CMP_LANES), 1)
    head = lax.broadcasted_iota(jnp.int32, (N_HEADS, 1), 0)
    q_pos = jnp.full((N_HEADS, 1), PAST_LEN, jnp.int32)
    c_rows = cmp_ref[...].astype(bf16)
    s = lax.dot_general(q8.astype(bf16), c_rows, _NT, preferred_element_type=f32) * ATT_SCALE
    s = jnp.where((lane + 1) * CMP_BLOCK - 1 <= q_pos, s, -jnp.inf)
    m = jnp.max(s, axis=1, keepdims=True)
    m = jnp.where(m == -jnp.inf, 0.0, m)
    p = jnp.exp(s - m)
    p = p / jnp.maximum(jnp.sum(p, axis=1, keepdims=True), F32_TINY)
    o_cmp = jnp.dot(p.astype(bf16), c_rows, preferred_element_type=f32)
    imp = jnp.where(head < Q_PER_KV,
                    jnp.sum(p[:Q_PER_KV], axis=0, keepdims=True),
                    jnp.sum(p[Q_PER_KV:], axis=0, keepdims=True))
    sel_blk = _select_blocks(imp, q_pos, lane)
    ratio = SLC_BLOCK // CMP_BLOCK
    per_page = PAGE_SIZE // SLC_BLOCK
    masks = []
    for pg in range(N_PAGES):
        msk = sel_blk[:, ratio * per_page * pg:ratio * per_page * pg + 1]
        for j in range(1, per_page):
            b = per_page * pg + j
            msk = jnp.where(lane < j * SLC_BLOCK, msk, sel_blk[:, ratio * b:ratio * b + 1])
        masks.append(msk)
    o_slc = _step_attention(q8, pages, masks, slcnew_ref[...], None)
    wbuf = win_ref.shape[0]
    wchunks = wbuf // LANES
    wblocks = [win_ref.at[pl.ds(c * LANES, LANES), :] for c in range(wchunks)]
    wmasks = [jnp.where(PAST_LEN - wbuf + c * LANES + lane > q_pos - WINDOW, 1.0, 0.0) for c in range(wchunks)]
    o_win = _step_attention(q8, wblocks, wmasks, winnew_ref[...], None)
    gate = jax.nn.sigmoid(gate_ref[...])
    wide = lambda g: jnp.concatenate([g] * (KV_ROW // LANES), axis=1)
    o_ref[...] = wide(gate[0]) * o_cmp + wide(gate[1]) * o_slc + wide(gate[2]) * o_win


def nsa_step(nq, ng, skv, wkv, pool_cmp, pool_slc, win_buf, page_table, cmp_weights):
    n = nq.shape[0]
    blocks = N_PAGES * BLOCKS_PER_PAGE
    wbuf = win_buf.shape[1]
    assert (PAST_LEN + 1) // CMP_BLOCK == blocks and blocks <= CMP_LANES and wbuf % LANES == 0
    assert PAST_LEN // SLC_BLOCK + 1 <= CMP_LANES // 2 and PAGE_SIZE == LANES
    cmp_rows = nsa_compress_paged(pool_cmp, page_table, *cmp_weights)
    cmp_rows = jnp.pad(cmp_rows, ((0, 0), (0, CMP_LANES - blocks), (0, 0)))
    gates = jnp.broadcast_to(ng.reshape(n, N_HEADS, 3).transpose(0, 2, 1)[..., None], (n, 3, N_HEADS, LANES))
    o8 = pl.pallas_call(
        _nsa_step_kernel,
        out_shape=jax.ShapeDtypeStruct((n, N_HEADS, KV_ROW), f32),
        grid_spec=pltpu.PrefetchScalarGridSpec(
            num_scalar_prefetch=1, grid=(n,),
            in_specs=[_seq_spec(N_HEADS, KV_ROW), _seq_spec(3, N_HEADS, LANES), _seq_spec(CMP_LANES, KV_ROW),
                      _seq_spec(1, KV_ROW), _seq_spec(wbuf, KV_ROW), _seq_spec(1, KV_ROW)]
            + _page_specs((None, PAGE_SIZE, KV_ROW)),
            out_specs=_seq_spec(N_HEADS, KV_ROW)),
        compiler_params=pltpu.CompilerParams(
            dimension_semantics=("parallel",), vmem_limit_bytes=VMEM_LIMIT),
        name="nsa_step",
    )(page_table, _heads_on_kv_lanes(nq), gates, cmp_rows, skv.reshape(n, 1, KV_ROW),
      win_buf.reshape(n, wbuf, KV_ROW), wkv.reshape(n, 1, KV_ROW),
      *([pool_slc.reshape(-1, PAGE_SIZE, KV_ROW)] * N_PAGES))
    return _heads_from_kv_lanes(o8)


def _tokens_minor(cache):
    token_axis = 2
    return jnp.moveaxis(cache, token_axis, -1)


def _layer_page_specs(block_tail, layer, seqs_per_step=1):
    zeros = (0,) * len(block_tail)
    return [pl.BlockSpec((None, None) + block_tail,
                         lambda i, pt, s=s, p=p: (layer, pt[i * seqs_per_step + s, p]) + zeros)
            for s in range(seqs_per_step) for p in range(N_PAGES)]


def _layer_seq_spec(tail, layer):
    zeros = (0,) * len(tail)
    return pl.BlockSpec((None, None) + tail, lambda i, pt: (layer, i) + zeros)


def _decode_attention(q8, blocks, masks, kv_new, new_ok, is_g0):
    qb = q8.astype(bf16)
    by_group = lambda fn: jnp.where(is_g0, fn(0), fn(1))
    s_new = by_group(lambda g: jnp.sum(q8 * kv_new[2 * g:2 * g + 1], axis=1, keepdims=True)) * ATT_SCALE
    if new_ok is not None:
        s_new = jnp.where(new_ok, s_new, MASKED)
    m = s_new
    scores = []
    for blk, msk in zip(blocks, masks):
        s = by_group(lambda g: jnp.dot(qb, blk[g, 0].astype(bf16), preferred_element_type=f32)) * ATT_SCALE
        s = jnp.where(msk > 0.5, s, MASKED)
        scores.append(s)
        m = jnp.maximum(m, jnp.max(s, axis=1, keepdims=True))
    p_new = jnp.exp(s_new - m)
    if new_ok is not None:
        p_new = jnp.where(new_ok, p_new, 0.0)
    l = p_new
    acc = p_new * by_group(lambda g: kv_new[2 * g + 1:2 * g + 2])
    for blk, msk, s in zip(blocks, masks, scores):
        p = jnp.where(msk > 0.5, jnp.exp(s - m), 0.0)
        l = l + jnp.sum(p, axis=1, keepdims=True)
        pb = p.astype(bf16)
        acc = acc + by_group(lambda g: lax.dot_general(pb, blk[g, 1].astype(bf16), _NT,
                                                       preferred_element_type=f32))
    return acc / l


def _dsa_decode_scores_kernel(pt_ref, iq_ref, iw_ref, iknew_ref, *rest):
    pages, keys_ref = rest[:N_PAGES], rest[N_PAGES]
    iq = iq_ref[...]
    iqb = iq.astype(bf16)
    iw = iw_ref[...] * IDX_HEADS ** -0.5
    weigh = lambda dots, w: jnp.sum(w * jnp.maximum(dots * IDX_DIM ** -0.5, 0.0), axis=0, keepdims=True)
    for p in range(N_PAGES):
        dots = jnp.dot(iqb, pages[p][...].astype(bf16), preferred_element_type=f32)
        keys_ref[:, p * PAGE_SIZE:(p + 1) * PAGE_SIZE] = _sortable_key(weigh(dots, iw))
    new_score = weigh(jnp.sum(iq * iknew_ref[...], axis=1, keepdims=True), iw[:, :1])
    lane = lax.broadcasted_iota(jnp.int32, (1, LANES), 1)
    keys_ref[:, PAST_LEN:] = jnp.where(lane == 0, _sortable_key(jnp.broadcast_to(new_score, (1, LANES))), INT_MIN)


def _dsa_decode_attend_kernel(pt_ref, q_ref, sel_ref, kvnew_ref, *rest):
    pages, o_ref = rest[:N_PAGES], rest[N_PAGES]
    masks = [sel_ref[:, p * PAGE_SIZE:(p + 1) * PAGE_SIZE] for p in range(N_PAGES)]
    new_ok = sel_ref[:, PAST_LEN:PAST_LEN + 1] > 0.5
    is_g0 = lax.broadcasted_iota(jnp.int32, (N_HEADS, 1), 0) < Q_PER_KV
    o_ref[...] = _decode_attention(q_ref[...], pages, masks, kvnew_ref[...], new_ok, is_g0)


def dsa_decode(dq, dkv, iq, ik, iw, pools_kv, pools_idx, layer, page_table):
    n = dq.shape[0]
    assert min(DSA_TOPK, (PAST_LEN + 1) // 4) == DSA_TOPK and PAGE_SIZE == LANES
    pad_heads = ((0, 0), (0, STEP_ROWS - IDX_HEADS), (0, 0))
    iq8 = jnp.pad(iq.reshape(n, IDX_HEADS, IDX_DIM), pad_heads)
    iw8 = jnp.broadcast_to(jnp.pad(iw.reshape(n, IDX_HEADS, 1), pad_heads), (n, STEP_ROWS, LANES))
    keys = pl.pallas_call(
        _dsa_decode_scores_kernel,
        out_shape=jax.ShapeDtypeStruct((n, 1, KEY_PAD), jnp.int32),
        grid_spec=pltpu.PrefetchScalarGridSpec(
            num_scalar_prefetch=1, grid=(n,),
            in_specs=[_seq_spec(STEP_ROWS, IDX_DIM), _seq_spec(STEP_ROWS, LANES), _seq_spec(1, IDX_DIM)]
            + _layer_page_specs((IDX_DIM, PAGE_SIZE), layer),
            out_specs=_seq_spec(1, KEY_PAD)),
        compiler_params=pltpu.CompilerParams(
            dimension_semantics=("parallel",), vmem_limit_bytes=VMEM_LIMIT),
        name="dsa_decode_scores",
    )(page_table, iq8, iw8, ik.reshape(n, 1, IDX_DIM), *([pools_idx] * N_PAGES))
    sel = pl.pallas_call(
        functools.partial(_topk_rows_kernel, topk=DSA_TOPK, idx_bits=int(math.ceil(math.log2(KEY_PAD))),
                          n_valid=PAST_LEN + 1),
        out_shape=jax.ShapeDtypeStruct((n, KEY_PAD), f32),
        compiler_params=pltpu.CompilerParams(vmem_limit_bytes=VMEM_LIMIT),
        name="dsa_decode_topk",
    )(keys.reshape(n, KEY_PAD))
    o8 = pl.pallas_call(
        _dsa_decode_attend_kernel,
        out_shape=jax.ShapeDtypeStruct((n, N_HEADS, HEAD_DIM), f32),
        grid_spec=pltpu.PrefetchScalarGridSpec(
            num_scalar_prefetch=1, grid=(n,),
            in_specs=[_seq_spec(N_HEADS, HEAD_DIM), _seq_spec(1, KEY_PAD), _seq_spec(2 * KV_HEADS, HEAD_DIM)]
            + _layer_page_specs((KV_HEADS, 2, HEAD_DIM, PAGE_SIZE), layer),
            out_specs=_seq_spec(N_HEADS, HEAD_DIM)),
        compiler_params=pltpu.CompilerParams(
            dimension_semantics=("parallel",), vmem_limit_bytes=VMEM_LIMIT),
        name="dsa_decode_attend",
    )(page_table, dq.reshape(n, N_HEADS, HEAD_DIM), sel.reshape(n, 1, KEY_PAD),
      dkv.reshape(n, 2 * KV_HEADS, HEAD_DIM), *([pools_kv] * N_PAGES))
    return o8.reshape(n, GROUP_WIDTH)


def _nsa_compress_decode_kernel(pt_ref, pe_ref, w1_ref, w2_ref, *rest):
    npg = CMP_STEP_SEQS * N_PAGES
    pages, o_ref, rows_ref = rest[:npg], rest[npg], rest[npg + 1]
    nblk = npg * BLOCKS_PER_PAGE
    blocks_per_seq = N_PAGES * BLOCKS_PER_PAGE
    for s in range(npg):
        for gj in range(2 * KV_HEADS):
            rows_ref[gj, s * PAGE_SIZE:(s + 1) * PAGE_SIZE, :] = pages[s][gj // 2, gj % 2].T
    for gj in range(2 * KV_HEADS):
        j = gj % 2

        def token_body(t, hid):
            x = rows_ref[gj, pl.ds(t, nblk, stride=CMP_BLOCK), :] + pe_ref[t, j:j + 1, :]
            w = w1_ref[j, pl.ds(pl.multiple_of(t * HEAD_DIM, HEAD_DIM), HEAD_DIM), :]
            return hid + jnp.dot(x.astype(bf16), w, preferred_element_type=f32)

        hid = lax.fori_loop(0, CMP_BLOCK, token_body, jnp.zeros((nblk, CMP_HIDDEN), f32))
        out = jnp.dot(jax.nn.gelu(hid).astype(bf16), w2_ref[j], preferred_element_type=f32)
        for s in range(CMP_STEP_SEQS):
            o_ref[s, gj] = out[s * blocks_per_seq:(s + 1) * blocks_per_seq]


def nsa_compress_decode(pools_cmp, layer, page_table, cmp_pe, cmp_w1, cmp_w2):
    n = page_table.shape[0]
    blocks = N_PAGES * BLOCKS_PER_PAGE
    const = lambda shape: pl.BlockSpec(shape, lambda i, pt: (0,) * len(shape))
    return pl.pallas_call(
        _nsa_compress_decode_kernel,
        out_shape=jax.ShapeDtypeStruct((n, 2 * KV_HEADS, blocks, HEAD_DIM), f32),
        grid_spec=pltpu.PrefetchScalarGridSpec(
            num_scalar_prefetch=1, grid=(n // CMP_STEP_SEQS,),
            in_specs=[const((CMP_BLOCK, 2, HEAD_DIM)), const((2, CMP_BLOCK * HEAD_DIM, CMP_HIDDEN)),
                      const((2, CMP_HIDDEN, HEAD_DIM))]
            + _layer_page_specs((KV_HEADS, 2, HEAD_DIM, PAGE_SIZE), layer, CMP_STEP_SEQS),
            out_specs=pl.BlockSpec((CMP_STEP_SEQS, 2 * KV_HEADS, blocks, HEAD_DIM), lambda i, pt: (i, 0, 0, 0)),
            scratch_shapes=[pltpu.VMEM((2 * KV_HEADS, CMP_STEP_SEQS * N_PAGES * PAGE_SIZE, HEAD_DIM), f32)]),
        compiler_params=pltpu.CompilerParams(
            dimension_semantics=("parallel",), vmem_limit_bytes=VMEM_LIMIT),
        name="nsa_compress_decode",
    )(page_table, cmp_pe, cmp_w1.astype(bf16), cmp_w2.astype(bf16),
      *([pools_cmp] * (CMP_STEP_SEQS * N_PAGES)))


def _nsa_decode_kernel(pt_ref, q_ref, gate_ref, cmp_ref, slcnew_ref, win_ref, winnew_ref, *rest):
    pages, o_ref = rest[:N_PAGES], rest[N_PAGES]
    q8 = q_ref[...]
    qb = q8.astype(bf16)
    lane = lax.broadcasted_iota(jnp.int32, (N_HEADS, CMP_LANES), 1)
    head = lax.broadcasted_iota(jnp.int32, (N_HEADS, 1), 0)
    is_g0 = head < Q_PER_KV
    by_group = lambda fn: jnp.where(is_g0, fn(0), fn(1))
    q_pos = jnp.full((N_HEADS, 1), PAST_LEN, jnp.int32)
    blocks = cmp_ref.shape[1]
    s = by_group(lambda g: lax.dot_general(qb, cmp_ref[2 * g].astype(bf16), _NT,
                                           preferred_element_type=f32)) * ATT_SCALE
    lane_c = lane[:, :blocks]
    s = jnp.where((lane_c + 1) * CMP_BLOCK - 1 <= q_pos, s, -jnp.inf)
    m = jnp.max(s, axis=1, keepdims=True)
    m = jnp.where(m == -jnp.inf, 0.0, m)
    p = jnp.exp(s - m)
    p = p / jnp.maximum(jnp.sum(p, axis=1, keepdims=True), F32_TINY)
    pb = p.astype(bf16)
    o_cmp = by_group(lambda g: jnp.dot(pb, cmp_ref[2 * g + 1].astype(bf16), preferred_element_type=f32))
    imp = jnp.where(is_g0, jnp.sum(p[:Q_PER_KV], axis=0, keepdims=True),
                    jnp.sum(p[Q_PER_KV:], axis=0, keepdims=True))
    imp = jnp.concatenate([imp, jnp.zeros((N_HEADS, CMP_LANES - blocks), f32)], axis=1)
    sel_blk = _select_blocks(imp, q_pos, lane)
    ratio = SLC_BLOCK // CMP_BLOCK
    per_page = PAGE_SIZE // SLC_BLOCK
    masks = []
    for pg in range(N_PAGES):
        msk = sel_blk[:, ratio * per_page * pg:ratio * per_page * pg + 1]
        for j in range(1, per_page):
            b = per_page * pg + j
            msk = jnp.where(lane < j * SLC_BLOCK, msk, sel_blk[:, ratio * b:ratio * b + 1])
        masks.append(msk)
    o_slc = _decode_attention(q8, pages, masks, slcnew_ref[...], None, is_g0)
    wbuf = win_ref.shape[-1]
    slot = lax.broadcasted_iota(jnp.int32, (1, wbuf), 1)
    wmask = jnp.where(PAST_LEN - wbuf + slot > PAST_LEN - WINDOW, 1.0, 0.0)
    o_win = _decode_attention(q8, [win_ref], [wmask], winnew_ref[...], None, is_g0)
    gate = jax.nn.sigmoid(gate_ref[...])
    o_ref[...] = (gate[0][:, :HEAD_DIM] * o_cmp + gate[1][:, :HEAD_DIM] * o_slc
                  + gate[2][:, :HEAD_DIM] * o_win)


def nsa_decode(nq, ng, skv, wkv, pool_cmp, pools_slc, wins, layer, page_table, cmp_weights):
    n = nq.shape[0]
    blocks = N_PAGES * BLOCKS_PER_PAGE
    wbuf = wins.shape[-1]
    assert (PAST_LEN + 1) // CMP_BLOCK == blocks and blocks <= CMP_LANES
    assert PAST_LEN // SLC_BLOCK + 1 <= CMP_LANES // 2 and PAGE_SIZE == LANES
    cmp_rows = nsa_compress_paged(pool_cmp, page_table, *cmp_weights)
    cmp_rows = cmp_rows.reshape(n, blocks, 2 * KV_HEADS, HEAD_DIM).transpose(0, 2, 1, 3)
    gates = jnp.broadcast_to(ng.reshape(n, N_HEADS, 3).transpose(0, 2, 1)[..., None], (n, 3, N_HEADS, LANES))
    kv_tile = (KV_HEADS, 2, HEAD_DIM)
    o8 = pl.pallas_call(
        _nsa_decode_kernel,
        out_shape=jax.ShapeDtypeStruct((n, N_HEADS, HEAD_DIM), f32),
        grid_spec=pltpu.PrefetchScalarGridSpec(
            num_scalar_prefetch=1, grid=(n,),
            in_specs=[_seq_spec(N_HEADS, HEAD_DIM), _seq_spec(3, N_HEADS, LANES),
                      _seq_spec(2 * KV_HEADS, blocks, HEAD_DIM), _seq_spec(2 * KV_HEADS, HEAD_DIM),
                      _layer_seq_spec(kv_tile + (wbuf,), layer), _seq_spec(2 * KV_HEADS, HEAD_DIM)]
            + _layer_page_specs(kv_tile + (PAGE_SIZE,), layer),
            out_specs=_seq_spec(N_HEADS, HEAD_DIM)),
        compiler_params=pltpu.CompilerParams(
            dimension_semantics=("parallel",), vmem_limit_bytes=VMEM_LIMIT),
        name="nsa_decode",
    )(page_table, nq.reshape(n, N_HEADS, HEAD_DIM), gates, cmp_rows, skv.reshape(n, 2 * KV_HEADS, HEAD_DIM),
      wins, wkv.reshape(n, 2 * KV_HEADS, HEAD_DIM), *([pools_slc] * N_PAGES))
    return o8.reshape(n, GROUP_WIDTH)


def masked_softmax(s, mask):
    s = jnp.where(mask, s.astype(jnp.float32), -jnp.inf)
    m = jnp.max(s, axis=-1, keepdims=True)
    m = jnp.where(jnp.isfinite(m), m, 0.0)
    p = jnp.exp(s - m)
    return p / jnp.maximum(p.sum(-1, keepdims=True), jnp.finfo(jnp.float32).tiny)


def gather_seq(rows, pos, *extra):
    n_idx = jnp.arange(rows.shape[0]).reshape((-1,) + (1,) * (pos.ndim - 1))
    return rows[(n_idx, pos) + tuple(extra)]


def gather_pages(pool, page_table):
    g = pool[page_table]
    return g.reshape((g.shape[0], -1) + g.shape[3:])


def gather_paged_rows(pool, page_table, pos, *extra):
    n_idx = jnp.arange(page_table.shape[0]).reshape((-1,) + (1,) * (pos.ndim - 1))
    phys = page_table[n_idx, pos // PAGE_SIZE]
    return pool[(phys, pos % PAGE_SIZE) + tuple(extra)]


def gather_past_and_new(pool, page_table, new_rows, pos, *extra):
    past = gather_paged_rows(pool, page_table, jnp.minimum(pos, PAST_LEN - 1), *extra)
    new = gather_seq(new_rows, jnp.clip(pos - PAST_LEN, 0, new_rows.shape[1] - 1), *extra)
    in_past = (pos < PAST_LEN).reshape(pos.shape + (1,) * (past.ndim - pos.ndim))
    return jnp.where(in_past, past, new)


def split_projection(cols, n, t):
    (dq, dkv, iq, ik, iw, u, nq, ckv, skv, wkv, ng, gq, gk, gv, ga, gr) = cols
    kv_shape = (n, t, KV_HEADS, 2, HEAD_DIM)
    return {
        'dsa_q': dq.reshape(n, t, N_HEADS, HEAD_DIM),
        'dsa_kv': dkv.reshape(kv_shape),
        'idx_q': iq.reshape(n, t, IDX_HEADS, IDX_DIM),
        'idx_k': ik,
        'idx_w': iw,
        's5_u': u,
        'nsa_q': nq.reshape(n, t, N_HEADS, HEAD_DIM),
        'nsa_cmp': ckv.reshape(kv_shape),
        'nsa_slc': skv.reshape(kv_shape),
        'nsa_win': wkv.reshape(kv_shape),
        'nsa_gate': jax.nn.sigmoid(ng).reshape(n, t, N_HEADS, 3),
        'gla_q': gq.reshape(n, t, GLA_HEADS, GLA_DK),
        'gla_k': gk.reshape(n, t, GLA_HEADS, GLA_DK),
        'gla_v': gv.reshape(n, t, GLA_HEADS, GLA_DV),
        'gla_a': ga,
        'gla_r': gr,
    }


def indexer_scores(iq, iw, ik):
    dots = jnp.einsum('nqhd,nld->nqhl', iq, ik) * IDX_DIM ** -0.5
    return jnp.einsum('nqh,nqhl->nql', iw * IDX_HEADS ** -0.5, jax.nn.relu(dots)).astype(jnp.float32)


def dsa_select(scores, q_pos, n_keys):
    causal = jnp.arange(n_keys)[None, :] <= q_pos[:, None]
    k_sel = min(DSA_TOPK, n_keys // 4)
    _, sel = lax.top_k(jnp.where(causal, scores, -jnp.inf), k_sel)
    return sel, sel <= q_pos[None, :, None]


def dsa_attend(q, kv_sel, valid):
    n, nq = q.shape[:2]
    qg = q.reshape(n, nq, KV_HEADS, Q_PER_KV, HEAD_DIM)
    s = jnp.einsum('nqgrd,nqkgd->nqgrk', qg, kv_sel[..., 0, :]) * HEAD_DIM ** -0.5
    p = masked_softmax(s, valid[:, :, None, None, :])
    o = jnp.einsum('nqgrk,nqkgd->nqgrd', p, kv_sel[..., 1, :])
    return o.reshape(n, nq, GROUP_WIDTH)


def dsa_sample(p, pool_kv, pool_idx, page_table):
    s_new = p['dsa_q'].shape[1]
    n_keys = PAST_LEN + s_new
    ik_all = jnp.concatenate([gather_pages(pool_idx, page_table), p['idx_k']], axis=1)
    q_pos = PAST_LEN + jnp.arange(s_new)
    sel, valid = dsa_select(indexer_scores(p['idx_q'], p['idx_w'], ik_all), q_pos, n_keys)
    kv_sel = gather_past_and_new(pool_kv, page_table, p['dsa_kv'], sel)
    return dsa_attend(p['dsa_q'], kv_sel, valid)


def complex_linear_combine(e1, e2):
    a1r, a1i, b1r, b1i = e1
    a2r, a2i, b2r, b2i = e2
    return (a2r * a1r - a2i * a1i,
            a2r * a1i + a2i * a1r,
            a2r * b1r - a2i * b1i + b2r,
            a2r * b1i + a2i * b1r + b2i)


def s5_mixer(u, h0, w):
    n, t = u.shape[:2]
    uf = u.reshape(n, t, S5_GROUPS, S5_CH)
    a_re, a_im = w['a_re'], w['a_im']
    dt = jnp.exp(w['log_dt'])[:, None]
    mag = jnp.exp(a_re * dt)
    ab_re, ab_im = mag * jnp.cos(a_im * dt), mag * jnp.sin(a_im * dt)
    den = a_re * a_re + a_im * a_im
    nr, ni = ab_re - 1.0, ab_im
    f_re = (nr * a_re + ni * a_im) / den
    f_im = (ni * a_re - nr * a_im) / den
    b_re, b_im = w['b_re'], w['b_im']
    bb_re = f_re[..., None] * b_re - f_im[..., None] * b_im
    bb_im = f_re[..., None] * b_im + f_im[..., None] * b_re
    bu_re = jnp.einsum('gsc,btgc->btgs', bb_re, uf)
    bu_im = jnp.einsum('gsc,btgc->btgs', bb_im, uf)
    cum_re, cum_im, sc_re, sc_im = lax.associative_scan(
        complex_linear_combine,
        (jnp.broadcast_to(ab_re, bu_re.shape), jnp.broadcast_to(ab_im, bu_im.shape), bu_re, bu_im),
        axis=1)
    h0r, h0i = h0[:, 0][:, None], h0[:, 1][:, None]
    h_re = cum_re * h0r - cum_im * h0i + sc_re
    h_im = cum_re * h0i + cum_im * h0r + sc_im
    y = (jnp.einsum('gcs,btgs->btgc', w['c_re'], h_re)
         - jnp.einsum('gcs,btgs->btgc', w['c_im'], h_im)
         + w['d'] * uf)
    y = jax.nn.gelu(y.reshape(n, t, GROUP_WIDTH))
    y = y * jax.nn.sigmoid(y @ w['w_glu'] + w['b_glu'])
    h_last = jnp.stack([h_re[:, -1], h_im[:, -1]], axis=1)
    return y, h_last


def nsa_compress(rows, w):
    n, length = rows.shape[:2]
    nc = length // CMP_BLOCK
    blk = rows[:, :nc * CMP_BLOCK].reshape(n, nc, CMP_BLOCK, KV_HEADS, 2, HEAD_DIM)
    blk = blk + w['cmp_pe'][:, None]
    flat = jnp.moveaxis(blk, 2, 4).reshape(n, nc, KV_HEADS, 2, CMP_BLOCK * HEAD_DIM)
    hid = jax.nn.gelu(jnp.einsum('ncgjf,jfe->ncgje', flat, w['cmp_w1']))
    return jnp.einsum('ncgje,jed->ncgjd', hid, w['cmp_w2'])


def nsa_core(q, gate, q_pos, cmp_kv, n_slc_blocks, slc_gather, win_kv, win_pos):
    n, nq = q.shape[:2]
    qg = q.reshape(n, nq, KV_HEADS, Q_PER_KV, HEAD_DIM)
    scale = HEAD_DIM ** -0.5
    nc = cmp_kv.shape[1]
    c_vis = (jnp.arange(nc) + 1) * CMP_BLOCK - 1 <= q_pos[:, None]
    s_c = jnp.einsum('nqgrd,ncgd->nqgrc', qg, cmp_kv[..., 0, :]) * scale
    p_c = masked_softmax(s_c, c_vis[None, :, None, None, :])
    o_cmp = jnp.einsum('nqgrc,ncgd->nqgrd', p_c, cmp_kv[..., 1, :])
    ratio = SLC_BLOCK // CMP_BLOCK
    imp = p_c.sum(3)
    imp = jnp.pad(imp, ((0, 0), (0, 0), (0, 0), (0, n_slc_blocks * ratio - nc)))
    imp = imp.reshape(n, nq, KV_HEADS, n_slc_blocks, ratio).sum(-1)
    blk = jnp.arange(n_slc_blocks)[None, :]
    cur = (q_pos // SLC_BLOCK)[:, None]
    forced = (blk == 0) | (blk == cur) | (blk == cur - 1)
    future = blk > cur
    score = jnp.where(future[None, :, None, :], -jnp.inf,
                      jnp.where(forced[None, :, None, :], jnp.inf, imp))
    n_sel = min(SLC_TOPN, n_slc_blocks)
    _, sel = lax.top_k(score, n_sel)
    pos = (sel[..., None] * SLC_BLOCK + jnp.arange(SLC_BLOCK)).reshape(n, nq, KV_HEADS, n_sel * SLC_BLOCK)
    valid = pos <= q_pos[None, :, None, None]
    kv_s = slc_gather(pos)
    s_s = jnp.einsum('nqgrd,nqgkd->nqgrk', qg, kv_s[..., 0, :]) * scale
    p_s = masked_softmax(s_s, valid[:, :, :, None, :])
    o_slc = jnp.einsum('nqgrk,nqgkd->nqgrd', p_s, kv_s[..., 1, :])
    w_vis = ((win_pos[None, :] <= q_pos[:, None]) & (win_pos[None, :] > q_pos[:, None] - WINDOW)
             & (win_pos[None, :] >= 0))
    s_w = jnp.einsum('nqgrd,nwgd->nqgrw', qg, win_kv[..., 0, :]) * scale
    p_w = masked_softmax(s_w, w_vis[None, :, None, None, :])
    o_win = jnp.einsum('nqgrw,nwgd->nqgrd', p_w, win_kv[..., 1, :])
    g = gate.reshape(n, nq, KV_HEADS, Q_PER_KV, 3)
    o = g[..., 0:1] * o_cmp + g[..., 1:2] * o_slc + g[..., 2:3] * o_win
    return o.reshape(n, nq, GROUP_WIDTH)


def nsa_sample(p, w, pool_cmp, pool_slc, win_buf, page_table):
    s_new = p['nsa_q'].shape[1]
    n_keys = PAST_LEN + s_new
    cmp_all = jnp.concatenate([gather_pages(pool_cmp, page_table), p['nsa_cmp']], axis=1)
    cmp_kv = nsa_compress(cmp_all, w)
    wbuf = win_buf.shape[1]
    win_all = jnp.concatenate([win_buf, p['nsa_win']], axis=1)
    win_pos = PAST_LEN - wbuf + jnp.arange(wbuf + s_new)
    q_pos = PAST_LEN + jnp.arange(s_new)
    g_idx = jnp.arange(KV_HEADS)[None, None, :, None]
    out = nsa_core(p['nsa_q'], p['nsa_gate'], q_pos, cmp_kv, -(-n_keys // SLC_BLOCK),
                   lambda pos: gather_past_and_new(pool_slc, page_table, p['nsa_slc'], pos, g_idx),
                   win_all, win_pos)
    return out, win_all[:, -wbuf:]


def gla_recurrence(q, k, v, log_a, s0):
    n, t = q.shape[:2]
    c = min(GLA_CHUNK, t)
    tp = -(-t // c) * c

    def chunks(a):
        a = jnp.pad(a, ((0, 0), (0, tp - t)) + ((0, 0),) * (a.ndim - 2))
        return jnp.moveaxis(a.reshape((n, tp // c, c) + a.shape[2:]), 1, 0)

    tril = jnp.tril(jnp.ones((c, c), dtype=bool))[None, :, :, None, None]

    def step(s, inp):
        qc, kc, vc, gc = inp
        b = jnp.cumsum(gc, axis=1)
        o_inter = jnp.einsum('nchk,nhkv->nchv', qc * jnp.exp(b), s)
        decay = jnp.exp(jnp.where(tril, b[:, :, None] - b[:, None, :], -jnp.inf))
        att = jnp.einsum('nthk,nshk,ntshk->nths', qc, kc, decay)
        o_intra = jnp.einsum('nths,nshv->nthv', att, vc)
        b_end = b[:, -1]
        s_new = (jnp.exp(b_end)[..., None] * s
                 + jnp.einsum('nshk,nshv->nhkv', kc * jnp.exp(b_end[:, None] - b), vc))
        return s_new, o_inter + o_intra

    s_last, o = lax.scan(step, s0, (chunks(q), chunks(k), chunks(v), chunks(log_a)))
    o = jnp.moveaxis(o, 0, 1).reshape((n, tp) + o.shape[3:])[:, :t]
    return o, s_last


def gla_mixer(p, s0, w):
    n, t = p['gla_q'].shape[:2]
    q = p['gla_q'] * GLA_DK ** -0.5
    k = p['gla_k']
    v = p['gla_v']
    z = p['gla_a'] @ w['gla_w_gate'] + w['gla_b_gate']
    log_a = (jax.nn.log_sigmoid(z) / GLA_TAU).reshape(n, t, GLA_HEADS, GLA_DK)
    o, s_last = gla_recurrence(q, k, v, log_a, s0)
    o = o * lax.rsqrt(jnp.mean(o * o, -1, keepdims=True) + LN_EPS) * w['gla_norm_g']
    o = o.reshape(n, t, GROUP_WIDTH) * jax.nn.silu(p['gla_r'])
    return o, s_last


def mix_prompt(cols, n, t, w, sw, cmp_weights):
    (dq, dkv, iq, ik, iw, u, nq, ckv, skv, wkv, ng, gq, gk, gv, ga, gr) = [
        c.reshape(n, t, c.shape[-1]) for c in cols]
    kv_shape = (n, t, KV_HEADS, 2, HEAD_DIM)
    o_dsa = dsa_prompt(dq, dkv, iq, ik, iw)
    o_s5, h_s5 = s5_prompt(u, jnp.zeros((n, 2, S5_WIDTH), f32), sw)
    o_nsa = nsa_prompt(nq, ng, ckv, skv, wkv, cmp_weights)
    o_gla, s_gla = gla_prompt(gq, gk, gv, ga, gr, w['gla_w_gate'], w['gla_b_gate'], w['gla_norm_g'])
    win_state = wkv[:, -min(WINDOW, t):].reshape((n, min(WINDOW, t)) + kv_shape[2:])
    mixers = [o.reshape(n * t, GROUP_WIDTH) for o in (o_dsa, o_s5, o_nsa, o_gla)]
    return mixers, (dkv.reshape(kv_shape), ik, ckv.reshape(kv_shape), skv.reshape(kv_shape), win_state,
                    h_s5.reshape(n, 2, S5_GROUPS, S5_STATE), s_gla)


def mix_sample(cols, w, sw, cmp_weights, layer, pools, win_buf, h_s5, s_gla, page_table):
    (dq, dkv, iq, ik, iw, u, nq, ckv, skv, wkv, ng, gq, gk, gv, ga, gr) = cols
    n, t = dq.shape[0], 1
    kv_shape = (n, t, KV_HEADS, 2, HEAD_DIM)
    o_dsa = dsa_decode(dq, dkv, iq, ik, iw, pools['dsa_kv'], pools['dsa_idx'], layer, page_table)
    o_s5, h_new = s5_step(u, h_s5.reshape(n, 2 * S5_WIDTH), sw)
    o_nsa = nsa_decode(nq, ng, skv, wkv, pools['nsa_cmp'][layer], pools['nsa_slc'], pools['nsa_win'], layer,
                       page_table, cmp_weights)
    o_gla, s_new = gla_step(gq, gk, gv, ga, gr, w['gla_w_gate'], w['gla_b_gate'], w['gla_norm_g'], s_gla)
    win_new = jnp.concatenate([win_buf[:, 1:], wkv.reshape(kv_shape)], axis=1)
    return [o_dsa, o_s5, o_nsa, o_gla], (
        dkv.reshape(kv_shape), ik.reshape(n, t, IDX_DIM), ckv.reshape(kv_shape), skv.reshape(kv_shape),
        win_new, h_new.reshape(n, 2, S5_GROUPS, S5_STATE), s_new)


def kernel(x_prompt, x_sample, cache_dsa_kv, cache_dsa_idx, cache_nsa_cmp, cache_nsa_slc, cache_nsa_win, state_s5, state_gla, page_table, w_in, s5_a_re, s5_a_im, s5_b_re, s5_b_im, s5_c_re, s5_c_im, s5_d, s5_log_dt, s5_w_glu, s5_b_glu, nsa_cmp_pe, nsa_cmp_w1, nsa_cmp_w2, gla_w_gate, gla_b_gate, gla_norm_g, w_out, ln1_g, ln1_b, ffn_w_gate, ffn_w_up, ffn_w_down, ln2_g, ln2_b):
    np_, tp_ = x_prompt.shape[:2]
    ns_, ts_ = x_sample.shape[:2]
    assert ts_ == 1
    w_in_pad = pad_in_projection(w_in)
    w_out_b = w_out.astype(bf16)
    wg_b, wu_b, wd_b = ffn_w_gate.astype(bf16), ffn_w_up.astype(bf16), ffn_w_down.astype(bf16)
    hp = x_prompt.reshape(np_ * tp_, D_MODEL)
    hs = x_sample.reshape(ns_ * ts_, D_MODEL)
    outs_p = [[] for _ in range(7)]
    outs_s = [[] for _ in range(7)]
    pools = {'dsa_kv': _tokens_minor(cache_dsa_kv), 'dsa_idx': _tokens_minor(cache_dsa_idx),
             'nsa_cmp': cache_nsa_cmp, 'nsa_slc': _tokens_minor(cache_nsa_slc),
             'nsa_win': _tokens_minor(cache_nsa_win)}
    for l in range(DEPTH):
        w = {'a_re': s5_a_re[l], 'a_im': s5_a_im[l], 'b_re': s5_b_re[l], 'b_im': s5_b_im[l],
             'c_re': s5_c_re[l], 'c_im': s5_c_im[l], 'd': s5_d[l], 'log_dt': s5_log_dt[l],
             'w_glu': s5_w_glu[l], 'b_glu': s5_b_glu[l],
             'cmp_pe': nsa_cmp_pe[l], 'cmp_w1': nsa_cmp_w1[l], 'cmp_w2': nsa_cmp_w2[l],
             'gla_w_gate': gla_w_gate[l], 'gla_b_gate': gla_b_gate[l], 'gla_norm_g': gla_norm_g[l]}
        sw = s5_discretize(w)
        cmp_weights = nsa_compress_weights(nsa_cmp_pe[l], nsa_cmp_w1[l], nsa_cmp_w2[l])
        mixers_p, st_p = mix_prompt(in_projection(hp, w_in_pad, l), np_, tp_, w, sw, cmp_weights)
        mixers_s, st_s = mix_sample(in_projection(hs, w_in_pad, l), w, sw, cmp_weights, l, pools, cache_nsa_win[l],
                                    state_s5[l], state_gla[l], page_table)
        hp = outproj_ln(hp, mixers_p, w_out_b, l, ln1_g[l], ln1_b[l])
        hs = outproj_ln(hs, mixers_s, w_out_b, l, ln1_g[l], ln1_b[l])
        hp = ffn_ln(hp, wg_b, wu_b, wd_b, l, ln2_g[l], ln2_b[l])
        hs = ffn_ln(hs, wg_b, wu_b, wd_b, l, ln2_g[l], ln2_b[l])
        for lst, st in zip(outs_p, st_p):
            lst.append(st)
        for lst, st in zip(outs_s, st_s):
            lst.append(st)
    dsa_kv_p, dsa_idx_p, nsa_cmp_p, nsa_slc_p, nsa_win_p, s5_p, gla_p = [jnp.stack(a) for a in outs_p]
    dsa_kv_s, dsa_idx_s, nsa_cmp_s, nsa_slc_s, nsa_win_s, s5_s, gla_s = [jnp.stack(a) for a in outs_s]
    return (hp.reshape(np_, tp_, D_MODEL), hs.reshape(ns_, ts_, D_MODEL),
            dsa_kv_p, dsa_kv_s, dsa_idx_p, dsa_idx_s, nsa_cmp_p, nsa_cmp_s,
            nsa_slc_p, nsa_slc_s, nsa_win_p, nsa_win_s, s5_p, s5_s, gla_p, gla_s)
```

```python
import functools
import math

import jax
import jax.numpy as jnp
from jax import lax
import numpy as np
from jax.experimental import pallas as pl
from jax.experimental.pallas import tpu as pltpu

D_MODEL = 2048
DEPTH = 2
PAST_LEN = 2048
PAGE_SIZE = 128
GROUP_WIDTH = 512
HEAD_DIM = 64
N_HEADS = 8
KV_HEADS = 2
Q_PER_KV = 4
KV_ROW = 256
IDX_HEADS = 4
IDX_DIM = 64
DSA_TOPK = 256
S5_GROUPS = 32
S5_STATE = 64
S5_WIDTH = S5_GROUPS * S5_STATE
CMP_BLOCK = 32
CMP_HIDDEN = 128
SLC_BLOCK = 64
SLC_TOPN = 16
WINDOW = 512
QBLOCK = 128
GLA_HEADS = 4
GLA_DK = 64
GLA_DV = 128
GLA_LOWRANK = 16
GLA_TAU = 16.0
GLA_CHUNK = 64
DEEPNORM_ALPHA = (2 * DEPTH) ** 0.25
LN_EPS = 1e-5

IN_SIZES = (512, 256, 256, 64, 4, 512, 512, 256, 256, 256, 24, 256, 256, 512, 16, 512)
D_IN = sum(IN_SIZES)

VMEM_LIMIT = 56 * 1024 * 1024
LANES = 128
KEY_CHUNK = 512
CMP_LANES = 128
S5_TIME_CHUNK = 256
ATT_SCALE = HEAD_DIM ** -0.5
MASKED = -1e30
INT_MIN = -2 ** 31
F32_TINY = float(np.finfo(np.float32).tiny)

bf16 = jnp.bfloat16
f32 = jnp.float32


IN_PAD_SIZES = tuple(-(-s // LANES) * LANES for s in IN_SIZES)
IN_PAD_OFFSETS = tuple(int(o) for o in np.cumsum((0,) + IN_PAD_SIZES[:-1]))


def pad_in_projection(w_in):
    parts, off = [], 0
    for size, size_pad in zip(IN_SIZES, IN_PAD_SIZES):
        parts.append(jnp.pad(w_in[:, :, off:off + size], ((0, 0), (0, 0), (0, size_pad - size))))
        off += size
    return jnp.concatenate(parts, axis=-1).astype(bf16)


def _in_projection_kernel(x_ref, w_ref, *o_refs):
    xb = x_ref[...].astype(bf16)
    for o_ref, off, size, size_pad in zip(o_refs, IN_PAD_OFFSETS, IN_SIZES, IN_PAD_SIZES):
        y = jnp.dot(xb, w_ref[:, off:off + size_pad], preferred_element_type=f32)
        o_ref[...] = y[:, :size]


def in_projection(x, w_pad, layer):
    m, k = x.shape
    tm = min(256, m)
    return pl.pallas_call(
        _in_projection_kernel,
        out_shape=tuple(jax.ShapeDtypeStruct((m, s), f32) for s in IN_SIZES),
        grid=(m // tm,),
        in_specs=[pl.BlockSpec((tm, k), lambda i: (i, 0)),
                  pl.BlockSpec((None, k, w_pad.shape[2]), lambda i: (layer, 0, 0),
                               pipeline_mode=pl.Buffered(1))],
        out_specs=tuple(pl.BlockSpec((tm, s), lambda i: (i, 0)) for s in IN_SIZES),
        compiler_params=pltpu.CompilerParams(
            dimension_semantics=("parallel",), vmem_limit_bytes=VMEM_LIMIT),
        name="in_projection",
    )(x, w_pad)


def _layer_norm_rows(z, g, b):
    mu = jnp.mean(z, axis=-1, keepdims=True)
    zc = z - mu
    var = jnp.mean(zc * zc, axis=-1, keepdims=True)
    return zc * lax.rsqrt(var + LN_EPS) * g + b


def _outproj_ln_kernel(x_ref, m0_ref, m1_ref, m2_ref, m3_ref, w_ref, g_ref, b_ref, o_ref):
    y = DEEPNORM_ALPHA * x_ref[...]
    for j, m_ref in enumerate((m0_ref, m1_ref, m2_ref, m3_ref)):
        y = y + jnp.dot(m_ref[...].astype(bf16), w_ref[j * GROUP_WIDTH:(j + 1) * GROUP_WIDTH, :],
                        preferred_element_type=f32)
    o_ref[...] = _layer_norm_rows(y, g_ref[...], b_ref[...])


def outproj_ln(x, mixers, w_bf16, layer, g, b):
    m, d = x.shape
    tm = min(256, m)
    mix_spec = pl.BlockSpec((tm, GROUP_WIDTH), lambda i: (i, 0))
    return pl.pallas_call(
        _outproj_ln_kernel,
        out_shape=jax.ShapeDtypeStruct((m, d), f32),
        grid=(m // tm,),
        in_specs=[pl.BlockSpec((tm, d), lambda i: (i, 0)),
                  mix_spec, mix_spec, mix_spec, mix_spec,
                  pl.BlockSpec((None, d, d), lambda i: (layer, 0, 0)),
                  pl.BlockSpec((1, d), lambda i: (0, 0)),
                  pl.BlockSpec((1, d), lambda i: (0, 0))],
        out_specs=pl.BlockSpec((tm, d), lambda i: (i, 0)),
        compiler_params=pltpu.CompilerParams(
            dimension_semantics=("parallel",), vmem_limit_bytes=VMEM_LIMIT),
        name="outproj_ln",
    )(x, *mixers, w_bf16, g.reshape(1, d), b.reshape(1, d))


def _ffn_ln_kernel(h_ref, wg_ref, wu_ref, wd_ref, g_ref, b_ref, o_ref, acc_ref):
    f = pl.program_id(1)

    @pl.when(f == 0)
    def _():
        acc_ref[...] = jnp.zeros_like(acc_ref)

    hb = h_ref[...].astype(bf16)
    a = jnp.dot(hb, wg_ref[...], preferred_element_type=f32)
    u = jnp.dot(hb, wu_ref[...], preferred_element_type=f32)
    act = (a * jax.nn.sigmoid(a) * u).astype(bf16)
    acc_ref[...] += jnp.dot(act, wd_ref[...], preferred_element_type=f32)

    @pl.when(f == pl.num_programs(1) - 1)
    def _():
        z = DEEPNORM_ALPHA * h_ref[...] + acc_ref[...]
        o_ref[...] = _layer_norm_rows(z, g_ref[...], b_ref[...])


def ffn_ln(h, wg, wu, wd, layer, g, b, tf=512):
    m, d = h.shape
    fh = wg.shape[2]
    tm = min(512, m)
    return pl.pallas_call(
        _ffn_ln_kernel,
        out_shape=jax.ShapeDtypeStruct((m, d), f32),
        grid=(m // tm, fh // tf),
        in_specs=[pl.BlockSpec((tm, d), lambda i, f: (i, 0)),
                  pl.BlockSpec((None, d, tf), lambda i, f: (layer, 0, f)),
                  pl.BlockSpec((None, d, tf), lambda i, f: (layer, 0, f)),
                  pl.BlockSpec((None, tf, d), lambda i, f: (layer, f, 0)),
                  pl.BlockSpec((1, d), lambda i, f: (0, 0)),
                  pl.BlockSpec((1, d), lambda i, f: (0, 0))],
        out_specs=pl.BlockSpec((tm, d), lambda i, f: (i, 0)),
        scratch_shapes=[pltpu.VMEM((tm, d), f32)],
        compiler_params=pltpu.CompilerParams(
            dimension_semantics=("parallel", "arbitrary"), vmem_limit_bytes=VMEM_LIMIT),
        name="ffn_ln",
    )(h, wg, wu, wd, g.reshape(1, d), b.reshape(1, d))


def _tile_rows(x, reps):
    return jnp.concatenate([x] * reps, axis=0)


def _scaled_queries(q):
    assert math.log2(HEAD_DIM) % 2 == 0
    return (q * ATT_SCALE).astype(bf16)


def _masked_flash(qg, kt_at, v_at, c_lo, c_hi, bias_at):
    rows = qg.shape[0]

    def body(c, carry):
        m, l, acc = carry
        s = jnp.dot(qg, kt_at(c), preferred_element_type=f32)
        bias = bias_at(c)
        s = jnp.concatenate([s[r * QBLOCK:(r + 1) * QBLOCK] + bias for r in range(rows // QBLOCK)], axis=0)
        m_new = jnp.maximum(m, jnp.max(s, axis=1, keepdims=True))
        p = jnp.exp(s - m_new)
        alpha = jnp.exp(m - m_new)
        l = alpha * l + jnp.sum(p, axis=1, keepdims=True)
        acc = alpha * acc + jnp.dot(p.astype(bf16), v_at(c), preferred_element_type=f32)
        return m_new, l, acc

    init = (jnp.full((rows, 1), MASKED, f32), jnp.zeros((rows, 1), f32),
            jnp.zeros((rows, HEAD_DIM), f32))
    _, l, acc = lax.fori_loop(c_lo, c_hi, body, init)
    return acc / l


def _sortable_key(x):
    bits = pltpu.bitcast(x, jnp.int32)
    return bits ^ (jnp.right_shift(bits, 31) & jnp.int32(0x7FFFFFFF))


def _topk_threshold(count, topk):
    topk = float(topk)
    cnt_nonneg = count(lambda k, i: k >= 0)
    thr0 = jnp.where(cnt_nonneg >= topk, 0, INT_MIN).astype(jnp.int32)

    def bit_body(b, thr):
        cand = thr + jnp.left_shift(jnp.int32(1), 30 - b)
        return jnp.where(count(lambda k, i: k >= cand) >= topk, cand, thr)

    thr = lax.fori_loop(0, 31, bit_body, thr0)
    need = topk - count(lambda k, i: k > thr)
    return thr, need


def _tie_index_cut(count, rows, thr, need, idx_bits):
    def cut_body(b, cut):
        cand = cut + jnp.left_shift(jnp.int32(1), idx_bits - 1 - b)
        return jnp.where(count(lambda k, i: (k == thr) & (i < cand)) < need, cand, cut)

    return lax.fori_loop(0, idx_bits, cut_body, jnp.zeros((rows, 1), jnp.int32))


def _select_blocks(imp, q_pos, lane):
    ratio = SLC_BLOCK // CMP_BLOCK
    on_block_lane = (lane & (ratio - 1)) == 0
    blk = jnp.right_shift(lane, 1)
    cur = jnp.right_shift(q_pos, 6)
    future = blk > cur
    forced = (blk == 0) | (blk == cur) | (blk == cur - 1)
    imp = imp + pltpu.roll(imp, CMP_LANES - 1, 1)
    score = jnp.where(future, -jnp.inf, jnp.where(forced, jnp.inf, imp))
    score = jnp.where(on_block_lane, score, -jnp.inf)
    if imp.shape[0] == CMP_LANES:
        score_t = score.T
        blk_t = lax.broadcasted_iota(jnp.int32, score_t.shape, 0)
        rank_t = jnp.zeros(score_t.shape, f32)
        for b in range(CMP_LANES // ratio):
            other = score_t[ratio * b:ratio * b + 1, :]
            ahead = (other > score_t) | ((other == score_t) & (ratio * b < blk_t))
            rank_t = rank_t + jnp.where(ahead, 1.0, 0.0)
        rank = rank_t.T
    else:
        rank = jnp.zeros(imp.shape, f32)
        for b in range(CMP_LANES // ratio):
            col = score[:, ratio * b:ratio * b + 1]
            ahead = (col > score) | ((col == score) & (ratio * b < lane))
            rank = rank + jnp.where(ahead, 1.0, 0.0)
    return jnp.where((rank < float(SLC_TOPN)) & on_block_lane, 1.0, 0.0)


def _dsa_prompt_kernel(iq_ref, iw_ref, ikt_ref, q_ref, kt_ref, v_ref, tri_ref, o_ref, keys_ref, sel_ref):
    i = pl.program_id(1)
    nch = (i * QBLOCK + QBLOCK + KEY_CHUNK - 1) // KEY_CHUNK
    q_pos = i * QBLOCK + lax.broadcasted_iota(jnp.int32, (QBLOCK, 1), 0)
    lane = lax.broadcasted_iota(jnp.int32, (QBLOCK, KEY_CHUNK), 1)

    iq = iq_ref[...]
    iq_stack = jnp.concatenate(
        [iq[:, h * IDX_DIM:(h + 1) * IDX_DIM] for h in range(IDX_HEADS)], axis=0).astype(bf16)
    iw = iw_ref[...] * IDX_HEADS ** -0.5

    def score_body(c, carry):
        dots = jnp.dot(iq_stack, ikt_ref[c], preferred_element_type=f32) * IDX_DIM ** -0.5
        acc = jnp.zeros((QBLOCK, KEY_CHUNK), f32)
        for h in range(IDX_HEADS):
            acc = acc + iw[:, h:h + 1] * jnp.maximum(dots[h * QBLOCK:(h + 1) * QBLOCK], 0.0)
        causal = c * KEY_CHUNK + lane <= q_pos
        keys_ref[c] = jnp.where(causal, _sortable_key(acc), INT_MIN)
        return carry

    lax.fori_loop(0, nch, score_body, 0)

    def count(pred):
        def body(c, acc):
            hit = jnp.where(pred(keys_ref[c], c * KEY_CHUNK + lane), 1.0, 0.0)
            part = hit[:, :LANES]
            for j in range(1, KEY_CHUNK // LANES):
                part = part + hit[:, j * LANES:(j + 1) * LANES]
            return acc + part
        acc = lax.fori_loop(0, nch, body, jnp.zeros((QBLOCK, LANES), f32))
        return jnp.sum(acc, axis=1, keepdims=True)

    thr, need = _topk_threshold(count, DSA_TOPK)

    def sel_body(c, ties_before):
        k = keys_ref[c]
        tie = k == thr
        tie_rank = ties_before + jnp.dot(jnp.where(tie, 1.0, 0.0).astype(bf16), tri_ref[...],
                                         preferred_element_type=f32)
        chosen = ((k > thr) | (tie & (tie_rank <= need))) & (c * KEY_CHUNK + lane <= q_pos)
        sel_ref[c] = jnp.where(chosen, 0.0, MASKED)
        return tie_rank[:, KEY_CHUNK - 1:KEY_CHUNK]

    lax.fori_loop(0, nch, sel_body, jnp.zeros((QBLOCK, 1), f32))

    for g in range(KV_HEADS):
        o = _masked_flash(_scaled_queries(q_ref[g]), lambda c: kt_ref[g, c], lambda c: v_ref[g, c],
                          0, nch, lambda c: sel_ref[c])
        for r in range(Q_PER_KV):
            h = g * Q_PER_KV + r
            o_ref[:, h * HEAD_DIM:(h + 1) * HEAD_DIM] = o[r * QBLOCK:(r + 1) * QBLOCK]


def _stack_query_heads(q, n, t):
    q = q.reshape(n, t // QBLOCK, QBLOCK, KV_HEADS, Q_PER_KV, HEAD_DIM)
    q = q.transpose(0, 3, 1, 4, 2, 5)
    return q.reshape(n, KV_HEADS, t // QBLOCK, Q_PER_KV * QBLOCK, HEAD_DIM)


def _chunked_kv(kv, n, t, chunk):
    kv = kv.reshape(n, t // chunk, chunk, KV_HEADS, 2, HEAD_DIM).astype(bf16)
    kt = kv[:, :, :, :, 0, :].transpose(0, 3, 1, 4, 2)
    v = kv[:, :, :, :, 1, :].transpose(0, 3, 1, 2, 4)
    return kt, v


def dsa_prompt(dq, dkv, iq, ik, iw):
    n, t = dq.shape[:2]
    nchunks = t // KEY_CHUNK
    assert t % KEY_CHUNK == 0 and min(DSA_TOPK, t // 4) == DSA_TOPK
    ikt = ik.reshape(n, nchunks, KEY_CHUNK, IDX_DIM).transpose(0, 1, 3, 2).astype(bf16)
    qs = _stack_query_heads(dq, n, t)
    kt, v = _chunked_kv(dkv, n, t, KEY_CHUNK)
    at = jnp.arange(KEY_CHUNK)
    tri = (at[:, None] <= at[None, :]).astype(bf16)
    return pl.pallas_call(
        _dsa_prompt_kernel,
        out_shape=jax.ShapeDtypeStruct((n, t, GROUP_WIDTH), f32),
        grid=(n, t // QBLOCK),
        in_specs=[
            pl.BlockSpec((None, QBLOCK, IDX_HEADS * IDX_DIM), lambda b, i: (b, i, 0)),
            pl.BlockSpec((None, QBLOCK, IDX_HEADS), lambda b, i: (b, i, 0)),
            pl.BlockSpec((None, nchunks, IDX_DIM, KEY_CHUNK), lambda b, i: (b, 0, 0, 0)),
            pl.BlockSpec((None, KV_HEADS, None, Q_PER_KV * QBLOCK, HEAD_DIM), lambda b, i: (b, 0, i, 0, 0)),
            pl.BlockSpec((None, KV_HEADS, nchunks, HEAD_DIM, KEY_CHUNK), lambda b, i: (b, 0, 0, 0, 0)),
            pl.BlockSpec((None, KV_HEADS, nchunks, KEY_CHUNK, HEAD_DIM), lambda b, i: (b, 0, 0, 0, 0)),
            pl.BlockSpec((KEY_CHUNK, KEY_CHUNK), lambda b, i: (0, 0)),
        ],
        out_specs=pl.BlockSpec((None, QBLOCK, GROUP_WIDTH), lambda b, i: (b, i, 0)),
        scratch_shapes=[pltpu.VMEM((nchunks, QBLOCK, KEY_CHUNK), jnp.int32),
                        pltpu.VMEM((nchunks, QBLOCK, KEY_CHUNK), f32)],
        compiler_params=pltpu.CompilerParams(
            dimension_semantics=("parallel", "arbitrary"), vmem_limit_bytes=VMEM_LIMIT),
        name="dsa_prompt",
    )(iq, iw, ikt, qs, kt, v, tri)


def _nsa_compress_kernel(x_ref, pe_ref, w1_ref, w2_ref, o_ref, hid_ref):
    k = pl.program_id(0)

    @pl.when(k == 0)
    def _():
        hid_ref[...] = jnp.zeros_like(hid_ref)

    hid_ref[...] += jnp.dot((x_ref[...] + pe_ref[...]).astype(bf16), w1_ref[...],
                            preferred_element_type=f32)

    @pl.when(k == pl.num_programs(0) - 1)
    def _():
        hid = jax.nn.gelu(hid_ref[...]).astype(bf16)
        o_ref[...] = jnp.dot(hid, w2_ref[...], preferred_element_type=f32)


def nsa_compress_weights(cmp_pe, cmp_w1, cmp_w2):
    w1 = cmp_w1.reshape(2, CMP_BLOCK, HEAD_DIM, CMP_HIDDEN)
    eye_g = jnp.eye(KV_HEADS, dtype=cmp_w1.dtype)
    eye_j = jnp.eye(2, dtype=cmp_w1.dtype)
    w1_big = jnp.einsum('jtde,gh,jk->tgjdhke', w1, eye_g, eye_j)
    w1_big = w1_big.reshape(CMP_BLOCK * KV_ROW, KV_HEADS * 2 * CMP_HIDDEN).astype(bf16)
    w2_big = jnp.einsum('jed,gh,jk->gjehkd', cmp_w2, eye_g, eye_j)
    w2_big = w2_big.reshape(KV_HEADS * 2 * CMP_HIDDEN, KV_ROW).astype(bf16)
    pe_row = jnp.broadcast_to(cmp_pe[:, None], (CMP_BLOCK, KV_HEADS, 2, HEAD_DIM))
    return pe_row.reshape(1, CMP_BLOCK * KV_ROW), w1_big, w2_big


def nsa_compress_rows(x, pe_row, w1_big, w2_big, tk=2048):
    m, kdim = x.shape
    hw = w1_big.shape[1]
    return pl.pallas_call(
        _nsa_compress_kernel,
        out_shape=jax.ShapeDtypeStruct((m, KV_ROW), f32),
        grid=(kdim // tk,),
        in_specs=[pl.BlockSpec((m, tk), lambda k: (0, k)),
                  pl.BlockSpec((1, tk), lambda k: (0, k)),
                  pl.BlockSpec((tk, hw), lambda k: (k, 0)),
                  pl.BlockSpec((hw, KV_ROW), lambda k: (0, 0))],
        out_specs=pl.BlockSpec((m, KV_ROW), lambda k: (0, 0)),
        scratch_shapes=[pltpu.VMEM((m, hw), f32)],
        compiler_params=pltpu.CompilerParams(
            dimension_semantics=("arbitrary",), vmem_limit_bytes=VMEM_LIMIT),
        name="nsa_compress",
    )(x, pe_row, w1_big, w2_big)


def _nsa_prompt_kernel(gate_ref, q_ref, ckt_ref, cv_ref, skt_ref, sv_ref, wkt_ref, wv_ref, o_ref):
    i = pl.program_id(1)
    nch = (i * QBLOCK + QBLOCK + KEY_CHUNK - 1) // KEY_CHUNK
    q_pos = i * QBLOCK + lax.broadcasted_iota(jnp.int32, (QBLOCK, 1), 0)
    lane = lax.broadcasted_iota(jnp.int32, (QBLOCK, CMP_LANES), 1)
    lane_k = lax.broadcasted_iota(jnp.int32, (QBLOCK, KEY_CHUNK), 1)
    expand_row = lax.broadcasted_iota(jnp.int32, (CMP_LANES, KEY_CHUNK), 0)
    expand_col = lax.broadcasted_iota(jnp.int32, (CMP_LANES, KEY_CHUNK), 1)
    gate = jax.nn.sigmoid(gate_ref[...])
    ratio = SLC_BLOCK // CMP_BLOCK
    c_vis = _tile_rows(jnp.where((lane + 1) * CMP_BLOCK - 1 <= q_pos, 1.0, 0.0), Q_PER_KV) > 0.5
    w_tiles = WINDOW // QBLOCK + 1
    w_lo = jnp.maximum(i - WINDOW // QBLOCK, 0)
    w_idx = w_lo * QBLOCK + lax.broadcasted_iota(jnp.int32, (QBLOCK, w_tiles * QBLOCK), 1)
    w_bias = _tile_rows(jnp.where((w_idx <= q_pos) & (w_idx > q_pos - WINDOW), 0.0, MASKED), Q_PER_KV)

    for g in range(KV_HEADS):
        qg = _scaled_queries(q_ref[g])
        s = jnp.dot(qg, ckt_ref[g], preferred_element_type=f32)
        s = jnp.where(c_vis, s, -jnp.inf)
        m = jnp.max(s, axis=1, keepdims=True)
        m = jnp.where(m == -jnp.inf, 0.0, m)
        p = jnp.exp(s - m)
        p = p / jnp.maximum(jnp.sum(p, axis=1, keepdims=True), F32_TINY)
        o_cmp = jnp.dot(p.astype(bf16), cv_ref[g], preferred_element_type=f32)
        imp = p[0:QBLOCK]
        for r in range(1, Q_PER_KV):
            imp = imp + p[r * QBLOCK:(r + 1) * QBLOCK]
        sel_blk = _select_blocks(imp, q_pos, lane).astype(bf16)

        def slc_mask(c):
            kidx = c * KEY_CHUNK + lane_k
            expand = jnp.where(
                expand_row == ratio * jnp.right_shift(c * KEY_CHUNK + expand_col, 6), 1.0, 0.0).astype(bf16)
            picked = jnp.dot(sel_blk, expand, preferred_element_type=f32)
            return jnp.where((picked > 0.5) & (kidx <= q_pos), 0.0, MASKED)

        o_slc = _masked_flash(qg, lambda c: skt_ref[g, c], lambda c: sv_ref[g, c], 0, nch, slc_mask)

        w_kt = jnp.concatenate([wkt_ref[g, w_lo + j] for j in range(w_tiles)], axis=1)
        w_v = jnp.concatenate([wv_ref[g, w_lo + j] for j in range(w_tiles)], axis=0)
        s = jnp.dot(qg, w_kt, preferred_element_type=f32) + w_bias
        pw = jnp.exp(s - jnp.max(s, axis=1, keepdims=True))
        o_win = (jnp.dot(pw.astype(bf16), w_v, preferred_element_type=f32)
                 / jnp.sum(pw, axis=1, keepdims=True))
        for r in range(Q_PER_KV):
            h = g * Q_PER_KV + r
            rows = slice(r * QBLOCK, (r + 1) * QBLOCK)
            o_ref[:, h * HEAD_DIM:(h + 1) * HEAD_DIM] = (
                gate[:, 3 * h:3 * h + 1] * o_cmp[rows]
                + gate[:, 3 * h + 1:3 * h + 2] * o_slc[rows]
                + gate[:, 3 * h + 2:3 * h + 3] * o_win[rows])


def nsa_prompt(nq, ng, ckv, skv, wkv, cmp_weights):
    n, t = nq.shape[:2]
    nc = t // CMP_BLOCK
    assert nc <= CMP_LANES and t % KEY_CHUNK == 0 and SLC_BLOCK == 64 and SLC_BLOCK // CMP_BLOCK == 2
    assert t >= WINDOW + QBLOCK and WINDOW % QBLOCK == 0
    cmp_rows = nsa_compress_rows(ckv.reshape(n * nc, CMP_BLOCK * KV_ROW), *cmp_weights)
    cmp_rows = cmp_rows.reshape(n, nc, KV_HEADS, 2, HEAD_DIM)
    cmp_rows = jnp.pad(cmp_rows, ((0, 0), (0, CMP_LANES - nc), (0, 0), (0, 0), (0, 0))).astype(bf16)
    ckt = cmp_rows[:, :, :, 0, :].transpose(0, 2, 3, 1)
    cv = cmp_rows[:, :, :, 1, :].transpose(0, 2, 1, 3)
    qs = _stack_query_heads(nq, n, t)
    skt, sv = _chunked_kv(skv, n, t, KEY_CHUNK)
    wkt, wv = _chunked_kv(wkv, n, t, QBLOCK)
    nk, nw = t // KEY_CHUNK, t // QBLOCK
    return pl.pallas_call(
        _nsa_prompt_kernel,
        out_shape=jax.ShapeDtypeStruct((n, t, GROUP_WIDTH), f32),
        grid=(n, t // QBLOCK),
        in_specs=[
            pl.BlockSpec((None, QBLOCK, 3 * N_HEADS), lambda b, i: (b, i, 0)),
            pl.BlockSpec((None, KV_HEADS, None, Q_PER_KV * QBLOCK, HEAD_DIM), lambda b, i: (b, 0, i, 0, 0)),
            pl.BlockSpec((None, KV_HEADS, HEAD_DIM, CMP_LANES), lambda b, i: (b, 0, 0, 0)),
            pl.BlockSpec((None, KV_HEADS, CMP_LANES, HEAD_DIM), lambda b, i: (b, 0, 0, 0)),
            pl.BlockSpec((None, KV_HEADS, nk, HEAD_DIM, KEY_CHUNK), lambda b, i: (b, 0, 0, 0, 0)),
            pl.BlockSpec((None, KV_HEADS, nk, KEY_CHUNK, HEAD_DIM), lambda b, i: (b, 0, 0, 0, 0)),
            pl.BlockSpec((None, KV_HEADS, nw, HEAD_DIM, QBLOCK), lambda b, i: (b, 0, 0, 0, 0)),
            pl.BlockSpec((None, KV_HEADS, nw, QBLOCK, HEAD_DIM), lambda b, i: (b, 0, 0, 0, 0)),
        ],
        out_specs=pl.BlockSpec((None, QBLOCK, GROUP_WIDTH), lambda b, i: (b, i, 0)),
        compiler_params=pltpu.CompilerParams(
            dimension_semantics=("parallel", "arbitrary"), vmem_limit_bytes=VMEM_LIMIT),
        name="nsa_prompt",
    )(ng, qs, ckt, cv, skt, sv, wkt, wv)


def s5_discretize(w):
    a_re, a_im = w['a_re'], w['a_im']
    dt = jnp.exp(w['log_dt'])[:, None]
    mag = jnp.exp(a_re * dt)
    ab_re, ab_im = mag * jnp.cos(a_im * dt), mag * jnp.sin(a_im * dt)
    den = a_re * a_re + a_im * a_im
    nr, ni = ab_re - 1.0, ab_im
    f_re = (nr * a_re + ni * a_im) / den
    f_im = (ni * a_re - nr * a_im) / den
    b_re, b_im = w['b_re'], w['b_im']
    bb_re = f_re[..., None] * b_re - f_im[..., None] * b_im
    bb_im = f_re[..., None] * b_im + f_im[..., None] * b_re
    eye = jnp.eye(S5_GROUPS, dtype=f32)

    def in_map(bb):
        return jnp.einsum('gsc,gh->gchs', bb, eye).reshape(GROUP_WIDTH, S5_WIDTH).astype(bf16)

    def out_map(cc):
        return jnp.einsum('gcs,gh->gshc', cc, eye).reshape(S5_WIDTH, GROUP_WIDTH).astype(bf16)

    return dict(a_re=ab_re.reshape(1, S5_WIDTH), a_im=ab_im.reshape(1, S5_WIDTH),
                b_re=in_map(bb_re), b_im=in_map(bb_im),
                c_re=out_map(w['c_re']), c_im=out_map(w['c_im']),
                d=w['d'].reshape(1, GROUP_WIDTH), w_glu=w['w_glu'].astype(bf16),
                b_glu=w['b_glu'].reshape(1, GROUP_WIDTH))


def _s5_prompt_kernel(u_ref, h0_ref, are_ref, aim_ref, bre_ref, bim_ref, cre_ref, cim_ref, d_ref,
                      wglu_ref, bglu_ref, o_ref, hlast_ref, state_ref, bure_ref, buim_ref, hre_ref, him_ref):
    j = pl.program_id(1)

    @pl.when(j == 0)
    def _():
        state_ref[...] = h0_ref[...]

    u = u_ref[...]
    ub = u.astype(bf16)
    bure_ref[...] = jnp.dot(ub, bre_ref[...], preferred_element_type=f32)
    buim_ref[...] = jnp.dot(ub, bim_ref[...], preferred_element_type=f32)
    a_re, a_im = are_ref[...], aim_ref[...]

    def step(t, carry):
        h_re, h_im = carry
        row = pl.ds(t, 1)
        n_re = a_re * h_re - a_im * h_im + bure_ref[row, :]
        n_im = a_re * h_im + a_im * h_re + buim_ref[row, :]
        hre_ref[row, :] = n_re
        him_ref[row, :] = n_im
        return n_re, n_im

    h_re, h_im = lax.fori_loop(0, u.shape[0], step, (state_ref[0:1, :], state_ref[1:2, :]), unroll=8)
    state_ref[0:1, :] = h_re
    state_ref[1:2, :] = h_im
    hlast_ref[...] = state_ref[...]
    y = (jnp.dot(hre_ref[...].astype(bf16), cre_ref[...], preferred_element_type=f32)
         - jnp.dot(him_ref[...].astype(bf16), cim_ref[...], preferred_element_type=f32)
         + d_ref[...] * u)
    y = jax.nn.gelu(y)
    o_ref[...] = y * jax.nn.sigmoid(
        jnp.dot(y.astype(bf16), wglu_ref[...], preferred_element_type=f32) + bglu_ref[...])


def s5_prompt(u, h0, sw):
    n, t = u.shape[:2]
    tc = min(S5_TIME_CHUNK, t)
    const = lambda shape: pl.BlockSpec(shape, lambda b, j: (0,) * len(shape))
    return pl.pallas_call(
        _s5_prompt_kernel,
        out_shape=(jax.ShapeDtypeStruct((n, t, GROUP_WIDTH), f32),
                   jax.ShapeDtypeStruct((n, 2, S5_WIDTH), f32)),
        grid=(n, t // tc),
        in_specs=[pl.BlockSpec((None, tc, GROUP_WIDTH), lambda b, j: (b, j, 0)),
                  pl.BlockSpec((None, 2, S5_WIDTH), lambda b, j: (b, 0, 0)),
                  const((1, S5_WIDTH)), const((1, S5_WIDTH)),
                  const((GROUP_WIDTH, S5_WIDTH)), const((GROUP_WIDTH, S5_WIDTH)),
                  const((S5_WIDTH, GROUP_WIDTH)), const((S5_WIDTH, GROUP_WIDTH)),
                  const((1, GROUP_WIDTH)), const((GROUP_WIDTH, GROUP_WIDTH)), const((1, GROUP_WIDTH))],
        out_specs=(pl.BlockSpec((None, tc, GROUP_WIDTH), lambda b, j: (b, j, 0)),
                   pl.BlockSpec((None, 2, S5_WIDTH), lambda b, j: (b, 0, 0))),
        scratch_shapes=[pltpu.VMEM((2, S5_WIDTH), f32)] + [pltpu.VMEM((tc, S5_WIDTH), f32)] * 4,
        compiler_params=pltpu.CompilerParams(
            dimension_semantics=("parallel", "arbitrary"), vmem_limit_bytes=VMEM_LIMIT),
        name="s5_prompt",
    )(u, h0, sw['a_re'], sw['a_im'], sw['b_re'], sw['b_im'], sw['c_re'], sw['c_im'], sw['d'],
      sw['w_glu'], sw['b_glu'])


def _s5_step_kernel(u_ref, h0_ref, are_ref, aim_ref, bre_ref, bim_ref, cre_ref, cim_ref, d_ref,
                    wglu_ref, bglu_ref, o_ref, h_ref):
    u = u_ref[...]
    ub = u.astype(bf16)
    h0_re, h0_im = h0_ref[:, :S5_WIDTH], h0_ref[:, S5_WIDTH:]
    a_re, a_im = are_ref[...], aim_ref[...]
    h_re = a_re * h0_re - a_im * h0_im + jnp.dot(ub, bre_ref[...], preferred_element_type=f32)
    h_im = a_re * h0_im + a_im * h0_re + jnp.dot(ub, bim_ref[...], preferred_element_type=f32)
    h_ref[:, :S5_WIDTH] = h_re
    h_ref[:, S5_WIDTH:] = h_im
    y = (jnp.dot(h_re.astype(bf16), cre_ref[...], preferred_element_type=f32)
         - jnp.dot(h_im.astype(bf16), cim_ref[...], preferred_element_type=f32)
         + d_ref[...] * u)
    y = jax.nn.gelu(y)
    o_ref[...] = y * jax.nn.sigmoid(
        jnp.dot(y.astype(bf16), wglu_ref[...], preferred_element_type=f32) + bglu_ref[...])


def s5_step(u, h0, sw):
    n = u.shape[0]
    return pl.pallas_call(
        _s5_step_kernel,
        out_shape=(jax.ShapeDtypeStruct((n, GROUP_WIDTH), f32),
                   jax.ShapeDtypeStruct((n, 2 * S5_WIDTH), f32)),
        compiler_params=pltpu.CompilerParams(vmem_limit_bytes=VMEM_LIMIT),
        name="s5_step",
    )(u, h0, sw['a_re'], sw['a_im'], sw['b_re'], sw['b_im'], sw['c_re'], sw['c_im'], sw['d'],
      sw['w_glu'], sw['b_glu'])


GLA_SUB = 16
GLA_QK = GLA_HEADS * GLA_DK
HIGHEST = lax.Precision.HIGHEST
_NT = (((1,), (1,)), ((), ()))
_TN = (((0,), (0,)), ((), ()))


def _log_decay(ga, wgate_ref, bgate_ref):
    z = jnp.dot(ga, wgate_ref[...], preferred_element_type=f32, precision=HIGHEST) + bgate_ref[...]
    return jax.nn.log_sigmoid(z) / GLA_TAU


def _head_sum_matrix():
    r = lax.broadcasted_iota(jnp.int32, (GLA_QK, GROUP_WIDTH), 0) // GLA_DK
    c = lax.broadcasted_iota(jnp.int32, (GLA_QK, GROUP_WIDTH), 1) // GLA_DV
    return jnp.where(r == c, 1.0, 0.0)


def _gla_finish(o, gr, normg_ref):
    outs = []
    for h in range(GLA_HEADS):
        oh = o[:, h * GLA_DV:(h + 1) * GLA_DV]
        outs.append(oh * lax.rsqrt(jnp.mean(oh * oh, axis=-1, keepdims=True) + LN_EPS) * normg_ref[...])
    return jnp.concatenate(outs, axis=1) * (gr * jax.nn.sigmoid(gr))


def _gla_prompt_kernel(q_ref, k_ref, v_ref, ga_ref, gr_ref, wgate_ref, bgate_ref, normg_ref, s0_ref,
                       o_ref, slast_ref, st_ref, oi_ref):
    j = pl.program_id(1)

    @pl.when(j == 0)
    def _():
        st_ref[...] = s0_ref[...]

    rows = q_ref.shape[0]
    nsub = rows // GLA_SUB
    q = q_ref[...] * GLA_DK ** -0.5
    k = k_ref[...]
    v = v_ref[...]
    log_a = _log_decay(ga_ref[...], wgate_ref, bgate_ref)
    ri = lax.broadcasted_iota(jnp.int32, (rows, rows), 0)
    ci = lax.broadcasted_iota(jnp.int32, (rows, rows), 1)
    tri = jnp.where((ri // GLA_SUB == ci // GLA_SUB) & (ci <= ri), 1.0, 0.0)
    b = jnp.dot(tri, log_a, preferred_element_type=f32, precision=HIGHEST)
    row_in_sub = lax.broadcasted_iota(jnp.int32, (rows, 1), 0) % GLA_SUB
    head_sum = _head_sum_matrix().astype(bf16)

    def sub_row(x, jj):
        x3 = x.reshape(nsub, GLA_SUB, x.shape[1])
        return jnp.broadcast_to(x3[:, jj:jj + 1, :], x3.shape).reshape(x.shape)

    o = jnp.zeros((rows, GROUP_WIDTH), f32)
    for jj in range(GLA_SUB):
        decay = jnp.exp(jnp.minimum(b - sub_row(b, jj), 0.0))
        prod = jnp.where(row_in_sub >= jj, q * sub_row(k, jj) * decay, 0.0)
        prod_hi = prod.astype(bf16)
        prod_lo = (prod - prod_hi.astype(f32)).astype(bf16)
        att = (jnp.dot(prod_hi, head_sum, preferred_element_type=f32)
               + jnp.dot(prod_lo, head_sum, preferred_element_type=f32))
        o = o + att * sub_row(v, jj)

    for i in range(nsub):
        blk = slice(i * GLA_SUB, (i + 1) * GLA_SUB)
        b_blk = b[blk]
        b_end = b[(i + 1) * GLA_SUB - 1:(i + 1) * GLA_SUB]
        q_blk = (q[blk] * jnp.exp(b_blk)).astype(bf16)
        k_blk = (k[blk] * jnp.exp(b_end - b_blk)).astype(bf16)
        v_blk = v[blk].astype(bf16)
        a_end = jnp.exp(b_end)
        for h in range(GLA_HEADS):
            ks = slice(h * GLA_DK, (h + 1) * GLA_DK)
            vs = slice(h * GLA_DV, (h + 1) * GLA_DV)
            st = st_ref[h]
            oi_ref[blk, vs] = lax.dot_general(q_blk[:, ks], st.astype(bf16), _NT, preferred_element_type=f32)
            st_ref[h] = st * a_end[:, ks] + lax.dot_general(v_blk[:, vs], k_blk[:, ks], _TN,
                                                            preferred_element_type=f32)

    o_ref[...] = _gla_finish(o + oi_ref[...], gr_ref[...], normg_ref)
    slast_ref[...] = st_ref[...]


def gla_prompt(gq, gk, gv, ga, gr, w_gate, b_gate, norm_g):
    n, t = gq.shape[:2]
    tc = min(GLA_CHUNK, t)
    rows = lambda width: pl.BlockSpec((None, tc, width), lambda b, j: (b, j, 0))
    const = lambda shape: pl.BlockSpec(shape, lambda b, j: (0,) * len(shape))
    state = pl.BlockSpec((None, GLA_HEADS, GLA_DV, GLA_DK), lambda b, j: (b, 0, 0, 0))
    o, s_t = pl.pallas_call(
        _gla_prompt_kernel,
        out_shape=(jax.ShapeDtypeStruct((n, t, GROUP_WIDTH), f32),
                   jax.ShapeDtypeStruct((n, GLA_HEADS, GLA_DV, GLA_DK), f32)),
        grid=(n, t // tc),
        in_specs=[rows(GLA_QK), rows(GLA_QK), rows(GROUP_WIDTH), rows(GLA_LOWRANK), rows(GROUP_WIDTH),
                  const((GLA_LOWRANK, GLA_QK)), const((1, GLA_QK)), const((1, GLA_DV)), state],
        out_specs=(rows(GROUP_WIDTH), state),
        scratch_shapes=[pltpu.VMEM((GLA_HEADS, GLA_DV, GLA_DK), f32), pltpu.VMEM((tc, GROUP_WIDTH), f32)],
        compiler_params=pltpu.CompilerParams(
            dimension_semantics=("parallel", "arbitrary"), vmem_limit_bytes=VMEM_LIMIT),
        name="gla_prompt",
    )(gq, gk, gv, ga, gr, w_gate, b_gate.reshape(1, GLA_QK), norm_g.reshape(1, GLA_DV),
      jnp.zeros((n, GLA_HEADS, GLA_DV, GLA_DK), f32))
    return o, s_t.transpose(0, 1, 3, 2)


GLA_STEP_SEQS = 8


def _gla_step_kernel(q_ref, k_ref, v_ref, ga_ref, gr_ref, wgate_ref, bgate_ref, normg_ref, s_ref,
                     o_ref, snew_ref):
    q = q_ref[...] * GLA_DK ** -0.5
    k = k_ref[...]
    v = v_ref[...]
    a = jnp.exp(_log_decay(ga_ref[...], wgate_ref, bgate_ref))
    qa = q * a
    seq = lax.broadcasted_iota(jnp.int32, (GLA_STEP_SEQS, 1), 0)
    o = jnp.dot(q * k, _head_sum_matrix(), preferred_element_type=f32, precision=HIGHEST) * v
    o_heads = [jnp.zeros((GLA_STEP_SEQS, GLA_DV), f32) for _ in range(GLA_HEADS)]
    for i in range(GLA_STEP_SEQS):
        mine = seq == i
        pick = jnp.broadcast_to(jnp.where(mine, 1.0, 0.0), (GLA_STEP_SEQS, GLA_DV))
        qa_i = jnp.where(mine, qa, 0.0).astype(bf16)
        k_i = jnp.where(mine, k, 0.0).astype(bf16)
        vb = v.astype(bf16)
        for h in range(GLA_HEADS):
            ks = slice(h * GLA_DK, (h + 1) * GLA_DK)
            vs = slice(h * GLA_DV, (h + 1) * GLA_DV)
            s_old = s_ref[i, h]
            a_rows = lax.dot_general(a[:, ks], pick, _TN, preferred_element_type=f32, precision=HIGHEST)
            o_heads[h] = o_heads[h] + jnp.dot(qa_i[:, ks], s_old.astype(bf16), preferred_element_type=f32)
            snew_ref[i, h] = a_rows * s_old + lax.dot_general(k_i[:, ks], vb[:, vs], _TN,
                                                               preferred_element_type=f32)
    o_ref[...] = _gla_finish(o + jnp.concatenate(o_heads, axis=1), gr_ref[...], normg_ref)


def gla_step(gq, gk, gv, ga, gr, w_gate, b_gate, norm_g, s0):
    n = gq.shape[0]
    rows = lambda width: pl.BlockSpec((GLA_STEP_SEQS, width), lambda i: (i, 0))
    const = lambda shape: pl.BlockSpec(shape, lambda i: (0,) * len(shape))
    state = pl.BlockSpec((GLA_STEP_SEQS, GLA_HEADS, GLA_DK, GLA_DV), lambda i: (i, 0, 0, 0))
    return pl.pallas_call(
        _gla_step_kernel,
        out_shape=(jax.ShapeDtypeStruct((n, GROUP_WIDTH), f32),
                   jax.ShapeDtypeStruct((n, GLA_HEADS, GLA_DK, GLA_DV), f32)),
        grid=(n // GLA_STEP_SEQS,),
        in_specs=[rows(GLA_QK), rows(GLA_QK), rows(GROUP_WIDTH), rows(GLA_LOWRANK), rows(GROUP_WIDTH),
                  const((GLA_LOWRANK, GLA_QK)), const((1, GLA_QK)), const((1, GLA_DV)), state],
        out_specs=(rows(GROUP_WIDTH), state),
        compiler_params=pltpu.CompilerParams(
            dimension_semantics=("parallel",), vmem_limit_bytes=VMEM_LIMIT),
        name="gla_step",
    )(gq, gk, gv, ga, gr, w_gate, b_gate.reshape(1, GLA_QK), norm_g.reshape(1, GLA_DV), s0)


N_PAGES = PAST_LEN // PAGE_SIZE
KEY_PAD = PAST_LEN + LANES
STEP_ROWS = 8
CMP_STEP_SEQS = 2
BLOCKS_PER_PAGE = PAGE_SIZE // CMP_BLOCK
CMP_ROW = CMP_BLOCK * KV_ROW


def _tokens_minor(cache):
    token_axis = 2
    return jnp.moveaxis(cache, token_axis, -1)


def _seq_spec(*tail):
    zeros = (0,) * len(tail)
    return pl.BlockSpec((None,) + tail, lambda i, pt: (i,) + zeros)


def _layer_page_specs(block_tail, layer, seqs_per_step=1):
    zeros = (0,) * len(block_tail)
    return [pl.BlockSpec((None, None) + block_tail,
                         lambda i, pt, s=s, p=p: (layer, pt[i * seqs_per_step + s, p]) + zeros)
            for s in range(seqs_per_step) for p in range(N_PAGES)]


def _layer_seq_spec(tail, layer):
    zeros = (0,) * len(tail)
    return pl.BlockSpec((None, None) + tail, lambda i, pt: (layer, i) + zeros)


def _topk_rows_kernel(keys_ref, sel_ref, *, topk, idx_bits, n_valid):
    keys = keys_ref[...]
    lane = lax.broadcasted_iota(jnp.int32, keys.shape, 1)
    count = lambda pred: jnp.sum(jnp.where(pred(keys, lane), 1.0, 0.0), axis=1, keepdims=True)
    thr, need = _topk_threshold(count, topk)
    cut = _tie_index_cut(count, keys.shape[0], thr, need, idx_bits)
    chosen = ((keys > thr) | ((keys == thr) & (lane <= cut))) & (lane < n_valid)
    sel_ref[...] = jnp.where(chosen, 1.0, 0.0)


def _nsa_compress_paged_kernel(pt_ref, pe_ref, w1_ref, w2_ref, *rest):
    npg = CMP_STEP_SEQS * N_PAGES
    pages, o_ref, x_ref = rest[:npg], rest[npg], rest[npg + 1]
    for s in range(npg):
        x_ref[s * BLOCKS_PER_PAGE:(s + 1) * BLOCKS_PER_PAGE, :] = pages[s][...] + pe_ref[...]
    tk = 2048
    hid = jnp.zeros((npg * BLOCKS_PER_PAGE, w1_ref.shape[1]), f32)
    for kk in range(CMP_ROW // tk):
        cols = slice(kk * tk, (kk + 1) * tk)
        hid = hid + jnp.dot(x_ref[:, cols].astype(bf16), w1_ref[cols, :], preferred_element_type=f32)
    o_ref[...] = jnp.dot(jax.nn.gelu(hid).astype(bf16), w2_ref[...], preferred_element_type=f32)


def nsa_compress_paged(pools_cmp, layer, page_table, pe_row, w1_big, w2_big):
    n = page_table.shape[0]
    blocks = N_PAGES * BLOCKS_PER_PAGE
    hw = w1_big.shape[1]
    rows = CMP_STEP_SEQS * blocks
    const = lambda shape: pl.BlockSpec(shape, lambda i, pt: (0,) * len(shape))
    out = pl.pallas_call(
        _nsa_compress_paged_kernel,
        out_shape=jax.ShapeDtypeStruct((n * blocks, KV_ROW), f32),
        grid_spec=pltpu.PrefetchScalarGridSpec(
            num_scalar_prefetch=1, grid=(n // CMP_STEP_SEQS,),
            in_specs=[const((1, CMP_ROW)), const((CMP_ROW, hw)), const((hw, KV_ROW))]
            + _layer_page_specs((BLOCKS_PER_PAGE, CMP_ROW), layer, CMP_STEP_SEQS),
            out_specs=pl.BlockSpec((rows, KV_ROW), lambda i, pt: (i, 0)),
            scratch_shapes=[pltpu.VMEM((rows, CMP_ROW), f32)]),
        compiler_params=pltpu.CompilerParams(
            dimension_semantics=("parallel",), vmem_limit_bytes=VMEM_LIMIT),
        name="nsa_compress_paged",
    )(page_table, pe_row, w1_big, w2_big, *([pools_cmp] * (CMP_STEP_SEQS * N_PAGES)))
    return out.reshape(n, blocks, KV_ROW)


def _decode_attention(q8, blocks, masks, kv_new, new_ok, is_g0):
    qb = q8.astype(bf16)
    by_group = lambda fn: jnp.where(is_g0, fn(0), fn(1))
    s_new = by_group(lambda g: jnp.sum(q8 * kv_new[2 * g:2 * g + 1], axis=1, keepdims=True)) * ATT_SCALE
    if new_ok is not None:
        s_new = jnp.where(new_ok, s_new, MASKED)
    m = s_new
    scores = []
    for blk, msk in zip(blocks, masks):
        s = by_group(lambda g: jnp.dot(qb, blk[g, 0].astype(bf16), preferred_element_type=f32)) * ATT_SCALE
        s = jnp.where(msk > 0.5, s, MASKED)
        scores.append(s)
        m = jnp.maximum(m, jnp.max(s, axis=1, keepdims=True))
    p_new = jnp.exp(s_new - m)
    if new_ok is not None:
        p_new = jnp.where(new_ok, p_new, 0.0)
    l = p_new
    acc = p_new * by_group(lambda g: kv_new[2 * g + 1:2 * g + 2])
    for blk, msk, s in zip(blocks, masks, scores):
        p = jnp.where(msk > 0.5, jnp.exp(s - m), 0.0)
        l = l + jnp.sum(p, axis=1, keepdims=True)
        pb = p.astype(bf16)
        acc = acc + by_group(lambda g: lax.dot_general(pb, blk[g, 1].astype(bf16), _NT,
                                                       preferred_element_type=f32))
    return acc / l


def _dsa_decode_scores_kernel(pt_ref, iq_ref, iw_ref, iknew_ref, *rest):
    pages, keys_ref = rest[:N_PAGES], rest[N_PAGES]
    iq = iq_ref[...]
    iqb = iq.astype(bf16)
    iw = iw_ref[...] * IDX_HEADS ** -0.5
    weigh = lambda dots, w: jnp.sum(w * jnp.maximum(dots * IDX_DIM ** -0.5, 0.0), axis=0, keepdims=True)
    for p in range(N_PAGES):
        dots = jnp.dot(iqb, pages[p][...].astype(bf16), preferred_element_type=f32)
        keys_ref[:, p * PAGE_SIZE:(p + 1) * PAGE_SIZE] = _sortable_key(weigh(dots, iw))
    new_score = weigh(jnp.sum(iq * iknew_ref[...], axis=1, keepdims=True), iw[:, :1])
    lane = lax.broadcasted_iota(jnp.int32, (1, LANES), 1)
    keys_ref[:, PAST_LEN:] = jnp.where(lane == 0, _sortable_key(jnp.broadcast_to(new_score, (1, LANES))), INT_MIN)


def _dsa_decode_attend_kernel(pt_ref, q_ref, sel_ref, kvnew_ref, *rest):
    pages, o_ref = rest[:N_PAGES], rest[N_PAGES]
    masks = [sel_ref[:, p * PAGE_SIZE:(p + 1) * PAGE_SIZE] for p in range(N_PAGES)]
    new_ok = sel_ref[:, PAST_LEN:PAST_LEN + 1] > 0.5
    is_g0 = lax.broadcasted_iota(jnp.int32, (N_HEADS, 1), 0) < Q_PER_KV
    o_ref[...] = _decode_attention(q_ref[...], pages, masks, kvnew_ref[...], new_ok, is_g0)


def dsa_decode(dq, dkv, iq, ik, iw, pools_kv, pools_idx, layer, page_table):
    n = dq.shape[0]
    assert min(DSA_TOPK, (PAST_LEN + 1) // 4) == DSA_TOPK and PAGE_SIZE == LANES
    pad_heads = ((0, 0), (0, STEP_ROWS - IDX_HEADS), (0, 0))
    iq8 = jnp.pad(iq.reshape(n, IDX_HEADS, IDX_DIM), pad_heads)
    iw8 = jnp.broadcast_to(jnp.pad(iw.reshape(n, IDX_HEADS, 1), pad_heads), (n, STEP_ROWS, LANES))
    keys = pl.pallas_call(
        _dsa_decode_scores_kernel,
        out_shape=jax.ShapeDtypeStruct((n, 1, KEY_PAD), jnp.int32),
        grid_spec=pltpu.PrefetchScalarGridSpec(
            num_scalar_prefetch=1, grid=(n,),
            in_specs=[_seq_spec(STEP_ROWS, IDX_DIM), _seq_spec(STEP_ROWS, LANES), _seq_spec(1, IDX_DIM)]
            + _layer_page_specs((IDX_DIM, PAGE_SIZE), layer),
            out_specs=_seq_spec(1, KEY_PAD)),
        compiler_params=pltpu.CompilerParams(
            dimension_semantics=("parallel",), vmem_limit_bytes=VMEM_LIMIT),
        name="dsa_decode_scores",
    )(page_table, iq8, iw8, ik.reshape(n, 1, IDX_DIM), *([pools_idx] * N_PAGES))
    sel = pl.pallas_call(
        functools.partial(_topk_rows_kernel, topk=DSA_TOPK, idx_bits=int(math.ceil(math.log2(KEY_PAD))),
                          n_valid=PAST_LEN + 1),
        out_shape=jax.ShapeDtypeStruct((n, KEY_PAD), f32),
        compiler_params=pltpu.CompilerParams(vmem_limit_bytes=VMEM_LIMIT),
        name="dsa_decode_topk",
    )(keys.reshape(n, KEY_PAD))
    o8 = pl.pallas_call(
        _dsa_decode_attend_kernel,
        out_shape=jax.ShapeDtypeStruct((n, N_HEADS, HEAD_DIM), f32),
        grid_spec=pltpu.PrefetchScalarGridSpec(
            num_scalar_prefetch=1, grid=(n,),
            in_specs=[_seq_spec(N_HEADS, HEAD_DIM), _seq_spec(1, KEY_PAD), _seq_spec(2 * KV_HEADS, HEAD_DIM)]
            + _layer_page_specs((KV_HEADS, 2, HEAD_DIM, PAGE_SIZE), layer),
            out_specs=_seq_spec(N_HEADS, HEAD_DIM)),
        compiler_params=pltpu.CompilerParams(
            dimension_semantics=("parallel",), vmem_limit_bytes=VMEM_LIMIT),
        name="dsa_decode_attend",
    )(page_table, dq.reshape(n, N_HEADS, HEAD_DIM), sel.reshape(n, 1, KEY_PAD),
      dkv.reshape(n, 2 * KV_HEADS, HEAD_DIM), *([pools_kv] * N_PAGES))
    return o8.reshape(n, GROUP_WIDTH)


def _nsa_decode_kernel(pt_ref, q_ref, gate_ref, cmp_ref, slcnew_ref, win_ref, winnew_ref, *rest):
    pages, o_ref = rest[:N_PAGES], rest[N_PAGES]
    q8 = q_ref[...]
    qb = q8.astype(bf16)
    lane = lax.broadcasted_iota(jnp.int32, (N_HEADS, CMP_LANES), 1)
    head = lax.broadcasted_iota(jnp.int32, (N_HEADS, 1), 0)
    is_g0 = head < Q_PER_KV
    by_group = lambda fn: jnp.where(is_g0, fn(0), fn(1))
    q_pos = jnp.full((N_HEADS, 1), PAST_LEN, jnp.int32)
    blocks = cmp_ref.shape[1]
    s = by_group(lambda g: lax.dot_general(qb, cmp_ref[2 * g].astype(bf16), _NT,
                                           preferred_element_type=f32)) * ATT_SCALE
    lane_c = lane[:, :blocks]
    s = jnp.where((lane_c + 1) * CMP_BLOCK - 1 <= q_pos, s, -jnp.inf)
    m = jnp.max(s, axis=1, keepdims=True)
    m = jnp.where(m == -jnp.inf, 0.0, m)
    p = jnp.exp(s - m)
    p = p / jnp.maximum(jnp.sum(p, axis=1, keepdims=True), F32_TINY)
    pb = p.astype(bf16)
    o_cmp = by_group(lambda g: jnp.dot(pb, cmp_ref[2 * g + 1].astype(bf16), preferred_element_type=f32))
    imp = jnp.where(is_g0, jnp.sum(p[:Q_PER_KV], axis=0, keepdims=True),
                    jnp.sum(p[Q_PER_KV:], axis=0, keepdims=True))
    imp = jnp.concatenate([imp, jnp.zeros((N_HEADS, CMP_LANES - blocks), f32)], axis=1)
    sel_blk = _select_blocks(imp, q_pos, lane)
    ratio = SLC_BLOCK // CMP_BLOCK
    per_page = PAGE_SIZE // SLC_BLOCK
    masks = []
    for pg in range(N_PAGES):
        msk = sel_blk[:, ratio * per_page * pg:ratio * per_page * pg + 1]
        for j in range(1, per_page):
            b = per_page * pg + j
            msk = jnp.where(lane < j * SLC_BLOCK, msk, sel_blk[:, ratio * b:ratio * b + 1])
        masks.append(msk)
    o_slc = _decode_attention(q8, pages, masks, slcnew_ref[...], None, is_g0)
    wbuf = win_ref.shape[-1]
    slot = lax.broadcasted_iota(jnp.int32, (1, wbuf), 1)
    wmask = jnp.where(PAST_LEN - wbuf + slot > PAST_LEN - WINDOW, 1.0, 0.0)
    o_win = _decode_attention(q8, [win_ref], [wmask], winnew_ref[...], None, is_g0)
    gate = jax.nn.sigmoid(gate_ref[...])
    o_ref[...] = (gate[0][:, :HEAD_DIM] * o_cmp + gate[1][:, :HEAD_DIM] * o_slc
                  + gate[2][:, :HEAD_DIM] * o_win)


def nsa_decode(nq, ng, skv, wkv, pools_cmp, pools_slc, wins, layer, page_table, cmp_weights):
    n = nq.shape[0]
    blocks = N_PAGES * BLOCKS_PER_PAGE
    wbuf = wins.shape[-1]
    assert (PAST_LEN + 1) // CMP_BLOCK == blocks and blocks <= CMP_LANES
    assert PAST_LEN // SLC_BLOCK + 1 <= CMP_LANES // 2 and PAGE_SIZE == LANES
    cmp_rows = nsa_compress_paged(pools_cmp, layer, page_table, *cmp_weights)
    cmp_rows = cmp_rows.reshape(n, blocks, 2 * KV_HEADS, HEAD_DIM).transpose(0, 2, 1, 3)
    gates = jnp.broadcast_to(ng.reshape(n, N_HEADS, 3).transpose(0, 2, 1)[..., None], (n, 3, N_HEADS, LANES))
    kv_tile = (KV_HEADS, 2, HEAD_DIM)
    o8 = pl.pallas_call(
        _nsa_decode_kernel,
        out_shape=jax.ShapeDtypeStruct((n, N_HEADS, HEAD_DIM), f32),
        grid_spec=pltpu.PrefetchScalarGridSpec(
            num_scalar_prefetch=1, grid=(n,),
            in_specs=[_seq_spec(N_HEADS, HEAD_DIM), _seq_spec(3, N_HEADS, LANES),
                      _seq_spec(2 * KV_HEADS, blocks, HEAD_DIM), _seq_spec(2 * KV_HEADS, HEAD_DIM),
                      _layer_seq_spec(kv_tile + (wbuf,), layer), _seq_spec(2 * KV_HEADS, HEAD_DIM)]
            + _layer_page_specs(kv_tile + (PAGE_SIZE,), layer),
            out_specs=_seq_spec(N_HEADS, HEAD_DIM)),
        compiler_params=pltpu.CompilerParams(
            dimension_semantics=("parallel",), vmem_limit_bytes=VMEM_LIMIT),
        name="nsa_decode",
    )(page_table, nq.reshape(n, N_HEADS, HEAD_DIM), gates, cmp_rows, skv.reshape(n, 2 * KV_HEADS, HEAD_DIM),
      wins, wkv.reshape(n, 2 * KV_HEADS, HEAD_DIM), *([pools_slc] * N_PAGES))
    return o8.reshape(n, GROUP_WIDTH)


def mix_prompt(cols, n, t, w, sw, cmp_weights):
    (dq, dkv, iq, ik, iw, u, nq, ckv, skv, wkv, ng, gq, gk, gv, ga, gr) = [
        c.reshape(n, t, c.shape[-1]) for c in cols]
    kv_shape = (n, t, KV_HEADS, 2, HEAD_DIM)
    o_dsa = dsa_prompt(dq, dkv, iq, ik, iw)
    o_s5, h_s5 = s5_prompt(u, jnp.zeros((n, 2, S5_WIDTH), f32), sw)
    o_nsa = nsa_prompt(nq, ng, ckv, skv, wkv, cmp_weights)
    o_gla, s_gla = gla_prompt(gq, gk, gv, ga, gr, w['gla_w_gate'], w['gla_b_gate'], w['gla_norm_g'])
    win_state = wkv[:, -min(WINDOW, t):].reshape((n, min(WINDOW, t)) + kv_shape[2:])
    mixers = [o.reshape(n * t, GROUP_WIDTH) for o in (o_dsa, o_s5, o_nsa, o_gla)]
    return mixers, (dkv.reshape(kv_shape), ik, ckv.reshape(kv_shape), skv.reshape(kv_shape), win_state,
                    h_s5.reshape(n, 2, S5_GROUPS, S5_STATE), s_gla)


def mix_sample(cols, w, sw, cmp_weights, layer, pools, win_buf, h_s5, s_gla, page_table):
    (dq, dkv, iq, ik, iw, u, nq, ckv, skv, wkv, ng, gq, gk, gv, ga, gr) = cols
    n, t = dq.shape[0], 1
    kv_shape = (n, t, KV_HEADS, 2, HEAD_DIM)
    o_dsa = dsa_decode(dq, dkv, iq, ik, iw, pools['dsa_kv'], pools['dsa_idx'], layer, page_table)
    o_s5, h_new = s5_step(u, h_s5.reshape(n, 2 * S5_WIDTH), sw)
    o_nsa = nsa_decode(nq, ng, skv, wkv, pools['nsa_cmp'], pools['nsa_slc'], pools['nsa_win'], layer,
                       page_table, cmp_weights)
    o_gla, s_new = gla_step(gq, gk, gv, ga, gr, w['gla_w_gate'], w['gla_b_gate'], w['gla_norm_g'], s_gla)
    win_new = jnp.concatenate([win_buf[:, 1:], wkv.reshape(kv_shape)], axis=1)
    return [o_dsa, o_s5, o_nsa, o_gla], (
        dkv.reshape(kv_shape), ik.reshape(n, t, IDX_DIM), ckv.reshape(kv_shape), skv.reshape(kv_shape),
        win_new, h_new.reshape(n, 2, S5_GROUPS, S5_STATE), s_new)


def kernel(x_prompt, x_sample, cache_dsa_kv, cache_dsa_idx, cache_nsa_cmp, cache_nsa_slc, cache_nsa_win, state_s5, state_gla, page_table, w_in, s5_a_re, s5_a_im, s5_b_re, s5_b_im, s5_c_re, s5_c_im, s5_d, s5_log_dt, s5_w_glu, s5_b_glu, nsa_cmp_pe, nsa_cmp_w1, nsa_cmp_w2, gla_w_gate, gla_b_gate, gla_norm_g, w_out, ln1_g, ln1_b, ffn_w_gate, ffn_w_up, ffn_w_down, ln2_g, ln2_b):
    np_, tp_ = x_prompt.shape[:2]
    ns_, ts_ = x_sample.shape[:2]
    assert ts_ == 1
    w_in_pad = pad_in_projection(w_in)
    w_out_b = w_out.astype(bf16)
    wg_b, wu_b, wd_b = ffn_w_gate.astype(bf16), ffn_w_up.astype(bf16), ffn_w_down.astype(bf16)
    hp = x_prompt.reshape(np_ * tp_, D_MODEL)
    hs = x_sample.reshape(ns_ * ts_, D_MODEL)
    outs_p = [[] for _ in range(7)]
    outs_s = [[] for _ in range(7)]
    pools = {'dsa_kv': _tokens_minor(cache_dsa_kv), 'dsa_idx': _tokens_minor(cache_dsa_idx),
             'nsa_cmp': cache_nsa_cmp.reshape(DEPTH, -1, BLOCKS_PER_PAGE, CMP_ROW),
             'nsa_slc': _tokens_minor(cache_nsa_slc),
             'nsa_win': _tokens_minor(cache_nsa_win)}
    for l in range(DEPTH):
        w = {'a_re': s5_a_re[l], 'a_im': s5_a_im[l], 'b_re': s5_b_re[l], 'b_im': s5_b_im[l],
             'c_re': s5_c_re[l], 'c_im': s5_c_im[l], 'd': s5_d[l], 'log_dt': s5_log_dt[l],
             'w_glu': s5_w_glu[l], 'b_glu': s5_b_glu[l],
             'gla_w_gate': gla_w_gate[l], 'gla_b_gate': gla_b_gate[l], 'gla_norm_g': gla_norm_g[l]}
        sw = s5_discretize(w)
        cmp_weights = nsa_compress_weights(nsa_cmp_pe[l], nsa_cmp_w1[l], nsa_cmp_w2[l])
        mixers_p, st_p = mix_prompt(in_projection(hp, w_in_pad, l), np_, tp_, w, sw, cmp_weights)
        mixers_s, st_s = mix_sample(in_projection(hs, w_in_pad, l), w, sw, cmp_weights, l, pools, cache_nsa_win[l],
                                    state_s5[l], state_gla[l], page_table)
        hp = outproj_ln(hp, mixers_p, w_out_b, l, ln1_g[l], ln1_b[l])
        hs = outproj_ln(hs, mixers_s, w_out_b, l, ln1_g[l], ln1_b[l])
        hp = ffn_ln(hp, wg_b, wu_b, wd_b, l, ln2_g[l], ln2_b[l])
        hs = ffn_ln(hs, wg_b, wu_b, wd_b, l, ln2_g[l], ln2_b[l])
        for lst, st in zip(outs_p, st_p):
            lst.append(st)
        for lst, st in zip(outs_s, st_s):
            lst.append(st)
    dsa_kv_p, dsa_idx_p, nsa_cmp_p, nsa_slc_p, nsa_win_p, s5_p, gla_p = [jnp.stack(a) for a in outs_p]
    dsa_kv_s, dsa_idx_s, nsa_cmp_s, nsa_slc_s, nsa_win_s, s5_s, gla_s = [jnp.stack(a) for a in outs_s]
    return (hp.reshape(np_, tp_, D_MODEL), hs.reshape(ns_, ts_, D_MODEL),
            dsa_kv_p, dsa_kv_s, dsa_idx_p, dsa_idx_s, nsa_cmp_p, nsa_cmp_s,
            nsa_slc_p, nsa_slc_s, nsa_win_p, nsa_win_s, s5_p, s5_s, gla_p, gla_s)
```

```python
import functools
import math

import jax
import jax.numpy as jnp
from jax import lax
import numpy as np
from jax.experimental import pallas as pl
from jax.experimental.pallas import tpu as pltpu

D_MODEL = 2048
DEPTH = 2
PAST_LEN = 2048
PAGE_SIZE = 128
GROUP_WIDTH = 512
HEAD_DIM = 64
N_HEADS = 8
KV_HEADS = 2
Q_PER_KV = 4
KV_ROW = 256
IDX_HEADS = 4
IDX_DIM = 64
DSA_TOPK = 256
S5_GROUPS = 32
S5_STATE = 64
S5_WIDTH = S5_GROUPS * S5_STATE
CMP_BLOCK = 32
CMP_HIDDEN = 128
SLC_BLOCK = 64
SLC_TOPN = 16
WINDOW = 512
QBLOCK = 128
GLA_HEADS = 4
GLA_DK = 64
GLA_DV = 128
GLA_LOWRANK = 16
GLA_TAU = 16.0
GLA_CHUNK = 64
DEEPNORM_ALPHA = (2 * DEPTH) ** 0.25
LN_EPS = 1e-5

IN_SIZES = (512, 256, 256, 64, 4, 512, 512, 256, 256, 256, 24, 256, 256, 512, 16, 512)
D_IN = sum(IN_SIZES)

VMEM_LIMIT = 56 * 1024 * 1024
LANES = 128
KEY_CHUNK = 512
CMP_LANES = 128
S5_TIME_CHUNK = 256
ATT_SCALE = HEAD_DIM ** -0.5
MASKED = -1e30
INT_MIN = -2 ** 31
F32_TINY = float(np.finfo(np.float32).tiny)

bf16 = jnp.bfloat16
f32 = jnp.float32


IN_PAD_SIZES = tuple(-(-s // LANES) * LANES for s in IN_SIZES)
IN_PAD_OFFSETS = tuple(int(o) for o in np.cumsum((0,) + IN_PAD_SIZES[:-1]))


def pad_in_projection(w_in):
    parts, off = [], 0
    for size, size_pad in zip(IN_SIZES, IN_PAD_SIZES):
        parts.append(jnp.pad(w_in[:, :, off:off + size], ((0, 0), (0, 0), (0, size_pad - size))))
        off += size
    return jnp.concatenate(parts, axis=-1).astype(bf16)


def _in_projection_kernel(x_ref, w_ref, *o_refs):
    xb = x_ref[...].astype(bf16)
    for o_ref, off, size, size_pad in zip(o_refs, IN_PAD_OFFSETS, IN_SIZES, IN_PAD_SIZES):
        y = jnp.dot(xb, w_ref[:, off:off + size_pad], preferred_element_type=f32)
        o_ref[...] = y[:, :size]


def in_projection(x, w_pad, layer):
    m, k = x.shape
    tm = min(256, m)
    return pl.pallas_call(
        _in_projection_kernel,
        out_shape=tuple(jax.ShapeDtypeStruct((m, s), f32) for s in IN_SIZES),
        grid=(m // tm,),
        in_specs=[pl.BlockSpec((tm, k), lambda i: (i, 0)),
                  pl.BlockSpec((None, k, w_pad.shape[2]), lambda i: (layer, 0, 0),
                               pipeline_mode=pl.Buffered(1))],
        out_specs=tuple(pl.BlockSpec((tm, s), lambda i: (i, 0)) for s in IN_SIZES),
        compiler_params=pltpu.CompilerParams(
            dimension_semantics=("parallel",), vmem_limit_bytes=VMEM_LIMIT),
        name="in_projection",
    )(x, w_pad)


def _layer_norm_rows(z, g, b):
    mu = jnp.mean(z, axis=-1, keepdims=True)
    zc = z - mu
    var = jnp.mean(zc * zc, axis=-1, keepdims=True)
    return zc * lax.rsqrt(var + LN_EPS) * g + b


def _outproj_ln_kernel(x_ref, m0_ref, m1_ref, m2_ref, m3_ref, w_ref, g_ref, b_ref, o_ref):
    y = DEEPNORM_ALPHA * x_ref[...]
    for j, m_ref in enumerate((m0_ref, m1_ref, m2_ref, m3_ref)):
        y = y + jnp.dot(m_ref[...].astype(bf16), w_ref[j * GROUP_WIDTH:(j + 1) * GROUP_WIDTH, :],
                        preferred_element_type=f32)
    o_ref[...] = _layer_norm_rows(y, g_ref[...], b_ref[...])


def outproj_ln(x, mixers, w_bf16, layer, g, b):
    m, d = x.shape
    tm = min(256, m)
    mix_spec = pl.BlockSpec((tm, GROUP_WIDTH), lambda i: (i, 0))
    return pl.pallas_call(
        _outproj_ln_kernel,
        out_shape=jax.ShapeDtypeStruct((m, d), f32),
        grid=(m // tm,),
        in_specs=[pl.BlockSpec((tm, d), lambda i: (i, 0)),
                  mix_spec, mix_spec, mix_spec, mix_spec,
                  pl.BlockSpec((None, d, d), lambda i: (layer, 0, 0)),
                  pl.BlockSpec((1, d), lambda i: (0, 0)),
                  pl.BlockSpec((1, d), lambda i: (0, 0))],
        out_specs=pl.BlockSpec((tm, d), lambda i: (i, 0)),
        compiler_params=pltpu.CompilerParams(
            dimension_semantics=("parallel",), vmem_limit_bytes=VMEM_LIMIT),
        name="outproj_ln",
    )(x, *mixers, w_bf16, g.reshape(1, d), b.reshape(1, d))


def _ffn_ln_kernel(h_ref, wg_ref, wu_ref, wd_ref, g_ref, b_ref, o_ref, acc_ref):
    f = pl.program_id(1)

    @pl.when(f == 0)
    def _():
        acc_ref[...] = jnp.zeros_like(acc_ref)

    hb = h_ref[...].astype(bf16)
    a = jnp.dot(hb, wg_ref[...], preferred_element_type=f32)
    u = jnp.dot(hb, wu_ref[...], preferred_element_type=f32)
    act = (a * jax.nn.sigmoid(a) * u).astype(bf16)
    acc_ref[...] += jnp.dot(act, wd_ref[...], preferred_element_type=f32)

    @pl.when(f == pl.num_programs(1) - 1)
    def _():
        z = DEEPNORM_ALPHA * h_ref[...] + acc_ref[...]
        o_ref[...] = _layer_norm_rows(z, g_ref[...], b_ref[...])


def ffn_ln(h, wg, wu, wd, layer, g, b, tf=512):
    m, d = h.shape
    fh = wg.shape[2]
    tm = min(512, m)
    return pl.pallas_call(
        _ffn_ln_kernel,
        out_shape=jax.ShapeDtypeStruct((m, d), f32),
        grid=(m // tm, fh // tf),
        in_specs=[pl.BlockSpec((tm, d), lambda i, f: (i, 0)),
                  pl.BlockSpec((None, d, tf), lambda i, f: (layer, 0, f)),
                  pl.BlockSpec((None, d, tf), lambda i, f: (layer, 0, f)),
                  pl.BlockSpec((None, tf, d), lambda i, f: (layer, f, 0)),
                  pl.BlockSpec((1, d), lambda i, f: (0, 0)),
                  pl.BlockSpec((1, d), lambda i, f: (0, 0))],
        out_specs=pl.BlockSpec((tm, d), lambda i, f: (i, 0)),
        scratch_shapes=[pltpu.VMEM((tm, d), f32)],
        compiler_params=pltpu.CompilerParams(
            dimension_semantics=("parallel", "arbitrary"), vmem_limit_bytes=VMEM_LIMIT),
        name="ffn_ln",
    )(h, wg, wu, wd, g.reshape(1, d), b.reshape(1, d))


def _tile_rows(x, reps):
    return jnp.concatenate([x] * reps, axis=0)


def _scaled_queries(q):
    assert math.log2(HEAD_DIM) % 2 == 0
    return (q * ATT_SCALE).astype(bf16)


def _masked_flash(qg, kt_at, v_at, c_lo, c_hi, bias_at):
    rows = qg.shape[0]

    def body(c, carry):
        m, l, acc = carry
        s = jnp.dot(qg, kt_at(c), preferred_element_type=f32)
        bias = bias_at(c)
        s = jnp.concatenate([s[r * QBLOCK:(r + 1) * QBLOCK] + bias for r in range(rows // QBLOCK)], axis=0)
        m_new = jnp.maximum(m, jnp.max(s, axis=1, keepdims=True))
        p = jnp.exp(s - m_new)
        alpha = jnp.exp(m - m_new)
        l = alpha * l + jnp.sum(p, axis=1, keepdims=True)
        acc = alpha * acc + jnp.dot(p.astype(bf16), v_at(c), preferred_element_type=f32)
        return m_new, l, acc

    init = (jnp.full((rows, 1), MASKED, f32), jnp.zeros((rows, 1), f32),
            jnp.zeros((rows, HEAD_DIM), f32))
    _, l, acc = lax.fori_loop(c_lo, c_hi, body, init)
    return acc / l


def _sortable_key(x):
    bits = pltpu.bitcast(x, jnp.int32)
    return bits ^ (jnp.right_shift(bits, 31) & jnp.int32(0x7FFFFFFF))


def _topk_threshold(count, topk):
    topk = float(topk)
    cnt_nonneg = count(lambda k, i: k >= 0)
    thr0 = jnp.where(cnt_nonneg >= topk, 0, INT_MIN).astype(jnp.int32)

    def bit_body(b, thr):
        cand = thr + jnp.left_shift(jnp.int32(1), 30 - b)
        return jnp.where(count(lambda k, i: k >= cand) >= topk, cand, thr)

    thr = lax.fori_loop(0, 31, bit_body, thr0)
    need = topk - count(lambda k, i: k > thr)
    return thr, need


def _tie_index_cut(count, rows, thr, need, idx_bits):
    def cut_body(b, cut):
        cand = cut + jnp.left_shift(jnp.int32(1), idx_bits - 1 - b)
        return jnp.where(count(lambda k, i: (k == thr) & (i < cand)) < need, cand, cut)

    return lax.fori_loop(0, idx_bits, cut_body, jnp.zeros((rows, 1), jnp.int32))


def _select_blocks(imp, q_pos, lane):
    ratio = SLC_BLOCK // CMP_BLOCK
    on_block_lane = (lane & (ratio - 1)) == 0
    blk = jnp.right_shift(lane, 1)
    cur = jnp.right_shift(q_pos, 6)
    future = blk > cur
    forced = (blk == 0) | (blk == cur) | (blk == cur - 1)
    imp = imp + pltpu.roll(imp, CMP_LANES - 1, 1)
    score = jnp.where(future, -jnp.inf, jnp.where(forced, jnp.inf, imp))
    score = jnp.where(on_block_lane, score, -jnp.inf)
    if imp.shape[0] == CMP_LANES:
        score_t = score.T
        blk_t = lax.broadcasted_iota(jnp.int32, score_t.shape, 0)
        rank_t = jnp.zeros(score_t.shape, f32)
        for b in range(CMP_LANES // ratio):
            other = score_t[ratio * b:ratio * b + 1, :]
            ahead = (other > score_t) | ((other == score_t) & (ratio * b < blk_t))
            rank_t = rank_t + jnp.where(ahead, 1.0, 0.0)
        rank = rank_t.T
    else:
        rank = jnp.zeros(imp.shape, f32)
        for b in range(CMP_LANES // ratio):
            col = score[:, ratio * b:ratio * b + 1]
            ahead = (col > score) | ((col == score) & (ratio * b < lane))
            rank = rank + jnp.where(ahead, 1.0, 0.0)
    return jnp.where((rank < float(SLC_TOPN)) & on_block_lane, 1.0, 0.0)


def _dsa_prompt_kernel(iq_ref, iw_ref, ikt_ref, q_ref, kt_ref, v_ref, tri_ref, o_ref, keys_ref, sel_ref):
    i = pl.program_id(1)
    nch = (i * QBLOCK + QBLOCK + KEY_CHUNK - 1) // KEY_CHUNK
    q_pos = i * QBLOCK + lax.broadcasted_iota(jnp.int32, (QBLOCK, 1), 0)
    lane = lax.broadcasted_iota(jnp.int32, (QBLOCK, KEY_CHUNK), 1)

    iq = iq_ref[...]
    iq_stack = jnp.concatenate(
        [iq[:, h * IDX_DIM:(h + 1) * IDX_DIM] for h in range(IDX_HEADS)], axis=0).astype(bf16)
    iw = iw_ref[...] * IDX_HEADS ** -0.5

    def score_body(c, carry):
        dots = jnp.dot(iq_stack, ikt_ref[c], preferred_element_type=f32) * IDX_DIM ** -0.5
        acc = jnp.zeros((QBLOCK, KEY_CHUNK), f32)
        for h in range(IDX_HEADS):
            acc = acc + iw[:, h:h + 1] * jnp.maximum(dots[h * QBLOCK:(h + 1) * QBLOCK], 0.0)
        causal = c * KEY_CHUNK + lane <= q_pos
        keys_ref[c] = jnp.where(causal, _sortable_key(acc), INT_MIN)
        return carry

    lax.fori_loop(0, nch, score_body, 0)

    def count(pred):
        def body(c, acc):
            hit = jnp.where(pred(keys_ref[c], c * KEY_CHUNK + lane), 1.0, 0.0)
            part = hit[:, :LANES]
            for j in range(1, KEY_CHUNK // LANES):
                part = part + hit[:, j * LANES:(j + 1) * LANES]
            return acc + part
        acc = lax.fori_loop(0, nch, body, jnp.zeros((QBLOCK, LANES), f32))
        return jnp.sum(acc, axis=1, keepdims=True)

    thr, need = _topk_threshold(count, DSA_TOPK)

    def sel_body(c, ties_before):
        k = keys_ref[c]
        tie = k == thr
        tie_rank = ties_before + jnp.dot(jnp.where(tie, 1.0, 0.0).astype(bf16), tri_ref[...],
                                         preferred_element_type=f32)
        chosen = ((k > thr) | (tie & (tie_rank <= need))) & (c * KEY_CHUNK + lane <= q_pos)
        sel_ref[c] = jnp.where(chosen, 0.0, MASKED)
        return tie_rank[:, KEY_CHUNK - 1:KEY_CHUNK]

    lax.fori_loop(0, nch, sel_body, jnp.zeros((QBLOCK, 1), f32))

    for g in range(KV_HEADS):
        o = _masked_flash(_scaled_queries(q_ref[g]), lambda c: kt_ref[g, c], lambda c: v_ref[g, c],
                          0, nch, lambda c: sel_ref[c])
        for r in range(Q_PER_KV):
            h = g * Q_PER_KV + r
            o_ref[:, h * HEAD_DIM:(h + 1) * HEAD_DIM] = o[r * QBLOCK:(r + 1) * QBLOCK]


def _stack_query_heads(q, n, t):
    q = q.reshape(n, t // QBLOCK, QBLOCK, KV_HEADS, Q_PER_KV, HEAD_DIM)
    q = q.transpose(0, 3, 1, 4, 2, 5)
    return q.reshape(n, KV_HEADS, t // QBLOCK, Q_PER_KV * QBLOCK, HEAD_DIM)


def _chunked_kv(kv, n, t, chunk):
    kv = kv.reshape(n, t // chunk, chunk, KV_HEADS, 2, HEAD_DIM).astype(bf16)
    kt = kv[:, :, :, :, 0, :].transpose(0, 3, 1, 4, 2)
    v = kv[:, :, :, :, 1, :].transpose(0, 3, 1, 2, 4)
    return kt, v


def dsa_prompt(dq, dkv, iq, ik, iw):
    n, t = dq.shape[:2]
    nchunks = t // KEY_CHUNK
    assert t % KEY_CHUNK == 0 and min(DSA_TOPK, t // 4) == DSA_TOPK
    ikt = ik.reshape(n, nchunks, KEY_CHUNK, IDX_DIM).transpose(0, 1, 3, 2).astype(bf16)
    qs = _stack_query_heads(dq, n, t)
    kt, v = _chunked_kv(dkv, n, t, KEY_CHUNK)
    at = jnp.arange(KEY_CHUNK)
    tri = (at[:, None] <= at[None, :]).astype(bf16)
    return pl.pallas_call(
        _dsa_prompt_kernel,
        out_shape=jax.ShapeDtypeStruct((n, t, GROUP_WIDTH), f32),
        grid=(n, t // QBLOCK),
        in_specs=[
            pl.BlockSpec((None, QBLOCK, IDX_HEADS * IDX_DIM), lambda b, i: (b, i, 0)),
            pl.BlockSpec((None, QBLOCK, IDX_HEADS), lambda b, i: (b, i, 0)),
            pl.BlockSpec((None, nchunks, IDX_DIM, KEY_CHUNK), lambda b, i: (b, 0, 0, 0)),
            pl.BlockSpec((None, KV_HEADS, None, Q_PER_KV * QBLOCK, HEAD_DIM), lambda b, i: (b, 0, i, 0, 0)),
            pl.BlockSpec((None, KV_HEADS, nchunks, HEAD_DIM, KEY_CHUNK), lambda b, i: (b, 0, 0, 0, 0)),
            pl.BlockSpec((None, KV_HEADS, nchunks, KEY_CHUNK, HEAD_DIM), lambda b, i: (b, 0, 0, 0, 0)),
            pl.BlockSpec((KEY_CHUNK, KEY_CHUNK), lambda b, i: (0, 0)),
        ],
        out_specs=pl.BlockSpec((None, QBLOCK, GROUP_WIDTH), lambda b, i: (b, i, 0)),
        scratch_shapes=[pltpu.VMEM((nchunks, QBLOCK, KEY_CHUNK), jnp.int32),
                        pltpu.VMEM((nchunks, QBLOCK, KEY_CHUNK), f32)],
        compiler_params=pltpu.CompilerParams(
            dimension_semantics=("parallel", "arbitrary"), vmem_limit_bytes=VMEM_LIMIT),
        name="dsa_prompt",
    )(iq, iw, ikt, qs, kt, v, tri)


def _nsa_compress_kernel(x_ref, pe_ref, w1_ref, w2_ref, o_ref, hid_ref):
    k = pl.program_id(0)

    @pl.when(k == 0)
    def _():
        hid_ref[...] = jnp.zeros_like(hid_ref)

    hid_ref[...] += jnp.dot((x_ref[...] + pe_ref[...]).astype(bf16), w1_ref[...],
                            preferred_element_type=f32)

    @pl.when(k == pl.num_programs(0) - 1)
    def _():
        hid = jax.nn.gelu(hid_ref[...]).astype(bf16)
        o_ref[...] = jnp.dot(hid, w2_ref[...], preferred_element_type=f32)


def nsa_compress_weights(cmp_pe, cmp_w1, cmp_w2):
    w1 = cmp_w1.reshape(2, CMP_BLOCK, HEAD_DIM, CMP_HIDDEN)
    eye_g = jnp.eye(KV_HEADS, dtype=cmp_w1.dtype)
    eye_j = jnp.eye(2, dtype=cmp_w1.dtype)
    w1_big = jnp.einsum('jtde,gh,jk->tgjdhke', w1, eye_g, eye_j)
    w1_big = w1_big.reshape(CMP_BLOCK * KV_ROW, KV_HEADS * 2 * CMP_HIDDEN).astype(bf16)
    w2_big = jnp.einsum('jed,gh,jk->gjehkd', cmp_w2, eye_g, eye_j)
    w2_big = w2_big.reshape(KV_HEADS * 2 * CMP_HIDDEN, KV_ROW).astype(bf16)
    pe_row = jnp.broadcast_to(cmp_pe[:, None], (CMP_BLOCK, KV_HEADS, 2, HEAD_DIM))
    return pe_row.reshape(1, CMP_BLOCK * KV_ROW), w1_big, w2_big


def nsa_compress_rows(x, pe_row, w1_big, w2_big, tk=2048):
    m, kdim = x.shape
    hw = w1_big.shape[1]
    return pl.pallas_call(
        _nsa_compress_kernel,
        out_shape=jax.ShapeDtypeStruct((m, KV_ROW), f32),
        grid=(kdim // tk,),
        in_specs=[pl.BlockSpec((m, tk), lambda k: (0, k)),
                  pl.BlockSpec((1, tk), lambda k: (0, k)),
                  pl.BlockSpec((tk, hw), lambda k: (k, 0)),
                  pl.BlockSpec((hw, KV_ROW), lambda k: (0, 0))],
        out_specs=pl.BlockSpec((m, KV_ROW), lambda k: (0, 0)),
        scratch_shapes=[pltpu.VMEM((m, hw), f32)],
        compiler_params=pltpu.CompilerParams(
            dimension_semantics=("arbitrary",), vmem_limit_bytes=VMEM_LIMIT),
        name="nsa_compress",
    )(x, pe_row, w1_big, w2_big)


def _nsa_prompt_kernel(gate_ref, q_ref, ckt_ref, cv_ref, skt_ref, sv_ref, wkt_ref, wv_ref, o_ref):
    i = pl.program_id(1)
    nch = (i * QBLOCK + QBLOCK + KEY_CHUNK - 1) // KEY_CHUNK
    q_pos = i * QBLOCK + lax.broadcasted_iota(jnp.int32, (QBLOCK, 1), 0)
    lane = lax.broadcasted_iota(jnp.int32, (QBLOCK, CMP_LANES), 1)
    lane_k = lax.broadcasted_iota(jnp.int32, (QBLOCK, KEY_CHUNK), 1)
    expand_row = lax.broadcasted_iota(jnp.int32, (CMP_LANES, KEY_CHUNK), 0)
    expand_col = lax.broadcasted_iota(jnp.int32, (CMP_LANES, KEY_CHUNK), 1)
    gate = jax.nn.sigmoid(gate_ref[...])
    ratio = SLC_BLOCK // CMP_BLOCK
    c_vis = _tile_rows(jnp.where((lane + 1) * CMP_BLOCK - 1 <= q_pos, 1.0, 0.0), Q_PER_KV) > 0.5
    w_tiles = WINDOW // QBLOCK + 1
    w_lo = jnp.maximum(i - WINDOW // QBLOCK, 0)
    w_idx = w_lo * QBLOCK + lax.broadcasted_iota(jnp.int32, (QBLOCK, w_tiles * QBLOCK), 1)
    w_bias = _tile_rows(jnp.where((w_idx <= q_pos) & (w_idx > q_pos - WINDOW), 0.0, MASKED), Q_PER_KV)

    for g in range(KV_HEADS):
        qg = _scaled_queries(q_ref[g])
        s = jnp.dot(qg, ckt_ref[g], preferred_element_type=f32)
        s = jnp.where(c_vis, s, -jnp.inf)
        m = jnp.max(s, axis=1, keepdims=True)
        m = jnp.where(m == -jnp.inf, 0.0, m)
        p = jnp.exp(s - m)
        p = p / jnp.maximum(jnp.sum(p, axis=1, keepdims=True), F32_TINY)
        o_cmp = jnp.dot(p.astype(bf16), cv_ref[g], preferred_element_type=f32)
        imp = p[0:QBLOCK]
        for r in range(1, Q_PER_KV):
            imp = imp + p[r * QBLOCK:(r + 1) * QBLOCK]
        sel_blk = _select_blocks(imp, q_pos, lane).astype(bf16)

        def slc_mask(c):
            kidx = c * KEY_CHUNK + lane_k
            expand = jnp.where(
                expand_row == ratio * jnp.right_shift(c * KEY_CHUNK + expand_col, 6), 1.0, 0.0).astype(bf16)
            picked = jnp.dot(sel_blk, expand, preferred_element_type=f32)
            return jnp.where((picked > 0.5) & (kidx <= q_pos), 0.0, MASKED)

        o_slc = _masked_flash(qg, lambda c: skt_ref[g, c], lambda c: sv_ref[g, c], 0, nch, slc_mask)

        w_kt = jnp.concatenate([wkt_ref[g, w_lo + j] for j in range(w_tiles)], axis=1)
        w_v = jnp.concatenate([wv_ref[g, w_lo + j] for j in range(w_tiles)], axis=0)
        s = jnp.dot(qg, w_kt, preferred_element_type=f32) + w_bias
        pw = jnp.exp(s - jnp.max(s, axis=1, keepdims=True))
        o_win = (jnp.dot(pw.astype(bf16), w_v, preferred_element_type=f32)
                 / jnp.sum(pw, axis=1, keepdims=True))
        for r in range(Q_PER_KV):
            h = g * Q_PER_KV + r
            rows = slice(r * QBLOCK, (r + 1) * QBLOCK)
            o_ref[:, h * HEAD_DIM:(h + 1) * HEAD_DIM] = (
                gate[:, 3 * h:3 * h + 1] * o_cmp[rows]
                + gate[:, 3 * h + 1:3 * h + 2] * o_slc[rows]
                + gate[:, 3 * h + 2:3 * h + 3] * o_win[rows])


def nsa_prompt(nq, ng, ckv, skv, wkv, cmp_weights):
    n, t = nq.shape[:2]
    nc = t // CMP_BLOCK
    assert nc <= CMP_LANES and t % KEY_CHUNK == 0 and SLC_BLOCK == 64 and SLC_BLOCK // CMP_BLOCK == 2
    assert t >= WINDOW + QBLOCK and WINDOW % QBLOCK == 0
    cmp_rows = nsa_compress_rows(ckv.reshape(n * nc, CMP_BLOCK * KV_ROW), *cmp_weights)
    cmp_rows = cmp_rows.reshape(n, nc, KV_HEADS, 2, HEAD_DIM)
    cmp_rows = jnp.pad(cmp_rows, ((0, 0), (0, CMP_LANES - nc), (0, 0), (0, 0), (0, 0))).astype(bf16)
    ckt = cmp_rows[:, :, :, 0, :].transpose(0, 2, 3, 1)
    cv = cmp_rows[:, :, :, 1, :].transpose(0, 2, 1, 3)
    qs = _stack_query_heads(nq, n, t)
    skt, sv = _chunked_kv(skv, n, t, KEY_CHUNK)
    wkt, wv = _chunked_kv(wkv, n, t, QBLOCK)
    nk, nw = t // KEY_CHUNK, t // QBLOCK
    return pl.pallas_call(
        _nsa_prompt_kernel,
        out_shape=jax.ShapeDtypeStruct((n, t, GROUP_WIDTH), f32),
        grid=(n, t // QBLOCK),
        in_specs=[
            pl.BlockSpec((None, QBLOCK, 3 * N_HEADS), lambda b, i: (b, i, 0)),
            pl.BlockSpec((None, KV_HEADS, None, Q_PER_KV * QBLOCK, HEAD_DIM), lambda b, i: (b, 0, i, 0, 0)),
            pl.BlockSpec((None, KV_HEADS, HEAD_DIM, CMP_LANES), lambda b, i: (b, 0, 0, 0)),
            pl.BlockSpec((None, KV_HEADS, CMP_LANES, HEAD_DIM), lambda b, i: (b, 0, 0, 0)),
            pl.BlockSpec((None, KV_HEADS, nk, HEAD_DIM, KEY_CHUNK), lambda b, i: (b, 0, 0, 0, 0)),
            pl.BlockSpec((None, KV_HEADS, nk, KEY_CHUNK, HEAD_DIM), lambda b, i: (b, 0, 0, 0, 0)),
            pl.BlockSpec((None, KV_HEADS, nw, HEAD_DIM, QBLOCK), lambda b, i: (b, 0, 0, 0, 0)),
            pl.BlockSpec((None, KV_HEADS, nw, QBLOCK, HEAD_DIM), lambda b, i: (b, 0, 0, 0, 0)),
        ],
        out_specs=pl.BlockSpec((None, QBLOCK, GROUP_WIDTH), lambda b, i: (b, i, 0)),
        compiler_params=pltpu.CompilerParams(
            dimension_semantics=("parallel", "arbitrary"), vmem_limit_bytes=VMEM_LIMIT),
        name="nsa_prompt",
    )(ng, qs, ckt, cv, skt, sv, wkt, wv)


def s5_discretize(w):
    a_re, a_im = w['a_re'], w['a_im']
    dt = jnp.exp(w['log_dt'])[:, None]
    mag = jnp.exp(a_re * dt)
    ab_re, ab_im = mag * jnp.cos(a_im * dt), mag * jnp.sin(a_im * dt)
    den = a_re * a_re + a_im * a_im
    nr, ni = ab_re - 1.0, ab_im
    f_re = (nr * a_re + ni * a_im) / den
    f_im = (ni * a_re - nr * a_im) / den
    b_re, b_im = w['b_re'], w['b_im']
    bb_re = f_re[..., None] * b_re - f_im[..., None] * b_im
    bb_im = f_re[..., None] * b_im + f_im[..., None] * b_re
    eye = jnp.eye(S5_GROUPS, dtype=f32)

    def in_map(bb):
        return jnp.einsum('gsc,gh->gchs', bb, eye).reshape(GROUP_WIDTH, S5_WIDTH).astype(bf16)

    def out_map(cc):
        return jnp.einsum('gcs,gh->gshc', cc, eye).reshape(S5_WIDTH, GROUP_WIDTH).astype(bf16)

    return dict(a_re=ab_re.reshape(1, S5_WIDTH), a_im=ab_im.reshape(1, S5_WIDTH),
                b_re=in_map(bb_re), b_im=in_map(bb_im),
                c_re=out_map(w['c_re']), c_im=out_map(w['c_im']),
                d=w['d'].reshape(1, GROUP_WIDTH), w_glu=w['w_glu'].astype(bf16),
                b_glu=w['b_glu'].reshape(1, GROUP_WIDTH))


def _s5_prompt_kernel(u_ref, h0_ref, are_ref, aim_ref, bre_ref, bim_ref, cre_ref, cim_ref, d_ref,
                      wglu_ref, bglu_ref, o_ref, hlast_ref, state_ref, bure_ref, buim_ref, hre_ref, him_ref):
    j = pl.program_id(1)

    @pl.when(j == 0)
    def _():
        state_ref[...] = h0_ref[...]

    u = u_ref[...]
    ub = u.astype(bf16)
    bure_ref[...] = jnp.dot(ub, bre_ref[...], preferred_element_type=f32)
    buim_ref[...] = jnp.dot(ub, bim_ref[...], preferred_element_type=f32)
    a_re, a_im = are_ref[...], aim_ref[...]

    def step(t, carry):
        h_re, h_im = carry
        row = pl.ds(t, 1)
        n_re = a_re * h_re - a_im * h_im + bure_ref[row, :]
        n_im = a_re * h_im + a_im * h_re + buim_ref[row, :]
        hre_ref[row, :] = n_re
        him_ref[row, :] = n_im
        return n_re, n_im

    h_re, h_im = lax.fori_loop(0, u.shape[0], step, (state_ref[0:1, :], state_ref[1:2, :]), unroll=8)
    state_ref[0:1, :] = h_re
    state_ref[1:2, :] = h_im
    hlast_ref[...] = state_ref[...]
    y = (jnp.dot(hre_ref[...].astype(bf16), cre_ref[...], preferred_element_type=f32)
         - jnp.dot(him_ref[...].astype(bf16), cim_ref[...], preferred_element_type=f32)
         + d_ref[...] * u)
    y = jax.nn.gelu(y)
    o_ref[...] = y * jax.nn.sigmoid(
        jnp.dot(y.astype(bf16), wglu_ref[...], preferred_element_type=f32) + bglu_ref[...])


def s5_prompt(u, h0, sw):
    n, t = u.shape[:2]
    tc = min(S5_TIME_CHUNK, t)
    const = lambda shape: pl.BlockSpec(shape, lambda b, j: (0,) * len(shape))
    return pl.pallas_call(
        _s5_prompt_kernel,
        out_shape=(jax.ShapeDtypeStruct((n, t, GROUP_WIDTH), f32),
                   jax.ShapeDtypeStruct((n, 2, S5_WIDTH), f32)),
        grid=(n, t // tc),
        in_specs=[pl.BlockSpec((None, tc, GROUP_WIDTH), lambda b, j: (b, j, 0)),
                  pl.BlockSpec((None, 2, S5_WIDTH), lambda b, j: (b, 0, 0)),
                  const((1, S5_WIDTH)), const((1, S5_WIDTH)),
                  const((GROUP_WIDTH, S5_WIDTH)), const((GROUP_WIDTH, S5_WIDTH)),
                  const((S5_WIDTH, GROUP_WIDTH)), const((S5_WIDTH, GROUP_WIDTH)),
                  const((1, GROUP_WIDTH)), const((GROUP_WIDTH, GROUP_WIDTH)), const((1, GROUP_WIDTH))],
        out_specs=(pl.BlockSpec((None, tc, GROUP_WIDTH), lambda b, j: (b, j, 0)),
                   pl.BlockSpec((None, 2, S5_WIDTH), lambda b, j: (b, 0, 0))),
        scratch_shapes=[pltpu.VMEM((2, S5_WIDTH), f32)] + [pltpu.VMEM((tc, S5_WIDTH), f32)] * 4,
        compiler_params=pltpu.CompilerParams(
            dimension_semantics=("parallel", "arbitrary"), vmem_limit_bytes=VMEM_LIMIT),
        name="s5_prompt",
    )(u, h0, sw['a_re'], sw['a_im'], sw['b_re'], sw['b_im'], sw['c_re'], sw['c_im'], sw['d'],
      sw['w_glu'], sw['b_glu'])


def _s5_step_kernel(u_ref, h0_ref, are_ref, aim_ref, bre_ref, bim_ref, cre_ref, cim_ref, d_ref,
                    wglu_ref, bglu_ref, o_ref, h_ref):
    u = u_ref[...]
    ub = u.astype(bf16)
    h0_re, h0_im = h0_ref[:, :S5_WIDTH], h0_ref[:, S5_WIDTH:]
    a_re, a_im = are_ref[...], aim_ref[...]
    h_re = a_re * h0_re - a_im * h0_im + jnp.dot(ub, bre_ref[...], preferred_element_type=f32)
    h_im = a_re * h0_im + a_im * h0_re + jnp.dot(ub, bim_ref[...], preferred_element_type=f32)
    h_ref[:, :S5_WIDTH] = h_re
    h_ref[:, S5_WIDTH:] = h_im
    y = (jnp.dot(h_re.astype(bf16), cre_ref[...], preferred_element_type=f32)
         - jnp.dot(h_im.astype(bf16), cim_ref[...], preferred_element_type=f32)
         + d_ref[...] * u)
    y = jax.nn.gelu(y)
    o_ref[...] = y * jax.nn.sigmoid(
        jnp.dot(y.astype(bf16), wglu_ref[...], preferred_element_type=f32) + bglu_ref[...])


def s5_step(u, h0, sw):
    n = u.shape[0]
    return pl.pallas_call(
        _s5_step_kernel,
        out_shape=(jax.ShapeDtypeStruct((n, GROUP_WIDTH), f32),
                   jax.ShapeDtypeStruct((n, 2 * S5_WIDTH), f32)),
        compiler_params=pltpu.CompilerParams(vmem_limit_bytes=VMEM_LIMIT),
        name="s5_step",
    )(u, h0, sw['a_re'], sw['a_im'], sw['b_re'], sw['b_im'], sw['c_re'], sw['c_im'], sw['d'],
      sw['w_glu'], sw['b_glu'])


GLA_SUB = 16
GLA_QK = GLA_HEADS * GLA_DK
HIGHEST = lax.Precision.HIGHEST
_NT = (((1,), (1,)), ((), ()))
_TN = (((0,), (0,)), ((), ()))


def _log_decay(ga, wgate_ref, bgate_ref):
    z = jnp.dot(ga, wgate_ref[...], preferred_element_type=f32, precision=HIGHEST) + bgate_ref[...]
    return jax.nn.log_sigmoid(z) / GLA_TAU


def _head_sum_matrix():
    r = lax.broadcasted_iota(jnp.int32, (GLA_QK, GROUP_WIDTH), 0) // GLA_DK
    c = lax.broadcasted_iota(jnp.int32, (GLA_QK, GROUP_WIDTH), 1) // GLA_DV
    return jnp.where(r == c, 1.0, 0.0)


def _gla_finish(o, gr, normg_ref):
    outs = []
    for h in range(GLA_HEADS):
        oh = o[:, h * GLA_DV:(h + 1) * GLA_DV]
        outs.append(oh * lax.rsqrt(jnp.mean(oh * oh, axis=-1, keepdims=True) + LN_EPS) * normg_ref[...])
    return jnp.concatenate(outs, axis=1) * (gr * jax.nn.sigmoid(gr))


def _gla_prompt_kernel(q_ref, k_ref, v_ref, ga_ref, gr_ref, wgate_ref, bgate_ref, normg_ref, s0_ref,
                       o_ref, slast_ref, st_ref, oi_ref):
    j = pl.program_id(1)

    @pl.when(j == 0)
    def _():
        st_ref[...] = s0_ref[...]

    rows = q_ref.shape[0]
    nsub = rows // GLA_SUB
    q = q_ref[...] * GLA_DK ** -0.5
    k = k_ref[...]
    v = v_ref[...]
    log_a = _log_decay(ga_ref[...], wgate_ref, bgate_ref)
    ri = lax.broadcasted_iota(jnp.int32, (rows, rows), 0)
    ci = lax.broadcasted_iota(jnp.int32, (rows, rows), 1)
    tri = jnp.where((ri // GLA_SUB == ci // GLA_SUB) & (ci <= ri), 1.0, 0.0)
    b = jnp.dot(tri, log_a, preferred_element_type=f32, precision=HIGHEST)
    row_in_sub = lax.broadcasted_iota(jnp.int32, (rows, 1), 0) % GLA_SUB
    head_sum = _head_sum_matrix().astype(bf16)

    def sub_row(x, jj):
        x3 = x.reshape(nsub, GLA_SUB, x.shape[1])
        return jnp.broadcast_to(x3[:, jj:jj + 1, :], x3.shape).reshape(x.shape)

    o = jnp.zeros((rows, GROUP_WIDTH), f32)
    for jj in range(GLA_SUB):
        decay = jnp.exp(jnp.minimum(b - sub_row(b, jj), 0.0))
        prod = jnp.where(row_in_sub >= jj, q * sub_row(k, jj) * decay, 0.0)
        prod_hi = prod.astype(bf16)
        prod_lo = (prod - prod_hi.astype(f32)).astype(bf16)
        att = (jnp.dot(prod_hi, head_sum, preferred_element_type=f32)
               + jnp.dot(prod_lo, head_sum, preferred_element_type=f32))
        o = o + att * sub_row(v, jj)

    for i in range(nsub):
        blk = slice(i * GLA_SUB, (i + 1) * GLA_SUB)
        b_blk = b[blk]
        b_end = b[(i + 1) * GLA_SUB - 1:(i + 1) * GLA_SUB]
        q_blk = (q[blk] * jnp.exp(b_blk)).astype(bf16)
        k_blk = (k[blk] * jnp.exp(b_end - b_blk)).astype(bf16)
        v_blk = v[blk].astype(bf16)
        a_end = jnp.exp(b_end)
        for h in range(GLA_HEADS):
            ks = slice(h * GLA_DK, (h + 1) * GLA_DK)
            vs = slice(h * GLA_DV, (h + 1) * GLA_DV)
            st = st_ref[h]
            oi_ref[blk, vs] = lax.dot_general(q_blk[:, ks], st.astype(bf16), _NT, preferred_element_type=f32)
            st_ref[h] = st * a_end[:, ks] + lax.dot_general(v_blk[:, vs], k_blk[:, ks], _TN,
                                                            preferred_element_type=f32)

    o_ref[...] = _gla_finish(o + oi_ref[...], gr_ref[...], normg_ref)
    slast_ref[...] = st_ref[...]


def gla_prompt(gq, gk, gv, ga, gr, w_gate, b_gate, norm_g):
    n, t = gq.shape[:2]
    tc = min(GLA_CHUNK, t)
    rows = lambda width: pl.BlockSpec((None, tc, width), lambda b, j: (b, j, 0))
    const = lambda shape: pl.BlockSpec(shape, lambda b, j: (0,) * len(shape))
    state = pl.BlockSpec((None, GLA_HEADS, GLA_DV, GLA_DK), lambda b, j: (b, 0, 0, 0))
    o, s_t = pl.pallas_call(
        _gla_prompt_kernel,
        out_shape=(jax.ShapeDtypeStruct((n, t, GROUP_WIDTH), f32),
                   jax.ShapeDtypeStruct((n, GLA_HEADS, GLA_DV, GLA_DK), f32)),
        grid=(n, t // tc),
        in_specs=[rows(GLA_QK), rows(GLA_QK), rows(GROUP_WIDTH), rows(GLA_LOWRANK), rows(GROUP_WIDTH),
                  const((GLA_LOWRANK, GLA_QK)), const((1, GLA_QK)), const((1, GLA_DV)), state],
        out_specs=(rows(GROUP_WIDTH), state),
        scratch_shapes=[pltpu.VMEM((GLA_HEADS, GLA_DV, GLA_DK), f32), pltpu.VMEM((tc, GROUP_WIDTH), f32)],
        compiler_params=pltpu.CompilerParams(
            dimension_semantics=("parallel", "arbitrary"), vmem_limit_bytes=VMEM_LIMIT),
        name="gla_prompt",
    )(gq, gk, gv, ga, gr, w_gate, b_gate.reshape(1, GLA_QK), norm_g.reshape(1, GLA_DV),
      jnp.zeros((n, GLA_HEADS, GLA_DV, GLA_DK), f32))
    return o, s_t.transpose(0, 1, 3, 2)


GLA_STEP_SEQS = 8


def _gla_step_kernel(q_ref, k_ref, v_ref, ga_ref, gr_ref, wgate_ref, bgate_ref, normg_ref, s_ref,
                     o_ref, snew_ref):
    q = q_ref[...] * GLA_DK ** -0.5
    k = k_ref[...]
    v = v_ref[...]
    a = jnp.exp(_log_decay(ga_ref[...], wgate_ref, bgate_ref))
    qa = q * a
    seq = lax.broadcasted_iota(jnp.int32, (GLA_STEP_SEQS, 1), 0)
    o = jnp.dot(q * k, _head_sum_matrix(), preferred_element_type=f32, precision=HIGHEST) * v
    o_heads = [jnp.zeros((GLA_STEP_SEQS, GLA_DV), f32) for _ in range(GLA_HEADS)]
    for i in range(GLA_STEP_SEQS):
        mine = seq == i
        pick = jnp.broadcast_to(jnp.where(mine, 1.0, 0.0), (GLA_STEP_SEQS, GLA_DV))
        qa_i = jnp.where(mine, qa, 0.0).astype(bf16)
        k_i = jnp.where(mine, k, 0.0).astype(bf16)
        vb = v.astype(bf16)
        for h in range(GLA_HEADS):
            ks = slice(h * GLA_DK, (h + 1) * GLA_DK)
            vs = slice(h * GLA_DV, (h + 1) * GLA_DV)
            s_old = s_ref[i, h]
            a_rows = lax.dot_general(a[:, ks], pick, _TN, preferred_element_type=f32, precision=HIGHEST)
            o_heads[h] = o_heads[h] + jnp.dot(qa_i[:, ks], s_old.astype(bf16), preferred_element_type=f32)
            snew_ref[i, h] = a_rows * s_old + lax.dot_general(k_i[:, ks], vb[:, vs], _TN,
                                                               preferred_element_type=f32)
    o_ref[...] = _gla_finish(o + jnp.concatenate(o_heads, axis=1), gr_ref[...], normg_ref)


def gla_step(gq, gk, gv, ga, gr, w_gate, b_gate, norm_g, s0):
    n = gq.shape[0]
    rows = lambda width: pl.BlockSpec((GLA_STEP_SEQS, width), lambda i: (i, 0))
    const = lambda shape: pl.BlockSpec(shape, lambda i: (0,) * len(shape))
    state = pl.BlockSpec((GLA_STEP_SEQS, GLA_HEADS, GLA_DK, GLA_DV), lambda i: (i, 0, 0, 0))
    return pl.pallas_call(
        _gla_step_kernel,
        out_shape=(jax.ShapeDtypeStruct((n, GROUP_WIDTH), f32),
                   jax.ShapeDtypeStruct((n, GLA_HEADS, GLA_DK, GLA_DV), f32)),
        grid=(n // GLA_STEP_SEQS,),
        in_specs=[rows(GLA_QK), rows(GLA_QK), rows(GROUP_WIDTH), rows(GLA_LOWRANK), rows(GROUP_WIDTH),
                  const((GLA_LOWRANK, GLA_QK)), const((1, GLA_QK)), const((1, GLA_DV)), state],
        out_specs=(rows(GROUP_WIDTH), state),
        compiler_params=pltpu.CompilerParams(
            dimension_semantics=("parallel",), vmem_limit_bytes=VMEM_LIMIT),
        name="gla_step",
    )(gq, gk, gv, ga, gr, w_gate, b_gate.reshape(1, GLA_QK), norm_g.reshape(1, GLA_DV), s0)


N_PAGES = PAST_LEN // PAGE_SIZE
KEY_PAD = PAST_LEN + LANES
STEP_ROWS = 8
CMP_STEP_SEQS = 2
BLOCKS_PER_PAGE = PAGE_SIZE // CMP_BLOCK


def _tokens_minor(cache):
    token_axis = 2
    return jnp.moveaxis(cache, token_axis, -1)


def _seq_spec(*tail):
    zeros = (0,) * len(tail)
    return pl.BlockSpec((None,) + tail, lambda i, pt: (i,) + zeros)


def _layer_page_specs(block_tail, layer, seqs_per_step=1):
    zeros = (0,) * len(block_tail)
    return [pl.BlockSpec((None, None) + block_tail,
                         lambda i, pt, s=s, p=p: (layer, pt[i * seqs_per_step + s, p]) + zeros)
            for s in range(seqs_per_step) for p in range(N_PAGES)]


def _layer_seq_spec(tail, layer):
    zeros = (0,) * len(tail)
    return pl.BlockSpec((None, None) + tail, lambda i, pt: (layer, i) + zeros)


def _topk_rows_kernel(keys_ref, sel_ref, *, topk, idx_bits, n_valid):
    keys = keys_ref[...]
    lane = lax.broadcasted_iota(jnp.int32, keys.shape, 1)
    count = lambda pred: jnp.sum(jnp.where(pred(keys, lane), 1.0, 0.0), axis=1, keepdims=True)
    thr, need = _topk_threshold(count, topk)
    cut = _tie_index_cut(count, keys.shape[0], thr, need, idx_bits)
    chosen = ((keys > thr) | ((keys == thr) & (lane <= cut))) & (lane < n_valid)
    sel_ref[...] = jnp.where(chosen, 1.0, 0.0)


TOKEN_PAIRS = CMP_BLOCK // 2
PAIR_ROWS = 2 * BLOCKS_PER_PAGE


def _pair_regroup_matrix():
    r = np.arange(TOKEN_PAIRS * PAIR_ROWS)
    k = np.arange(2 * PAGE_SIZE)
    tp, page, blk = r // PAIR_ROWS, (r // BLOCKS_PER_PAGE) % 2, r % BLOCKS_PER_PAGE
    page_k, blk_k, t_k = k // PAGE_SIZE, (k % PAGE_SIZE) // CMP_BLOCK, k % CMP_BLOCK
    hit = ((page[:, None] == page_k[None]) & (blk[:, None] == blk_k[None]) & (tp[:, None] == t_k[None] // 2))
    return jnp.asarray(hit, bf16)


def _nsa_compress_paged_kernel(pt_ref, pe_ref, regroup_ref, w1_ref, w2_ref, *rest):
    npg = CMP_STEP_SEQS * N_PAGES
    pages, o_ref, x_ref = rest[:npg], rest[npg], rest[npg + 1]
    blocks_per_seq = N_PAGES * BLOCKS_PER_PAGE
    even_token = (lax.broadcasted_iota(jnp.int32, (HEAD_DIM, 2 * PAGE_SIZE), 1) & 1) == 0
    regroup = regroup_ref[...]
    for pair in range(npg // 2):
        for gj in range(2 * KV_HEADS):
            g, j = gj // 2, gj % 2
            two = jnp.concatenate([pages[2 * pair][g, j] + pe_ref[j], pages[2 * pair + 1][g, j] + pe_ref[j]],
                                  axis=1)
            split = jnp.concatenate([jnp.where(even_token, two, 0.0), jnp.where(even_token, 0.0, two)],
                                    axis=0).astype(bf16)
            rows = lax.dot_general(regroup, split, _NT, preferred_element_type=f32)
            for tp in range(TOKEN_PAIRS):
                x_ref[gj, tp, pair * PAIR_ROWS:(pair + 1) * PAIR_ROWS, :] = rows[tp * PAIR_ROWS:(tp + 1) * PAIR_ROWS]
    for gj in range(2 * KV_HEADS):
        j = gj % 2
        x = jnp.concatenate([x_ref[gj, tp] for tp in range(TOKEN_PAIRS)], axis=1).astype(bf16)
        hid = jnp.dot(x, w1_ref[j], preferred_element_type=f32)
        out = jnp.dot(jax.nn.gelu(hid).astype(bf16), w2_ref[j], preferred_element_type=f32)
        for s in range(CMP_STEP_SEQS):
            o_ref[s, gj] = out[s * blocks_per_seq:(s + 1) * blocks_per_seq]


def nsa_compress_paged(pools_cmp, layer, page_table, cmp_pe, cmp_w1, cmp_w2):
    n = page_table.shape[0]
    blocks = N_PAGES * BLOCKS_PER_PAGE
    assert (CMP_STEP_SEQS * N_PAGES) % 2 == 0 and CMP_BLOCK % 2 == 0 and PAIR_ROWS == 8
    pe_page = jnp.tile(cmp_pe.transpose(1, 2, 0), (1, 1, BLOCKS_PER_PAGE))
    const = lambda shape: pl.BlockSpec(shape, lambda i, pt: (0,) * len(shape))
    return pl.pallas_call(
        _nsa_compress_paged_kernel,
        out_shape=jax.ShapeDtypeStruct((n, 2 * KV_HEADS, blocks, HEAD_DIM), f32),
        grid_spec=pltpu.PrefetchScalarGridSpec(
            num_scalar_prefetch=1, grid=(n // CMP_STEP_SEQS,),
            in_specs=[const((2, HEAD_DIM, PAGE_SIZE)), const((TOKEN_PAIRS * PAIR_ROWS, 2 * PAGE_SIZE)),
                      const((2, CMP_BLOCK * HEAD_DIM, CMP_HIDDEN)), const((2, CMP_HIDDEN, HEAD_DIM))]
            + _layer_page_specs((KV_HEADS, 2, HEAD_DIM, PAGE_SIZE), layer, CMP_STEP_SEQS),
            out_specs=pl.BlockSpec((CMP_STEP_SEQS, 2 * KV_HEADS, blocks, HEAD_DIM), lambda i, pt: (i, 0, 0, 0)),
            scratch_shapes=[pltpu.VMEM((2 * KV_HEADS, TOKEN_PAIRS, CMP_STEP_SEQS * blocks, 2 * HEAD_DIM), f32)]),
        compiler_params=pltpu.CompilerParams(
            dimension_semantics=("parallel",), vmem_limit_bytes=VMEM_LIMIT),
        name="nsa_compress_paged",
    )(page_table, pe_page, _pair_regroup_matrix(), cmp_w1.astype(bf16), cmp_w2.astype(bf16),
      *([pools_cmp] * (CMP_STEP_SEQS * N_PAGES)))


def _decode_attention(q8, blocks, masks, kv_new, new_ok, is_g0):
    qb = q8.astype(bf16)
    by_group = lambda fn: jnp.where(is_g0, fn(0), fn(1))
    s_new = by_group(lambda g: jnp.sum(q8 * kv_new[2 * g:2 * g + 1], axis=1, keepdims=True)) * ATT_SCALE
    if new_ok is not None:
        s_new = jnp.where(new_ok, s_new, MASKED)
    m = s_new
    scores = []
    for blk, msk in zip(blocks, masks):
        s = by_group(lambda g: jnp.dot(qb, blk[g, 0].astype(bf16), preferred_element_type=f32)) * ATT_SCALE
        s = jnp.where(msk > 0.5, s, MASKED)
        scores.append(s)
        m = jnp.maximum(m, jnp.max(s, axis=1, keepdims=True))
    p_new = jnp.exp(s_new - m)
    if new_ok is not None:
        p_new = jnp.where(new_ok, p_new, 0.0)
    l = p_new
    acc = p_new * by_group(lambda g: kv_new[2 * g + 1:2 * g + 2])
    for blk, msk, s in zip(blocks, masks, scores):
        p = jnp.where(msk > 0.5, jnp.exp(s - m), 0.0)
        l = l + jnp.sum(p, axis=1, keepdims=True)
        pb = p.astype(bf16)
        acc = acc + by_group(lambda g: lax.dot_general(pb, blk[g, 1].astype(bf16), _NT,
                                                       preferred_element_type=f32))
    return acc / l


def _dsa_decode_scores_kernel(pt_ref, iq_ref, iw_ref, iknew_ref, *rest):
    pages, keys_ref = rest[:N_PAGES], rest[N_PAGES]
    iq = iq_ref[...]
    iqb = iq.astype(bf16)
    iw = iw_ref[...] * IDX_HEADS ** -0.5
    weigh = lambda dots, w: jnp.sum(w * jnp.maximum(dots * IDX_DIM ** -0.5, 0.0), axis=0, keepdims=True)
    for p in range(N_PAGES):
        dots = jnp.dot(iqb, pages[p][...].astype(bf16), preferred_element_type=f32)
        keys_ref[:, p * PAGE_SIZE:(p + 1) * PAGE_SIZE] = _sortable_key(weigh(dots, iw))
    new_score = weigh(jnp.sum(iq * iknew_ref[...], axis=1, keepdims=True), iw[:, :1])
    lane = lax.broadcasted_iota(jnp.int32, (1, LANES), 1)
    keys_ref[:, PAST_LEN:] = jnp.where(lane == 0, _sortable_key(jnp.broadcast_to(new_score, (1, LANES))), INT_MIN)


def _dsa_decode_attend_kernel(pt_ref, q_ref, sel_ref, kvnew_ref, *rest):
    pages, o_ref = rest[:N_PAGES], rest[N_PAGES]
    masks = [sel_ref[:, p * PAGE_SIZE:(p + 1) * PAGE_SIZE] for p in range(N_PAGES)]
    new_ok = sel_ref[:, PAST_LEN:PAST_LEN + 1] > 0.5
    is_g0 = lax.broadcasted_iota(jnp.int32, (N_HEADS, 1), 0) < Q_PER_KV
    o_ref[...] = _decode_attention(q_ref[...], pages, masks, kvnew_ref[...], new_ok, is_g0)


def dsa_decode(dq, dkv, iq, ik, iw, pools_kv, pools_idx, layer, page_table):
    n = dq.shape[0]
    assert min(DSA_TOPK, (PAST_LEN + 1) // 4) == DSA_TOPK and PAGE_SIZE == LANES
    pad_heads = ((0, 0), (0, STEP_ROWS - IDX_HEADS), (0, 0))
    iq8 = jnp.pad(iq.reshape(n, IDX_HEADS, IDX_DIM), pad_heads)
    iw8 = jnp.broadcast_to(jnp.pad(iw.reshape(n, IDX_HEADS, 1), pad_heads), (n, STEP_ROWS, LANES))
    keys = pl.pallas_call(
        _dsa_decode_scores_kernel,
        out_shape=jax.ShapeDtypeStruct((n, 1, KEY_PAD), jnp.int32),
        grid_spec=pltpu.PrefetchScalarGridSpec(
            num_scalar_prefetch=1, grid=(n,),
            in_specs=[_seq_spec(STEP_ROWS, IDX_DIM), _seq_spec(STEP_ROWS, LANES), _seq_spec(1, IDX_DIM)]
            + _layer_page_specs((IDX_DIM, PAGE_SIZE), layer),
            out_specs=_seq_spec(1, KEY_PAD)),
        compiler_params=pltpu.CompilerParams(
            dimension_semantics=("parallel",), vmem_limit_bytes=VMEM_LIMIT),
        name="dsa_decode_scores",
    )(page_table, iq8, iw8, ik.reshape(n, 1, IDX_DIM), *([pools_idx] * N_PAGES))
    sel = pl.pallas_call(
        functools.partial(_topk_rows_kernel, topk=DSA_TOPK, idx_bits=int(math.ceil(math.log2(KEY_PAD))),
                          n_valid=PAST_LEN + 1),
        out_shape=jax.ShapeDtypeStruct((n, KEY_PAD), f32),
        compiler_params=pltpu.CompilerParams(vmem_limit_bytes=VMEM_LIMIT),
        name="dsa_decode_topk",
    )(keys.reshape(n, KEY_PAD))
    o8 = pl.pallas_call(
        _dsa_decode_attend_kernel,
        out_shape=jax.ShapeDtypeStruct((n, N_HEADS, HEAD_DIM), f32),
        grid_spec=pltpu.PrefetchScalarGridSpec(
            num_scalar_prefetch=1, grid=(n,),
            in_specs=[_seq_spec(N_HEADS, HEAD_DIM), _seq_spec(1, KEY_PAD), _seq_spec(2 * KV_HEADS, HEAD_DIM)]
            + _layer_page_specs((KV_HEADS, 2, HEAD_DIM, PAGE_SIZE), layer),
            out_specs=_seq_spec(N_HEADS, HEAD_DIM)),
        compiler_params=pltpu.CompilerParams(
            dimension_semantics=("parallel",), vmem_limit_bytes=VMEM_LIMIT),
        name="dsa_decode_attend",
    )(page_table, dq.reshape(n, N_HEADS, HEAD_DIM), sel.reshape(n, 1, KEY_PAD),
      dkv.reshape(n, 2 * KV_HEADS, HEAD_DIM), *([pools_kv] * N_PAGES))
    return o8.reshape(n, GROUP_WIDTH)


def _nsa_decode_kernel(pt_ref, q_ref, gate_ref, cmp_ref, slcnew_ref, win_ref, winnew_ref, *rest):
    pages, o_ref = rest[:N_PAGES], rest[N_PAGES]
    q8 = q_ref[...]
    qb = q8.astype(bf16)
    lane = lax.broadcasted_iota(jnp.int32, (N_HEADS, CMP_LANES), 1)
    head = lax.broadcasted_iota(jnp.int32, (N_HEADS, 1), 0)
    is_g0 = head < Q_PER_KV
    by_group = lambda fn: jnp.where(is_g0, fn(0), fn(1))
    q_pos = jnp.full((N_HEADS, 1), PAST_LEN, jnp.int32)
    blocks = cmp_ref.shape[1]
    s = by_group(lambda g: lax.dot_general(qb, cmp_ref[2 * g].astype(bf16), _NT,
                                           preferred_element_type=f32)) * ATT_SCALE
    lane_c = lane[:, :blocks]
    s = jnp.where((lane_c + 1) * CMP_BLOCK - 1 <= q_pos, s, -jnp.inf)
    m = jnp.max(s, axis=1, keepdims=True)
    m = jnp.where(m == -jnp.inf, 0.0, m)
    p = jnp.exp(s - m)
    p = p / jnp.maximum(jnp.sum(p, axis=1, keepdims=True), F32_TINY)
    pb = p.astype(bf16)
    o_cmp = by_group(lambda g: jnp.dot(pb, cmp_ref[2 * g + 1].astype(bf16), preferred_element_type=f32))
    imp = jnp.where(is_g0, jnp.sum(p[:Q_PER_KV], axis=0, keepdims=True),
                    jnp.sum(p[Q_PER_KV:], axis=0, keepdims=True))
    imp = jnp.concatenate([imp, jnp.zeros((N_HEADS, CMP_LANES - blocks), f32)], axis=1)
    sel_blk = _select_blocks(imp, q_pos, lane)
    ratio = SLC_BLOCK // CMP_BLOCK
    per_page = PAGE_SIZE // SLC_BLOCK
    masks = []
    for pg in range(N_PAGES):
        msk = sel_blk[:, ratio * per_page * pg:ratio * per_page * pg + 1]
        for j in range(1, per_page):
            b = per_page * pg + j
            msk = jnp.where(lane < j * SLC_BLOCK, msk, sel_blk[:, ratio * b:ratio * b + 1])
        masks.append(msk)
    o_slc = _decode_attention(q8, pages, masks, slcnew_ref[...], None, is_g0)
    wbuf = win_ref.shape[-1]
    slot = lax.broadcasted_iota(jnp.int32, (1, wbuf), 1)
    wmask = jnp.where(PAST_LEN - wbuf + slot > PAST_LEN - WINDOW, 1.0, 0.0)
    o_win = _decode_attention(q8, [win_ref], [wmask], winnew_ref[...], None, is_g0)
    gate = jax.nn.sigmoid(gate_ref[...])
    o_ref[...] = (gate[0][:, :HEAD_DIM] * o_cmp + gate[1][:, :HEAD_DIM] * o_slc
                  + gate[2][:, :HEAD_DIM] * o_win)


def nsa_decode(nq, ng, skv, wkv, pools_cmp, pools_slc, wins, layer, page_table, cmp_pe, cmp_w1, cmp_w2):
    n = nq.shape[0]
    blocks = N_PAGES * BLOCKS_PER_PAGE
    wbuf = wins.shape[-1]
    assert (PAST_LEN + 1) // CMP_BLOCK == blocks and blocks <= CMP_LANES
    assert PAST_LEN // SLC_BLOCK + 1 <= CMP_LANES // 2 and PAGE_SIZE == LANES
    cmp_rows = nsa_compress_paged(pools_cmp, layer, page_table, cmp_pe, cmp_w1, cmp_w2)
    gates =jnp.broadcast_to(ng.reshape(n, N_HEADS, 3).transpose(0, 2, 1)[..., None], (n, 3, N_HEADS, LANES))
    kv_tile = (KV_HEADS, 2, HEAD_DIM)
    o8 = pl.pallas_call(
        _nsa_decode_kernel,
        out_shape=jax.ShapeDtypeStruct((n, N_HEADS, HEAD_DIM), f32),
        grid_spec=pltpu.PrefetchScalarGridSpec(
            num_scalar_prefetch=1, grid=(n,),
            in_specs=[_seq_spec(N_HEADS, HEAD_DIM), _seq_spec(3, N_HEADS, LANES),
                      _seq_spec(2 * KV_HEADS, blocks, HEAD_DIM), _seq_spec(2 * KV_HEADS, HEAD_DIM),
                      _layer_seq_spec(kv_tile + (wbuf,), layer), _seq_spec(2 * KV_HEADS, HEAD_DIM)]
            + _layer_page_specs(kv_tile + (PAGE_SIZE,), layer),
            out_specs=_seq_spec(N_HEADS, HEAD_DIM)),
        compiler_params=pltpu.CompilerParams(
            dimension_semantics=("parallel",), vmem_limit_bytes=VMEM_LIMIT),
        name="nsa_decode",
    )(page_table, nq.reshape(n, N_HEADS, HEAD_DIM), gates, cmp_rows, skv.reshape(n, 2 * KV_HEADS, HEAD_DIM),
      wins, wkv.reshape(n, 2 * KV_HEADS, HEAD_DIM), *([pools_slc] * N_PAGES))
    return o8.reshape(n, GROUP_WIDTH)


def mix_prompt(cols, n, t, w, sw, cmp_weights):
    (dq, dkv, iq, ik, iw, u, nq, ckv, skv, wkv, ng, gq, gk, gv, ga, gr) = [
        c.reshape(n, t, c.shape[-1]) for c in cols]
    kv_shape = (n, t, KV_HEADS, 2, HEAD_DIM)
    o_dsa = dsa_prompt(dq, dkv, iq, ik, iw)
    o_s5, h_s5 = s5_prompt(u, jnp.zeros((n, 2, S5_WIDTH), f32), sw)
    o_nsa = nsa_prompt(nq, ng, ckv, skv, wkv, cmp_weights)
    o_gla, s_gla = gla_prompt(gq, gk, gv, ga, gr, w['gla_w_gate'], w['gla_b_gate'], w['gla_norm_g'])
    win_state = wkv[:, -min(WINDOW, t):].reshape((n, min(WINDOW, t)) + kv_shape[2:])
    mixers = [o.reshape(n * t, GROUP_WIDTH) for o in (o_dsa, o_s5, o_nsa, o_gla)]
    return mixers, (dkv.reshape(kv_shape), ik, ckv.reshape(kv_shape), skv.reshape(kv_shape), win_state,
                    h_s5.reshape(n, 2, S5_GROUPS, S5_STATE), s_gla)


def mix_sample(cols, w, sw, layer, pools, win_buf, h_s5, s_gla, page_table):
    (dq, dkv, iq, ik, iw, u, nq, ckv, skv, wkv, ng, gq, gk, gv, ga, gr) = cols
    n, t = dq.shape[0], 1
    kv_shape = (n, t, KV_HEADS, 2, HEAD_DIM)
    o_dsa = dsa_decode(dq, dkv, iq, ik, iw, pools['dsa_kv'], pools['dsa_idx'], layer, page_table)
    o_s5, h_new = s5_step(u, h_s5.reshape(n, 2 * S5_WIDTH), sw)
    o_nsa = nsa_decode(nq, ng, skv, wkv, pools['nsa_cmp'], pools['nsa_slc'], pools['nsa_win'], layer,
                       page_table, w['cmp_pe'], w['cmp_w1'], w['cmp_w2'])
    o_gla, s_new = gla_step(gq, gk, gv, ga, gr, w['gla_w_gate'], w['gla_b_gate'], w['gla_norm_g'], s_gla)
    win_new = jnp.concatenate([win_buf[:, 1:], wkv.reshape(kv_shape)], axis=1)
    return [o_dsa, o_s5, o_nsa, o_gla], (
        dkv.reshape(kv_shape), ik.reshape(n, t, IDX_DIM), ckv.reshape(kv_shape), skv.reshape(kv_shape),
        win_new, h_new.reshape(n, 2, S5_GROUPS, S5_STATE), s_new)


def kernel(x_prompt, x_sample, cache_dsa_kv, cache_dsa_idx, cache_nsa_cmp, cache_nsa_slc, cache_nsa_win, state_s5, state_gla, page_table, w_in, s5_a_re, s5_a_im, s5_b_re, s5_b_im, s5_c_re, s5_c_im, s5_d, s5_log_dt, s5_w_glu, s5_b_glu, nsa_cmp_pe, nsa_cmp_w1, nsa_cmp_w2, gla_w_gate, gla_b_gate, gla_norm_g, w_out, ln1_g, ln1_b, ffn_w_gate, ffn_w_up, ffn_w_down, ln2_g, ln2_b):
    np_, tp_ = x_prompt.shape[:2]
    ns_, ts_ = x_sample.shape[:2]
    assert ts_ == 1
    w_in_pad = pad_in_projection(w_in)
    w_out_b = w_out.astype(bf16)
    wg_b, wu_b, wd_b = ffn_w_gate.astype(bf16), ffn_w_up.astype(bf16), ffn_w_down.astype(bf16)
    hp = x_prompt.reshape(np_ * tp_, D_MODEL)
    hs = x_sample.reshape(ns_ * ts_, D_MODEL)
    outs_p = [[] for _ in range(7)]
    outs_s = [[] for _ in range(7)]
    pools = {'dsa_kv': _tokens_minor(cache_dsa_kv), 'dsa_idx': _tokens_minor(cache_dsa_idx),
             'nsa_cmp': _tokens_minor(cache_nsa_cmp), 'nsa_slc': _tokens_minor(cache_nsa_slc),
             'nsa_win': _tokens_minor(cache_nsa_win)}
    for l in range(DEPTH):
        w = {'a_re': s5_a_re[l], 'a_im': s5_a_im[l], 'b_re': s5_b_re[l], 'b_im': s5_b_im[l],
             'c_re': s5_c_re[l], 'c_im': s5_c_im[l], 'd': s5_d[l], 'log_dt': s5_log_dt[l],
             'w_glu': s5_w_glu[l], 'b_glu': s5_b_glu[l],
             'cmp_pe': nsa_cmp_pe[l], 'cmp_w1': nsa_cmp_w1[l], 'cmp_w2': nsa_cmp_w2[l],
             'gla_w_gate': gla_w_gate[l], 'gla_b_gate': gla_b_gate[l], 'gla_norm_g': gla_norm_g[l]}
        sw = s5_discretize(w)
        cmp_weights = nsa_compress_weights(nsa_cmp_pe[l], nsa_cmp_w1[l], nsa_cmp_w2[l])
        mixers_p, st_p = mix_prompt(in_projection(hp, w_in_pad, l), np_, tp_, w, sw, cmp_weights)
        mixers_s, st_s = mix_sample(in_projection(hs, w_in_pad, l), w, sw, l, pools, cache_nsa_win[l],
                                    state_s5[l], state_gla[l], page_table)
        hp = outproj_ln(hp, mixers_p, w_out_b, l, ln1_g[l], ln1_b[l])
        hs = outproj_ln(hs, mixers_s, w_out_b, l, ln1_g[l], ln1_b[l])
        hp = ffn_ln(hp, wg_b, wu_b, wd_b, l, ln2_g[l], ln2_b[l])
        hs = ffn_ln(hs, wg_b, wu_b, wd_b, l, ln2_g[l], ln2_b[l])
        for lst, st in zip(outs_p, st_p):
            lst.append(st)
        for lst, st in zip(outs_s, st_s):
            lst.append(st)
    dsa_kv_p, dsa_idx_p, nsa_cmp_p, nsa_slc_p, nsa_win_p, s5_p, gla_p = [jnp.stack(a) for a in outs_p]
    dsa_kv_s, dsa_idx_s, nsa_cmp_s, nsa_slc_s, nsa_win_s, s5_s, gla_s = [jnp.stack(a) for a in outs_s]
    return (hp.reshape(np_, tp_, D_MODEL), hs.reshape(ns_, ts_, D_MODEL),
            dsa_kv_p, dsa_kv_s, dsa_idx_p, dsa_idx_s, nsa_cmp_p, nsa_cmp_s,
            nsa_slc_p, nsa_slc_s, nsa_win_p, nsa_win_s, s5_p, s5_s, gla_p, gla_s)
```

```python
import functools
import math

import jax
import jax.numpy as jnp
from jax import lax
import numpy as np
from jax.experimental import pallas as pl
from jax.experimental.pallas import tpu as pltpu

D_MODEL = 2048
DEPTH = 2
PAST_LEN = 2048
PAGE_SIZE = 128
GROUP_WIDTH = 512
HEAD_DIM = 64
N_HEADS = 8
KV_HEADS = 2
Q_PER_KV = 4
KV_ROW = 256
IDX_HEADS = 4
IDX_DIM = 64
DSA_TOPK = 256
S5_GROUPS = 32
S5_STATE = 64
S5_WIDTH = S5_GROUPS * S5_STATE
CMP_BLOCK = 32
CMP_HIDDEN = 128
SLC_BLOCK = 64
SLC_TOPN = 16
WINDOW = 512
QBLOCK = 128
GLA_HEADS = 4
GLA_DK = 64
GLA_DV = 128
GLA_LOWRANK = 16
GLA_TAU = 16.0
GLA_CHUNK = 64
DEEPNORM_ALPHA = (2 * DEPTH) ** 0.25
LN_EPS = 1e-5

IN_SIZES = (512, 256, 256, 64, 4, 512, 512, 256, 256, 256, 24, 256, 256, 512, 16, 512)
D_IN = sum(IN_SIZES)

VMEM_LIMIT = 56 * 1024 * 1024
LANES = 128
KEY_CHUNK = 512
CMP_LANES = 128
S5_TIME_CHUNK = 256
ATT_SCALE = HEAD_DIM ** -0.5
MASKED = -1e30
INT_MIN = -2 ** 31
F32_TINY = float(np.finfo(np.float32).tiny)

bf16 = jnp.bfloat16
f32 = jnp.float32


IN_PAD_SIZES = tuple(-(-s // LANES) * LANES for s in IN_SIZES)
IN_PAD_OFFSETS = tuple(int(o) for o in np.cumsum((0,) + IN_PAD_SIZES[:-1]))


def pad_in_projection(w_in):
    parts, off = [], 0
    for size, size_pad in zip(IN_SIZES, IN_PAD_SIZES):
        parts.append(jnp.pad(w_in[:, :, off:off + size], ((0, 0), (0, 0), (0, size_pad - size))))
        off += size
    return jnp.concatenate(parts, axis=-1).astype(bf16)


def _in_projection_kernel(x_ref, w_ref, *o_refs):
    xb = x_ref[...].astype(bf16)
    for o_ref, off, size, size_pad in zip(o_refs, IN_PAD_OFFSETS, IN_SIZES, IN_PAD_SIZES):
        y = jnp.dot(xb, w_ref[:, off:off + size_pad], preferred_element_type=f32)
        o_ref[...] = y[:, :size]


def in_projection(x, w_pad, layer):
    m, k = x.shape
    tm = min(256, m)
    return pl.pallas_call(
        _in_projection_kernel,
        out_shape=tuple(jax.ShapeDtypeStruct((m, s), f32) for s in IN_SIZES),
        grid=(m // tm,),
        in_specs=[pl.BlockSpec((tm, k), lambda i: (i, 0)),
                  pl.BlockSpec((None, k, w_pad.shape[2]), lambda i: (layer, 0, 0),
                               pipeline_mode=pl.Buffered(1))],
        out_specs=tuple(pl.BlockSpec((tm, s), lambda i: (i, 0)) for s in IN_SIZES),
        compiler_params=pltpu.CompilerParams(
            dimension_semantics=("parallel",), vmem_limit_bytes=VMEM_LIMIT),
        name="in_projection",
    )(x, w_pad)


def _layer_norm_rows(z, g, b):
    mu = jnp.mean(z, axis=-1, keepdims=True)
    zc = z - mu
    var = jnp.mean(zc * zc, axis=-1, keepdims=True)
    return zc * lax.rsqrt(var + LN_EPS) * g + b


def _outproj_ln_kernel(x_ref, m0_ref, m1_ref, m2_ref, m3_ref, w_ref, g_ref, b_ref, o_ref):
    y = DEEPNORM_ALPHA * x_ref[...]
    for j, m_ref in enumerate((m0_ref, m1_ref, m2_ref, m3_ref)):
        y = y + jnp.dot(m_ref[...].astype(bf16), w_ref[j * GROUP_WIDTH:(j + 1) * GROUP_WIDTH, :],
                        preferred_element_type=f32)
    o_ref[...] = _layer_norm_rows(y, g_ref[...], b_ref[...])


def outproj_ln(x, mixers, w_bf16, layer, g, b):
    m, d = x.shape
    tm = min(256, m)
    mix_spec = pl.BlockSpec((tm, GROUP_WIDTH), lambda i: (i, 0))
    return pl.pallas_call(
        _outproj_ln_kernel,
        out_shape=jax.ShapeDtypeStruct((m, d), f32),
        grid=(m // tm,),
        in_specs=[pl.BlockSpec((tm, d), lambda i: (i, 0)),
                  mix_spec, mix_spec, mix_spec, mix_spec,
                  pl.BlockSpec((None, d, d), lambda i: (layer, 0, 0)),
                  pl.BlockSpec((1, d), lambda i: (0, 0)),
                  pl.BlockSpec((1, d), lambda i: (0, 0))],
        out_specs=pl.BlockSpec((tm, d), lambda i: (i, 0)),
        compiler_params=pltpu.CompilerParams(
            dimension_semantics=("parallel",), vmem_limit_bytes=VMEM_LIMIT),
        name="outproj_ln",
    )(x, *mixers, w_bf16, g.reshape(1, d), b.reshape(1, d))


def _ffn_ln_kernel(h_ref, wg_ref, wu_ref, wd_ref, g_ref, b_ref, o_ref, acc_ref):
    f = pl.program_id(1)

    @pl.when(f == 0)
    def _():
        acc_ref[...] = jnp.zeros_like(acc_ref)

    hb = h_ref[...].astype(bf16)
    a = jnp.dot(hb, wg_ref[...], preferred_element_type=f32)
    u = jnp.dot(hb, wu_ref[...], preferred_element_type=f32)
    act = (a * jax.nn.sigmoid(a) * u).astype(bf16)
    acc_ref[...] += jnp.dot(act, wd_ref[...], preferred_element_type=f32)

    @pl.when(f == pl.num_programs(1) - 1)
    def _():
        z = DEEPNORM_ALPHA * h_ref[...] + acc_ref[...]
        o_ref[...] = _layer_norm_rows(z, g_ref[...], b_ref[...])


def ffn_ln(h, wg, wu, wd, layer, g, b, tf=512):
    m, d = h.shape
    fh = wg.shape[2]
    tm = min(512, m)
    return pl.pallas_call(
        _ffn_ln_kernel,
        out_shape=jax.ShapeDtypeStruct((m, d), f32),
        grid=(m // tm, fh // tf),
        in_specs=[pl.BlockSpec((tm, d), lambda i, f: (i, 0)),
                  pl.BlockSpec((None, d, tf), lambda i, f: (layer, 0, f)),
                  pl.BlockSpec((None, d, tf), lambda i, f: (layer, 0, f)),
                  pl.BlockSpec((None, tf, d), lambda i, f: (layer, f, 0)),
                  pl.BlockSpec((1, d), lambda i, f: (0, 0)),
                  pl.BlockSpec((1, d), lambda i, f: (0, 0))],
        out_specs=pl.BlockSpec((tm, d), lambda i, f: (i, 0)),
        scratch_shapes=[pltpu.VMEM((tm, d), f32)],
        compiler_params=pltpu.CompilerParams(
            dimension_semantics=("parallel", "arbitrary"), vmem_limit_bytes=VMEM_LIMIT),
        name="ffn_ln",
    )(h, wg, wu, wd, g.reshape(1, d), b.reshape(1, d))


def _tile_rows(x, reps):
    return jnp.concatenate([x] * reps, axis=0)


def _scaled_queries(q):
    assert math.log2(HEAD_DIM) % 2 == 0
    return (q * ATT_SCALE).astype(bf16)


def _masked_flash(qg, kt_at, v_at, c_lo, c_hi, bias_at):
    rows = qg.shape[0]

    def body(c, carry):
        m, l, acc = carry
        s = jnp.dot(qg, kt_at(c), preferred_element_type=f32)
        bias = bias_at(c)
        s = jnp.concatenate([s[r * QBLOCK:(r + 1) * QBLOCK] + bias for r in range(rows // QBLOCK)], axis=0)
        m_new = jnp.maximum(m, jnp.max(s, axis=1, keepdims=True))
        p = jnp.exp(s - m_new)
        alpha = jnp.exp(m - m_new)
        l = alpha * l + jnp.sum(p, axis=1, keepdims=True)
        acc = alpha * acc + jnp.dot(p.astype(bf16), v_at(c), preferred_element_type=f32)
        return m_new, l, acc

    init = (jnp.full((rows, 1), MASKED, f32), jnp.zeros((rows, 1), f32),
            jnp.zeros((rows, HEAD_DIM), f32))
    _, l, acc = lax.fori_loop(c_lo, c_hi, body, init)
    return acc / l


def _sortable_key(x):
    bits = pltpu.bitcast(x, jnp.int32)
    return bits ^ (jnp.right_shift(bits, 31) & jnp.int32(0x7FFFFFFF))


def _topk_threshold(count, topk):
    topk = float(topk)
    cnt_nonneg = count(lambda k, i: k >= 0)
    thr0 = jnp.where(cnt_nonneg >= topk, 0, INT_MIN).astype(jnp.int32)

    def bit_body(b, thr):
        cand = thr + jnp.left_shift(jnp.int32(1), 30 - b)
        return jnp.where(count(lambda k, i: k >= cand) >= topk, cand, thr)

    thr = lax.fori_loop(0, 31, bit_body, thr0)
    need = topk - count(lambda k, i: k > thr)
    return thr, need


def _tie_index_cut(count, rows, thr, need, idx_bits):
    def cut_body(b, cut):
        cand = cut + jnp.left_shift(jnp.int32(1), idx_bits - 1 - b)
        return jnp.where(count(lambda k, i: (k == thr) & (i < cand)) < need, cand, cut)

    return lax.fori_loop(0, idx_bits, cut_body, jnp.zeros((rows, 1), jnp.int32))


def _select_blocks(imp, q_pos, lane):
    ratio = SLC_BLOCK // CMP_BLOCK
    on_block_lane = (lane & (ratio - 1)) == 0
    blk = jnp.right_shift(lane, 1)
    cur = jnp.right_shift(q_pos, 6)
    future = blk > cur
    forced = (blk == 0) | (blk == cur) | (blk == cur - 1)
    imp = imp + pltpu.roll(imp, CMP_LANES - 1, 1)
    score = jnp.where(future, -jnp.inf, jnp.where(forced, jnp.inf, imp))
    score = jnp.where(on_block_lane, score, -jnp.inf)
    if imp.shape[0] == CMP_LANES:
        score_t = score.T
        blk_t = lax.broadcasted_iota(jnp.int32, score_t.shape, 0)
        rank_t = jnp.zeros(score_t.shape, f32)
        for b in range(CMP_LANES // ratio):
            other = score_t[ratio * b:ratio * b + 1, :]
            ahead = (other > score_t) | ((other == score_t) & (ratio * b < blk_t))
            rank_t = rank_t + jnp.where(ahead, 1.0, 0.0)
        rank = rank_t.T
    else:
        rank = jnp.zeros(imp.shape, f32)
        for b in range(CMP_LANES // ratio):
            col = score[:, ratio * b:ratio * b + 1]
            ahead = (col > score) | ((col == score) & (ratio * b < lane))
            rank = rank + jnp.where(ahead, 1.0, 0.0)
    return jnp.where((rank < float(SLC_TOPN)) & on_block_lane, 1.0, 0.0)


def _dsa_prompt_kernel(iq_ref, iw_ref, ikt_ref, q_ref, kt_ref, v_ref, tri_ref, o_ref, keys_ref, sel_ref):
    i = pl.program_id(1)
    nch = (i * QBLOCK + QBLOCK + KEY_CHUNK - 1) // KEY_CHUNK
    q_pos = i * QBLOCK + lax.broadcasted_iota(jnp.int32, (QBLOCK, 1), 0)
    lane = lax.broadcasted_iota(jnp.int32, (QBLOCK, KEY_CHUNK), 1)

    iq = iq_ref[...]
    iq_stack = jnp.concatenate(
        [iq[:, h * IDX_DIM:(h + 1) * IDX_DIM] for h in range(IDX_HEADS)], axis=0).astype(bf16)
    iw = iw_ref[...] * IDX_HEADS ** -0.5

    def score_body(c, carry):
        dots = jnp.dot(iq_stack, ikt_ref[c], preferred_element_type=f32) * IDX_DIM ** -0.5
        acc = jnp.zeros((QBLOCK, KEY_CHUNK), f32)
        for h in range(IDX_HEADS):
            acc = acc + iw[:, h:h + 1] * jnp.maximum(dots[h * QBLOCK:(h + 1) * QBLOCK], 0.0)
        causal = c * KEY_CHUNK + lane <= q_pos
        keys_ref[c] = jnp.where(causal, _sortable_key(acc), INT_MIN)
        return carry

    lax.fori_loop(0, nch, score_body, 0)

    def count(pred):
        def body(c, acc):
            hit = jnp.where(pred(keys_ref[c], c * KEY_CHUNK + lane), 1.0, 0.0)
            part = hit[:, :LANES]
            for j in range(1, KEY_CHUNK // LANES):
                part = part + hit[:, j * LANES:(j + 1) * LANES]
            return acc + part
        acc = lax.fori_loop(0, nch, body, jnp.zeros((QBLOCK, LANES), f32))
        return jnp.sum(acc, axis=1, keepdims=True)

    thr, need = _topk_threshold(count, DSA_TOPK)

    def sel_body(c, ties_before):
        k = keys_ref[c]
        tie = k == thr
        tie_rank = ties_before + jnp.dot(jnp.where(tie, 1.0, 0.0).astype(bf16), tri_ref[...],
                                         preferred_element_type=f32)
        chosen = ((k > thr) | (tie & (tie_rank <= need))) & (c * KEY_CHUNK + lane <= q_pos)
        sel_ref[c] = jnp.where(chosen, 0.0, MASKED)
        return tie_rank[:, KEY_CHUNK - 1:KEY_CHUNK]

    lax.fori_loop(0, nch, sel_body, jnp.zeros((QBLOCK, 1), f32))

    for g in range(KV_HEADS):
        o = _masked_flash(_scaled_queries(q_ref[g]), lambda c: kt_ref[g, c], lambda c: v_ref[g, c],
                          0, nch, lambda c: sel_ref[c])
        for r in range(Q_PER_KV):
            h = g * Q_PER_KV + r
            o_ref[:, h * HEAD_DIM:(h + 1) * HEAD_DIM] = o[r * QBLOCK:(r + 1) * QBLOCK]


def _stack_query_heads(q, n, t):
    q = q.reshape(n, t // QBLOCK, QBLOCK, KV_HEADS, Q_PER_KV, HEAD_DIM)
    q = q.transpose(0, 3, 1, 4, 2, 5)
    return q.reshape(n, KV_HEADS, t // QBLOCK, Q_PER_KV * QBLOCK, HEAD_DIM)


def _chunked_kv(kv, n, t, chunk):
    kv = kv.reshape(n, t // chunk, chunk, KV_HEADS, 2, HEAD_DIM).astype(bf16)
    kt = kv[:, :, :, :, 0, :].transpose(0, 3, 1, 4, 2)
    v = kv[:, :, :, :, 1, :].transpose(0, 3, 1, 2, 4)
    return kt, v


def dsa_prompt(dq, dkv, iq, ik, iw):
    n, t = dq.shape[:2]
    nchunks = t // KEY_CHUNK
    assert t % KEY_CHUNK == 0 and min(DSA_TOPK, t // 4) == DSA_TOPK
    ikt = ik.reshape(n, nchunks, KEY_CHUNK, IDX_DIM).transpose(0, 1, 3, 2).astype(bf16)
    qs = _stack_query_heads(dq, n, t)
    kt, v = _chunked_kv(dkv, n, t, KEY_CHUNK)
    at = jnp.arange(KEY_CHUNK)
    tri = (at[:, None] <= at[None, :]).astype(bf16)
    return pl.pallas_call(
        _dsa_prompt_kernel,
        out_shape=jax.ShapeDtypeStruct((n, t, GROUP_WIDTH), f32),
        grid=(n, t // QBLOCK),
        in_specs=[
            pl.BlockSpec((None, QBLOCK, IDX_HEADS * IDX_DIM), lambda b, i: (b, i, 0)),
            pl.BlockSpec((None, QBLOCK, IDX_HEADS), lambda b, i: (b, i, 0)),
            pl.BlockSpec((None, nchunks, IDX_DIM, KEY_CHUNK), lambda b, i: (b, 0, 0, 0)),
            pl.BlockSpec((None, KV_HEADS, None, Q_PER_KV * QBLOCK, HEAD_DIM), lambda b, i: (b, 0, i, 0, 0)),
            pl.BlockSpec((None, KV_HEADS, nchunks, HEAD_DIM, KEY_CHUNK), lambda b, i: (b, 0, 0, 0, 0)),
            pl.BlockSpec((None, KV_HEADS, nchunks, KEY_CHUNK, HEAD_DIM), lambda b, i: (b, 0, 0, 0, 0)),
            pl.BlockSpec((KEY_CHUNK, KEY_CHUNK), lambda b, i: (0, 0)),
        ],
        out_specs=pl.BlockSpec((None, QBLOCK, GROUP_WIDTH), lambda b, i: (b, i, 0)),
        scratch_shapes=[pltpu.VMEM((nchunks, QBLOCK, KEY_CHUNK), jnp.int32),
                        pltpu.VMEM((nchunks, QBLOCK, KEY_CHUNK), f32)],
        compiler_params=pltpu.CompilerParams(
            dimension_semantics=("parallel", "arbitrary"), vmem_limit_bytes=VMEM_LIMIT),
        name="dsa_prompt",
    )(iq, iw, ikt, qs, kt, v, tri)


def _nsa_compress_kernel(x_ref, pe_ref, w1_ref, w2_ref, o_ref, hid_ref):
    k = pl.program_id(0)

    @pl.when(k == 0)
    def _():
        hid_ref[...] = jnp.zeros_like(hid_ref)

    hid_ref[...] += jnp.dot((x_ref[...] + pe_ref[...]).astype(bf16), w1_ref[...],
                            preferred_element_type=f32)

    @pl.when(k == pl.num_programs(0) - 1)
    def _():
        hid = jax.nn.gelu(hid_ref[...]).astype(bf16)
        o_ref[...] = jnp.dot(hid, w2_ref[...], preferred_element_type=f32)


def nsa_compress_weights(cmp_pe, cmp_w1, cmp_w2):
    w1 = cmp_w1.reshape(2, CMP_BLOCK, HEAD_DIM, CMP_HIDDEN)
    eye_g = jnp.eye(KV_HEADS, dtype=cmp_w1.dtype)
    eye_j = jnp.eye(2, dtype=cmp_w1.dtype)
    w1_big = jnp.einsum('jtde,gh,jk->tgjdhke', w1, eye_g, eye_j)
    w1_big = w1_big.reshape(CMP_BLOCK * KV_ROW, KV_HEADS * 2 * CMP_HIDDEN).astype(bf16)
    w2_big = jnp.einsum('jed,gh,jk->gjehkd', cmp_w2, eye_g, eye_j)
    w2_big = w2_big.reshape(KV_HEADS * 2 * CMP_HIDDEN, KV_ROW).astype(bf16)
    pe_row = jnp.broadcast_to(cmp_pe[:, None], (CMP_BLOCK, KV_HEADS, 2, HEAD_DIM))
    return pe_row.reshape(1, CMP_BLOCK * KV_ROW), w1_big, w2_big


def nsa_compress_rows(x, pe_row, w1_big, w2_big, tk=2048):
    m, kdim = x.shape
    hw = w1_big.shape[1]
    return pl.pallas_call(
        _nsa_compress_kernel,
        out_shape=jax.ShapeDtypeStruct((m, KV_ROW), f32),
        grid=(kdim // tk,),
        in_specs=[pl.BlockSpec((m, tk), lambda k: (0, k)),
                  pl.BlockSpec((1, tk), lambda k: (0, k)),
                  pl.BlockSpec((tk, hw), lambda k: (k, 0)),
                  pl.BlockSpec((hw, KV_ROW), lambda k: (0, 0))],
        out_specs=pl.BlockSpec((m, KV_ROW), lambda k: (0, 0)),
        scratch_shapes=[pltpu.VMEM((m, hw), f32)],
        compiler_params=pltpu.CompilerParams(
            dimension_semantics=("arbitrary",), vmem_limit_bytes=VMEM_LIMIT),
        name="nsa_compress",
    )(x, pe_row, w1_big, w2_big)


def _nsa_prompt_kernel(gate_ref, q_ref, ckt_ref, cv_ref, skt_ref, sv_ref, wkt_ref, wv_ref, o_ref):
    i = pl.program_id(1)
    nch = (i * QBLOCK + QBLOCK + KEY_CHUNK - 1) // KEY_CHUNK
    q_pos = i * QBLOCK + lax.broadcasted_iota(jnp.int32, (QBLOCK, 1), 0)
    lane = lax.broadcasted_iota(jnp.int32, (QBLOCK, CMP_LANES), 1)
    lane_k = lax.broadcasted_iota(jnp.int32, (QBLOCK, KEY_CHUNK), 1)
    expand_row = lax.broadcasted_iota(jnp.int32, (CMP_LANES, KEY_CHUNK), 0)
    expand_col = lax.broadcasted_iota(jnp.int32, (CMP_LANES, KEY_CHUNK), 1)
    gate = jax.nn.sigmoid(gate_ref[...])
    ratio = SLC_BLOCK // CMP_BLOCK
    c_vis = _tile_rows(jnp.where((lane + 1) * CMP_BLOCK - 1 <= q_pos, 1.0, 0.0), Q_PER_KV) > 0.5
    w_tiles = WINDOW // QBLOCK + 1
    w_lo = jnp.maximum(i - WINDOW // QBLOCK, 0)
    w_idx = w_lo * QBLOCK + lax.broadcasted_iota(jnp.int32, (QBLOCK, w_tiles * QBLOCK), 1)
    w_bias = _tile_rows(jnp.where((w_idx <= q_pos) & (w_idx > q_pos - WINDOW), 0.0, MASKED), Q_PER_KV)

    for g in range(KV_HEADS):
        qg = _scaled_queries(q_ref[g])
        s = jnp.dot(qg, ckt_ref[g], preferred_element_type=f32)
        s = jnp.where(c_vis, s, -jnp.inf)
        m = jnp.max(s, axis=1, keepdims=True)
        m = jnp.where(m == -jnp.inf, 0.0, m)
        p = jnp.exp(s - m)
        p = p / jnp.maximum(jnp.sum(p, axis=1, keepdims=True), F32_TINY)
        o_cmp = jnp.dot(p.astype(bf16), cv_ref[g], preferred_element_type=f32)
        imp = p[0:QBLOCK]
        for r in range(1, Q_PER_KV):
            imp = imp + p[r * QBLOCK:(r + 1) * QBLOCK]
        sel_blk = _select_blocks(imp, q_pos, lane).astype(bf16)

        def slc_mask(c):
            kidx = c * KEY_CHUNK + lane_k
            expand = jnp.where(
                expand_row == ratio * jnp.right_shift(c * KEY_CHUNK + expand_col, 6), 1.0, 0.0).astype(bf16)
            picked = jnp.dot(sel_blk, expand, preferred_element_type=f32)
            return jnp.where((picked > 0.5) & (kidx <= q_pos), 0.0, MASKED)

        o_slc = _masked_flash(qg, lambda c: skt_ref[g, c], lambda c: sv_ref[g, c], 0, nch, slc_mask)

        w_kt = jnp.concatenate([wkt_ref[g, w_lo + j] for j in range(w_tiles)], axis=1)
        w_v = jnp.concatenate([wv_ref[g, w_lo + j] for j in range(w_tiles)], axis=0)
        s = jnp.dot(qg, w_kt, preferred_element_type=f32) + w_bias
        pw = jnp.exp(s - jnp.max(s, axis=1, keepdims=True))
        o_win = (jnp.dot(pw.astype(bf16), w_v, preferred_element_type=f32)
                 / jnp.sum(pw, axis=1, keepdims=True))
        for r in range(Q_PER_KV):
            h = g * Q_PER_KV + r
            rows = slice(r * QBLOCK, (r + 1) * QBLOCK)
            o_ref[:, h * HEAD_DIM:(h + 1) * HEAD_DIM] = (
                gate[:, 3 * h:3 * h + 1] * o_cmp[rows]
                + gate[:, 3 * h + 1:3 * h + 2] * o_slc[rows]
                + gate[:, 3 * h + 2:3 * h + 3] * o_win[rows])


def nsa_prompt(nq, ng, ckv, skv, wkv, cmp_weights):
    n, t = nq.shape[:2]
    nc = t // CMP_BLOCK
    assert nc <= CMP_LANES and t % KEY_CHUNK == 0 and SLC_BLOCK == 64 and SLC_BLOCK // CMP_BLOCK == 2
    assert t >= WINDOW + QBLOCK and WINDOW % QBLOCK == 0
    cmp_rows = nsa_compress_rows(ckv.reshape(n * nc, CMP_BLOCK * KV_ROW), *cmp_weights)
    cmp_rows = cmp_rows.reshape(n, nc, KV_HEADS, 2, HEAD_DIM)
    cmp_rows = jnp.pad(cmp_rows, ((0, 0), (0, CMP_LANES - nc), (0, 0), (0, 0), (0, 0))).astype(bf16)
    ckt = cmp_rows[:, :, :, 0, :].transpose(0, 2, 3, 1)
    cv = cmp_rows[:, :, :, 1, :].transpose(0, 2, 1, 3)
    qs = _stack_query_heads(nq, n, t)
    skt, sv = _chunked_kv(skv, n, t, KEY_CHUNK)
    wkt, wv = _chunked_kv(wkv, n, t, QBLOCK)
    nk, nw = t // KEY_CHUNK, t // QBLOCK
    return pl.pallas_call(
        _nsa_prompt_kernel,
        out_shape=jax.ShapeDtypeStruct((n, t, GROUP_WIDTH), f32),
        grid=(n, t // QBLOCK),
        in_specs=[
            pl.BlockSpec((None, QBLOCK, 3 * N_HEADS), lambda b, i: (b, i, 0)),
            pl.BlockSpec((None, KV_HEADS, None, Q_PER_KV * QBLOCK, HEAD_DIM), lambda b, i: (b, 0, i, 0, 0)),
            pl.BlockSpec((None, KV_HEADS, HEAD_DIM, CMP_LANES), lambda b, i: (b, 0, 0, 0)),
            pl.BlockSpec((None, KV_HEADS, CMP_LANES, HEAD_DIM), lambda b, i: (b, 0, 0, 0)),
            pl.BlockSpec((None, KV_HEADS, nk, HEAD_DIM, KEY_CHUNK), lambda b, i: (b, 0, 0, 0, 0)),
            pl.BlockSpec((None, KV_HEADS, nk, KEY_CHUNK, HEAD_DIM), lambda b, i: (b, 0, 0, 0, 0)),
            pl.BlockSpec((None, KV_HEADS, nw, HEAD_DIM, QBLOCK), lambda b, i: (b, 0, 0, 0, 0)),
            pl.BlockSpec((None, KV_HEADS, nw, QBLOCK, HEAD_DIM), lambda b, i: (b, 0, 0, 0, 0)),
        ],
        out_specs=pl.BlockSpec((None, QBLOCK, GROUP_WIDTH), lambda b, i: (b, i, 0)),
        compiler_params=pltpu.CompilerParams(
            dimension_semantics=("parallel", "arbitrary"), vmem_limit_bytes=VMEM_LIMIT),
        name="nsa_prompt",
    )(ng, qs, ckt, cv, skt, sv, wkt, wv)


def s5_discretize(w):
    a_re, a_im = w['a_re'], w['a_im']
    dt = jnp.exp(w['log_dt'])[:, None]
    mag = jnp.exp(a_re * dt)
    ab_re, ab_im = mag * jnp.cos(a_im * dt), mag * jnp.sin(a_im * dt)
    den = a_re * a_re + a_im * a_im
    nr, ni = ab_re - 1.0, ab_im
    f_re = (nr * a_re + ni * a_im) / den
    f_im = (ni * a_re - nr * a_im) / den
    b_re, b_im = w['b_re'], w['b_im']
    bb_re = f_re[..., None] * b_re - f_im[..., None] * b_im
    bb_im = f_re[..., None] * b_im + f_im[..., None] * b_re
    eye = jnp.eye(S5_GROUPS, dtype=f32)

    def in_map(bb):
        return jnp.einsum('gsc,gh->gchs', bb, eye).reshape(GROUP_WIDTH, S5_WIDTH).astype(bf16)

    def out_map(cc):
        return jnp.einsum('gcs,gh->gshc', cc, eye).reshape(S5_WIDTH, GROUP_WIDTH).astype(bf16)

    return dict(a_re=ab_re.reshape(1, S5_WIDTH), a_im=ab_im.reshape(1, S5_WIDTH),
                b_re=in_map(bb_re), b_im=in_map(bb_im),
                c_re=out_map(w['c_re']), c_im=out_map(w['c_im']),
                d=w['d'].reshape(1, GROUP_WIDTH), w_glu=w['w_glu'].astype(bf16),
                b_glu=w['b_glu'].reshape(1, GROUP_WIDTH))


def _s5_prompt_kernel(u_ref, h0_ref, are_ref, aim_ref, bre_ref, bim_ref, cre_ref, cim_ref, d_ref,
                      wglu_ref, bglu_ref, o_ref, hlast_ref, state_ref, bure_ref, buim_ref, hre_ref, him_ref):
    j = pl.program_id(1)

    @pl.when(j == 0)
    def _():
        state_ref[...] = h0_ref[...]

    u = u_ref[...]
    ub = u.astype(bf16)
    for half in range(2):
        ch = slice(half * GROUP_WIDTH // 2, (half + 1) * GROUP_WIDTH // 2)
        st = slice(half * S5_WIDTH // 2, (half + 1) * S5_WIDTH // 2)
        bure_ref[:, st] = jnp.dot(ub[:, ch], bre_ref[ch, st], preferred_element_type=f32)
        buim_ref[:, st] = jnp.dot(ub[:, ch], bim_ref[ch, st], preferred_element_type=f32)
    a_re, a_im = are_ref[...], aim_ref[...]

    def step(t, carry):
        h_re, h_im = carry
        row = pl.ds(t, 1)
        n_re = a_re * h_re - a_im * h_im + bure_ref[row, :]
        n_im = a_re * h_im + a_im * h_re + buim_ref[row, :]
        hre_ref[row, :] = n_re
        him_ref[row, :] = n_im
        return n_re, n_im

    h_re, h_im = lax.fori_loop(0, u.shape[0], step, (state_ref[0:1, :], state_ref[1:2, :]), unroll=8)
    state_ref[0:1, :] = h_re
    state_ref[1:2, :] = h_im
    hlast_ref[...] = state_ref[...]
    ys = []
    for half in range(2):
        ch = slice(half * GROUP_WIDTH // 2, (half + 1) * GROUP_WIDTH // 2)
        st = slice(half * S5_WIDTH // 2, (half + 1) * S5_WIDTH // 2)
        ys.append(jnp.dot(hre_ref[:, st].astype(bf16), cre_ref[st, ch], preferred_element_type=f32)
                  - jnp.dot(him_ref[:, st].astype(bf16), cim_ref[st, ch], preferred_element_type=f32))
    y = jnp.concatenate(ys, axis=1) + d_ref[...] * u
    y = jax.nn.gelu(y)
    o_ref[...] = y * jax.nn.sigmoid(
        jnp.dot(y.astype(bf16), wglu_ref[...], preferred_element_type=f32) + bglu_ref[...])


def s5_prompt(u, h0, sw):
    n, t = u.shape[:2]
    tc = min(S5_TIME_CHUNK, t)
    const = lambda shape: pl.BlockSpec(shape, lambda b, j: (0,) * len(shape))
    return pl.pallas_call(
        _s5_prompt_kernel,
        out_shape=(jax.ShapeDtypeStruct((n, t, GROUP_WIDTH), f32),
                   jax.ShapeDtypeStruct((n, 2, S5_WIDTH), f32)),
        grid=(n, t // tc),
        in_specs=[pl.BlockSpec((None, tc, GROUP_WIDTH), lambda b, j: (b, j, 0)),
                  pl.BlockSpec((None, 2, S5_WIDTH), lambda b, j: (b, 0, 0)),
                  const((1, S5_WIDTH)), const((1, S5_WIDTH)),
                  const((GROUP_WIDTH, S5_WIDTH)), const((GROUP_WIDTH, S5_WIDTH)),
                  const((S5_WIDTH, GROUP_WIDTH)), const((S5_WIDTH, GROUP_WIDTH)),
                  const((1, GROUP_WIDTH)), const((GROUP_WIDTH, GROUP_WIDTH)), const((1, GROUP_WIDTH))],
        out_specs=(pl.BlockSpec((None, tc, GROUP_WIDTH), lambda b, j: (b, j, 0)),
                   pl.BlockSpec((None, 2, S5_WIDTH), lambda b, j: (b, 0, 0))),
        scratch_shapes=[pltpu.VMEM((2, S5_WIDTH), f32)] + [pltpu.VMEM((tc, S5_WIDTH), f32)] * 4,
        compiler_params=pltpu.CompilerParams(
            dimension_semantics=("parallel", "arbitrary"), vmem_limit_bytes=VMEM_LIMIT),
        name="s5_prompt",
    )(u, h0, sw['a_re'], sw['a_im'], sw['b_re'], sw['b_im'], sw['c_re'], sw['c_im'], sw['d'],
      sw['w_glu'], sw['b_glu'])


def _s5_step_kernel(u_ref, h0_ref, are_ref, aim_ref, bre_ref, bim_ref, cre_ref, cim_ref, d_ref,
                    wglu_ref, bglu_ref, o_ref, h_ref):
    u = u_ref[...]
    ub = u.astype(bf16)
    h0_re, h0_im = h0_ref[:, :S5_WIDTH], h0_ref[:, S5_WIDTH:]
    a_re, a_im = are_ref[...], aim_ref[...]
    h_re = a_re * h0_re - a_im * h0_im + jnp.dot(ub, bre_ref[...], preferred_element_type=f32)
    h_im = a_re * h0_im + a_im * h0_re + jnp.dot(ub, bim_ref[...], preferred_element_type=f32)
    h_ref[:, :S5_WIDTH] = h_re
    h_ref[:, S5_WIDTH:] = h_im
    y = (jnp.dot(h_re.astype(bf16), cre_ref[...], preferred_element_type=f32)
         - jnp.dot(h_im.astype(bf16), cim_ref[...], preferred_element_type=f32)
         + d_ref[...] * u)
    y = jax.nn.gelu(y)
    o_ref[...] = y * jax.nn.sigmoid(
        jnp.dot(y.astype(bf16), wglu_ref[...], preferred_element_type=f32) + bglu_ref[...])


def s5_step(u, h0, sw):
    n = u.shape[0]
    return pl.pallas_call(
        _s5_step_kernel,
        out_shape=(jax.ShapeDtypeStruct((n, GROUP_WIDTH), f32),
                   jax.ShapeDtypeStruct((n, 2 * S5_WIDTH), f32)),
        compiler_params=pltpu.CompilerParams(vmem_limit_bytes=VMEM_LIMIT),
        name="s5_step",
    )(u, h0, sw['a_re'], sw['a_im'], sw['b_re'], sw['b_im'], sw['c_re'], sw['c_im'], sw['d'],
      sw['w_glu'], sw['b_glu'])


GLA_SUB = 16
GLA_QK = GLA_HEADS * GLA_DK
HIGHEST = lax.Precision.HIGHEST
_NT = (((1,), (1,)), ((), ()))
_TN = (((0,), (0,)), ((), ()))


def _log_decay(ga, wgate_ref, bgate_ref):
    z = jnp.dot(ga, wgate_ref[...], preferred_element_type=f32, precision=HIGHEST) + bgate_ref[...]
    return jax.nn.log_sigmoid(z) / GLA_TAU


def _head_sum_matrix():
    r = lax.broadcasted_iota(jnp.int32, (GLA_QK, GROUP_WIDTH), 0) // GLA_DK
    c = lax.broadcasted_iota(jnp.int32, (GLA_QK, GROUP_WIDTH), 1) // GLA_DV
    return jnp.where(r == c, 1.0, 0.0)


def _gla_finish(o, gr, normg_ref):
    outs = []
    for h in range(GLA_HEADS):
        oh = o[:, h * GLA_DV:(h + 1) * GLA_DV]
        outs.append(oh * lax.rsqrt(jnp.mean(oh * oh, axis=-1, keepdims=True) + LN_EPS) * normg_ref[...])
    return jnp.concatenate(outs, axis=1) * (gr * jax.nn.sigmoid(gr))


def _gla_prompt_kernel(q_ref, k_ref, v_ref, ga_ref, gr_ref, wgate_ref, bgate_ref, normg_ref, s0_ref,
                       o_ref, slast_ref, st_ref, oi_ref):
    j = pl.program_id(1)

    @pl.when(j == 0)
    def _():
        st_ref[...] = s0_ref[...]

    rows = q_ref.shape[0]
    nsub = rows // GLA_SUB
    q = q_ref[...] * GLA_DK ** -0.5
    k = k_ref[...]
    v = v_ref[...]
    log_a = _log_decay(ga_ref[...], wgate_ref, bgate_ref)
    ri = lax.broadcasted_iota(jnp.int32, (rows, rows), 0)
    ci = lax.broadcasted_iota(jnp.int32, (rows, rows), 1)
    tri = jnp.where((ri // GLA_SUB == ci // GLA_SUB) & (ci <= ri), 1.0, 0.0)
    b = jnp.dot(tri, log_a, preferred_element_type=f32, precision=HIGHEST)
    row_in_sub = lax.broadcasted_iota(jnp.int32, (rows, 1), 0) % GLA_SUB
    head_sum = _head_sum_matrix().astype(bf16)

    def sub_row(x, jj):
        x3 = x.reshape(nsub, GLA_SUB, x.shape[1])
        return jnp.broadcast_to(x3[:, jj:jj + 1, :], x3.shape).reshape(x.shape)

    o = jnp.zeros((rows, GROUP_WIDTH), f32)
    for jj in range(GLA_SUB):
        decay = jnp.exp(jnp.minimum(b - sub_row(b, jj), 0.0))
        prod = jnp.where(row_in_sub >= jj, q * sub_row(k, jj) * decay, 0.0)
        prod_hi = prod.astype(bf16)
        prod_lo = (prod - prod_hi.astype(f32)).astype(bf16)
        att = []
        for half in range(2):
            ks = slice(half * GLA_QK // 2, (half + 1) * GLA_QK // 2)
            vs = slice(half * GROUP_WIDTH // 2, (half + 1) * GROUP_WIDTH // 2)
            att.append(jnp.dot(prod_hi[:, ks], head_sum[ks, vs], preferred_element_type=f32)
                       + jnp.dot(prod_lo[:, ks], head_sum[ks, vs], preferred_element_type=f32))
        o = o + jnp.concatenate(att, axis=1) * sub_row(v, jj)

    for i in range(nsub):
        blk = slice(i * GLA_SUB, (i + 1) * GLA_SUB)
        b_blk = b[blk]
        b_end = b[(i + 1) * GLA_SUB - 1:(i + 1) * GLA_SUB]
        q_blk = (q[blk] * jnp.exp(b_blk)).astype(bf16)
        k_blk = (k[blk] * jnp.exp(b_end - b_blk)).astype(bf16)
        v_blk = v[blk].astype(bf16)
        a_end = jnp.exp(b_end)
        for h in range(GLA_HEADS):
            ks = slice(h * GLA_DK, (h + 1) * GLA_DK)
            vs = slice(h * GLA_DV, (h + 1) * GLA_DV)
            st = st_ref[h]
            oi_ref[blk, vs] = lax.dot_general(q_blk[:, ks], st.astype(bf16), _NT, preferred_element_type=f32)
            st_ref[h] = st * a_end[:, ks] + lax.dot_general(v_blk[:, vs], k_blk[:, ks], _TN,
                                                            preferred_element_type=f32)

    o_ref[...] = _gla_finish(o + oi_ref[...], gr_ref[...], normg_ref)
    slast_ref[...] = st_ref[...]


def gla_prompt(gq, gk, gv, ga, gr, w_gate, b_gate, norm_g):
    n, t = gq.shape[:2]
    tc = min(GLA_CHUNK, t)
    rows = lambda width: pl.BlockSpec((None, tc, width), lambda b, j: (b, j, 0))
    const = lambda shape: pl.BlockSpec(shape, lambda b, j: (0,) * len(shape))
    state = pl.BlockSpec((None, GLA_HEADS, GLA_DV, GLA_DK), lambda b, j: (b, 0, 0, 0))
    o, s_t = pl.pallas_call(
        _gla_prompt_kernel,
        out_shape=(jax.ShapeDtypeStruct((n, t, GROUP_WIDTH), f32),
                   jax.ShapeDtypeStruct((n, GLA_HEADS, GLA_DV, GLA_DK), f32)),
        grid=(n, t // tc),
        in_specs=[rows(GLA_QK), rows(GLA_QK), rows(GROUP_WIDTH), rows(GLA_LOWRANK), rows(GROUP_WIDTH),
                  const((GLA_LOWRANK, GLA_QK)), const((1, GLA_QK)), const((1, GLA_DV)), state],
        out_specs=(rows(GROUP_WIDTH), state),
        scratch_shapes=[pltpu.VMEM((GLA_HEADS, GLA_DV, GLA_DK), f32), pltpu.VMEM((tc, GROUP_WIDTH), f32)],
        compiler_params=pltpu.CompilerParams(
            dimension_semantics=("parallel", "arbitrary"), vmem_limit_bytes=VMEM_LIMIT),
        name="gla_prompt",
    )(gq, gk, gv, ga, gr, w_gate, b_gate.reshape(1, GLA_QK), norm_g.reshape(1, GLA_DV),
      jnp.zeros((n, GLA_HEADS, GLA_DV, GLA_DK), f32))
    return o, s_t.transpose(0, 1, 3, 2)


GLA_STEP_SEQS = 8


def _gla_step_kernel(q_ref, k_ref, v_ref, ga_ref, gr_ref, wgate_ref, bgate_ref, normg_ref, s_ref,
                     o_ref, snew_ref):
    q = q_ref[...] * GLA_DK ** -0.5
    k = k_ref[...]
    v = v_ref[...]
    a = jnp.exp(_log_decay(ga_ref[...], wgate_ref, bgate_ref))
    qa = q * a
    seq = lax.broadcasted_iota(jnp.int32, (GLA_STEP_SEQS, 1), 0)
    o = jnp.dot(q * k, _head_sum_matrix(), preferred_element_type=f32, precision=HIGHEST) * v
    o_heads = [jnp.zeros((GLA_STEP_SEQS, GLA_DV), f32) for _ in range(GLA_HEADS)]
    for i in range(GLA_STEP_SEQS):
        mine = seq == i
        pick = jnp.broadcast_to(jnp.where(mine, 1.0, 0.0), (GLA_STEP_SEQS, GLA_DV))
        qa_i = jnp.where(mine, qa, 0.0).astype(bf16)
        k_i = jnp.where(mine, k, 0.0).astype(bf16)
        vb = v.astype(bf16)
        for h in range(GLA_HEADS):
            ks = slice(h * GLA_DK, (h + 1) * GLA_DK)
            vs = slice(h * GLA_DV, (h + 1) * GLA_DV)
            s_old = s_ref[i, h]
            a_rows = lax.dot_general(a[:, ks], pick, _TN, preferred_element_type=f32, precision=HIGHEST)
            o_heads[h] = o_heads[h] + jnp.dot(qa_i[:, ks], s_old.astype(bf16), preferred_element_type=f32)
            snew_ref[i, h] = a_rows * s_old + lax.dot_general(k_i[:, ks], vb[:, vs], _TN,
                                                               preferred_element_type=f32)
    o_ref[...] = _gla_finish(o + jnp.concatenate(o_heads, axis=1), gr_ref[...], normg_ref)


def gla_step(gq, gk, gv, ga, gr, w_gate, b_gate, norm_g, s0):
    n = gq.shape[0]
    rows = lambda width: pl.BlockSpec((GLA_STEP_SEQS, width), lambda i: (i, 0))
    const = lambda shape: pl.BlockSpec(shape, lambda i: (0,) * len(shape))
    state = pl.BlockSpec((GLA_STEP_SEQS, GLA_HEADS, GLA_DK, GLA_DV), lambda i: (i, 0, 0, 0))
    return pl.pallas_call(
        _gla_step_kernel,
        out_shape=(jax.ShapeDtypeStruct((n, GROUP_WIDTH), f32),
                   jax.ShapeDtypeStruct((n, GLA_HEADS, GLA_DK, GLA_DV), f32)),
        grid=(n // GLA_STEP_SEQS,),
        in_specs=[rows(GLA_QK), rows(GLA_QK), rows(GROUP_WIDTH), rows(GLA_LOWRANK), rows(GROUP_WIDTH),
                  const((GLA_LOWRANK, GLA_QK)), const((1, GLA_QK)), const((1, GLA_DV)), state],
        out_specs=(rows(GROUP_WIDTH), state),
        compiler_params=pltpu.CompilerParams(
            dimension_semantics=("parallel",), vmem_limit_bytes=VMEM_LIMIT),
        name="gla_step",
    )(gq, gk, gv, ga, gr, w_gate, b_gate.reshape(1, GLA_QK), norm_g.reshape(1, GLA_DV), s0)


N_PAGES = PAST_LEN // PAGE_SIZE
KEY_PAD = PAST_LEN + LANES
STEP_ROWS = 8
CMP_STEP_SEQS = 2
BLOCKS_PER_PAGE = PAGE_SIZE // CMP_BLOCK


def _tokens_minor(cache):
    token_axis = 2
    return jnp.moveaxis(cache, token_axis, -1)


def _seq_spec(*tail):
    zeros = (0,) * len(tail)
    return pl.BlockSpec((None,) + tail, lambda i, pt: (i,) + zeros)


def _layer_page_specs(block_tail, layer, seqs_per_step=1):
    zeros = (0,) * len(block_tail)
    return [pl.BlockSpec((None, None) + block_tail,
                         lambda i, pt, s=s, p=p: (layer, pt[i * seqs_per_step + s, p]) + zeros)
            for s in range(seqs_per_step) for p in range(N_PAGES)]


def _layer_seq_spec(tail, layer):
    zeros = (0,) * len(tail)
    return pl.BlockSpec((None, None) + tail, lambda i, pt: (layer, i) + zeros)


def _topk_rows_kernel(keys_ref, sel_ref, *, topk, idx_bits, n_valid):
    keys = keys_ref[...]
    lane = lax.broadcasted_iota(jnp.int32, keys.shape, 1)
    count = lambda pred: jnp.sum(jnp.where(pred(keys, lane), 1.0, 0.0), axis=1, keepdims=True)
    thr, need = _topk_threshold(count, topk)
    cut = _tie_index_cut(count, keys.shape[0], thr, need, idx_bits)
    chosen = ((keys > thr) | ((keys == thr) & (lane <= cut))) & (lane < n_valid)
    sel_ref[...] = jnp.where(chosen, 1.0, 0.0)


TOKEN_PAIRS = CMP_BLOCK // 2
PAIR_ROWS = 2 * BLOCKS_PER_PAGE


def _pair_regroup_matrix():
    r = np.arange(TOKEN_PAIRS * PAIR_ROWS)
    k = np.arange(2 * PAGE_SIZE)
    tp, page, blk = r // PAIR_ROWS, (r // BLOCKS_PER_PAGE) % 2, r % BLOCKS_PER_PAGE
    page_k, blk_k, t_k = k // PAGE_SIZE, (k % PAGE_SIZE) // CMP_BLOCK, k % CMP_BLOCK
    hit = ((page[:, None] == page_k[None]) & (blk[:, None] == blk_k[None]) & (tp[:, None] == t_k[None] // 2))
    return jnp.asarray(hit, bf16)


def _nsa_compress_paged_kernel(pt_ref, pe_ref, regroup_ref, w1_ref, w2_ref, *rest):
    npg = CMP_STEP_SEQS * N_PAGES
    pages, o_ref, x_ref = rest[:npg], rest[npg], rest[npg + 1]
    blocks_per_seq = N_PAGES * BLOCKS_PER_PAGE
    even_token = (lax.broadcasted_iota(jnp.int32, (HEAD_DIM, 2 * PAGE_SIZE), 1) & 1) == 0
    regroup = regroup_ref[...]
    for pair in range(npg // 2):
        for gj in range(2 * KV_HEADS):
            g, j = gj // 2, gj % 2
            two = jnp.concatenate([pages[2 * pair][g, j] + pe_ref[j], pages[2 * pair + 1][g, j] + pe_ref[j]],
                                  axis=1)
            split = jnp.concatenate([jnp.where(even_token, two, 0.0), jnp.where(even_token, 0.0, two)],
                                    axis=0).astype(bf16)
            rows = lax.dot_general(regroup, split, _NT, preferred_element_type=f32)
            for tp in range(TOKEN_PAIRS):
                x_ref[gj, tp, pair * PAIR_ROWS:(pair + 1) * PAIR_ROWS, :] = rows[tp * PAIR_ROWS:(tp + 1) * PAIR_ROWS]
    for gj in range(2 * KV_HEADS):
        j = gj % 2
        x = jnp.concatenate([x_ref[gj, tp] for tp in range(TOKEN_PAIRS)], axis=1).astype(bf16)
        hid = jnp.dot(x, w1_ref[j], preferred_element_type=f32)
        out = jnp.dot(jax.nn.gelu(hid).astype(bf16), w2_ref[j], preferred_element_type=f32)
        for s in range(CMP_STEP_SEQS):
            o_ref[s, gj] = out[s * blocks_per_seq:(s + 1) * blocks_per_seq]


def nsa_compress_paged(pools_cmp, layer, page_table, cmp_pe, cmp_w1, cmp_w2):
    n = page_table.shape[0]
    blocks = N_PAGES * BLOCKS_PER_PAGE
    assert (CMP_STEP_SEQS * N_PAGES) % 2 == 0 and CMP_BLOCK % 2 == 0 and PAIR_ROWS == 8
    pe_page = jnp.tile(cmp_pe.transpose(1, 2, 0), (1, 1, BLOCKS_PER_PAGE))
    const = lambda shape: pl.BlockSpec(shape, lambda i, pt: (0,) * len(shape))
    return pl.pallas_call(
        _nsa_compress_paged_kernel,
        out_shape=jax.ShapeDtypeStruct((n, 2 * KV_HEADS, blocks, HEAD_DIM), f32),
        grid_spec=pltpu.PrefetchScalarGridSpec(
            num_scalar_prefetch=1, grid=(n // CMP_STEP_SEQS,),
            in_specs=[const((2, HEAD_DIM, PAGE_SIZE)), const((TOKEN_PAIRS * PAIR_ROWS, 2 * PAGE_SIZE)),
                      const((2, CMP_BLOCK * HEAD_DIM, CMP_HIDDEN)), const((2, CMP_HIDDEN, HEAD_DIM))]
            + _layer_page_specs((KV_HEADS, 2, HEAD_DIM, PAGE_SIZE), layer, CMP_STEP_SEQS),
            out_specs=pl.BlockSpec((CMP_STEP_SEQS, 2 * KV_HEADS, blocks, HEAD_DIM), lambda i, pt: (i, 0, 0, 0)),
            scratch_shapes=[pltpu.VMEM((2 * KV_HEADS, TOKEN_PAIRS, CMP_STEP_SEQS * blocks, 2 * HEAD_DIM), f32)]),
        compiler_params=pltpu.CompilerParams(
            dimension_semantics=("parallel",), vmem_limit_bytes=VMEM_LIMIT),
        name="nsa_compress_paged",
    )(page_table, pe_page, _pair_regroup_matrix(), cmp_w1.astype(bf16), cmp_w2.astype(bf16),
      *([pools_cmp] * (CMP_STEP_SEQS * N_PAGES)))


def _decode_attention(q8, blocks, masks, kv_new, new_ok, is_g0):
    qb = q8.astype(bf16)
    by_group = lambda fn: jnp.where(is_g0, fn(0), fn(1))
    s_new = by_group(lambda g: jnp.sum(q8 * kv_new[2 * g:2 * g + 1], axis=1, keepdims=True)) * ATT_SCALE
    if new_ok is not None:
        s_new = jnp.where(new_ok, s_new, MASKED)
    m = s_new
    scores = []
    for blk, msk in zip(blocks, masks):
        s = by_group(lambda g: jnp.dot(qb, blk[g, 0].astype(bf16), preferred_element_type=f32)) * ATT_SCALE
        s = jnp.where(msk > 0.5, s, MASKED)
        scores.append(s)
        m = jnp.maximum(m, jnp.max(s, axis=1, keepdims=True))
    p_new = jnp.exp(s_new - m)
    if new_ok is not None:
        p_new = jnp.where(new_ok, p_new, 0.0)
    l = p_new
    acc = p_new * by_group(lambda g: kv_new[2 * g + 1:2 * g + 2])
    for blk, msk, s in zip(blocks, masks, scores):
        p = jnp.where(msk > 0.5, jnp.exp(s - m), 0.0)
        l = l + jnp.sum(p, axis=1, keepdims=True)
        pb = p.astype(bf16)
        acc = acc + by_group(lambda g: lax.dot_general(pb, blk[g, 1].astype(bf16), _NT,
                                                       preferred_element_type=f32))
    return acc / l


def _dsa_decode_scores_kernel(pt_ref, iq_ref, iw_ref, iknew_ref, *rest):
    pages, keys_ref = rest[:N_PAGES], rest[N_PAGES]
    iq = iq_ref[...]
    iqb = iq.astype(bf16)
    iw = iw_ref[...] * IDX_HEADS ** -0.5
    weigh = lambda dots, w: jnp.sum(w * jnp.maximum(dots * IDX_DIM ** -0.5, 0.0), axis=0, keepdims=True)
    for p in range(N_PAGES):
        dots = jnp.dot(iqb, pages[p][...].astype(bf16), preferred_element_type=f32)
        keys_ref[:, p * PAGE_SIZE:(p + 1) * PAGE_SIZE] = _sortable_key(weigh(dots, iw))
    new_score = weigh(jnp.sum(iq * iknew_ref[...], axis=1, keepdims=True), iw[:, :1])
    lane = lax.broadcasted_iota(jnp.int32, (1, LANES), 1)
    keys_ref[:, PAST_LEN:] = jnp.where(lane == 0, _sortable_key(jnp.broadcast_to(new_score, (1, LANES))), INT_MIN)


def _dsa_decode_attend_kernel(pt_ref, q_ref, sel_ref, kvnew_ref, *rest):
    pages, o_ref = rest[:N_PAGES], rest[N_PAGES]
    masks = [sel_ref[:, p * PAGE_SIZE:(p + 1) * PAGE_SIZE] for p in range(N_PAGES)]
    new_ok = sel_ref[:, PAST_LEN:PAST_LEN + 1] > 0.5
    is_g0 = lax.broadcasted_iota(jnp.int32, (N_HEADS, 1), 0) < Q_PER_KV
    o_ref[...] = _decode_attention(q_ref[...], pages, masks, kvnew_ref[...], new_ok, is_g0)


def dsa_decode(dq, dkv, iq, ik, iw, pools_kv, pools_idx, layer, page_table):
    n = dq.shape[0]
    assert min(DSA_TOPK, (PAST_LEN + 1) // 4) == DSA_TOPK and PAGE_SIZE == LANES
    pad_heads = ((0, 0), (0, STEP_ROWS - IDX_HEADS), (0, 0))
    iq8 = jnp.pad(iq.reshape(n, IDX_HEADS, IDX_DIM), pad_heads)
    iw8 = jnp.broadcast_to(jnp.pad(iw.reshape(n, IDX_HEADS, 1), pad_heads), (n, STEP_ROWS, LANES))
    keys = pl.pallas_call(
        _dsa_decode_scores_kernel,
        out_shape=jax.ShapeDtypeStruct((n, 1, KEY_PAD), jnp.int32),
        grid_spec=pltpu.PrefetchScalarGridSpec(
            num_scalar_prefetch=1, grid=(n,),
            in_specs=[_seq_spec(STEP_ROWS, IDX_DIM), _seq_spec(STEP_ROWS, LANES), _seq_spec(1, IDX_DIM)]
            + _layer_page_specs((IDX_DIM, PAGE_SIZE), layer),
            out_specs=_seq_spec(1, KEY_PAD)),
        compiler_params=pltpu.CompilerParams(
            dimension_semantics=("parallel",), vmem_limit_bytes=VMEM_LIMIT),
        name="dsa_decode_scores",
    )(page_table, iq8, iw8, ik.reshape(n, 1, IDX_DIM), *([pools_idx] * N_PAGES))
    sel = pl.pallas_call(
        functools.partial(_topk_rows_kernel, topk=DSA_TOPK, idx_bits=int(math.ceil(math.log2(KEY_PAD))),
                          n_valid=PAST_LEN + 1),
        out_shape=jax.ShapeDtypeStruct((n, KEY_PAD), f32),
        compiler_params=pltpu.CompilerParams(vmem_limit_bytes=VMEM_LIMIT),
        name="dsa_decode_topk",
    )(keys.reshape(n, KEY_PAD))
    o8 = pl.pallas_call(
        _dsa_decode_attend_kernel,
        out_shape=jax.ShapeDtypeStruct((n, N_HEADS, HEAD_DIM), f32),
        grid_spec=pltpu.PrefetchScalarGridSpec(
            num_scalar_prefetch=1, grid=(n,),
            in_specs=[_seq_spec(N_HEADS, HEAD_DIM), _seq_spec(1, KEY_PAD), _seq_spec(2 * KV_HEADS, HEAD_DIM)]
            + _layer_page_specs((KV_HEADS, 2, HEAD_DIM, PAGE_SIZE), layer),
            out_specs=_seq_spec(N_HEADS, HEAD_DIM)),
        compiler_params=pltpu.CompilerParams(
            dimension_semantics=("parallel",), vmem_limit_bytes=VMEM_LIMIT),
        name="dsa_decode_attend",
    )(page_table, dq.reshape(n, N_HEADS, HEAD_DIM), sel.reshape(n, 1, KEY_PAD),
      dkv.reshape(n, 2 * KV_HEADS, HEAD_DIM), *([pools_kv] * N_PAGES))
    return o8.reshape(n, GROUP_WIDTH)


def _nsa_decode_kernel(pt_ref, q_ref, gate_ref, cmp_ref, slcnew_ref, win_ref, winnew_ref, *rest):
    pages, o_ref = rest[:N_PAGES], rest[N_PAGES]
    q8 = q_ref[...]
    qb = q8.astype(bf16)
    lane = lax.broadcasted_iota(jnp.int32, (N_HEADS, CMP_LANES), 1)
    head = lax.broadcasted_iota(jnp.int32, (N_HEADS, 1), 0)
    is_g0 = head < Q_PER_KV
    by_group = lambda fn: jnp.where(is_g0, fn(0), fn(1))
    q_pos = jnp.full((N_HEADS, 1), PAST_LEN, jnp.int32)
    blocks = cmp_ref.shape[1]
    s = by_group(lambda g: lax.dot_general(qb, cmp_ref[2 * g].astype(bf16), _NT,
                                           preferred_element_type=f32)) * ATT_SCALE
    lane_c = lane[:, :blocks]
    s = jnp.where((lane_c + 1) * CMP_BLOCK - 1 <= q_pos, s, -jnp.inf)
    m = jnp.max(s, axis=1, keepdims=True)
    m = jnp.where(m == -jnp.inf, 0.0, m)
    p = jnp.exp(s - m)
    p = p / jnp.maximum(jnp.sum(p, axis=1, keepdims=True), F32_TINY)
    pb = p.astype(bf16)
    o_cmp = by_group(lambda g: jnp.dot(pb, cmp_ref[2 * g + 1].astype(bf16), preferred_element_type=f32))
    imp = jnp.where(is_g0, jnp.sum(p[:Q_PER_KV], axis=0, keepdims=True),
                    jnp.sum(p[Q_PER_KV:], axis=0, keepdims=True))
    imp = jnp.concatenate([imp, jnp.zeros((N_HEADS, CMP_LANES - blocks), f32)], axis=1)
    sel_blk = _select_blocks(imp, q_pos, lane)
    ratio = SLC_BLOCK // CMP_BLOCK
    per_page = PAGE_SIZE // SLC_BLOCK
    masks = []
    for pg in range(N_PAGES):
        msk = sel_blk[:, ratio * per_page * pg:ratio * per_page * pg + 1]
        for j in range(1, per_page):
            b = per_page * pg + j
            msk = jnp.where(lane < j * SLC_BLOCK, msk, sel_blk[:, ratio * b:ratio * b + 1])
        masks.append(msk)
    o_slc = _decode_attention(q8, pages, masks, slcnew_ref[...], None, is_g0)
    wbuf = win_ref.shape[-1]
    slot = lax.broadcasted_iota(jnp.int32, (1, wbuf), 1)
    wmask = jnp.where(PAST_LEN - wbuf + slot > PAST_LEN - WINDOW, 1.0, 0.0)
    o_win = _decode_attention(q8, [win_ref], [wmask], winnew_ref[...], None, is_g0)
    gate = jax.nn.sigmoid(gate_ref[...])
    o_ref[...] = (gate[0][:, :HEAD_DIM] * o_cmp + gate[1][:, :HEAD_DIM] * o_slc
                  + gate[2][:, :HEAD_DIM] * o_win)


def nsa_decode(nq, ng, skv, wkv, pools_cmp, pools_slc, wins, layer, page_table, cmp_pe, cmp_w1, cmp_w2):
    n = nq.shape[0]
    blocks = N_PAGES * BLOCKS_PER_PAGE
    wbuf = wins.shape[-1]
    assert (PAST_LEN + 1) // CMP_BLOCK == blocks and blocks <= CMP_LANES
    assert PAST_LEN // SLC_BLOCK + 1 <= CMP_LANES // 2 and PAGE_SIZE == LANES
    cmp_rows = nsa_compress_paged(pools_cmp, layer, page_table, cmp_pe, cmp_w1, cmp_w2)
    gates =jnp.broadcast_to(ng.reshape(n, N_HEADS, 3).transpose(0, 2, 1)[..., None], (n, 3, N_HEADS, LANES))
    kv_tile = (KV_HEADS, 2, HEAD_DIM)
    o8 = pl.pallas_call(
        _nsa_decode_kernel,
        out_shape=jax.ShapeDtypeStruct((n, N_HEADS, HEAD_DIM), f32),
        grid_spec=pltpu.PrefetchScalarGridSpec(
            num_scalar_prefetch=1, grid=(n,),
            in_specs=[_seq_spec(N_HEADS, HEAD_DIM), _seq_spec(3, N_HEADS, LANES),
                      _seq_spec(2 * KV_HEADS, blocks, HEAD_DIM), _seq_spec(2 * KV_HEADS, HEAD_DIM),
                      _layer_seq_spec(kv_tile + (wbuf,), layer), _seq_spec(2 * KV_HEADS, HEAD_DIM)]
            + _layer_page_specs(kv_tile + (PAGE_SIZE,), layer),
            out_specs=_seq_spec(N_HEADS, HEAD_DIM)),
        compiler_params=pltpu.CompilerParams(
            dimension_semantics=("parallel",), vmem_limit_bytes=VMEM_LIMIT),
        name="nsa_decode",
    )(page_table, nq.reshape(n, N_HEADS, HEAD_DIM), gates, cmp_rows, skv.reshape(n, 2 * KV_HEADS, HEAD_DIM),
      wins, wkv.reshape(n, 2 * KV_HEADS, HEAD_DIM), *([pools_slc] * N_PAGES))
    return o8.reshape(n, GROUP_WIDTH)


def mix_prompt(cols, n, t, w, sw, cmp_weights):
    (dq, dkv, iq, ik, iw, u, nq, ckv, skv, wkv, ng, gq, gk, gv, ga, gr) = [
        c.reshape(n, t, c.shape[-1]) for c in cols]
    kv_shape = (n, t, KV_HEADS, 2, HEAD_DIM)
    o_dsa = dsa_prompt(dq, dkv, iq, ik, iw)
    o_s5, h_s5 = s5_prompt(u, jnp.zeros((n, 2, S5_WIDTH), f32), sw)
    o_nsa = nsa_prompt(nq, ng, ckv, skv, wkv, cmp_weights)
    o_gla, s_gla = gla_prompt(gq, gk, gv, ga, gr, w['gla_w_gate'], w['gla_b_gate'], w['gla_norm_g'])
    win_state = wkv[:, -min(WINDOW, t):].reshape((n, min(WINDOW, t)) + kv_shape[2:])
    mixers = [o.reshape(n * t, GROUP_WIDTH) for o in (o_dsa, o_s5, o_nsa, o_gla)]
    return mixers, (dkv.reshape(kv_shape), ik, ckv.reshape(kv_shape), skv.reshape(kv_shape), win_state,
                    h_s5.reshape(n, 2, S5_GROUPS, S5_STATE), s_gla)


def mix_sample(cols, w, sw, layer, pools, win_buf, h_s5, s_gla, page_table):
    (dq, dkv, iq, ik, iw, u, nq, ckv, skv, wkv, ng, gq, gk, gv, ga, gr) = cols
    n, t = dq.shape[0], 1
    kv_shape = (n, t, KV_HEADS, 2, HEAD_DIM)
    o_dsa = dsa_decode(dq, dkv, iq, ik, iw, pools['dsa_kv'], pools['dsa_idx'], layer, page_table)
    o_s5, h_new = s5_step(u, h_s5.reshape(n, 2 * S5_WIDTH), sw)
    o_nsa = nsa_decode(nq, ng, skv, wkv, pools['nsa_cmp'], pools['nsa_slc'], pools['nsa_win'], layer,
                       page_table, w['cmp_pe'], w['cmp_w1'], w['cmp_w2'])
    o_gla, s_new = gla_step(gq, gk, gv, ga, gr, w['gla_w_gate'], w['gla_b_gate'], w['gla_norm_g'], s_gla)
    win_new = jnp.concatenate([win_buf[:, 1:], wkv.reshape(kv_shape)], axis=1)
    return [o_dsa, o_s5, o_nsa, o_gla], (
        dkv.reshape(kv_shape), ik.reshape(n, t, IDX_DIM), ckv.reshape(kv_shape), skv.reshape(kv_shape),
        win_new, h_new.reshape(n, 2, S5_GROUPS, S5_STATE), s_new)


def kernel(x_prompt, x_sample, cache_dsa_kv, cache_dsa_idx, cache_nsa_cmp, cache_nsa_slc, cache_nsa_win, state_s5, state_gla, page_table, w_in, s5_a_re, s5_a_im, s5_b_re, s5_b_im, s5_c_re, s5_c_im, s5_d, s5_log_dt, s5_w_glu, s5_b_glu, nsa_cmp_pe, nsa_cmp_w1, nsa_cmp_w2, gla_w_gate, gla_b_gate, gla_norm_g, w_out, ln1_g, ln1_b, ffn_w_gate, ffn_w_up, ffn_w_down, ln2_g, ln2_b):
    np_, tp_ = x_prompt.shape[:2]
    ns_, ts_ = x_sample.shape[:2]
    assert ts_ == 1
    w_in_pad = pad_in_projection(w_in)
    w_out_b = w_out.astype(bf16)
    wg_b, wu_b, wd_b = ffn_w_gate.astype(bf16), ffn_w_up.astype(bf16), ffn_w_down.astype(bf16)
    hp = x_prompt.reshape(np_ * tp_, D_MODEL)
    hs = x_sample.reshape(ns_ * ts_, D_MODEL)
    outs_p = [[] for _ in range(7)]
    outs_s = [[] for _ in range(7)]
    pools = {'dsa_kv': _tokens_minor(cache_dsa_kv), 'dsa_idx': _tokens_minor(cache_dsa_idx),
             'nsa_cmp': _tokens_minor(cache_nsa_cmp), 'nsa_slc': _tokens_minor(cache_nsa_slc),
             'nsa_win': _tokens_minor(cache_nsa_win)}
    for l in range(DEPTH):
        w = {'a_re': s5_a_re[l], 'a_im': s5_a_im[l], 'b_re': s5_b_re[l], 'b_im': s5_b_im[l],
             'c_re': s5_c_re[l], 'c_im': s5_c_im[l], 'd': s5_d[l], 'log_dt': s5_log_dt[l],
             'w_glu': s5_w_glu[l], 'b_glu': s5_b_glu[l],
             'cmp_pe': nsa_cmp_pe[l], 'cmp_w1': nsa_cmp_w1[l], 'cmp_w2': nsa_cmp_w2[l],
             'gla_w_gate': gla_w_gate[l], 'gla_b_gate': gla_b_gate[l], 'gla_norm_g': gla_norm_g[l]}
        sw = s5_discretize(w)
        cmp_weights = nsa_compress_weights(nsa_cmp_pe[l], nsa_cmp_w1[l], nsa_cmp_w2[l])
        mixers_p, st_p = mix_prompt(in_projection(hp, w_in_pad, l), np_, tp_, w, sw, cmp_weights)
        mixers_s, st_s = mix_sample(in_projection(hs, w_in_pad, l), w, sw, l, pools, cache_nsa_win[l],
                                    state_s5[l], state_gla[l], page_table)
        hp = outproj_ln(hp, mixers_p, w_out_b, l, ln1_g[l], ln1_b[l])
        hs = outproj_ln(hs, mixers_s, w_out_b, l, ln1_g[l], ln1_b[l])
        hp = ffn_ln(hp, wg_b, wu_b, wd_b, l, ln2_g[l], ln2_b[l])
        hs = ffn_ln(hs, wg_b, wu_b, wd_b, l, ln2_g[l], ln2_b[l])
        for lst, st in zip(outs_p, st_p):
            lst.append(st)
        for lst, st in zip(outs_s, st_s):
            lst.append(st)
    dsa_kv_p, dsa_idx_p, nsa_cmp_p, nsa_slc_p, nsa_win_p, s5_p, gla_p = [jnp.stack(a) for a in outs_p]
    dsa_kv_s, dsa_idx_s, nsa_cmp_s, nsa_slc_s, nsa_win_s, s5_s, gla_s = [jnp.stack(a) for a in outs_s]
    return (hp.reshape(np_, tp_, D_MODEL), hs.reshape(ns_, ts_, D_MODEL),
            dsa_kv_p, dsa_kv_s, dsa_idx_p, dsa_idx_s, nsa_cmp_p, nsa_cmp_s,
            nsa_slc_p, nsa_slc_s, nsa_win_p, nsa_win_s, s5_p, s5_s, gla_p, gla_s)
```

```python
import functools
import math

import jax
import jax.numpy as jnp
from jax import lax
import numpy as np
from jax.experimental import pallas as pl
from jax.experimental.pallas import tpu as pltpu

D_MODEL = 2048
DEPTH = 2
PAST_LEN = 2048
PAGE_SIZE = 128
GROUP_WIDTH = 512
HEAD_DIM = 64
N_HEADS = 8
KV_HEADS = 2
Q_PER_KV = 4
KV_ROW = 256
IDX_HEADS = 4
IDX_DIM = 64
DSA_TOPK = 256
S5_GROUPS = 32
S5_STATE = 64
S5_WIDTH = S5_GROUPS * S5_STATE
CMP_BLOCK = 32
CMP_HIDDEN = 128
SLC_BLOCK = 64
SLC_TOPN = 16
WINDOW = 512
QBLOCK = 128
GLA_HEADS = 4
GLA_DK = 64
GLA_DV = 128
GLA_LOWRANK = 16
GLA_TAU = 16.0
GLA_CHUNK = 64
DEEPNORM_ALPHA = (2 * DEPTH) ** 0.25
LN_EPS = 1e-5

IN_SIZES = (512, 256, 256, 64, 4, 512, 512, 256, 256, 256, 24, 256, 256, 512, 16, 512)
D_IN = sum(IN_SIZES)

VMEM_LIMIT = 56 * 1024 * 1024
LANES = 128
KEY_CHUNK = 512
CMP_LANES = 128
S5_TIME_CHUNK = 256
ATT_SCALE = HEAD_DIM ** -0.5
MASKED = -1e30
INT_MIN = -2 ** 31
F32_TINY = float(np.finfo(np.float32).tiny)

bf16 = jnp.bfloat16
f32 = jnp.float32


IN_PAD_SIZES = tuple(-(-s // LANES) * LANES for s in IN_SIZES)
IN_PAD_OFFSETS = tuple(int(o) for o in np.cumsum((0,) + IN_PAD_SIZES[:-1]))


def pad_in_projection(w_in):
    parts, off = [], 0
    for size, size_pad in zip(IN_SIZES, IN_PAD_SIZES):
        parts.append(jnp.pad(w_in[:, :, off:off + size], ((0, 0), (0, 0), (0, size_pad - size))))
        off += size
    return jnp.concatenate(parts, axis=-1).astype(bf16)


def _in_projection_kernel(x_ref, w_ref, *o_refs):
    xb = x_ref[...].astype(bf16)
    for o_ref, off, size, size_pad in zip(o_refs, IN_PAD_OFFSETS, IN_SIZES, IN_PAD_SIZES):
        y = jnp.dot(xb, w_ref[:, off:off + size_pad], preferred_element_type=f32)
        o_ref[...] = y[:, :size]


def in_projection(x, w_pad, layer):
    m, k = x.shape
    tm = min(256, m)
    return pl.pallas_call(
        _in_projection_kernel,
        out_shape=tuple(jax.ShapeDtypeStruct((m, s), f32) for s in IN_SIZES),
        grid=(m // tm,),
        in_specs=[pl.BlockSpec((tm, k), lambda i: (i, 0)),
                  pl.BlockSpec((None, k, w_pad.shape[2]), lambda i: (layer, 0, 0),
                               pipeline_mode=pl.Buffered(1))],
        out_specs=tuple(pl.BlockSpec((tm, s), lambda i: (i, 0)) for s in IN_SIZES),
        compiler_params=pltpu.CompilerParams(
            dimension_semantics=("parallel",), vmem_limit_bytes=VMEM_LIMIT),
        name="in_projection",
    )(x, w_pad)


def _layer_norm_rows(z, g, b):
    mu = jnp.mean(z, axis=-1, keepdims=True)
    zc = z - mu
    var = jnp.mean(zc * zc, axis=-1, keepdims=True)
    return zc * lax.rsqrt(var + LN_EPS) * g + b


def _outproj_ln_kernel(x_ref, m0_ref, m1_ref, m2_ref, m3_ref, w_ref, g_ref, b_ref, o_ref):
    y = DEEPNORM_ALPHA * x_ref[...]
    for j, m_ref in enumerate((m0_ref, m1_ref, m2_ref, m3_ref)):
        y = y + jnp.dot(m_ref[...].astype(bf16), w_ref[j * GROUP_WIDTH:(j + 1) * GROUP_WIDTH, :],
                        preferred_element_type=f32)
    o_ref[...] = _layer_norm_rows(y, g_ref[...], b_ref[...])


def outproj_ln(x, mixers, w_bf16, layer, g, b):
    m, d = x.shape
    tm = min(256, m)
    mix_spec = pl.BlockSpec((tm, GROUP_WIDTH), lambda i: (i, 0))
    return pl.pallas_call(
        _outproj_ln_kernel,
        out_shape=jax.ShapeDtypeStruct((m, d), f32),
        grid=(m // tm,),
        in_specs=[pl.BlockSpec((tm, d), lambda i: (i, 0)),
                  mix_spec, mix_spec, mix_spec, mix_spec,
                  pl.BlockSpec((None, d, d), lambda i: (layer, 0, 0)),
                  pl.BlockSpec((1, d), lambda i: (0, 0)),
                  pl.BlockSpec((1, d), lambda i: (0, 0))],
        out_specs=pl.BlockSpec((tm, d), lambda i: (i, 0)),
        compiler_params=pltpu.CompilerParams(
            dimension_semantics=("parallel",), vmem_limit_bytes=VMEM_LIMIT),
        name="outproj_ln",
    )(x, *mixers, w_bf16, g.reshape(1, d), b.reshape(1, d))


def _ffn_ln_kernel(h_ref, wg_ref, wu_ref, wd_ref, g_ref, b_ref, o_ref, acc_ref):
    f = pl.program_id(1)

    @pl.when(f == 0)
    def _():
        acc_ref[...] = jnp.zeros_like(acc_ref)

    hb = h_ref[...].astype(bf16)
    a = jnp.dot(hb, wg_ref[...], preferred_element_type=f32)
    u = jnp.dot(hb, wu_ref[...], preferred_element_type=f32)
    act = (a * jax.nn.sigmoid(a) * u).astype(bf16)
    acc_ref[...] += jnp.dot(act, wd_ref[...], preferred_element_type=f32)

    @pl.when(f == pl.num_programs(1) - 1)
    def _():
        z = DEEPNORM_ALPHA * h_ref[...] + acc_ref[...]
        o_ref[...] = _layer_norm_rows(z, g_ref[...], b_ref[...])


def ffn_ln(h, wg, wu, wd, layer, g, b, tf=512):
    m, d = h.shape
    fh = wg.shape[2]
    tm = min(512, m)
    return pl.pallas_call(
        _ffn_ln_kernel,
        out_shape=jax.ShapeDtypeStruct((m, d), f32),
        grid=(m // tm, fh // tf),
        in_specs=[pl.BlockSpec((tm, d), lambda i, f: (i, 0)),
                  pl.BlockSpec((None, d, tf), lambda i, f: (layer, 0, f)),
                  pl.BlockSpec((None, d, tf), lambda i, f: (layer, 0, f)),
                  pl.BlockSpec((None, tf, d), lambda i, f: (layer, f, 0)),
                  pl.BlockSpec((1, d), lambda i, f: (0, 0)),
                  pl.BlockSpec((1, d), lambda i, f: (0, 0))],
        out_specs=pl.BlockSpec((tm, d), lambda i, f: (i, 0)),
        scratch_shapes=[pltpu.VMEM((tm, d), f32)],
        compiler_params=pltpu.CompilerParams(
            dimension_semantics=("parallel", "arbitrary"), vmem_limit_bytes=VMEM_LIMIT),
        name="ffn_ln",
    )(h, wg, wu, wd, g.reshape(1, d), b.reshape(1, d))


def _tile_rows(x, reps):
    return jnp.concatenate([x] * reps, axis=0)


def _scaled_queries(q):
    assert math.log2(HEAD_DIM) % 2 == 0
    return (q * ATT_SCALE).astype(bf16)


def _masked_flash(qg, kt_at, v_at, c_lo, c_hi, bias_at):
    rows = qg.shape[0]

    def body(c, carry):
        m, l, acc = carry
        s = jnp.dot(qg, kt_at(c), preferred_element_type=f32)
        bias = bias_at(c)
        s = jnp.concatenate([s[r * QBLOCK:(r + 1) * QBLOCK] + bias for r in range(rows // QBLOCK)], axis=0)
        m_new = jnp.maximum(m, jnp.max(s, axis=1, keepdims=True))
        p = jnp.exp(s - m_new)
        alpha = jnp.exp(m - m_new)
        l = alpha * l + jnp.sum(p, axis=1, keepdims=True)
        acc = alpha * acc + jnp.dot(p.astype(bf16), v_at(c), preferred_element_type=f32)
        return m_new, l, acc

    init = (jnp.full((rows, 1), MASKED, f32), jnp.zeros((rows, 1), f32),
            jnp.zeros((rows, HEAD_DIM), f32))
    _, l, acc = lax.fori_loop(c_lo, c_hi, body, init)
    return acc / l


def _sortable_key(x):
    bits = pltpu.bitcast(x, jnp.int32)
    return bits ^ (jnp.right_shift(bits, 31) & jnp.int32(0x7FFFFFFF))


def _topk_threshold(count, topk):
    topk = float(topk)
    cnt_nonneg = count(lambda k, i: k >= 0)
    thr0 = jnp.where(cnt_nonneg >= topk, 0, INT_MIN).astype(jnp.int32)

    def bit_body(b, thr):
        cand = thr + jnp.left_shift(jnp.int32(1), 30 - b)
        return jnp.where(count(lambda k, i: k >= cand) >= topk, cand, thr)

    thr = lax.fori_loop(0, 31, bit_body, thr0)
    need = topk - count(lambda k, i: k > thr)
    return thr, need


def _tie_index_cut(count, rows, thr, need, idx_bits):
    def cut_body(b, cut):
        cand = cut + jnp.left_shift(jnp.int32(1), idx_bits - 1 - b)
        return jnp.where(count(lambda k, i: (k == thr) & (i < cand)) < need, cand, cut)

    return lax.fori_loop(0, idx_bits, cut_body, jnp.zeros((rows, 1), jnp.int32))


def _select_blocks(imp, q_pos, lane):
    ratio = SLC_BLOCK // CMP_BLOCK
    on_block_lane = (lane & (ratio - 1)) == 0
    blk = jnp.right_shift(lane, 1)
    cur = jnp.right_shift(q_pos, 6)
    future = blk > cur
    forced = (blk == 0) | (blk == cur) | (blk == cur - 1)
    imp = imp + pltpu.roll(imp, CMP_LANES - 1, 1)
    score = jnp.where(future, -jnp.inf, jnp.where(forced, jnp.inf, imp))
    score = jnp.where(on_block_lane, score, -jnp.inf)
    if imp.shape[0] == CMP_LANES:
        score_t = score.T
        blk_t = lax.broadcasted_iota(jnp.int32, score_t.shape, 0)
        rank_t = jnp.zeros(score_t.shape, f32)
        for b in range(CMP_LANES // ratio):
            other = score_t[ratio * b:ratio * b + 1, :]
            ahead = (other > score_t) | ((other == score_t) & (ratio * b < blk_t))
            rank_t = rank_t + jnp.where(ahead, 1.0, 0.0)
        rank = rank_t.T
    else:
        rank = jnp.zeros(imp.shape, f32)
        for b in range(CMP_LANES // ratio):
            col = score[:, ratio * b:ratio * b + 1]
            ahead = (col > score) | ((col == score) & (ratio * b < lane))
            rank = rank + jnp.where(ahead, 1.0, 0.0)
    return jnp.where((rank < float(SLC_TOPN)) & on_block_lane, 1.0, 0.0)


def _dsa_prompt_kernel(iq_ref, iw_ref, ikt_ref, q_ref, kt_ref, v_ref, tri_ref, o_ref, keys_ref, sel_ref):
    i = pl.program_id(1)
    nch = (i * QBLOCK + QBLOCK + KEY_CHUNK - 1) // KEY_CHUNK
    q_pos = i * QBLOCK + lax.broadcasted_iota(jnp.int32, (QBLOCK, 1), 0)
    lane = lax.broadcasted_iota(jnp.int32, (QBLOCK, KEY_CHUNK), 1)

    iq = iq_ref[...]
    iq_stack = jnp.concatenate(
        [iq[:, h * IDX_DIM:(h + 1) * IDX_DIM] for h in range(IDX_HEADS)], axis=0).astype(bf16)
    iw = iw_ref[...] * IDX_HEADS ** -0.5

    def score_body(c, carry):
        dots = jnp.dot(iq_stack, ikt_ref[c], preferred_element_type=f32) * IDX_DIM ** -0.5
        acc = jnp.zeros((QBLOCK, KEY_CHUNK), f32)
        for h in range(IDX_HEADS):
            acc = acc + iw[:, h:h + 1] * jnp.maximum(dots[h * QBLOCK:(h + 1) * QBLOCK], 0.0)
        causal = c * KEY_CHUNK + lane <= q_pos
        keys_ref[c] = jnp.where(causal, _sortable_key(acc), INT_MIN)
        return carry

    lax.fori_loop(0, nch, score_body, 0)

    def count(pred):
        def body(c, acc):
            hit = jnp.where(pred(keys_ref[c], c * KEY_CHUNK + lane), 1.0, 0.0)
            part = hit[:, :LANES]
            for j in range(1, KEY_CHUNK // LANES):
                part = part + hit[:, j * LANES:(j + 1) * LANES]
            return acc + part
        acc = lax.fori_loop(0, nch, body, jnp.zeros((QBLOCK, LANES), f32))
        return jnp.sum(acc, axis=1, keepdims=True)

    thr, need = _topk_threshold(count, DSA_TOPK)

    def sel_body(c, ties_before):
        k = keys_ref[c]
        tie = k == thr
        tie_rank = ties_before + jnp.dot(jnp.where(tie, 1.0, 0.0).astype(bf16), tri_ref[...],
                                         preferred_element_type=f32)
        chosen = ((k > thr) | (tie & (tie_rank <= need))) & (c * KEY_CHUNK + lane <= q_pos)
        sel_ref[c] = jnp.where(chosen, 0.0, MASKED)
        return tie_rank[:, KEY_CHUNK - 1:KEY_CHUNK]

    lax.fori_loop(0, nch, sel_body, jnp.zeros((QBLOCK, 1), f32))

    for g in range(KV_HEADS):
        o = _masked_flash(_scaled_queries(q_ref[g]), lambda c: kt_ref[g, c], lambda c: v_ref[g, c],
                          0, nch, lambda c: sel_ref[c])
        for r in range(Q_PER_KV):
            h = g * Q_PER_KV + r
            o_ref[:, h * HEAD_DIM:(h + 1) * HEAD_DIM] = o[r * QBLOCK:(r + 1) * QBLOCK]


def _stack_query_heads(q, n, t):
    q = q.reshape(n, t // QBLOCK, QBLOCK, KV_HEADS, Q_PER_KV, HEAD_DIM)
    q = q.transpose(0, 3, 1, 4, 2, 5)
    return q.reshape(n, KV_HEADS, t // QBLOCK, Q_PER_KV * QBLOCK, HEAD_DIM)


def _chunked_kv(kv, n, t, chunk):
    kv = kv.reshape(n, t // chunk, chunk, KV_HEADS, 2, HEAD_DIM).astype(bf16)
    kt = kv[:, :, :, :, 0, :].transpose(0, 3, 1, 4, 2)
    v = kv[:, :, :, :, 1, :].transpose(0, 3, 1, 2, 4)
    return kt, v


def dsa_prompt(dq, dkv, iq, ik, iw):
    n, t = dq.shape[:2]
    nchunks = t // KEY_CHUNK
    assert t % KEY_CHUNK == 0 and min(DSA_TOPK, t // 4) == DSA_TOPK
    ikt = ik.reshape(n, nchunks, KEY_CHUNK, IDX_DIM).transpose(0, 1, 3, 2).astype(bf16)
    qs = _stack_query_heads(dq, n, t)
    kt, v = _chunked_kv(dkv, n, t, KEY_CHUNK)
    at = jnp.arange(KEY_CHUNK)
    tri = (at[:, None] <= at[None, :]).astype(bf16)
    return pl.pallas_call(
        _dsa_prompt_kernel,
        out_shape=jax.ShapeDtypeStruct((n, t, GROUP_WIDTH), f32),
        grid=(n, t // QBLOCK),
        in_specs=[
            pl.BlockSpec((None, QBLOCK, IDX_HEADS * IDX_DIM), lambda b, i: (b, i, 0)),
            pl.BlockSpec((None, QBLOCK, IDX_HEADS), lambda b, i: (b, i, 0)),
            pl.BlockSpec((None, nchunks, IDX_DIM, KEY_CHUNK), lambda b, i: (b, 0, 0, 0)),
            pl.BlockSpec((None, KV_HEADS, None, Q_PER_KV * QBLOCK, HEAD_DIM), lambda b, i: (b, 0, i, 0, 0)),
            pl.BlockSpec((None, KV_HEADS, nchunks, HEAD_DIM, KEY_CHUNK), lambda b, i: (b, 0, 0, 0, 0)),
            pl.BlockSpec((None, KV_HEADS, nchunks, KEY_CHUNK, HEAD_DIM), lambda b, i: (b, 0, 0, 0, 0)),
            pl.BlockSpec((KEY_CHUNK, KEY_CHUNK), lambda b, i: (0, 0)),
        ],
        out_specs=pl.BlockSpec((None, QBLOCK, GROUP_WIDTH), lambda b, i: (b, i, 0)),
        scratch_shapes=[pltpu.VMEM((nchunks, QBLOCK, KEY_CHUNK), jnp.int32),
                        pltpu.VMEM((nchunks, QBLOCK, KEY_CHUNK), f32)],
        compiler_params=pltpu.CompilerParams(
            dimension_semantics=("parallel", "arbitrary"), vmem_limit_bytes=VMEM_LIMIT),
        name="dsa_prompt",
    )(iq, iw, ikt, qs, kt, v, tri)


def _nsa_compress_kernel(x_ref, pe_ref, w1_ref, w2_ref, o_ref, hid_ref):
    k = pl.program_id(0)

    @pl.when(k == 0)
    def _():
        hid_ref[...] = jnp.zeros_like(hid_ref)

    hid_ref[...] += jnp.dot((x_ref[...] + pe_ref[...]).astype(bf16), w1_ref[...],
                            preferred_element_type=f32)

    @pl.when(k == pl.num_programs(0) - 1)
    def _():
        hid = jax.nn.gelu(hid_ref[...]).astype(bf16)
        o_ref[...] = jnp.dot(hid, w2_ref[...], preferred_element_type=f32)


def nsa_compress_weights(cmp_pe, cmp_w1, cmp_w2):
    w1 = cmp_w1.reshape(2, CMP_BLOCK, HEAD_DIM, CMP_HIDDEN)
    eye_g = jnp.eye(KV_HEADS, dtype=cmp_w1.dtype)
    eye_j = jnp.eye(2, dtype=cmp_w1.dtype)
    w1_big = jnp.einsum('jtde,gh,jk->tgjdhke', w1, eye_g, eye_j)
    w1_big = w1_big.reshape(CMP_BLOCK * KV_ROW, KV_HEADS * 2 * CMP_HIDDEN).astype(bf16)
    w2_big = jnp.einsum('jed,gh,jk->gjehkd', cmp_w2, eye_g, eye_j)
    w2_big = w2_big.reshape(KV_HEADS * 2 * CMP_HIDDEN, KV_ROW).astype(bf16)
    pe_row = jnp.broadcast_to(cmp_pe[:, None], (CMP_BLOCK, KV_HEADS, 2, HEAD_DIM))
    return pe_row.reshape(1, CMP_BLOCK * KV_ROW), w1_big, w2_big


def nsa_compress_rows(x, pe_row, w1_big, w2_big, tk=2048):
    m, kdim = x.shape
    hw = w1_big.shape[1]
    return pl.pallas_call(
        _nsa_compress_kernel,
        out_shape=jax.ShapeDtypeStruct((m, KV_ROW), f32),
        grid=(kdim // tk,),
        in_specs=[pl.BlockSpec((m, tk), lambda k: (0, k)),
                  pl.BlockSpec((1, tk), lambda k: (0, k)),
                  pl.BlockSpec((tk, hw), lambda k: (k, 0)),
                  pl.BlockSpec((hw, KV_ROW), lambda k: (0, 0))],
        out_specs=pl.BlockSpec((m, KV_ROW), lambda k: (0, 0)),
        scratch_shapes=[pltpu.VMEM((m, hw), f32)],
        compiler_params=pltpu.CompilerParams(
            dimension_semantics=("arbitrary",), vmem_limit_bytes=VMEM_LIMIT),
        name="nsa_compress",
    )(x, pe_row, w1_big, w2_big)


def _nsa_prompt_kernel(gate_ref, q_ref, ckt_ref, cv_ref, skt_ref, sv_ref, wkt_ref, wv_ref, o_ref):
    i = pl.program_id(1)
    nch = (i * QBLOCK + QBLOCK + KEY_CHUNK - 1) // KEY_CHUNK
    q_pos = i * QBLOCK + lax.broadcasted_iota(jnp.int32, (QBLOCK, 1), 0)
    lane = lax.broadcasted_iota(jnp.int32, (QBLOCK, CMP_LANES), 1)
    lane_k = lax.broadcasted_iota(jnp.int32, (QBLOCK, KEY_CHUNK), 1)
    expand_row = lax.broadcasted_iota(jnp.int32, (CMP_LANES, KEY_CHUNK), 0)
    expand_col = lax.broadcasted_iota(jnp.int32, (CMP_LANES, KEY_CHUNK), 1)
    gate = jax.nn.sigmoid(gate_ref[...])
    ratio = SLC_BLOCK // CMP_BLOCK
    c_vis = _tile_rows(jnp.where((lane + 1) * CMP_BLOCK - 1 <= q_pos, 1.0, 0.0), Q_PER_KV) > 0.5
    w_tiles = WINDOW // QBLOCK + 1
    w_lo = jnp.maximum(i - WINDOW // QBLOCK, 0)
    w_idx = w_lo * QBLOCK + lax.broadcasted_iota(jnp.int32, (QBLOCK, w_tiles * QBLOCK), 1)
    w_bias = _tile_rows(jnp.where((w_idx <= q_pos) & (w_idx > q_pos - WINDOW), 0.0, MASKED), Q_PER_KV)

    for g in range(KV_HEADS):
        qg = _scaled_queries(q_ref[g])
        s = jnp.dot(qg, ckt_ref[g], preferred_element_type=f32)
        s = jnp.where(c_vis, s, -jnp.inf)
        m = jnp.max(s, axis=1, keepdims=True)
        m = jnp.where(m == -jnp.inf, 0.0, m)
        p = jnp.exp(s - m)
        p = p / jnp.maximum(jnp.sum(p, axis=1, keepdims=True), F32_TINY)
        o_cmp = jnp.dot(p.astype(bf16), cv_ref[g], preferred_element_type=f32)
        imp = p[0:QBLOCK]
        for r in range(1, Q_PER_KV):
            imp = imp + p[r * QBLOCK:(r + 1) * QBLOCK]
        sel_blk = _select_blocks(imp, q_pos, lane).astype(bf16)

        def slc_mask(c):
            kidx = c * KEY_CHUNK + lane_k
            expand = jnp.where(
                expand_row == ratio * jnp.right_shift(c * KEY_CHUNK + expand_col, 6), 1.0, 0.0).astype(bf16)
            picked = jnp.dot(sel_blk, expand, preferred_element_type=f32)
            return jnp.where((picked > 0.5) & (kidx <= q_pos), 0.0, MASKED)

        o_slc = _masked_flash(qg, lambda c: skt_ref[g, c], lambda c: sv_ref[g, c], 0, nch, slc_mask)

        w_kt = jnp.concatenate([wkt_ref[g, w_lo + j] for j in range(w_tiles)], axis=1)
        w_v = jnp.concatenate([wv_ref[g, w_lo + j] for j in range(w_tiles)], axis=0)
        s = jnp.dot(qg, w_kt, preferred_element_type=f32) + w_bias
        pw = jnp.exp(s - jnp.max(s, axis=1, keepdims=True))
        o_win = (jnp.dot(pw.astype(bf16), w_v, preferred_element_type=f32)
                 / jnp.sum(pw, axis=1, keepdims=True))
        for r in range(Q_PER_KV):
            h = g * Q_PER_KV + r
            rows = slice(r * QBLOCK, (r + 1) * QBLOCK)
            o_ref[:, h * HEAD_DIM:(h + 1) * HEAD_DIM] = (
                gate[:, 3 * h:3 * h + 1] * o_cmp[rows]
                + gate[:, 3 * h + 1:3 * h + 2] * o_slc[rows]
                + gate[:, 3 * h + 2:3 * h + 3] * o_win[rows])


def nsa_prompt(nq, ng, ckv, skv, wkv, cmp_weights):
    n, t = nq.shape[:2]
    nc = t // CMP_BLOCK
    assert nc <= CMP_LANES and t % KEY_CHUNK == 0 and SLC_BLOCK == 64 and SLC_BLOCK // CMP_BLOCK == 2
    assert t >= WINDOW + QBLOCK and WINDOW % QBLOCK == 0
    cmp_rows = nsa_compress_rows(ckv.reshape(n * nc, CMP_BLOCK * KV_ROW), *cmp_weights)
    cmp_rows = cmp_rows.reshape(n, nc, KV_HEADS, 2, HEAD_DIM)
    cmp_rows = jnp.pad(cmp_rows, ((0, 0), (0, CMP_LANES - nc), (0, 0), (0, 0), (0, 0))).astype(bf16)
    ckt = cmp_rows[:, :, :, 0, :].transpose(0, 2, 3, 1)
    cv = cmp_rows[:, :, :, 1, :].transpose(0, 2, 1, 3)
    qs = _stack_query_heads(nq, n, t)
    skt, sv = _chunked_kv(skv, n, t, KEY_CHUNK)
    wkt, wv = _chunked_kv(wkv, n, t, QBLOCK)
    nk, nw = t // KEY_CHUNK, t // QBLOCK
    return pl.pallas_call(
        _nsa_prompt_kernel,
        out_shape=jax.ShapeDtypeStruct((n, t, GROUP_WIDTH), f32),
        grid=(n, t // QBLOCK),
        in_specs=[
            pl.BlockSpec((None, QBLOCK, 3 * N_HEADS), lambda b, i: (b, i, 0)),
            pl.BlockSpec((None, KV_HEADS, None, Q_PER_KV * QBLOCK, HEAD_DIM), lambda b, i: (b, 0, i, 0, 0)),
            pl.BlockSpec((None, KV_HEADS, HEAD_DIM, CMP_LANES), lambda b, i: (b, 0, 0, 0)),
            pl.BlockSpec((None, KV_HEADS, CMP_LANES, HEAD_DIM), lambda b, i: (b, 0, 0, 0)),
            pl.BlockSpec((None, KV_HEADS, nk, HEAD_DIM, KEY_CHUNK), lambda b, i: (b, 0, 0, 0, 0)),
            pl.BlockSpec((None, KV_HEADS, nk, KEY_CHUNK, HEAD_DIM), lambda b, i: (b, 0, 0, 0, 0)),
            pl.BlockSpec((None, KV_HEADS, nw, HEAD_DIM, QBLOCK), lambda b, i: (b, 0, 0, 0, 0)),
            pl.BlockSpec((None, KV_HEADS, nw, QBLOCK, HEAD_DIM), lambda b, i: (b, 0, 0, 0, 0)),
        ],
        out_specs=pl.BlockSpec((None, QBLOCK, GROUP_WIDTH), lambda b, i: (b, i, 0)),
        compiler_params=pltpu.CompilerParams(
            dimension_semantics=("parallel", "arbitrary"), vmem_limit_bytes=VMEM_LIMIT),
        name="nsa_prompt",
    )(ng, qs, ckt, cv, skt, sv, wkt, wv)


def s5_discretize(w):
    a_re, a_im = w['a_re'], w['a_im']
    dt = jnp.exp(w['log_dt'])[:, None]
    mag = jnp.exp(a_re * dt)
    ab_re, ab_im = mag * jnp.cos(a_im * dt), mag * jnp.sin(a_im * dt)
    den = a_re * a_re + a_im * a_im
    nr, ni = ab_re - 1.0, ab_im
    f_re = (nr * a_re + ni * a_im) / den
    f_im = (ni * a_re - nr * a_im) / den
    b_re, b_im = w['b_re'], w['b_im']
    bb_re = f_re[..., None] * b_re - f_im[..., None] * b_im
    bb_im = f_re[..., None] * b_im + f_im[..., None] * b_re
    eye = jnp.eye(S5_GROUPS, dtype=f32)

    def in_map(bb):
        return jnp.einsum('gsc,gh->gchs', bb, eye).reshape(GROUP_WIDTH, S5_WIDTH).astype(bf16)

    def out_map(cc):
        return jnp.einsum('gcs,gh->gshc', cc, eye).reshape(S5_WIDTH, GROUP_WIDTH).astype(bf16)

    return dict(a_re=ab_re.reshape(1, S5_WIDTH), a_im=ab_im.reshape(1, S5_WIDTH),
                b_re=in_map(bb_re), b_im=in_map(bb_im),
                c_re=out_map(w['c_re']), c_im=out_map(w['c_im']),
                d=w['d'].reshape(1, GROUP_WIDTH), w_glu=w['w_glu'].astype(bf16),
                b_glu=w['b_glu'].reshape(1, GROUP_WIDTH))


def _s5_prompt_kernel(u_ref, h0_ref, are_ref, aim_ref, bre_ref, bim_ref, cre_ref, cim_ref, d_ref,
                      wglu_ref, bglu_ref, o_ref, hlast_ref, state_ref, bure_ref, buim_ref, hre_ref, him_ref):
    j = pl.program_id(1)

    @pl.when(j == 0)
    def _():
        state_ref[...] = h0_ref[...]

    u = u_ref[...]
    ub = u.astype(bf16)
    for half in range(2):
        ch = slice(half * GROUP_WIDTH // 2, (half + 1) * GROUP_WIDTH // 2)
        st = slice(half * S5_WIDTH // 2, (half + 1) * S5_WIDTH // 2)
        bure_ref[:, st] = jnp.dot(ub[:, ch], bre_ref[ch, st], preferred_element_type=f32)
        buim_ref[:, st] = jnp.dot(ub[:, ch], bim_ref[ch, st], preferred_element_type=f32)
    a_re, a_im = are_ref[...], aim_ref[...]

    def step(t, carry):
        h_re, h_im = carry
        row = pl.ds(t, 1)
        n_re = a_re * h_re - a_im * h_im + bure_ref[row, :]
        n_im = a_re * h_im + a_im * h_re + buim_ref[row, :]
        hre_ref[row, :] = n_re
        him_ref[row, :] = n_im
        return n_re, n_im

    h_re, h_im = lax.fori_loop(0, u.shape[0], step, (state_ref[0:1, :], state_ref[1:2, :]), unroll=8)
    state_ref[0:1, :] = h_re
    state_ref[1:2, :] = h_im
    hlast_ref[...] = state_ref[...]
    ys = []
    for half in range(2):
        ch = slice(half * GROUP_WIDTH // 2, (half + 1) * GROUP_WIDTH // 2)
        st = slice(half * S5_WIDTH // 2, (half + 1) * S5_WIDTH // 2)
        ys.append(jnp.dot(hre_ref[:, st].astype(bf16), cre_ref[st, ch], preferred_element_type=f32)
                  - jnp.dot(him_ref[:, st].astype(bf16), cim_ref[st, ch], preferred_element_type=f32))
    y = jnp.concatenate(ys, axis=1) + d_ref[...] * u
    y = jax.nn.gelu(y)
    o_ref[...] = y * jax.nn.sigmoid(
        jnp.dot(y.astype(bf16), wglu_ref[...], preferred_element_type=f32) + bglu_ref[...])


def s5_prompt(u, h0, sw):
    n, t = u.shape[:2]
    tc = min(S5_TIME_CHUNK, t)
    const = lambda shape: pl.BlockSpec(shape, lambda b, j: (0,) * len(shape))
    return pl.pallas_call(
        _s5_prompt_kernel,
        out_shape=(jax.ShapeDtypeStruct((n, t, GROUP_WIDTH), f32),
                   jax.ShapeDtypeStruct((n, 2, S5_WIDTH), f32)),
        grid=(n, t // tc),
        in_specs=[pl.BlockSpec((None, tc, GROUP_WIDTH), lambda b, j: (b, j, 0)),
                  pl.BlockSpec((None, 2, S5_WIDTH), lambda b, j: (b, 0, 0)),
                  const((1, S5_WIDTH)), const((1, S5_WIDTH)),
                  const((GROUP_WIDTH, S5_WIDTH)), const((GROUP_WIDTH, S5_WIDTH)),
                  const((S5_WIDTH, GROUP_WIDTH)), const((S5_WIDTH, GROUP_WIDTH)),
                  const((1, GROUP_WIDTH)), const((GROUP_WIDTH, GROUP_WIDTH)), const((1, GROUP_WIDTH))],
        out_specs=(pl.BlockSpec((None, tc, GROUP_WIDTH), lambda b, j: (b, j, 0)),
                   pl.BlockSpec((None, 2, S5_WIDTH), lambda b, j: (b, 0, 0))),
        scratch_shapes=[pltpu.VMEM((2, S5_WIDTH), f32)] + [pltpu.VMEM((tc, S5_WIDTH), f32)] * 4,
        compiler_params=pltpu.CompilerParams(
            dimension_semantics=("parallel", "arbitrary"), vmem_limit_bytes=VMEM_LIMIT),
        name="s5_prompt",
    )(u, h0, sw['a_re'], sw['a_im'], sw['b_re'], sw['b_im'], sw['c_re'], sw['c_im'], sw['d'],
      sw['w_glu'], sw['b_glu'])


def _s5_step_kernel(u_ref, h0_ref, are_ref, aim_ref, bre_ref, bim_ref, cre_ref, cim_ref, d_ref,
                    wglu_ref, bglu_ref, o_ref, h_ref):
    u = u_ref[...]
    ub = u.astype(bf16)
    h0_re, h0_im = h0_ref[:, :S5_WIDTH], h0_ref[:, S5_WIDTH:]
    a_re, a_im = are_ref[...], aim_ref[...]
    h_re = a_re * h0_re - a_im * h0_im + jnp.dot(ub, bre_ref[...], preferred_element_type=f32)
    h_im = a_re * h0_im + a_im * h0_re + jnp.dot(ub, bim_ref[...], preferred_element_type=f32)
    h_ref[:, :S5_WIDTH] = h_re
    h_ref[:, S5_WIDTH:] = h_im
    y = (jnp.dot(h_re.astype(bf16), cre_ref[...], preferred_element_type=f32)
         - jnp.dot(h_im.astype(bf16), cim_ref[...], preferred_element_type=f32)
         + d_ref[...] * u)
    y = jax.nn.gelu(y)
    o_ref[...] = y * jax.nn.sigmoid(
        jnp.dot(y.astype(bf16), wglu_ref[...], preferred_element_type=f32) + bglu_ref[...])


def s5_step(u, h0, sw):
    n = u.shape[0]
    return pl.pallas_call(
        _s5_step_kernel,
        out_shape=(jax.ShapeDtypeStruct((n, GROUP_WIDTH), f32),
                   jax.ShapeDtypeStruct((n, 2 * S5_WIDTH), f32)),
        compiler_params=pltpu.CompilerParams(vmem_limit_bytes=VMEM_LIMIT),
        name="s5_step",
    )(u, h0, sw['a_re'], sw['a_im'], sw['b_re'], sw['b_im'], sw['c_re'], sw['c_im'], sw['d'],
      sw['w_glu'], sw['b_glu'])


GLA_SUB = 16
GLA_QK = GLA_HEADS * GLA_DK
HIGHEST = lax.Precision.HIGHEST
_NT = (((1,), (1,)), ((), ()))
_TN = (((0,), (0,)), ((), ()))


def _log_decay(ga, wgate_ref, bgate_ref):
    z = jnp.dot(ga, wgate_ref[...], preferred_element_type=f32, precision=HIGHEST) + bgate_ref[...]
    return jax.nn.log_sigmoid(z) / GLA_TAU


def _head_sum_matrix():
    r = lax.broadcasted_iota(jnp.int32, (GLA_QK, GROUP_WIDTH), 0) // GLA_DK
    c = lax.broadcasted_iota(jnp.int32, (GLA_QK, GROUP_WIDTH), 1) // GLA_DV
    return jnp.where(r == c, 1.0, 0.0)


def _gla_finish(o, gr, normg_ref):
    outs = []
    for h in range(GLA_HEADS):
        oh = o[:, h * GLA_DV:(h + 1) * GLA_DV]
        outs.append(oh * lax.rsqrt(jnp.mean(oh * oh, axis=-1, keepdims=True) + LN_EPS) * normg_ref[...])
    return jnp.concatenate(outs, axis=1) * (gr * jax.nn.sigmoid(gr))


def _gla_prompt_kernel(q_ref, k_ref, v_ref, ga_ref, gr_ref, wgate_ref, bgate_ref, normg_ref, s0_ref,
                       o_ref, slast_ref, st_ref, oi_ref):
    j = pl.program_id(1)

    @pl.when(j == 0)
    def _():
        st_ref[...] = s0_ref[...]

    rows = q_ref.shape[0]
    nsub = rows // GLA_SUB
    q = q_ref[...] * GLA_DK ** -0.5
    k = k_ref[...]
    v = v_ref[...]
    log_a = _log_decay(ga_ref[...], wgate_ref, bgate_ref)
    ri = lax.broadcasted_iota(jnp.int32, (rows, rows), 0)
    ci = lax.broadcasted_iota(jnp.int32, (rows, rows), 1)
    tri = jnp.where((ri // GLA_SUB == ci // GLA_SUB) & (ci <= ri), 1.0, 0.0)
    b = jnp.dot(tri, log_a, preferred_element_type=f32, precision=HIGHEST)
    row_in_sub = lax.broadcasted_iota(jnp.int32, (rows, 1), 0) % GLA_SUB
    head_sum = _head_sum_matrix().astype(bf16)

    def sub_row(x, jj):
        x3 = x.reshape(nsub, GLA_SUB, x.shape[1])
        return jnp.broadcast_to(x3[:, jj:jj + 1, :], x3.shape).reshape(x.shape)

    o = jnp.zeros((rows, GROUP_WIDTH), f32)
    for jj in range(GLA_SUB):
        decay = jnp.exp(jnp.minimum(b - sub_row(b, jj), 0.0))
        prod = jnp.where(row_in_sub >= jj, q * sub_row(k, jj) * decay, 0.0)
        prod_hi = prod.astype(bf16)
        prod_lo = (prod - prod_hi.astype(f32)).astype(bf16)
        att = []
        for half in range(2):
            ks = slice(half * GLA_QK // 2, (half + 1) * GLA_QK // 2)
            vs = slice(half * GROUP_WIDTH // 2, (half + 1) * GROUP_WIDTH // 2)
            att.append(jnp.dot(prod_hi[:, ks], head_sum[ks, vs], preferred_element_type=f32)
                       + jnp.dot(prod_lo[:, ks], head_sum[ks, vs], preferred_element_type=f32))
        o = o + jnp.concatenate(att, axis=1) * sub_row(v, jj)

    for i in range(nsub):
        blk = slice(i * GLA_SUB, (i + 1) * GLA_SUB)
        b_blk = b[blk]
        b_end = b[(i + 1) * GLA_SUB - 1:(i + 1) * GLA_SUB]
        q_blk = (q[blk] * jnp.exp(b_blk)).astype(bf16)
        k_blk = (k[blk] * jnp.exp(b_end - b_blk)).astype(bf16)
        v_blk = v[blk].astype(bf16)
        a_end = jnp.exp(b_end)
        for h in range(GLA_HEADS):
            ks = slice(h * GLA_DK, (h + 1) * GLA_DK)
            vs = slice(h * GLA_DV, (h + 1) * GLA_DV)
            st = st_ref[h]
            oi_ref[blk, vs] = lax.dot_general(q_blk[:, ks], st.astype(bf16), _NT, preferred_element_type=f32)
            st_ref[h] = st * a_end[:, ks] + lax.dot_general(v_blk[:, vs], k_blk[:, ks], _TN,
                                                            preferred_element_type=f32)

    o_ref[...] = _gla_finish(o + oi_ref[...], gr_ref[...], normg_ref)
    slast_ref[...] = st_ref[...]


def gla_prompt(gq, gk, gv, ga, gr, w_gate, b_gate, norm_g):
    n, t = gq.shape[:2]
    tc = min(GLA_CHUNK, t)
    rows = lambda width: pl.BlockSpec((None, tc, width), lambda b, j: (b, j, 0))
    const = lambda shape: pl.BlockSpec(shape, lambda b, j: (0,) * len(shape))
    state = pl.BlockSpec((None, GLA_HEADS, GLA_DV, GLA_DK), lambda b, j: (b, 0, 0, 0))
    o, s_t = pl.pallas_call(
        _gla_prompt_kernel,
        out_shape=(jax.ShapeDtypeStruct((n, t, GROUP_WIDTH), f32),
                   jax.ShapeDtypeStruct((n, GLA_HEADS, GLA_DV, GLA_DK), f32)),
        grid=(n, t // tc),
        in_specs=[rows(GLA_QK), rows(GLA_QK), rows(GROUP_WIDTH), rows(GLA_LOWRANK), rows(GROUP_WIDTH),
                  const((GLA_LOWRANK, GLA_QK)), const((1, GLA_QK)), const((1, GLA_DV)), state],
        out_specs=(rows(GROUP_WIDTH), state),
        scratch_shapes=[pltpu.VMEM((GLA_HEADS, GLA_DV, GLA_DK), f32), pltpu.VMEM((tc, GROUP_WIDTH), f32)],
        compiler_params=pltpu.CompilerParams(
            dimension_semantics=("parallel", "arbitrary"), vmem_limit_bytes=VMEM_LIMIT),
        name="gla_prompt",
    )(gq, gk, gv, ga, gr, w_gate, b_gate.reshape(1, GLA_QK), norm_g.reshape(1, GLA_DV),
      jnp.zeros((n, GLA_HEADS, GLA_DV, GLA_DK), f32))
    return o, s_t.transpose(0, 1, 3, 2)


GLA_STEP_SEQS = 8


def _gla_step_kernel(q_ref, k_ref, v_ref, ga_ref, gr_ref, wgate_ref, bgate_ref, normg_ref, s_ref,
                     o_ref, snew_ref):
    q = q_ref[...] * GLA_DK ** -0.5
    k = k_ref[...]
    v = v_ref[...]
    a = jnp.exp(_log_decay(ga_ref[...], wgate_ref, bgate_ref))
    qa = q * a
    seq = lax.broadcasted_iota(jnp.int32, (GLA_STEP_SEQS, 1), 0)
    o = jnp.dot(q * k, _head_sum_matrix(), preferred_element_type=f32, precision=HIGHEST) * v
    o_heads = [jnp.zeros((GLA_STEP_SEQS, GLA_DV), f32) for _ in range(GLA_HEADS)]
    for i in range(GLA_STEP_SEQS):
        mine = seq == i
        pick = jnp.broadcast_to(jnp.where(mine, 1.0, 0.0), (GLA_STEP_SEQS, GLA_DV))
        qa_i = jnp.where(mine, qa, 0.0).astype(bf16)
        k_i = jnp.where(mine, k, 0.0).astype(bf16)
        vb = v.astype(bf16)
        for h in range(GLA_HEADS):
            ks = slice(h * GLA_DK, (h + 1) * GLA_DK)
            vs = slice(h * GLA_DV, (h + 1) * GLA_DV)
            s_old = s_ref[i, h]
            a_rows = lax.dot_general(a[:, ks], pick, _TN, preferred_element_type=f32, precision=HIGHEST)
            o_heads[h] = o_heads[h] + jnp.dot(qa_i[:, ks], s_old.astype(bf16), preferred_element_type=f32)
            snew_ref[i, h] = a_rows * s_old + lax.dot_general(k_i[:, ks], vb[:, vs], _TN,
                                                               preferred_element_type=f32)
    o_ref[...] = _gla_finish(o + jnp.concatenate(o_heads, axis=1), gr_ref[...], normg_ref)


def gla_step(gq, gk, gv, ga, gr, w_gate, b_gate, norm_g, s0):
    n = gq.shape[0]
    rows = lambda width: pl.BlockSpec((GLA_STEP_SEQS, width), lambda i: (i, 0))
    const = lambda shape: pl.BlockSpec(shape, lambda i: (0,) * len(shape))
    state = pl.BlockSpec((GLA_STEP_SEQS, GLA_HEADS, GLA_DK, GLA_DV), lambda i: (i, 0, 0, 0))
    return pl.pallas_call(
        _gla_step_kernel,
        out_shape=(jax.ShapeDtypeStruct((n, GROUP_WIDTH), f32),
                   jax.ShapeDtypeStruct((n, GLA_HEADS, GLA_DK, GLA_DV), f32)),
        grid=(n // GLA_STEP_SEQS,),
        in_specs=[rows(GLA_QK), rows(GLA_QK), rows(GROUP_WIDTH), rows(GLA_LOWRANK), rows(GROUP_WIDTH),
                  const((GLA_LOWRANK, GLA_QK)), const((1, GLA_QK)), const((1, GLA_DV)), state],
        out_specs=(rows(GROUP_WIDTH), state),
        compiler_params=pltpu.CompilerParams(
            dimension_semantics=("parallel",), vmem_limit_bytes=VMEM_LIMIT),
        name="gla_step",
    )(gq, gk, gv, ga, gr, w_gate, b_gate.reshape(1, GLA_QK), norm_g.reshape(1, GLA_DV), s0)


N_PAGES = PAST_LEN // PAGE_SIZE
KEY_PAD = PAST_LEN + LANES
STEP_ROWS = 8
CMP_STEP_SEQS = 2
BLOCKS_PER_PAGE = PAGE_SIZE // CMP_BLOCK


def _tokens_minor(cache):
    token_axis = 2
    return jnp.moveaxis(cache, token_axis, -1)


def _seq_spec(*tail):
    zeros = (0,) * len(tail)
    return pl.BlockSpec((None,) + tail, lambda i, pt: (i,) + zeros)


def _layer_page_specs(block_tail, layer, seqs_per_step=1):
    zeros = (0,) * len(block_tail)
    return [pl.BlockSpec((None, None) + block_tail,
                         lambda i, pt, s=s, p=p: (layer, pt[i * seqs_per_step + s, p]) + zeros)
            for s in range(seqs_per_step) for p in range(N_PAGES)]


def _layer_seq_spec(tail, layer):
    zeros = (0,) * len(tail)
    return pl.BlockSpec((None, None) + tail, lambda i, pt: (layer, i) + zeros)


def _topk_rows_kernel(keys_ref, sel_ref, *, topk, idx_bits, n_valid):
    keys = keys_ref[...]
    lane = lax.broadcasted_iota(jnp.int32, keys.shape, 1)
    count = lambda pred: jnp.sum(jnp.where(pred(keys, lane), 1.0, 0.0), axis=1, keepdims=True)
    thr, need = _topk_threshold(count, topk)
    cut = _tie_index_cut(count, keys.shape[0], thr, need, idx_bits)
    chosen = ((keys > thr) | ((keys == thr) & (lane <= cut))) & (lane < n_valid)
    sel_ref[...] = jnp.where(chosen, 1.0, 0.0)


TOKEN_PAIRS = CMP_BLOCK // 2
PAIR_ROWS = 2 * BLOCKS_PER_PAGE


def _pair_regroup_matrix():
    r = np.arange(TOKEN_PAIRS * PAIR_ROWS)
    k = np.arange(2 * PAGE_SIZE)
    tp, page, blk = r // PAIR_ROWS, (r // BLOCKS_PER_PAGE) % 2, r % BLOCKS_PER_PAGE
    page_k, blk_k, t_k = k // PAGE_SIZE, (k % PAGE_SIZE) // CMP_BLOCK, k % CMP_BLOCK
    hit = ((page[:, None] == page_k[None]) & (blk[:, None] == blk_k[None]) & (tp[:, None] == t_k[None] // 2))
    return jnp.asarray(hit, bf16)


def _nsa_compress_paged_kernel(pt_ref, pe_ref, regroup_ref, w1_ref, w2_ref, *rest):
    npg = CMP_STEP_SEQS * N_PAGES
    pages, o_ref, x_ref = rest[:npg], rest[npg], rest[npg + 1]
    blocks_per_seq = N_PAGES * BLOCKS_PER_PAGE
    even_token = (lax.broadcasted_iota(jnp.int32, (HEAD_DIM, 2 * PAGE_SIZE), 1) & 1) == 0
    regroup = regroup_ref[...]
    for pair in range(npg // 2):
        for gj in range(2 * KV_HEADS):
            g, j = gj // 2, gj % 2
            two = jnp.concatenate([pages[2 * pair][g, j] + pe_ref[j], pages[2 * pair + 1][g, j] + pe_ref[j]],
                                  axis=1)
            split = jnp.concatenate([jnp.where(even_token, two, 0.0), jnp.where(even_token, 0.0, two)],
                                    axis=0).astype(bf16)
            rows = lax.dot_general(regroup, split, _NT, preferred_element_type=f32)
            for tp in range(TOKEN_PAIRS):
                x_ref[gj, tp, pair * PAIR_ROWS:(pair + 1) * PAIR_ROWS, :] = rows[tp * PAIR_ROWS:(tp + 1) * PAIR_ROWS]
    for gj in range(2 * KV_HEADS):
        j = gj % 2
        x = jnp.concatenate([x_ref[gj, tp] for tp in range(TOKEN_PAIRS)], axis=1).astype(bf16)
        hid = jnp.dot(x, w1_ref[j], preferred_element_type=f32)
        out = jnp.dot(jax.nn.gelu(hid).astype(bf16), w2_ref[j], preferred_element_type=f32)
        for s in range(CMP_STEP_SEQS):
            o_ref[s, gj] = out[s * blocks_per_seq:(s + 1) * blocks_per_seq]


def nsa_compress_paged(pools_cmp, layer, page_table, cmp_pe, cmp_w1, cmp_w2):
    n = page_table.shape[0]
    blocks = N_PAGES * BLOCKS_PER_PAGE
    assert (CMP_STEP_SEQS * N_PAGES) % 2 == 0 and CMP_BLOCK % 2 == 0 and PAIR_ROWS == 8
    pe_page = jnp.tile(cmp_pe.transpose(1, 2, 0), (1, 1, BLOCKS_PER_PAGE))
    const = lambda shape: pl.BlockSpec(shape, lambda i, pt: (0,) * len(shape))
    return pl.pallas_call(
        _nsa_compress_paged_kernel,
        out_shape=jax.ShapeDtypeStruct((n, 2 * KV_HEADS, blocks, HEAD_DIM), f32),
        grid_spec=pltpu.PrefetchScalarGridSpec(
            num_scalar_prefetch=1, grid=(n // CMP_STEP_SEQS,),
            in_specs=[const((2, HEAD_DIM, PAGE_SIZE)), const((TOKEN_PAIRS * PAIR_ROWS, 2 * PAGE_SIZE)),
                      const((2, CMP_BLOCK * HEAD_DIM, CMP_HIDDEN)), const((2, CMP_HIDDEN, HEAD_DIM))]
            + _layer_page_specs((KV_HEADS, 2, HEAD_DIM, PAGE_SIZE), layer, CMP_STEP_SEQS),
            out_specs=pl.BlockSpec((CMP_STEP_SEQS, 2 * KV_HEADS, blocks, HEAD_DIM), lambda i, pt: (i, 0, 0, 0)),
            scratch_shapes=[pltpu.VMEM((2 * KV_HEADS, TOKEN_PAIRS, CMP_STEP_SEQS * blocks, 2 * HEAD_DIM), f32)]),
        compiler_params=pltpu.CompilerParams(
            dimension_semantics=("parallel",), vmem_limit_bytes=VMEM_LIMIT),
        name="nsa_compress_paged",
    )(page_table, pe_page, _pair_regroup_matrix(), cmp_w1.astype(bf16), cmp_w2.astype(bf16),
      *([pools_cmp] * (CMP_STEP_SEQS * N_PAGES)))


def _decode_attention(q8, blocks, masks, kv_new, new_ok, is_g0):
    qb = q8.astype(bf16)
    by_group = lambda fn: jnp.where(is_g0, fn(0), fn(1))
    s_new = by_group(lambda g: jnp.sum(q8 * kv_new[2 * g:2 * g + 1], axis=1, keepdims=True)) * ATT_SCALE
    if new_ok is not None:
        s_new = jnp.where(new_ok, s_new, MASKED)
    m = s_new
    scores = []
    for blk, msk in zip(blocks, masks):
        s = by_group(lambda g: jnp.dot(qb, blk[g, 0].astype(bf16), preferred_element_type=f32)) * ATT_SCALE
        s = jnp.where(msk > 0.5, s, MASKED)
        scores.append(s)
        m = jnp.maximum(m, jnp.max(s, axis=1, keepdims=True))
    p_new = jnp.exp(s_new - m)
    if new_ok is not None:
        p_new = jnp.where(new_ok, p_new, 0.0)
    l = p_new
    acc = p_new * by_group(lambda g: kv_new[2 * g + 1:2 * g + 2])
    for blk, msk, s in zip(blocks, masks, scores):
        p = jnp.where(msk > 0.5, jnp.exp(s - m), 0.0)
        l = l + jnp.sum(p, axis=1, keepdims=True)
        pb = p.astype(bf16)
        acc = acc + by_group(lambda g: lax.dot_general(pb, blk[g, 1].astype(bf16), _NT,
                                                       preferred_element_type=f32))
    return acc / l


def _dsa_decode_scores_kernel(pt_ref, iq_ref, iw_ref, iknew_ref, *rest):
    pages, keys_ref = rest[:N_PAGES], rest[N_PAGES]
    iq = iq_ref[...]
    iqb = iq.astype(bf16)
    iw = iw_ref[...] * IDX_HEADS ** -0.5
    weigh = lambda dots, w: jnp.sum(w * jnp.maximum(dots * IDX_DIM ** -0.5, 0.0), axis=0, keepdims=True)
    for p in range(N_PAGES):
        dots = jnp.dot(iqb, pages[p][...].astype(bf16), preferred_element_type=f32)
        keys_ref[:, p * PAGE_SIZE:(p + 1) * PAGE_SIZE] = _sortable_key(weigh(dots, iw))
    new_score = weigh(jnp.sum(iq * iknew_ref[...], axis=1, keepdims=True), iw[:, :1])
    lane = lax.broadcasted_iota(jnp.int32, (1, LANES), 1)
    keys_ref[:, PAST_LEN:] = jnp.where(lane == 0, _sortable_key(jnp.broadcast_to(new_score, (1, LANES))), INT_MIN)


def _dsa_decode_attend_kernel(pt_ref, q_ref, sel_ref, kvnew_ref, *rest):
    pages, o_ref = rest[:N_PAGES], rest[N_PAGES]
    masks = [sel_ref[:, p * PAGE_SIZE:(p + 1) * PAGE_SIZE] for p in range(N_PAGES)]
    new_ok = sel_ref[:, PAST_LEN:PAST_LEN + 1] > 0.5
    is_g0 = lax.broadcasted_iota(jnp.int32, (N_HEADS, 1), 0) < Q_PER_KV
    o_ref[...] = _decode_attention(q_ref[...], pages, masks, kvnew_ref[...], new_ok, is_g0)


def dsa_decode(dq, dkv, iq, ik, iw, pools_kv, pools_idx, layer, page_table):
    n = dq.shape[0]
    assert min(DSA_TOPK, (PAST_LEN + 1) // 4) == DSA_TOPK and PAGE_SIZE == LANES
    pad_heads = ((0, 0), (0, STEP_ROWS - IDX_HEADS), (0, 0))
    iq8 = jnp.pad(iq.reshape(n, IDX_HEADS, IDX_DIM), pad_heads)
    iw8 = jnp.broadcast_to(jnp.pad(iw.reshape(n, IDX_HEADS, 1), pad_heads), (n, STEP_ROWS, LANES))
    keys = pl.pallas_call(
        _dsa_decode_scores_kernel,
        out_shape=jax.ShapeDtypeStruct((n, 1, KEY_PAD), jnp.int32),
        grid_spec=pltpu.PrefetchScalarGridSpec(
            num_scalar_prefetch=1, grid=(n,),
            in_specs=[_seq_spec(STEP_ROWS, IDX_DIM), _seq_spec(STEP_ROWS, LANES), _seq_spec(1, IDX_DIM)]
            + _layer_page_specs((IDX_DIM, PAGE_SIZE), layer),
            out_specs=_seq_spec(1, KEY_PAD)),
        compiler_params=pltpu.CompilerParams(
            dimension_semantics=("parallel",), vmem_limit_bytes=VMEM_LIMIT),
        name="dsa_decode_scores",
    )(page_table, iq8, iw8, ik.reshape(n, 1, IDX_DIM), *([pools_idx] * N_PAGES))
    sel = pl.pallas_call(
        functools.partial(_topk_rows_kernel, topk=DSA_TOPK, idx_bits=int(math.ceil(math.log2(KEY_PAD))),
                          n_valid=PAST_LEN + 1),
        out_shape=jax.ShapeDtypeStruct((n, KEY_PAD), f32),
        compiler_params=pltpu.CompilerParams(vmem_limit_bytes=VMEM_LIMIT),
        name="dsa_decode_topk",
    )(keys.reshape(n, KEY_PAD))
    o8 = pl.pallas_call(
        _dsa_decode_attend_kernel,
        out_shape=jax.ShapeDtypeStruct((n, N_HEADS, HEAD_DIM), f32),
        grid_spec=pltpu.PrefetchScalarGridSpec(
            num_scalar_prefetch=1, grid=(n,),
            in_specs=[_seq_spec(N_HEADS, HEAD_DIM), _seq_spec(1, KEY_PAD), _seq_spec(2 * KV_HEADS, HEAD_DIM)]
            + _layer_page_specs((KV_HEADS, 2, HEAD_DIM, PAGE_SIZE), layer),
            out_specs=_seq_spec(N_HEADS, HEAD_DIM)),
        compiler_params=pltpu.CompilerParams(
            dimension_semantics=("parallel",), vmem_limit_bytes=VMEM_LIMIT),
        name="dsa_decode_attend",
    )(page_table, dq.reshape(n, N_HEADS, HEAD_DIM), sel.reshape(n, 1, KEY_PAD),
      dkv.reshape(n, 2 * KV_HEADS, HEAD_DIM), *([pools_kv] * N_PAGES))
    return o8.reshape(n, GROUP_WIDTH)


def _nsa_decode_kernel(pt_ref, q_ref, gate_ref, cmp_ref, slcnew_ref, win_ref, winnew_ref, *rest):
    pages, o_ref = rest[:N_PAGES], rest[N_PAGES]
    q8 = q_ref[...]
    qb = q8.astype(bf16)
    lane = lax.broadcasted_iota(jnp.int32, (N_HEADS, CMP_LANES), 1)
    head = lax.broadcasted_iota(jnp.int32, (N_HEADS, 1), 0)
    is_g0 = head < Q_PER_KV
    by_group = lambda fn: jnp.where(is_g0, fn(0), fn(1))
    q_pos = jnp.full((N_HEADS, 1), PAST_LEN, jnp.int32)
    blocks = cmp_ref.shape[1]
    s = by_group(lambda g: lax.dot_general(qb, cmp_ref[2 * g].astype(bf16), _NT,
                                           preferred_element_type=f32)) * ATT_SCALE
    lane_c = lane[:, :blocks]
    s = jnp.where((lane_c + 1) * CMP_BLOCK - 1 <= q_pos, s, -jnp.inf)
    m = jnp.max(s, axis=1, keepdims=True)
    m = jnp.where(m == -jnp.inf, 0.0, m)
    p = jnp.exp(s - m)
    p = p / jnp.maximum(jnp.sum(p, axis=1, keepdims=True), F32_TINY)
    pb = p.astype(bf16)
    o_cmp = by_group(lambda g: jnp.dot(pb, cmp_ref[2 * g + 1].astype(bf16), preferred_element_type=f32))
    imp = jnp.where(is_g0, jnp.sum(p[:Q_PER_KV], axis=0, keepdims=True),
                    jnp.sum(p[Q_PER_KV:], axis=0, keepdims=True))
    imp = jnp.concatenate([imp, jnp.zeros((N_HEADS, CMP_LANES - blocks), f32)], axis=1)
    sel_blk = _select_blocks(imp, q_pos, lane)
    ratio = SLC_BLOCK // CMP_BLOCK
    per_page = PAGE_SIZE // SLC_BLOCK
    masks = []
    for pg in range(N_PAGES):
        msk = sel_blk[:, ratio * per_page * pg:ratio * per_page * pg + 1]
        for j in range(1, per_page):
            b = per_page * pg + j
            msk = jnp.where(lane < j * SLC_BLOCK, msk, sel_blk[:, ratio * b:ratio * b + 1])
        masks.append(msk)
    o_slc = _decode_attention(q8, pages, masks, slcnew_ref[...], None, is_g0)
    wbuf = win_ref.shape[-1]
    slot = lax.broadcasted_iota(jnp.int32, (1, wbuf), 1)
    wmask = jnp.where(PAST_LEN - wbuf + slot > PAST_LEN - WINDOW, 1.0, 0.0)
    o_win = _decode_attention(q8, [win_ref], [wmask], winnew_ref[...], None, is_g0)
    gate = jax.nn.sigmoid(gate_ref[...])
    o_ref[...] = (gate[0][:, :HEAD_DIM] * o_cmp + gate[1][:, :HEAD_DIM] * o_slc
                  + gate[2][:, :HEAD_DIM] * o_win)


def nsa_decode(nq, ng, skv, wkv, pools_cmp, pools_slc, wins, layer, page_table, cmp_pe, cmp_w1, cmp_w2):
    n = nq.shape[0]
    blocks = N_PAGES * BLOCKS_PER_PAGE
    wbuf = wins.shape[-1]
    assert (PAST_LEN + 1) // CMP_BLOCK == blocks and blocks <= CMP_LANES
    assert PAST_LEN // SLC_BLOCK + 1 <= CMP_LANES // 2 and PAGE_SIZE == LANES
    cmp_rows = nsa_compress_paged(pools_cmp, layer, page_table, cmp_pe, cmp_w1, cmp_w2)
    gates =jnp.broadcast_to(ng.reshape(n, N_HEADS, 3).transpose(0, 2, 1)[..., None], (n, 3, N_HEADS, LANES))
    kv_tile = (KV_HEADS, 2, HEAD_DIM)
    o8 = pl.pallas_call(
        _nsa_decode_kernel,
        out_shape=jax.ShapeDtypeStruct((n, N_HEADS, HEAD_DIM), f32),
        grid_spec=pltpu.PrefetchScalarGridSpec(
            num_scalar_prefetch=1, grid=(n,),
            in_specs=[_seq_spec(N_HEADS, HEAD_DIM), _seq_spec(3, N_HEADS, LANES),
                      _seq_spec(2 * KV_HEADS, blocks, HEAD_DIM), _seq_spec(2 * KV_HEADS, HEAD_DIM),
                      _layer_seq_spec(kv_tile + (wbuf,), layer), _seq_spec(2 * KV_HEADS, HEAD_DIM)]
            + _layer_page_specs(kv_tile + (PAGE_SIZE,), layer),
            out_specs=_seq_spec(N_HEADS, HEAD_DIM)),
        compiler_params=pltpu.CompilerParams(
            dimension_semantics=("parallel",), vmem_limit_bytes=VMEM_LIMIT),
        name="nsa_decode",
    )(page_table, nq.reshape(n, N_HEADS, HEAD_DIM), gates, cmp_rows, skv.reshape(n, 2 * KV_HEADS, HEAD_DIM),
      wins, wkv.reshape(n, 2 * KV_HEADS, HEAD_DIM), *([pools_slc] * N_PAGES))
    return o8.reshape(n, GROUP_WIDTH)


WIN_STEP_SEQS = 8


def _window_push_kernel(win_ref, new_ref, o_ref):
    seqs, tiles = win_ref.shape[0], 2 * KV_HEADS
    wbuf = win_ref.shape[-1]
    last = lax.broadcasted_iota(jnp.int32, (HEAD_DIM, wbuf), 1) == wbuf - 1
    new_cols = new_ref[...].reshape(seqs * tiles, HEAD_DIM).T
    for s in range(seqs):
        for gj in range(tiles):
            g, j = gj // 2, gj % 2
            col = new_cols[:, s * tiles + gj:s * tiles + gj + 1]
            o_ref[s, g, j] = jnp.where(last, col, pltpu.roll(win_ref[s, g, j], wbuf - 1, 1))


def window_push(wins, new_rows):
    depth, n = wins.shape[:2]
    tile = wins.shape[2:]
    return pl.pallas_call(
        _window_push_kernel,
        out_shape=jax.ShapeDtypeStruct(wins.shape, f32),
        grid=(depth, n // WIN_STEP_SEQS),
        in_specs=[pl.BlockSpec((None, WIN_STEP_SEQS) + tile, lambda l, i: (l, i, 0, 0, 0, 0)),
                  pl.BlockSpec((None, WIN_STEP_SEQS, 2 * KV_HEADS, HEAD_DIM), lambda l, i: (l, i, 0, 0))],
        out_specs=pl.BlockSpec((None, WIN_STEP_SEQS) + tile, lambda l, i: (l, i, 0, 0, 0, 0)),
        compiler_params=pltpu.CompilerParams(
            dimension_semantics=("parallel", "parallel"), vmem_limit_bytes=VMEM_LIMIT),
        name="window_push",
    )(wins, new_rows)


def mix_prompt(cols, n, t, w, sw, cmp_weights):
    (dq, dkv, iq, ik, iw, u, nq, ckv, skv, wkv, ng, gq, gk, gv, ga, gr) = [
        c.reshape(n, t, c.shape[-1]) for c in cols]
    kv_shape = (n, t, KV_HEADS, 2, HEAD_DIM)
    o_dsa = dsa_prompt(dq, dkv, iq, ik, iw)
    o_s5, h_s5 = s5_prompt(u, jnp.zeros((n, 2, S5_WIDTH), f32), sw)
    o_nsa = nsa_prompt(nq, ng, ckv, skv, wkv, cmp_weights)
    o_gla, s_gla = gla_prompt(gq, gk, gv, ga, gr, w['gla_w_gate'], w['gla_b_gate'], w['gla_norm_g'])
    win_state = wkv[:, -min(WINDOW, t):].reshape((n, min(WINDOW, t)) + kv_shape[2:])
    mixers = [o.reshape(n * t, GROUP_WIDTH) for o in (o_dsa, o_s5, o_nsa, o_gla)]
    return mixers, (dkv.reshape(kv_shape), ik, ckv.reshape(kv_shape), skv.reshape(kv_shape), win_state,
                    h_s5.reshape(n, 2, S5_GROUPS, S5_STATE), s_gla)


def mix_sample(cols, w, sw, layer, pools, h_s5, s_gla, page_table):
    (dq, dkv, iq, ik, iw, u, nq, ckv, skv, wkv, ng, gq, gk, gv, ga, gr) = cols
    n, t = dq.shape[0], 1
    kv_shape = (n, t, KV_HEADS, 2, HEAD_DIM)
    o_dsa = dsa_decode(dq, dkv, iq, ik, iw, pools['dsa_kv'], pools['dsa_idx'], layer, page_table)
    o_s5, h_new = s5_step(u, h_s5.reshape(n, 2 * S5_WIDTH), sw)
    o_nsa = nsa_decode(nq, ng, skv, wkv, pools['nsa_cmp'], pools['nsa_slc'], pools['nsa_win'], layer,
                       page_table, w['cmp_pe'], w['cmp_w1'], w['cmp_w2'])
    o_gla, s_new = gla_step(gq, gk, gv, ga, gr, w['gla_w_gate'], w['gla_b_gate'], w['gla_norm_g'], s_gla)
    return [o_dsa, o_s5, o_nsa, o_gla], (
        dkv.reshape(kv_shape), ik.reshape(n, t, IDX_DIM), ckv.reshape(kv_shape), skv.reshape(kv_shape),
        wkv.reshape(n, 2 * KV_HEADS, HEAD_DIM), h_new.reshape(n, 2, S5_GROUPS, S5_STATE), s_new)


def kernel(x_prompt, x_sample, cache_dsa_kv, cache_dsa_idx, cache_nsa_cmp, cache_nsa_slc, cache_nsa_win, state_s5, state_gla, page_table, w_in, s5_a_re, s5_a_im, s5_b_re, s5_b_im, s5_c_re, s5_c_im, s5_d, s5_log_dt, s5_w_glu, s5_b_glu, nsa_cmp_pe, nsa_cmp_w1, nsa_cmp_w2, gla_w_gate, gla_b_gate, gla_norm_g, w_out, ln1_g, ln1_b, ffn_w_gate, ffn_w_up, ffn_w_down, ln2_g, ln2_b):
    np_, tp_ = x_prompt.shape[:2]
    ns_, ts_ = x_sample.shape[:2]
    assert ts_ == 1
    w_in_pad = pad_in_projection(w_in)
    w_out_b = w_out.astype(bf16)
    wg_b, wu_b, wd_b = ffn_w_gate.astype(bf16), ffn_w_up.astype(bf16), ffn_w_down.astype(bf16)
    hp = x_prompt.reshape(np_ * tp_, D_MODEL)
    hs = x_sample.reshape(ns_ * ts_, D_MODEL)
    outs_p = [[] for _ in range(7)]
    outs_s = [[] for _ in range(7)]
    pools = {'dsa_kv': _tokens_minor(cache_dsa_kv), 'dsa_idx': _tokens_minor(cache_dsa_idx),
             'nsa_cmp': _tokens_minor(cache_nsa_cmp), 'nsa_slc': _tokens_minor(cache_nsa_slc),
             'nsa_win': _tokens_minor(cache_nsa_win)}
    for l in range(DEPTH):
        w = {'a_re': s5_a_re[l], 'a_im': s5_a_im[l], 'b_re': s5_b_re[l], 'b_im': s5_b_im[l],
             'c_re': s5_c_re[l], 'c_im': s5_c_im[l], 'd': s5_d[l], 'log_dt': s5_log_dt[l],
             'w_glu': s5_w_glu[l], 'b_glu': s5_b_glu[l],
             'cmp_pe': nsa_cmp_pe[l], 'cmp_w1': nsa_cmp_w1[l], 'cmp_w2': nsa_cmp_w2[l],
             'gla_w_gate': gla_w_gate[l], 'gla_b_gate': gla_b_gate[l], 'gla_norm_g': gla_norm_g[l]}
        sw = s5_discretize(w)
        cmp_weights = nsa_compress_weights(nsa_cmp_pe[l], nsa_cmp_w1[l], nsa_cmp_w2[l])
        mixers_p, st_p = mix_prompt(in_projection(hp, w_in_pad, l), np_, tp_, w, sw, cmp_weights)
        mixers_s, st_s = mix_sample(in_projection(hs, w_in_pad, l), w, sw, l, pools,
                                    state_s5[l], state_gla[l], page_table)
        hp = outproj_ln(hp, mixers_p, w_out_b, l, ln1_g[l], ln1_b[l])
        hs = outproj_ln(hs, mixers_s, w_out_b, l, ln1_g[l], ln1_b[l])
        hp = ffn_ln(hp, wg_b, wu_b, wd_b, l, ln2_g[l], ln2_b[l])
        hs = ffn_ln(hs, wg_b, wu_b, wd_b, l, ln2_g[l], ln2_b[l])
        for lst, st in zip(outs_p, st_p):
            lst.append(st)
        for lst, st in zip(outs_s, st_s):
            lst.append(st)
    dsa_kv_p, dsa_idx_p, nsa_cmp_p, nsa_slc_p, nsa_win_p, s5_p, gla_p = [jnp.stack(a) for a in outs_p]
    dsa_kv_s, dsa_idx_s, nsa_cmp_s, nsa_slc_s, win_rows_s, s5_s, gla_s = [jnp.stack(a) for a in outs_s]
    nsa_win_s = jnp.moveaxis(window_push(pools['nsa_win'], win_rows_s), -1, 2)
    return (hp.reshape(np_, tp_, D_MODEL), hs.reshape(ns_, ts_, D_MODEL),
            dsa_kv_p, dsa_kv_s, dsa_idx_p, dsa_idx_s, nsa_cmp_p, nsa_cmp_s,
            nsa_slc_p, nsa_slc_s, nsa_win_p, nsa_win_s, s5_p, s5_s, gla_p, gla_s)
```

```python
import functools
import math

import jax
import jax.numpy as jnp
from jax import lax
import numpy as np
from jax.experimental import pallas as pl
from jax.experimental.pallas import tpu as pltpu

D_MODEL = 2048
DEPTH = 2
PAST_LEN = 2048
PAGE_SIZE = 128
GROUP_WIDTH = 512
HEAD_DIM = 64
N_HEADS = 8
KV_HEADS = 2
Q_PER_KV = 4
KV_ROW = 256
IDX_HEADS = 4
IDX_DIM = 64
DSA_TOPK = 256
S5_GROUPS = 32
S5_STATE = 64
S5_WIDTH = S5_GROUPS * S5_STATE
CMP_BLOCK = 32
CMP_HIDDEN = 128
SLC_BLOCK = 64
SLC_TOPN = 16
WINDOW = 512
QBLOCK = 128
GLA_HEADS = 4
GLA_DK = 64
GLA_DV = 128
GLA_LOWRANK = 16
GLA_TAU = 16.0
GLA_CHUNK = 64
DEEPNORM_ALPHA = (2 * DEPTH) ** 0.25
LN_EPS = 1e-5

IN_SIZES = (512, 256, 256, 64, 4, 512, 512, 256, 256, 256, 24, 256, 256, 512, 16, 512)
D_IN = sum(IN_SIZES)

VMEM_LIMIT = 56 * 1024 * 1024
LANES = 128
KEY_CHUNK = 512
CMP_LANES = 128
S5_TIME_CHUNK = 256
ATT_SCALE = HEAD_DIM ** -0.5
MASKED = -1e30
INT_MIN = -2 ** 31
F32_TINY = float(np.finfo(np.float32).tiny)

bf16 = jnp.bfloat16
f32 = jnp.float32


IN_PAD_SIZES = tuple(-(-s // LANES) * LANES for s in IN_SIZES)
IN_PAD_OFFSETS = tuple(int(o) for o in np.cumsum((0,) + IN_PAD_SIZES[:-1]))


def pad_in_projection(w_in):
    parts, off = [], 0
    for size, size_pad in zip(IN_SIZES, IN_PAD_SIZES):
        parts.append(jnp.pad(w_in[:, :, off:off + size], ((0, 0), (0, 0), (0, size_pad - size))))
        off += size
    return jnp.concatenate(parts, axis=-1).astype(bf16)


def _in_projection_kernel(x_ref, w_ref, *o_refs):
    xb = x_ref[...].astype(bf16)
    for o_ref, off, size, size_pad in zip(o_refs, IN_PAD_OFFSETS, IN_SIZES, IN_PAD_SIZES):
        y = jnp.dot(xb, w_ref[:, off:off + size_pad], preferred_element_type=f32)
        o_ref[...] = y[:, :size]


def in_projection(x, w_pad, layer):
    m, k = x.shape
    tm = min(256, m)
    return pl.pallas_call(
        _in_projection_kernel,
        out_shape=tuple(jax.ShapeDtypeStruct((m, s), f32) for s in IN_SIZES),
        grid=(m // tm,),
        in_specs=[pl.BlockSpec((tm, k), lambda i: (i, 0)),
                  pl.BlockSpec((None, k, w_pad.shape[2]), lambda i: (layer, 0, 0),
                               pipeline_mode=pl.Buffered(1))],
        out_specs=tuple(pl.BlockSpec((tm, s), lambda i: (i, 0)) for s in IN_SIZES),
        compiler_params=pltpu.CompilerParams(
            dimension_semantics=("parallel",), vmem_limit_bytes=VMEM_LIMIT),
        name="in_projection",
    )(x, w_pad)


def _layer_norm_rows(z, g, b):
    mu = jnp.mean(z, axis=-1, keepdims=True)
    zc = z - mu
    var = jnp.mean(zc * zc, axis=-1, keepdims=True)
    return zc * lax.rsqrt(var + LN_EPS) * g + b


def _outproj_ln_kernel(x_ref, m0_ref, m1_ref, m2_ref, m3_ref, w_ref, g_ref, b_ref, o_ref):
    y = DEEPNORM_ALPHA * x_ref[...]
    for j, m_ref in enumerate((m0_ref, m1_ref, m2_ref, m3_ref)):
        y = y + jnp.dot(m_ref[...].astype(bf16), w_ref[j * GROUP_WIDTH:(j + 1) * GROUP_WIDTH, :],
                        preferred_element_type=f32)
    o_ref[...] = _layer_norm_rows(y, g_ref[...], b_ref[...])


def outproj_ln(x, mixers, w_bf16, layer, g, b):
    m, d = x.shape
    tm = min(256, m)
    mix_spec = pl.BlockSpec((tm, GROUP_WIDTH), lambda i: (i, 0))
    return pl.pallas_call(
        _outproj_ln_kernel,
        out_shape=jax.ShapeDtypeStruct((m, d), f32),
        grid=(m // tm,),
        in_specs=[pl.BlockSpec((tm, d), lambda i: (i, 0)),
                  mix_spec, mix_spec, mix_spec, mix_spec,
                  pl.BlockSpec((None, d, d), lambda i: (layer, 0, 0)),
                  pl.BlockSpec((1, d), lambda i: (0, 0)),
                  pl.BlockSpec((1, d), lambda i: (0, 0))],
        out_specs=pl.BlockSpec((tm, d), lambda i: (i, 0)),
        compiler_params=pltpu.CompilerParams(
            dimension_semantics=("parallel",), vmem_limit_bytes=VMEM_LIMIT),
        name="outproj_ln",
    )(x, *mixers, w_bf16, g.reshape(1, d), b.reshape(1, d))


def _ffn_ln_kernel(h_ref, wg_ref, wu_ref, wd_ref, g_ref, b_ref, o_ref, acc_ref):
    f = pl.program_id(1)

    @pl.when(f == 0)
    def _():
        acc_ref[...] = jnp.zeros_like(acc_ref)

    hb = h_ref[...].astype(bf16)
    a = jnp.dot(hb, wg_ref[...], preferred_element_type=f32)
    u = jnp.dot(hb, wu_ref[...], preferred_element_type=f32)
    act = (a * jax.nn.sigmoid(a) * u).astype(bf16)
    acc_ref[...] += jnp.dot(act, wd_ref[...], preferred_element_type=f32)

    @pl.when(f == pl.num_programs(1) - 1)
    def _():
        z = DEEPNORM_ALPHA * h_ref[...] + acc_ref[...]
        o_ref[...] = _layer_norm_rows(z, g_ref[...], b_ref[...])


def ffn_ln(h, wg, wu, wd, layer, g, b, tf=512):
    m, d = h.shape
    fh = wg.shape[2]
    tm = min(512, m)
    return pl.pallas_call(
        _ffn_ln_kernel,
        out_shape=jax.ShapeDtypeStruct((m, d), f32),
        grid=(m // tm, fh // tf),
        in_specs=[pl.BlockSpec((tm, d), lambda i, f: (i, 0)),
                  pl.BlockSpec((None, d, tf), lambda i, f: (layer, 0, f)),
                  pl.BlockSpec((None, d, tf), lambda i, f: (layer, 0, f)),
                  pl.BlockSpec((None, tf, d), lambda i, f: (layer, f, 0)),
                  pl.BlockSpec((1, d), lambda i, f: (0, 0)),
                  pl.BlockSpec((1, d), lambda i, f: (0, 0))],
        out_specs=pl.BlockSpec((tm, d), lambda i, f: (i, 0)),
        scratch_shapes=[pltpu.VMEM((tm, d), f32)],
        compiler_params=pltpu.CompilerParams(
            dimension_semantics=("parallel", "arbitrary"), vmem_limit_bytes=VMEM_LIMIT),
        name="ffn_ln",
    )(h, wg, wu, wd, g.reshape(1, d), b.reshape(1, d))


def _tile_rows(x, reps):
    return jnp.concatenate([x] * reps, axis=0)


def _scaled_queries(q):
    assert math.log2(HEAD_DIM) % 2 == 0
    return (q * ATT_SCALE).astype(bf16)


def _masked_flash(qg, kt_at, v_at, c_lo, c_hi, bias_at):
    rows = qg.shape[0]

    def body(c, carry):
        m, l, acc = carry
        s = jnp.dot(qg, kt_at(c), preferred_element_type=f32)
        bias = bias_at(c)
        s = jnp.concatenate([s[r * QBLOCK:(r + 1) * QBLOCK] + bias for r in range(rows // QBLOCK)], axis=0)
        m_new = jnp.maximum(m, jnp.max(s, axis=1, keepdims=True))
        p = jnp.exp(s - m_new)
        alpha = jnp.exp(m - m_new)
        l = alpha * l + jnp.sum(p, axis=1, keepdims=True)
        acc = alpha * acc + jnp.dot(p.astype(bf16), v_at(c), preferred_element_type=f32)
        return m_new, l, acc

    init = (jnp.full((rows, 1), MASKED, f32), jnp.zeros((rows, 1), f32),
            jnp.zeros((rows, HEAD_DIM), f32))
    _, l, acc = lax.fori_loop(c_lo, c_hi, body, init)
    return acc / l


def _sortable_key(x):
    bits = pltpu.bitcast(x, jnp.int32)
    return bits ^ (jnp.right_shift(bits, 31) & jnp.int32(0x7FFFFFFF))


def _topk_threshold(count, topk):
    topk = float(topk)
    cnt_nonneg = count(lambda k, i: k >= 0)
    thr0 = jnp.where(cnt_nonneg >= topk, 0, INT_MIN).astype(jnp.int32)

    def bit_body(b, thr):
        cand = thr + jnp.left_shift(jnp.int32(1), 30 - b)
        return jnp.where(count(lambda k, i: k >= cand) >= topk, cand, thr)

    thr = lax.fori_loop(0, 31, bit_body, thr0)
    need = topk - count(lambda k, i: k > thr)
    return thr, need


def _tie_index_cut(count, rows, thr, need, idx_bits):
    def cut_body(b, cut):
        cand = cut + jnp.left_shift(jnp.int32(1), idx_bits - 1 - b)
        return jnp.where(count(lambda k, i: (k == thr) & (i < cand)) < need, cand, cut)

    return lax.fori_loop(0, idx_bits, cut_body, jnp.zeros((rows, 1), jnp.int32))


def _select_blocks(imp, q_pos, lane):
    ratio = SLC_BLOCK // CMP_BLOCK
    on_block_lane = (lane & (ratio - 1)) == 0
    blk = jnp.right_shift(lane, 1)
    cur = jnp.right_shift(q_pos, 6)
    future = blk > cur
    forced = (blk == 0) | (blk == cur) | (blk == cur - 1)
    imp = imp + pltpu.roll(imp, CMP_LANES - 1, 1)
    score = jnp.where(future, -jnp.inf, jnp.where(forced, jnp.inf, imp))
    score = jnp.where(on_block_lane, score, -jnp.inf)
    if imp.shape[0] == CMP_LANES:
        score_t = score.T
        blk_t = lax.broadcasted_iota(jnp.int32, score_t.shape, 0)
        rank_t = jnp.zeros(score_t.shape, f32)
        for b in range(CMP_LANES // ratio):
            other = score_t[ratio * b:ratio * b + 1, :]
            ahead = (other > score_t) | ((other == score_t) & (ratio * b < blk_t))
            rank_t = rank_t + jnp.where(ahead, 1.0, 0.0)
        rank = rank_t.T
    else:
        rank = jnp.zeros(imp.shape, f32)
        for b in range(CMP_LANES // ratio):
            col = score[:, ratio * b:ratio * b + 1]
            ahead = (col > score) | ((col == score) & (ratio * b < lane))
            rank = rank + jnp.where(ahead, 1.0, 0.0)
    return jnp.where((rank < float(SLC_TOPN)) & on_block_lane, 1.0, 0.0)


def _dsa_prompt_kernel(iq_ref, iw_ref, ikt_ref, q_ref, kt_ref, v_ref, tri_ref, o_ref, keys_ref, sel_ref):
    i = pl.program_id(1)
    nch = (i * QBLOCK + QBLOCK + KEY_CHUNK - 1) // KEY_CHUNK
    q_pos = i * QBLOCK + lax.broadcasted_iota(jnp.int32, (QBLOCK, 1), 0)
    lane = lax.broadcasted_iota(jnp.int32, (QBLOCK, KEY_CHUNK), 1)

    iq = iq_ref[...]
    iq_stack = jnp.concatenate(
        [iq[:, h * IDX_DIM:(h + 1) * IDX_DIM] for h in range(IDX_HEADS)], axis=0).astype(bf16)
    iw = iw_ref[...] * IDX_HEADS ** -0.5

    def score_body(c, carry):
        dots = jnp.dot(iq_stack, ikt_ref[c], preferred_element_type=f32) * IDX_DIM ** -0.5
        acc = jnp.zeros((QBLOCK, KEY_CHUNK), f32)
        for h in range(IDX_HEADS):
            acc = acc + iw[:, h:h + 1] * jnp.maximum(dots[h * QBLOCK:(h + 1) * QBLOCK], 0.0)
        causal = c * KEY_CHUNK + lane <= q_pos
        keys_ref[c] = jnp.where(causal, _sortable_key(acc), INT_MIN)
        return carry

    lax.fori_loop(0, nch, score_body, 0)

    def count(pred):
        def body(c, acc):
            hit = jnp.where(pred(keys_ref[c], c * KEY_CHUNK + lane), 1.0, 0.0)
            part = hit[:, :LANES]
            for j in range(1, KEY_CHUNK // LANES):
                part = part + hit[:, j * LANES:(j + 1) * LANES]
            return acc + part
        acc = lax.fori_loop(0, nch, body, jnp.zeros((QBLOCK, LANES), f32))
        return jnp.sum(acc, axis=1, keepdims=True)

    thr, need = _topk_threshold(count, DSA_TOPK)

    def sel_body(c, ties_before):
        k = keys_ref[c]
        tie = k == thr
        tie_rank = ties_before + jnp.dot(jnp.where(tie, 1.0, 0.0).astype(bf16), tri_ref[...],
                                         preferred_element_type=f32)
        chosen = ((k > thr) | (tie & (tie_rank <= need))) & (c * KEY_CHUNK + lane <= q_pos)
        sel_ref[c] = jnp.where(chosen, 0.0, MASKED)
        return tie_rank[:, KEY_CHUNK - 1:KEY_CHUNK]

    lax.fori_loop(0, nch, sel_body, jnp.zeros((QBLOCK, 1), f32))

    for g in range(KV_HEADS):
        o = _masked_flash(_scaled_queries(q_ref[g]), lambda c: kt_ref[g, c], lambda c: v_ref[g, c],
                          0, nch, lambda c: sel_ref[c])
        for r in range(Q_PER_KV):
            h = g * Q_PER_KV + r
            o_ref[:, h * HEAD_DIM:(h + 1) * HEAD_DIM] = o[r * QBLOCK:(r + 1) * QBLOCK]


def _stack_query_heads(q, n, t):
    q = q.reshape(n, t // QBLOCK, QBLOCK, KV_HEADS, Q_PER_KV, HEAD_DIM)
    q = q.transpose(0, 3, 1, 4, 2, 5)
    return q.reshape(n, KV_HEADS, t // QBLOCK, Q_PER_KV * QBLOCK, HEAD_DIM)


def _chunked_kv(kv, n, t, chunk):
    kv = kv.reshape(n, t // chunk, chunk, KV_HEADS, 2, HEAD_DIM).astype(bf16)
    kt = kv[:, :, :, :, 0, :].transpose(0, 3, 1, 4, 2)
    v = kv[:, :, :, :, 1, :].transpose(0, 3, 1, 2, 4)
    return kt, v


def dsa_prompt(dq, dkv, iq, ik, iw):
    n, t = dq.shape[:2]
    nchunks = t // KEY_CHUNK
    assert t % KEY_CHUNK == 0 and min(DSA_TOPK, t // 4) == DSA_TOPK
    ikt = ik.reshape(n, nchunks, KEY_CHUNK, IDX_DIM).transpose(0, 1, 3, 2).astype(bf16)
    qs = _stack_query_heads(dq, n, t)
    kt, v = _chunked_kv(dkv, n, t, KEY_CHUNK)
    at = jnp.arange(KEY_CHUNK)
    tri = (at[:, None] <= at[None, :]).astype(bf16)
    return pl.pallas_call(
        _dsa_prompt_kernel,
        out_shape=jax.ShapeDtypeStruct((n, t, GROUP_WIDTH), f32),
        grid=(n, t // QBLOCK),
        in_specs=[
            pl.BlockSpec((None, QBLOCK, IDX_HEADS * IDX_DIM), lambda b, i: (b, i, 0)),
            pl.BlockSpec((None, QBLOCK, IDX_HEADS), lambda b, i: (b, i, 0)),
            pl.BlockSpec((None, nchunks, IDX_DIM, KEY_CHUNK), lambda b, i: (b, 0, 0, 0)),
            pl.BlockSpec((None, KV_HEADS, None, Q_PER_KV * QBLOCK, HEAD_DIM), lambda b, i: (b, 0, i, 0, 0)),
            pl.BlockSpec((None, KV_HEADS, nchunks, HEAD_DIM, KEY_CHUNK), lambda b, i: (b, 0, 0, 0, 0)),
            pl.BlockSpec((None, KV_HEADS, nchunks, KEY_CHUNK, HEAD_DIM), lambda b, i: (b, 0, 0, 0, 0)),
            pl.BlockSpec((KEY_CHUNK, KEY_CHUNK), lambda b, i: (0, 0)),
        ],
        out_specs=pl.BlockSpec((None, QBLOCK, GROUP_WIDTH), lambda b, i: (b, i, 0)),
        scratch_shapes=[pltpu.VMEM((nchunks, QBLOCK, KEY_CHUNK), jnp.int32),
                        pltpu.VMEM((nchunks, QBLOCK, KEY_CHUNK), f32)],
        compiler_params=pltpu.CompilerParams(
            dimension_semantics=("parallel", "arbitrary"), vmem_limit_bytes=VMEM_LIMIT),
        name="dsa_prompt",
    )(iq, iw, ikt, qs, kt, v, tri)


def _nsa_compress_kernel(x_ref, pe_ref, w1_ref, w2_ref, o_ref, hid_ref):
    k = pl.program_id(0)

    @pl.when(k == 0)
    def _():
        hid_ref[...] = jnp.zeros_like(hid_ref)

    hid_ref[...] += jnp.dot((x_ref[...] + pe_ref[...]).astype(bf16), w1_ref[...],
                            preferred_element_type=f32)

    @pl.when(k == pl.num_programs(0) - 1)
    def _():
        hid = jax.nn.gelu(hid_ref[...]).astype(bf16)
        o_ref[...] = jnp.dot(hid, w2_ref[...], preferred_element_type=f32)


def nsa_compress_weights(cmp_pe, cmp_w1, cmp_w2):
    w1 = cmp_w1.reshape(2, CMP_BLOCK, HEAD_DIM, CMP_HIDDEN)
    eye_g = jnp.eye(KV_HEADS, dtype=cmp_w1.dtype)
    eye_j = jnp.eye(2, dtype=cmp_w1.dtype)
    w1_big = jnp.einsum('jtde,gh,jk->tgjdhke', w1, eye_g, eye_j)
    w1_big = w1_big.reshape(CMP_BLOCK * KV_ROW, KV_HEADS * 2 * CMP_HIDDEN).astype(bf16)
    w2_big = jnp.einsum('jed,gh,jk->gjehkd', cmp_w2, eye_g, eye_j)
    w2_big = w2_big.reshape(KV_HEADS * 2 * CMP_HIDDEN, KV_ROW).astype(bf16)
    pe_row = jnp.broadcast_to(cmp_pe[:, None], (CMP_BLOCK, KV_HEADS, 2, HEAD_DIM))
    return pe_row.reshape(1, CMP_BLOCK * KV_ROW), w1_big, w2_big


def nsa_compress_rows(x, pe_row, w1_big, w2_big, tk=2048):
    m, kdim = x.shape
    hw = w1_big.shape[1]
    return pl.pallas_call(
        _nsa_compress_kernel,
        out_shape=jax.ShapeDtypeStruct((m, KV_ROW), f32),
        grid=(kdim // tk,),
        in_specs=[pl.BlockSpec((m, tk), lambda k: (0, k)),
                  pl.BlockSpec((1, tk), lambda k: (0, k)),
                  pl.BlockSpec((tk, hw), lambda k: (k, 0)),
                  pl.BlockSpec((hw, KV_ROW), lambda k: (0, 0))],
        out_specs=pl.BlockSpec((m, KV_ROW), lambda k: (0, 0)),
        scratch_shapes=[pltpu.VMEM((m, hw), f32)],
        compiler_params=pltpu.CompilerParams(
            dimension_semantics=("arbitrary",), vmem_limit_bytes=VMEM_LIMIT),
        name="nsa_compress",
    )(x, pe_row, w1_big, w2_big)


def _nsa_prompt_kernel(gate_ref, q_ref, ckt_ref, cv_ref, skt_ref, sv_ref, wkt_ref, wv_ref, o_ref):
    i = pl.program_id(1)
    nch = (i * QBLOCK + QBLOCK + KEY_CHUNK - 1) // KEY_CHUNK
    q_pos = i * QBLOCK + lax.broadcasted_iota(jnp.int32, (QBLOCK, 1), 0)
    lane = lax.broadcasted_iota(jnp.int32, (QBLOCK, CMP_LANES), 1)
    lane_k = lax.broadcasted_iota(jnp.int32, (QBLOCK, KEY_CHUNK), 1)
    expand_row = lax.broadcasted_iota(jnp.int32, (CMP_LANES, KEY_CHUNK), 0)
    expand_col = lax.broadcasted_iota(jnp.int32, (CMP_LANES, KEY_CHUNK), 1)
    gate = jax.nn.sigmoid(gate_ref[...])
    ratio = SLC_BLOCK // CMP_BLOCK
    c_vis = _tile_rows(jnp.where((lane + 1) * CMP_BLOCK - 1 <= q_pos, 1.0, 0.0), Q_PER_KV) > 0.5
    w_tiles = WINDOW // QBLOCK + 1
    w_lo = jnp.maximum(i - WINDOW // QBLOCK, 0)
    w_idx = w_lo * QBLOCK + lax.broadcasted_iota(jnp.int32, (QBLOCK, w_tiles * QBLOCK), 1)
    w_bias = _tile_rows(jnp.where((w_idx <= q_pos) & (w_idx > q_pos - WINDOW), 0.0, MASKED), Q_PER_KV)

    for g in range(KV_HEADS):
        qg = _scaled_queries(q_ref[g])
        s = jnp.dot(qg, ckt_ref[g], preferred_element_type=f32)
        s = jnp.where(c_vis, s, -jnp.inf)
        m = jnp.max(s, axis=1, keepdims=True)
        m = jnp.where(m == -jnp.inf, 0.0, m)
        p = jnp.exp(s - m)
        p = p / jnp.maximum(jnp.sum(p, axis=1, keepdims=True), F32_TINY)
        o_cmp = jnp.dot(p.astype(bf16), cv_ref[g], preferred_element_type=f32)
        imp = p[0:QBLOCK]
        for r in range(1, Q_PER_KV):
            imp = imp + p[r * QBLOCK:(r + 1) * QBLOCK]
        sel_blk = _select_blocks(imp, q_pos, lane).astype(bf16)

        def slc_mask(c):
            kidx = c * KEY_CHUNK + lane_k
            expand = jnp.where(
                expand_row == ratio * jnp.right_shift(c * KEY_CHUNK + expand_col, 6), 1.0, 0.0).astype(bf16)
            picked = jnp.dot(sel_blk, expand, preferred_element_type=f32)
            return jnp.where((picked > 0.5) & (kidx <= q_pos), 0.0, MASKED)

        o_slc = _masked_flash(qg, lambda c: skt_ref[g, c], lambda c: sv_ref[g, c], 0, nch, slc_mask)

        w_kt = jnp.concatenate([wkt_ref[g, w_lo + j] for j in range(w_tiles)], axis=1)
        w_v = jnp.concatenate([wv_ref[g, w_lo + j] for j in range(w_tiles)], axis=0)
        s = jnp.dot(qg, w_kt, preferred_element_type=f32) + w_bias
        pw = jnp.exp(s - jnp.max(s, axis=1, keepdims=True))
        o_win = (jnp.dot(pw.astype(bf16), w_v, preferred_element_type=f32)
                 / jnp.sum(pw, axis=1, keepdims=True))
        for r in range(Q_PER_KV):
            h = g * Q_PER_KV + r
            rows = slice(r * QBLOCK, (r + 1) * QBLOCK)
            o_ref[:, h * HEAD_DIM:(h + 1) * HEAD_DIM] = (
                gate[:, 3 * h:3 * h + 1] * o_cmp[rows]
                + gate[:, 3 * h + 1:3 * h + 2] * o_slc[rows]
                + gate[:, 3 * h + 2:3 * h + 3] * o_win[rows])


def nsa_prompt(nq, ng, ckv, skv, wkv, cmp_weights):
    n, t = nq.shape[:2]
    nc = t // CMP_BLOCK
    assert nc <= CMP_LANES and t % KEY_CHUNK == 0 and SLC_BLOCK == 64 and SLC_BLOCK // CMP_BLOCK == 2
    assert t >= WINDOW + QBLOCK and WINDOW % QBLOCK == 0
    cmp_rows = nsa_compress_rows(ckv.reshape(n * nc, CMP_BLOCK * KV_ROW), *cmp_weights)
    cmp_rows = cmp_rows.reshape(n, nc, KV_HEADS, 2, HEAD_DIM)
    cmp_rows = jnp.pad(cmp_rows, ((0, 0), (0, CMP_LANES - nc), (0, 0), (0, 0), (0, 0))).astype(bf16)
    ckt = cmp_rows[:, :, :, 0, :].transpose(0, 2, 3, 1)
    cv = cmp_rows[:, :, :, 1, :].transpose(0, 2, 1, 3)
    qs = _stack_query_heads(nq, n, t)
    skt, sv = _chunked_kv(skv, n, t, KEY_CHUNK)
    wkt, wv = _chunked_kv(wkv, n, t, QBLOCK)
    nk, nw = t // KEY_CHUNK, t // QBLOCK
    return pl.pallas_call(
        _nsa_prompt_kernel,
        out_shape=jax.ShapeDtypeStruct((n, t, GROUP_WIDTH), f32),
        grid=(n, t // QBLOCK),
        in_specs=[
            pl.BlockSpec((None, QBLOCK, 3 * N_HEADS), lambda b, i: (b, i, 0)),
            pl.BlockSpec((None, KV_HEADS, None, Q_PER_KV * QBLOCK, HEAD_DIM), lambda b, i: (b, 0, i, 0, 0)),
            pl.BlockSpec((None, KV_HEADS, HEAD_DIM, CMP_LANES), lambda b, i: (b, 0, 0, 0)),
            pl.BlockSpec((None, KV_HEADS, CMP_LANES, HEAD_DIM), lambda b, i: (b, 0, 0, 0)),
            pl.BlockSpec((None, KV_HEADS, nk, HEAD_DIM, KEY_CHUNK), lambda b, i: (b, 0, 0, 0, 0)),
            pl.BlockSpec((None, KV_HEADS, nk, KEY_CHUNK, HEAD_DIM), lambda b, i: (b, 0, 0, 0, 0)),
            pl.BlockSpec((None, KV_HEADS, nw, HEAD_DIM, QBLOCK), lambda b, i: (b, 0, 0, 0, 0)),
            pl.BlockSpec((None, KV_HEADS, nw, QBLOCK, HEAD_DIM), lambda b, i: (b, 0, 0, 0, 0)),
        ],
        out_specs=pl.BlockSpec((None, QBLOCK, GROUP_WIDTH), lambda b, i: (b, i, 0)),
        compiler_params=pltpu.CompilerParams(
            dimension_semantics=("parallel", "arbitrary"), vmem_limit_bytes=VMEM_LIMIT),
        name="nsa_prompt",
    )(ng, qs, ckt, cv, skt, sv, wkt, wv)


def s5_discretize(w):
    a_re, a_im = w['a_re'], w['a_im']
    dt = jnp.exp(w['log_dt'])[:, None]
    mag = jnp.exp(a_re * dt)
    ab_re, ab_im = mag * jnp.cos(a_im * dt), mag * jnp.sin(a_im * dt)
    den = a_re * a_re + a_im * a_im
    nr, ni = ab_re - 1.0, ab_im
    f_re = (nr * a_re + ni * a_im) / den
    f_im = (ni * a_re - nr * a_im) / den
    b_re, b_im = w['b_re'], w['b_im']
    bb_re = f_re[..., None] * b_re - f_im[..., None] * b_im
    bb_im = f_re[..., None] * b_im + f_im[..., None] * b_re
    eye = jnp.eye(S5_GROUPS, dtype=f32)

    def in_map(bb):
        return jnp.einsum('gsc,gh->gchs', bb, eye).reshape(GROUP_WIDTH, S5_WIDTH).astype(bf16)

    def out_map(cc):
        return jnp.einsum('gcs,gh->gshc', cc, eye).reshape(S5_WIDTH, GROUP_WIDTH).astype(bf16)

    return dict(a_re=ab_re.reshape(1, S5_WIDTH), a_im=ab_im.reshape(1, S5_WIDTH),
                b_re=in_map(bb_re), b_im=in_map(bb_im),
                c_re=out_map(w['c_re']), c_im=out_map(w['c_im']),
                d=w['d'].reshape(1, GROUP_WIDTH), w_glu=w['w_glu'].astype(bf16),
                b_glu=w['b_glu'].reshape(1, GROUP_WIDTH))


def _s5_prompt_kernel(u_ref, h0_ref, are_ref, aim_ref, bre_ref, bim_ref, cre_ref, cim_ref, d_ref,
                      wglu_ref, bglu_ref, o_ref, hlast_ref, state_ref, bure_ref, buim_ref, hre_ref, him_ref):
    j = pl.program_id(1)

    @pl.when(j == 0)
    def _():
        state_ref[...] = h0_ref[...]

    u = u_ref[...]
    ub = u.astype(bf16)
    for half in range(2):
        ch = slice(half * GROUP_WIDTH // 2, (half + 1) * GROUP_WIDTH // 2)
        st = slice(half * S5_WIDTH // 2, (half + 1) * S5_WIDTH // 2)
        bure_ref[:, st] = jnp.dot(ub[:, ch], bre_ref[ch, st], preferred_element_type=f32)
        buim_ref[:, st] = jnp.dot(ub[:, ch], bim_ref[ch, st], preferred_element_type=f32)
    a_re, a_im = are_ref[...], aim_ref[...]

    def step(t, carry):
        h_re, h_im = carry
        row = pl.ds(t, 1)
        n_re = a_re * h_re - a_im * h_im + bure_ref[row, :]
        n_im = a_re * h_im + a_im * h_re + buim_ref[row, :]
        hre_ref[row, :] = n_re
        him_ref[row, :] = n_im
        return n_re, n_im

    h_re, h_im = lax.fori_loop(0, u.shape[0], step, (state_ref[0:1, :], state_ref[1:2, :]), unroll=8)
    state_ref[0:1, :] = h_re
    state_ref[1:2, :] = h_im
    hlast_ref[...] = state_ref[...]
    ys = []
    for half in range(2):
        ch = slice(half * GROUP_WIDTH // 2, (half + 1) * GROUP_WIDTH // 2)
        st = slice(half * S5_WIDTH // 2, (half + 1) * S5_WIDTH // 2)
        ys.append(jnp.dot(hre_ref[:, st].astype(bf16), cre_ref[st, ch], preferred_element_type=f32)
                  - jnp.dot(him_ref[:, st].astype(bf16), cim_ref[st, ch], preferred_element_type=f32))
    y = jnp.concatenate(ys, axis=1) + d_ref[...] * u
    y = jax.nn.gelu(y)
    o_ref[...] = y * jax.nn.sigmoid(
        jnp.dot(y.astype(bf16), wglu_ref[...], preferred_element_type=f32) + bglu_ref[...])


def s5_prompt(u, h0, sw):
    n, t = u.shape[:2]
    tc = min(S5_TIME_CHUNK, t)
    const = lambda shape: pl.BlockSpec(shape, lambda b, j: (0,) * len(shape))
    return pl.pallas_call(
        _s5_prompt_kernel,
        out_shape=(jax.ShapeDtypeStruct((n, t, GROUP_WIDTH), f32),
                   jax.ShapeDtypeStruct((n, 2, S5_WIDTH), f32)),
        grid=(n, t // tc),
        in_specs=[pl.BlockSpec((None, tc, GROUP_WIDTH), lambda b, j: (b, j, 0)),
                  pl.BlockSpec((None, 2, S5_WIDTH), lambda b, j: (b, 0, 0)),
                  const((1, S5_WIDTH)), const((1, S5_WIDTH)),
                  const((GROUP_WIDTH, S5_WIDTH)), const((GROUP_WIDTH, S5_WIDTH)),
                  const((S5_WIDTH, GROUP_WIDTH)), const((S5_WIDTH, GROUP_WIDTH)),
                  const((1, GROUP_WIDTH)), const((GROUP_WIDTH, GROUP_WIDTH)), const((1, GROUP_WIDTH))],
        out_specs=(pl.BlockSpec((None, tc, GROUP_WIDTH), lambda b, j: (b, j, 0)),
                   pl.BlockSpec((None, 2, S5_WIDTH), lambda b, j: (b, 0, 0))),
        scratch_shapes=[pltpu.VMEM((2, S5_WIDTH), f32)] + [pltpu.VMEM((tc, S5_WIDTH), f32)] * 4,
        compiler_params=pltpu.CompilerParams(
            dimension_semantics=("parallel", "arbitrary"), vmem_limit_bytes=VMEM_LIMIT),
        name="s5_prompt",
    )(u, h0, sw['a_re'], sw['a_im'], sw['b_re'], sw['b_im'], sw['c_re'], sw['c_im'], sw['d'],
      sw['w_glu'], sw['b_glu'])


def _s5_step_kernel(u_ref, h0_ref, are_ref, aim_ref, bre_ref, bim_ref, cre_ref, cim_ref, d_ref,
                    wglu_ref, bglu_ref, o_ref, h_ref):
    u = u_ref[...]
    ub = u.astype(bf16)
    h0_re, h0_im = h0_ref[:, :S5_WIDTH], h0_ref[:, S5_WIDTH:]
    a_re, a_im = are_ref[...], aim_ref[...]
    h_re = a_re * h0_re - a_im * h0_im + jnp.dot(ub, bre_ref[...], preferred_element_type=f32)
    h_im = a_re * h0_im + a_im * h0_re + jnp.dot(ub, bim_ref[...], preferred_element_type=f32)
    h_ref[:, :S5_WIDTH] = h_re
    h_ref[:, S5_WIDTH:] = h_im
    y = (jnp.dot(h_re.astype(bf16), cre_ref[...], preferred_element_type=f32)
         - jnp.dot(h_im.astype(bf16), cim_ref[...], preferred_element_type=f32)
         + d_ref[...] * u)
    y = jax.nn.gelu(y)
    o_ref[...] = y * jax.nn.sigmoid(
        jnp.dot(y.astype(bf16), wglu_ref[...], preferred_element_type=f32) + bglu_ref[...])


def s5_step(u, h0, sw):
    n = u.shape[0]
    return pl.pallas_call(
        _s5_step_kernel,
        out_shape=(jax.ShapeDtypeStruct((n, GROUP_WIDTH), f32),
                   jax.ShapeDtypeStruct((n, 2 * S5_WIDTH), f32)),
        compiler_params=pltpu.CompilerParams(vmem_limit_bytes=VMEM_LIMIT),
        name="s5_step",
    )(u, h0, sw['a_re'], sw['a_im'], sw['b_re'], sw['b_im'], sw['c_re'], sw['c_im'], sw['d'],
      sw['w_glu'], sw['b_glu'])


GLA_SUB = 16
GLA_QK = GLA_HEADS * GLA_DK
HIGHEST = lax.Precision.HIGHEST
_NT = (((1,), (1,)), ((), ()))
_TN = (((0,), (0,)), ((), ()))


def _log_decay(ga, wgate_ref, bgate_ref):
    z = jnp.dot(ga, wgate_ref[...], preferred_element_type=f32, precision=HIGHEST) + bgate_ref[...]
    return jax.nn.log_sigmoid(z) / GLA_TAU


def _head_sum_matrix():
    r = lax.broadcasted_iota(jnp.int32, (GLA_QK, GROUP_WIDTH), 0) // GLA_DK
    c = lax.broadcasted_iota(jnp.int32, (GLA_QK, GROUP_WIDTH), 1) // GLA_DV
    return jnp.where(r == c, 1.0, 0.0)


def _gla_finish(o, gr, normg_ref):
    outs = []
    for h in range(GLA_HEADS):
        oh = o[:, h * GLA_DV:(h + 1) * GLA_DV]
        outs.append(oh * lax.rsqrt(jnp.mean(oh * oh, axis=-1, keepdims=True) + LN_EPS) * normg_ref[...])
    return jnp.concatenate(outs, axis=1) * (gr * jax.nn.sigmoid(gr))


def _gla_prompt_kernel(q_ref, k_ref, v_ref, ga_ref, gr_ref, wgate_ref, bgate_ref, normg_ref, s0_ref,
                       o_ref, slast_ref, st_ref, oi_ref):
    j = pl.program_id(1)

    @pl.when(j == 0)
    def _():
        st_ref[...] = s0_ref[...]

    rows = q_ref.shape[0]
    nsub = rows // GLA_SUB
    q = q_ref[...] * GLA_DK ** -0.5
    k = k_ref[...]
    v = v_ref[...]
    log_a = _log_decay(ga_ref[...], wgate_ref, bgate_ref)
    ri = lax.broadcasted_iota(jnp.int32, (rows, rows), 0)
    ci = lax.broadcasted_iota(jnp.int32, (rows, rows), 1)
    tri = jnp.where((ri // GLA_SUB == ci // GLA_SUB) & (ci <= ri), 1.0, 0.0)
    b = jnp.dot(tri, log_a, preferred_element_type=f32, precision=HIGHEST)
    row_in_sub = lax.broadcasted_iota(jnp.int32, (rows, 1), 0) % GLA_SUB
    head_sum = _head_sum_matrix().astype(bf16)

    def sub_row(x, jj):
        x3 = x.reshape(nsub, GLA_SUB, x.shape[1])
        return jnp.broadcast_to(x3[:, jj:jj + 1, :], x3.shape).reshape(x.shape)

    o = jnp.zeros((rows, GROUP_WIDTH), f32)
    for jj in range(GLA_SUB):
        decay = jnp.exp(jnp.minimum(b - sub_row(b, jj), 0.0))
        prod = jnp.where(row_in_sub >= jj, q * sub_row(k, jj) * decay, 0.0)
        prod_hi = prod.astype(bf16)
        prod_lo = (prod - prod_hi.astype(f32)).astype(bf16)
        att = []
        for half in range(2):
            ks = slice(half * GLA_QK // 2, (half + 1) * GLA_QK // 2)
            vs = slice(half * GROUP_WIDTH // 2, (half + 1) * GROUP_WIDTH // 2)
            att.append(jnp.dot(prod_hi[:, ks], head_sum[ks, vs], preferred_element_type=f32)
                       + jnp.dot(prod_lo[:, ks], head_sum[ks, vs], preferred_element_type=f32))
        o = o + jnp.concatenate(att, axis=1) * sub_row(v, jj)

    for i in range(nsub):
        blk = slice(i * GLA_SUB, (i + 1) * GLA_SUB)
        b_blk = b[blk]
        b_end = b[(i + 1) * GLA_SUB - 1:(i + 1) * GLA_SUB]
        q_blk = (q[blk] * jnp.exp(b_blk)).astype(bf16)
        k_blk = (k[blk] * jnp.exp(b_end - b_blk)).astype(bf16)
        v_blk = v[blk].astype(bf16)
        a_end = jnp.exp(b_end)
        for h in range(GLA_HEADS):
            ks = slice(h * GLA_DK, (h + 1) * GLA_DK)
            vs = slice(h * GLA_DV, (h + 1) * GLA_DV)
            st = st_ref[h]
            oi_ref[blk, vs] = lax.dot_general(q_blk[:, ks], st.astype(bf16), _NT, preferred_element_type=f32)
            st_ref[h] = st * a_end[:, ks] + lax.dot_general(v_blk[:, vs], k_blk[:, ks], _TN,
                                                            preferred_element_type=f32)

    o_ref[...] = _gla_finish(o + oi_ref[...], gr_ref[...], normg_ref)
    slast_ref[...] = st_ref[...]


def gla_prompt(gq, gk, gv, ga, gr, w_gate, b_gate, norm_g):
    n, t = gq.shape[:2]
    tc = min(GLA_CHUNK, t)
    rows = lambda width: pl.BlockSpec((None, tc, width), lambda b, j: (b, j, 0))
    const = lambda shape: pl.BlockSpec(shape, lambda b, j: (0,) * len(shape))
    state = pl.BlockSpec((None, GLA_HEADS, GLA_DV, GLA_DK), lambda b, j: (b, 0, 0, 0))
    o, s_t = pl.pallas_call(
        _gla_prompt_kernel,
        out_shape=(jax.ShapeDtypeStruct((n, t, GROUP_WIDTH), f32),
                   jax.ShapeDtypeStruct((n, GLA_HEADS, GLA_DV, GLA_DK), f32)),
        grid=(n, t // tc),
        in_specs=[rows(GLA_QK), rows(GLA_QK), rows(GROUP_WIDTH), rows(GLA_LOWRANK), rows(GROUP_WIDTH),
                  const((GLA_LOWRANK, GLA_QK)), const((1, GLA_QK)), const((1, GLA_DV)), state],
        out_specs=(rows(GROUP_WIDTH), state),
        scratch_shapes=[pltpu.VMEM((GLA_HEADS, GLA_DV, GLA_DK), f32), pltpu.VMEM((tc, GROUP_WIDTH), f32)],
        compiler_params=pltpu.CompilerParams(
            dimension_semantics=("parallel", "arbitrary"), vmem_limit_bytes=VMEM_LIMIT),
        name="gla_prompt",
    )(gq, gk, gv, ga, gr, w_gate, b_gate.reshape(1, GLA_QK), norm_g.reshape(1, GLA_DV),
      jnp.zeros((n, GLA_HEADS, GLA_DV, GLA_DK), f32))
    return o, s_t.transpose(0, 1, 3, 2)


GLA_STEP_SEQS = 8


def _gla_step_kernel(q_ref, k_ref, v_ref, ga_ref, gr_ref, wgate_ref, bgate_ref, normg_ref, s_ref,
                     o_ref, snew_ref):
    q = q_ref[...] * GLA_DK ** -0.5
    k = k_ref[...]
    v = v_ref[...]
    a = jnp.exp(_log_decay(ga_ref[...], wgate_ref, bgate_ref))
    qa = q * a
    seq = lax.broadcasted_iota(jnp.int32, (GLA_STEP_SEQS, 1), 0)
    o = jnp.dot(q * k, _head_sum_matrix(), preferred_element_type=f32, precision=HIGHEST) * v
    o_heads = [jnp.zeros((GLA_STEP_SEQS, GLA_DV), f32) for _ in range(GLA_HEADS)]
    for i in range(GLA_STEP_SEQS):
        mine = seq == i
        pick = jnp.broadcast_to(jnp.where(mine, 1.0, 0.0), (GLA_STEP_SEQS, GLA_DV))
        qa_i = jnp.where(mine, qa, 0.0).astype(bf16)
        k_i = jnp.where(mine, k, 0.0).astype(bf16)
        vb = v.astype(bf16)
        for h in range(GLA_HEADS):
            ks = slice(h * GLA_DK, (h + 1) * GLA_DK)
            vs = slice(h * GLA_DV, (h + 1) * GLA_DV)
            s_old = s_ref[i, h]
            a_rows = lax.dot_general(a[:, ks], pick, _TN, preferred_element_type=f32, precision=HIGHEST)
            o_heads[h] = o_heads[h] + jnp.dot(qa_i[:, ks], s_old.astype(bf16), preferred_element_type=f32)
            snew_ref[i, h] = a_rows * s_old + lax.dot_general(k_i[:, ks], vb[:, vs], _TN,
                                                               preferred_element_type=f32)
    o_ref[...] = _gla_finish(o + jnp.concatenate(o_heads, axis=1), gr_ref[...], normg_ref)


def gla_step(gq, gk, gv, ga, gr, w_gate, b_gate, norm_g, s0):
    n = gq.shape[0]
    rows = lambda width: pl.BlockSpec((GLA_STEP_SEQS, width), lambda i: (i, 0))
    const = lambda shape: pl.BlockSpec(shape, lambda i: (0,) * len(shape))
    state = pl.BlockSpec((GLA_STEP_SEQS, GLA_HEADS, GLA_DK, GLA_DV), lambda i: (i, 0, 0, 0))
    return pl.pallas_call(
        _gla_step_kernel,
        out_shape=(jax.ShapeDtypeStruct((n, GROUP_WIDTH), f32),
                   jax.ShapeDtypeStruct((n, GLA_HEADS, GLA_DK, GLA_DV), f32)),
        grid=(n // GLA_STEP_SEQS,),
        in_specs=[rows(GLA_QK), rows(GLA_QK), rows(GROUP_WIDTH), rows(GLA_LOWRANK), rows(GROUP_WIDTH),
                  const((GLA_LOWRANK, GLA_QK)), const((1, GLA_QK)), const((1, GLA_DV)), state],
        out_specs=(rows(GROUP_WIDTH), state),
        compiler_params=pltpu.CompilerParams(
            dimension_semantics=("parallel",), vmem_limit_bytes=VMEM_LIMIT),
        name="gla_step",
    )(gq, gk, gv, ga, gr, w_gate, b_gate.reshape(1, GLA_QK), norm_g.reshape(1, GLA_DV), s0)


N_PAGES = PAST_LEN // PAGE_SIZE
KEY_PAD = PAST_LEN + LANES
STEP_ROWS = 8
CMP_STEP_SEQS = 2
BLOCKS_PER_PAGE = PAGE_SIZE // CMP_BLOCK


def _tokens_minor(cache):
    token_axis = 2
    return jnp.moveaxis(cache, token_axis, -1)


def _seq_spec(*tail):
    zeros = (0,) * len(tail)
    return pl.BlockSpec((None,) + tail, lambda i, pt: (i,) + zeros)


def _layer_page_specs(block_tail, layer, seqs_per_step=1):
    zeros = (0,) * len(block_tail)
    return [pl.BlockSpec((None, None) + block_tail,
                         lambda i, pt, s=s, p=p: (layer, pt[i * seqs_per_step + s, p]) + zeros)
            for s in range(seqs_per_step) for p in range(N_PAGES)]


def _layer_seq_spec(tail, layer):
    zeros = (0,) * len(tail)
    return pl.BlockSpec((None, None) + tail, lambda i, pt: (layer, i) + zeros)


def _topk_rows_kernel(keys_ref, sel_ref, *, topk, idx_bits, n_valid):
    keys = keys_ref[...]
    lane = lax.broadcasted_iota(jnp.int32, keys.shape, 1)
    count = lambda pred: jnp.sum(jnp.where(pred(keys, lane), 1.0, 0.0), axis=1, keepdims=True)
    thr, need = _topk_threshold(count, topk)
    cut = _tie_index_cut(count, keys.shape[0], thr, need, idx_bits)
    chosen = ((keys > thr) | ((keys == thr) & (lane <= cut))) & (lane < n_valid)
    sel_ref[...] = jnp.where(chosen, 1.0, 0.0)


TOKEN_PAIRS = CMP_BLOCK // 2
PAIR_ROWS = 2 * BLOCKS_PER_PAGE


def _pair_regroup_matrix():
    r = np.arange(TOKEN_PAIRS * PAIR_ROWS)
    k = np.arange(2 * PAGE_SIZE)
    tp, page, blk = r // PAIR_ROWS, (r // BLOCKS_PER_PAGE) % 2, r % BLOCKS_PER_PAGE
    page_k, blk_k, t_k = k // PAGE_SIZE, (k % PAGE_SIZE) // CMP_BLOCK, k % CMP_BLOCK
    hit = ((page[:, None] == page_k[None]) & (blk[:, None] == blk_k[None]) & (tp[:, None] == t_k[None] // 2))
    return jnp.asarray(hit, bf16)


def _nsa_compress_paged_kernel(pt_ref, pe_ref, regroup_ref, w1_ref, w2_ref, *rest):
    npg = CMP_STEP_SEQS * N_PAGES
    pages, o_ref, x_ref = rest[:npg], rest[npg], rest[npg + 1]
    blocks_per_seq = N_PAGES * BLOCKS_PER_PAGE
    even_token = (lax.broadcasted_iota(jnp.int32, (HEAD_DIM, 2 * PAGE_SIZE), 1) & 1) == 0
    regroup = regroup_ref[...]
    for pair in range(npg // 2):
        for gj in range(2 * KV_HEADS):
            g, j = gj // 2, gj % 2
            two = jnp.concatenate([pages[2 * pair][g, j] + pe_ref[j], pages[2 * pair + 1][g, j] + pe_ref[j]],
                                  axis=1)
            split = jnp.concatenate([jnp.where(even_token, two, 0.0), jnp.where(even_token, 0.0, two)],
                                    axis=0).astype(bf16)
            rows = lax.dot_general(regroup, split, _NT, preferred_element_type=f32)
            for tp in range(TOKEN_PAIRS):
                x_ref[gj, tp, pair * PAIR_ROWS:(pair + 1) * PAIR_ROWS, :] = rows[tp * PAIR_ROWS:(tp + 1) * PAIR_ROWS]
    for gj in range(2 * KV_HEADS):
        j = gj % 2
        x = jnp.concatenate([x_ref[gj, tp] for tp in range(TOKEN_PAIRS)], axis=1).astype(bf16)
        hid = jnp.dot(x, w1_ref[j], preferred_element_type=f32)
        out = jnp.dot(jax.nn.gelu(hid).astype(bf16), w2_ref[j], preferred_element_type=f32)
        for s in range(CMP_STEP_SEQS):
            o_ref[s, gj] = out[s * blocks_per_seq:(s + 1) * blocks_per_seq]


def nsa_compress_paged(pools_cmp, layer, page_table, cmp_pe, cmp_w1, cmp_w2):
    n = page_table.shape[0]
    blocks = N_PAGES * BLOCKS_PER_PAGE
    assert (CMP_STEP_SEQS * N_PAGES) % 2 == 0 and CMP_BLOCK % 2 == 0 and PAIR_ROWS == 8
    pe_page = jnp.tile(cmp_pe.transpose(1, 2, 0), (1, 1, BLOCKS_PER_PAGE))
    const = lambda shape: pl.BlockSpec(shape, lambda i, pt: (0,) * len(shape))
    return pl.pallas_call(
        _nsa_compress_paged_kernel,
        out_shape=jax.ShapeDtypeStruct((n, 2 * KV_HEADS, blocks, HEAD_DIM), f32),
        grid_spec=pltpu.PrefetchScalarGridSpec(
            num_scalar_prefetch=1, grid=(n // CMP_STEP_SEQS,),
            in_specs=[const((2, HEAD_DIM, PAGE_SIZE)), const((TOKEN_PAIRS * PAIR_ROWS, 2 * PAGE_SIZE)),
                      const((2, CMP_BLOCK * HEAD_DIM, CMP_HIDDEN)), const((2, CMP_HIDDEN, HEAD_DIM))]
            + _layer_page_specs((KV_HEADS, 2, HEAD_DIM, PAGE_SIZE), layer, CMP_STEP_SEQS),
            out_specs=pl.BlockSpec((CMP_STEP_SEQS, 2 * KV_HEADS, blocks, HEAD_DIM), lambda i, pt: (i, 0, 0, 0)),
            scratch_shapes=[pltpu.VMEM((2 * KV_HEADS, TOKEN_PAIRS, CMP_STEP_SEQS * blocks, 2 * HEAD_DIM), f32)]),
        compiler_params=pltpu.CompilerParams(
            dimension_semantics=("parallel",), vmem_limit_bytes=VMEM_LIMIT),
        name="nsa_compress_paged",
    )(page_table, pe_page, _pair_regroup_matrix(), cmp_w1.astype(bf16), cmp_w2.astype(bf16),
      *([pools_cmp] * (CMP_STEP_SEQS * N_PAGES)))


def _decode_attention(q8, blocks, masks, kv_new, new_ok, is_g0):
    qb = q8.astype(bf16)
    by_group = lambda fn: jnp.where(is_g0, fn(0), fn(1))
    s_new = by_group(lambda g: jnp.sum(q8 * kv_new[2 * g:2 * g + 1], axis=1, keepdims=True)) * ATT_SCALE
    if new_ok is not None:
        s_new = jnp.where(new_ok, s_new, MASKED)
    m = s_new
    scores = []
    for blk, msk in zip(blocks, masks):
        s = by_group(lambda g: jnp.dot(qb, blk[g, 0].astype(bf16), preferred_element_type=f32)) * ATT_SCALE
        s = jnp.where(msk > 0.5, s, MASKED)
        scores.append(s)
        m = jnp.maximum(m, jnp.max(s, axis=1, keepdims=True))
    p_new = jnp.exp(s_new - m)
    if new_ok is not None:
        p_new = jnp.where(new_ok, p_new, 0.0)
    l = p_new
    acc = p_new * by_group(lambda g: kv_new[2 * g + 1:2 * g + 2])
    for blk, msk, s in zip(blocks, masks, scores):
        p = jnp.where(msk > 0.5, jnp.exp(s - m), 0.0)
        l = l + jnp.sum(p, axis=1, keepdims=True)
        pb = p.astype(bf16)
        acc = acc + by_group(lambda g: lax.dot_general(pb, blk[g, 1].astype(bf16), _NT,
                                                       preferred_element_type=f32))
    return acc / l


def _dsa_decode_scores_kernel(pt_ref, iq_ref, iw_ref, iknew_ref, *rest):
    pages, keys_ref = rest[:N_PAGES], rest[N_PAGES]
    iq = iq_ref[...]
    iqb = iq.astype(bf16)
    iw = iw_ref[...] * IDX_HEADS ** -0.5
    weigh = lambda dots, w: jnp.sum(w * jnp.maximum(dots * IDX_DIM ** -0.5, 0.0), axis=0, keepdims=True)
    for p in range(N_PAGES):
        dots = jnp.dot(iqb, pages[p][...].astype(bf16), preferred_element_type=f32)
        keys_ref[:, p * PAGE_SIZE:(p + 1) * PAGE_SIZE] = _sortable_key(weigh(dots, iw))
    new_score = weigh(jnp.sum(iq * iknew_ref[...], axis=1, keepdims=True), iw[:, :1])
    lane = lax.broadcasted_iota(jnp.int32, (1, LANES), 1)
    keys_ref[:, PAST_LEN:] = jnp.where(lane == 0, _sortable_key(jnp.broadcast_to(new_score, (1, LANES))), INT_MIN)


def _dsa_decode_attend_kernel(pt_ref, q_ref, sel_ref, kvnew_ref, *rest):
    pages, o_ref = rest[:N_PAGES], rest[N_PAGES]
    masks = [sel_ref[:, p * PAGE_SIZE:(p + 1) * PAGE_SIZE] for p in range(N_PAGES)]
    new_ok = sel_ref[:, PAST_LEN:PAST_LEN + 1] > 0.5
    is_g0 = lax.broadcasted_iota(jnp.int32, (N_HEADS, 1), 0) < Q_PER_KV
    o_ref[...] = _decode_attention(q_ref[...], pages, masks, kvnew_ref[...], new_ok, is_g0)


def dsa_decode(dq, dkv, iq, ik, iw, pools_kv, pools_idx, layer, page_table):
    n = dq.shape[0]
    assert min(DSA_TOPK, (PAST_LEN + 1) // 4) == DSA_TOPK and PAGE_SIZE == LANES
    pad_heads = ((0, 0), (0, STEP_ROWS - IDX_HEADS), (0, 0))
    iq8 = jnp.pad(iq.reshape(n, IDX_HEADS, IDX_DIM), pad_heads)
    iw8 = jnp.broadcast_to(jnp.pad(iw.reshape(n, IDX_HEADS, 1), pad_heads), (n, STEP_ROWS, LANES))
    keys = pl.pallas_call(
        _dsa_decode_scores_kernel,
        out_shape=jax.ShapeDtypeStruct((n, 1, KEY_PAD), jnp.int32),
        grid_spec=pltpu.PrefetchScalarGridSpec(
            num_scalar_prefetch=1, grid=(n,),
            in_specs=[_seq_spec(STEP_ROWS, IDX_DIM), _seq_spec(STEP_ROWS, LANES), _seq_spec(1, IDX_DIM)]
            + _layer_page_specs((IDX_DIM, PAGE_SIZE), layer),
            out_specs=_seq_spec(1, KEY_PAD)),
        compiler_params=pltpu.CompilerParams(
            dimension_semantics=("parallel",), vmem_limit_bytes=VMEM_LIMIT),
        name="dsa_decode_scores",
    )(page_table, iq8, iw8, ik.reshape(n, 1, IDX_DIM), *([pools_idx] * N_PAGES))
    sel = pl.pallas_call(
        functools.partial(_topk_rows_kernel, topk=DSA_TOPK, idx_bits=int(math.ceil(math.log2(KEY_PAD))),
                          n_valid=PAST_LEN + 1),
        out_shape=jax.ShapeDtypeStruct((n, KEY_PAD), f32),
        compiler_params=pltpu.CompilerParams(vmem_limit_bytes=VMEM_LIMIT),
        name="dsa_decode_topk",
    )(keys.reshape(n, KEY_PAD))
    o8 = pl.pallas_call(
        _dsa_decode_attend_kernel,
        out_shape=jax.ShapeDtypeStruct((n, N_HEADS, HEAD_DIM), f32),
        grid_spec=pltpu.PrefetchScalarGridSpec(
            num_scalar_prefetch=1, grid=(n,),
            in_specs=[_seq_spec(N_HEADS, HEAD_DIM), _seq_spec(1, KEY_PAD), _seq_spec(2 * KV_HEADS, HEAD_DIM)]
            + _layer_page_specs((KV_HEADS, 2, HEAD_DIM, PAGE_SIZE), layer),
            out_specs=_seq_spec(N_HEADS, HEAD_DIM)),
        compiler_params=pltpu.CompilerParams(
            dimension_semantics=("parallel",), vmem_limit_bytes=VMEM_LIMIT),
        name="dsa_decode_attend",
    )(page_table, dq.reshape(n, N_HEADS, HEAD_DIM), sel.reshape(n, 1, KEY_PAD),
      dkv.reshape(n, 2 * KV_HEADS, HEAD_DIM), *([pools_kv] * N_PAGES))
    return o8.reshape(n, GROUP_WIDTH)


def _nsa_decode_kernel(pt_ref, q_ref, gate_ref, cmp_ref, slcnew_ref, win_ref, winnew_ref, *rest):
    pages, o_ref = rest[:N_PAGES], rest[N_PAGES]
    q8 = q_ref[...]
    qb = q8.astype(bf16)
    lane = lax.broadcasted_iota(jnp.int32, (N_HEADS, CMP_LANES), 1)
    head = lax.broadcasted_iota(jnp.int32, (N_HEADS, 1), 0)
    is_g0 = head < Q_PER_KV
    by_group = lambda fn: jnp.where(is_g0, fn(0), fn(1))
    q_pos = jnp.full((N_HEADS, 1), PAST_LEN, jnp.int32)
    blocks = cmp_ref.shape[1]
    s = by_group(lambda g: lax.dot_general(qb, cmp_ref[2 * g].astype(bf16), _NT,
                                           preferred_element_type=f32)) * ATT_SCALE
    lane_c = lane[:, :blocks]
    s = jnp.where((lane_c + 1) * CMP_BLOCK - 1 <= q_pos, s, -jnp.inf)
    m = jnp.max(s, axis=1, keepdims=True)
    m = jnp.where(m == -jnp.inf, 0.0, m)
    p = jnp.exp(s - m)
    p = p / jnp.maximum(jnp.sum(p, axis=1, keepdims=True), F32_TINY)
    pb = p.astype(bf16)
    o_cmp = by_group(lambda g: jnp.dot(pb, cmp_ref[2 * g + 1].astype(bf16), preferred_element_type=f32))
    imp = jnp.where(is_g0, jnp.sum(p[:Q_PER_KV], axis=0, keepdims=True),
                    jnp.sum(p[Q_PER_KV:], axis=0, keepdims=True))
    imp = jnp.concatenate([imp, jnp.zeros((N_HEADS, CMP_LANES - blocks), f32)], axis=1)
    sel_blk = _select_blocks(imp, q_pos, lane)
    ratio = SLC_BLOCK // CMP_BLOCK
    per_page = PAGE_SIZE // SLC_BLOCK
    masks = []
    for pg in range(N_PAGES):
        msk = sel_blk[:, ratio * per_page * pg:ratio * per_page * pg + 1]
        for j in range(1, per_page):
            b = per_page * pg + j
            msk = jnp.where(lane < j * SLC_BLOCK, msk, sel_blk[:, ratio * b:ratio * b + 1])
        masks.append(msk)
    o_slc = _decode_attention(q8, pages, masks, slcnew_ref[...], None, is_g0)
    wbuf = win_ref.shape[-1]
    slot = lax.broadcasted_iota(jnp.int32, (1, wbuf), 1)
    wmask = jnp.where(PAST_LEN - wbuf + slot > PAST_LEN - WINDOW, 1.0, 0.0)
    o_win = _decode_attention(q8, [win_ref], [wmask], winnew_ref[...], None, is_g0)
    gate = jax.nn.sigmoid(gate_ref[...])
    o_ref[...] = (gate[0][:, :HEAD_DIM] * o_cmp + gate[1][:, :HEAD_DIM] * o_slc
                  + gate[2][:, :HEAD_DIM] * o_win)


def nsa_decode(nq, ng, skv, wkv, pools_cmp, pools_slc, wins, layer, page_table, cmp_pe, cmp_w1, cmp_w2):
    n = nq.shape[0]
    blocks = N_PAGES * BLOCKS_PER_PAGE
    wbuf = wins.shape[-1]
    assert (PAST_LEN + 1) // CMP_BLOCK == blocks and blocks <= CMP_LANES
    assert PAST_LEN // SLC_BLOCK + 1 <= CMP_LANES // 2 and PAGE_SIZE == LANES
    cmp_rows = nsa_compress_paged(pools_cmp, layer, page_table, cmp_pe, cmp_w1, cmp_w2)
    gates =jnp.broadcast_to(ng.reshape(n, N_HEADS, 3).transpose(0, 2, 1)[..., None], (n, 3, N_HEADS, LANES))
    kv_tile = (KV_HEADS, 2, HEAD_DIM)
    o8 = pl.pallas_call(
        _nsa_decode_kernel,
        out_shape=jax.ShapeDtypeStruct((n, N_HEADS, HEAD_DIM), f32),
        grid_spec=pltpu.PrefetchScalarGridSpec(
            num_scalar_prefetch=1, grid=(n,),
            in_specs=[_seq_spec(N_HEADS, HEAD_DIM), _seq_spec(3, N_HEADS, LANES),
                      _seq_spec(2 * KV_HEADS, blocks, HEAD_DIM), _seq_spec(2 * KV_HEADS, HEAD_DIM),
                      _layer_seq_spec(kv_tile + (wbuf,), layer), _seq_spec(2 * KV_HEADS, HEAD_DIM)]
            + _layer_page_specs(kv_tile + (PAGE_SIZE,), layer),
            out_specs=_seq_spec(N_HEADS, HEAD_DIM)),
        compiler_params=pltpu.CompilerParams(
            dimension_semantics=("parallel",), vmem_limit_bytes=VMEM_LIMIT),
        name="nsa_decode",
    )(page_table, nq.reshape(n, N_HEADS, HEAD_DIM), gates, cmp_rows, skv.reshape(n, 2 * KV_HEADS, HEAD_DIM),
      wins, wkv.reshape(n, 2 * KV_HEADS, HEAD_DIM), *([pools_slc] * N_PAGES))
    return o8.reshape(n, GROUP_WIDTH)


WIN_STEP_SEQS = 8


def _window_push_kernel(win_ref, new_ref, o_ref):
    seqs, tiles = win_ref.shape[0], 2 * KV_HEADS
    wbuf = win_ref.shape[-1]
    last = lax.broadcasted_iota(jnp.int32, (HEAD_DIM, wbuf), 1) == wbuf - 1
    new_cols = new_ref[...].reshape(seqs * tiles, HEAD_DIM).T
    for s in range(seqs):
        for gj in range(tiles):
            g, j = gj // 2, gj % 2
            col = new_cols[:, s * tiles + gj:s * tiles + gj + 1]
            o_ref[s, g, j] = jnp.where(last, col, pltpu.roll(win_ref[s, g, j], wbuf - 1, 1))


def window_push(wins, new_rows):
    depth, n = wins.shape[:2]
    tile = wins.shape[2:]
    return pl.pallas_call(
        _window_push_kernel,
        out_shape=jax.ShapeDtypeStruct(wins.shape, f32),
        grid=(depth, n // WIN_STEP_SEQS),
        in_specs=[pl.BlockSpec((None, WIN_STEP_SEQS) + tile, lambda l, i: (l, i, 0, 0, 0, 0)),
                  pl.BlockSpec((None, WIN_STEP_SEQS, 2 * KV_HEADS, HEAD_DIM), lambda l, i: (l, i, 0, 0))],
        out_specs=pl.BlockSpec((None, WIN_STEP_SEQS) + tile, lambda l, i: (l, i, 0, 0, 0, 0)),
        compiler_params=pltpu.CompilerParams(
            dimension_semantics=("parallel", "parallel"), vmem_limit_bytes=VMEM_LIMIT),
        name="window_push",
    )(wins, new_rows)


def _pack_token_minor_kernel(a_ref, b_ref, o_ref):
    layer = pl.program_id(0)

    def emit(src_ref):
        o_ref[...] = src_ref[...].T.reshape(o_ref.shape)

    pl.when(layer == 0)(lambda: emit(a_ref))
    pl.when(layer == 1)(lambda: emit(b_ref))


def pack_token_minor(rows0, rows1, n, t, tile):
    width = rows0.shape[1]
    nt = t // KEY_CHUNK
    last = n * nt - 1
    zeros = (0,) * len(tile)
    return pl.pallas_call(
        _pack_token_minor_kernel,
        out_shape=jax.ShapeDtypeStruct((2, n) + tile + (t,), f32),
        grid=(2, n, nt),
        in_specs=[pl.BlockSpec((KEY_CHUNK, width), lambda l, b, i: (jnp.where(l == 0, b * nt + i, last), 0)),
                  pl.BlockSpec((KEY_CHUNK, width), lambda l, b, i: (jnp.where(l == 1, b * nt + i, 0), 0))],
        out_specs=pl.BlockSpec((None, None) + tile + (KEY_CHUNK,), lambda l, b, i: (l, b) + zeros + (i,)),
        compiler_params=pltpu.CompilerParams(
            dimension_semantics=("arbitrary", "arbitrary", "arbitrary"), vmem_limit_bytes=VMEM_LIMIT),
        name="pack_token_minor",
    )(rows0, rows1)


def mix_prompt(cols, n, t, w, sw, cmp_weights):
    rows = {name: cols[k] for name, k in (('dkv', 1), ('ik', 3), ('ckv', 7), ('skv', 8))}
    (dq, dkv, iq, ik, iw, u, nq, ckv, skv, wkv, ng, gq, gk, gv, ga, gr) = [
        c.reshape(n, t, c.shape[-1]) for c in cols]
    kv_shape = (n, t, KV_HEADS, 2, HEAD_DIM)
    o_dsa = dsa_prompt(dq, dkv, iq, ik, iw)
    o_s5, h_s5 = s5_prompt(u, jnp.zeros((n, 2, S5_WIDTH), f32), sw)
    o_nsa = nsa_prompt(nq, ng, ckv, skv, wkv, cmp_weights)
    o_gla, s_gla = gla_prompt(gq, gk, gv, ga, gr, w['gla_w_gate'], w['gla_b_gate'], w['gla_norm_g'])
    win_state = wkv[:, -min(WINDOW, t):].reshape((n, min(WINDOW, t)) + kv_shape[2:])
    mixers = [o.reshape(n * t, GROUP_WIDTH) for o in (o_dsa, o_s5, o_nsa, o_gla)]
    return mixers, (rows['dkv'], rows['ik'], rows['ckv'], rows['skv'], win_state,
                    h_s5.reshape(n, 2, S5_GROUPS, S5_STATE), s_gla)


def mix_sample(cols, w, sw, layer, pools, h_s5, s_gla, page_table):
    (dq, dkv, iq, ik, iw, u, nq, ckv, skv, wkv, ng, gq, gk, gv, ga, gr) = cols
    n, t = dq.shape[0], 1
    kv_shape = (n, t, KV_HEADS, 2, HEAD_DIM)
    o_dsa = dsa_decode(dq, dkv, iq, ik, iw, pools['dsa_kv'], pools['dsa_idx'], layer, page_table)
    o_s5, h_new = s5_step(u, h_s5.reshape(n, 2 * S5_WIDTH), sw)
    o_nsa = nsa_decode(nq, ng, skv, wkv, pools['nsa_cmp'], pools['nsa_slc'], pools['nsa_win'], layer,
                       page_table, w['cmp_pe'], w['cmp_w1'], w['cmp_w2'])
    o_gla, s_new = gla_step(gq, gk, gv, ga, gr, w['gla_w_gate'], w['gla_b_gate'], w['gla_norm_g'], s_gla)
    return [o_dsa, o_s5, o_nsa, o_gla], (
        dkv.reshape(kv_shape), ik.reshape(n, t, IDX_DIM), ckv.reshape(kv_shape), skv.reshape(kv_shape),
        wkv.reshape(n, 2 * KV_HEADS, HEAD_DIM), h_new.reshape(n, 2, S5_GROUPS, S5_STATE), s_new)


def kernel(x_prompt, x_sample, cache_dsa_kv, cache_dsa_idx, cache_nsa_cmp, cache_nsa_slc, cache_nsa_win, state_s5, state_gla, page_table, w_in, s5_a_re, s5_a_im, s5_b_re, s5_b_im, s5_c_re, s5_c_im, s5_d, s5_log_dt, s5_w_glu, s5_b_glu, nsa_cmp_pe, nsa_cmp_w1, nsa_cmp_w2, gla_w_gate, gla_b_gate, gla_norm_g, w_out, ln1_g, ln1_b, ffn_w_gate, ffn_w_up, ffn_w_down, ln2_g, ln2_b):
    np_, tp_ = x_prompt.shape[:2]
    ns_, ts_ = x_sample.shape[:2]
    assert ts_ == 1
    w_in_pad = pad_in_projection(w_in)
    w_out_b = w_out.astype(bf16)
    wg_b, wu_b, wd_b = ffn_w_gate.astype(bf16), ffn_w_up.astype(bf16), ffn_w_down.astype(bf16)
    hp = x_prompt.reshape(np_ * tp_, D_MODEL)
    hs = x_sample.reshape(ns_ * ts_, D_MODEL)
    outs_p = [[] for _ in range(7)]
    outs_s = [[] for _ in range(7)]
    pools = {'dsa_kv': _tokens_minor(cache_dsa_kv), 'dsa_idx': _tokens_minor(cache_dsa_idx),
             'nsa_cmp': _tokens_minor(cache_nsa_cmp), 'nsa_slc': _tokens_minor(cache_nsa_slc),
             'nsa_win': _tokens_minor(cache_nsa_win)}
    for l in range(DEPTH):
        w = {'a_re': s5_a_re[l], 'a_im': s5_a_im[l], 'b_re': s5_b_re[l], 'b_im': s5_b_im[l],
             'c_re': s5_c_re[l], 'c_im': s5_c_im[l], 'd': s5_d[l], 'log_dt': s5_log_dt[l],
             'w_glu': s5_w_glu[l], 'b_glu': s5_b_glu[l],
             'cmp_pe': nsa_cmp_pe[l], 'cmp_w1': nsa_cmp_w1[l], 'cmp_w2': nsa_cmp_w2[l],
             'gla_w_gate': gla_w_gate[l], 'gla_b_gate': gla_b_gate[l], 'gla_norm_g': gla_norm_g[l]}
        sw = s5_discretize(w)
        cmp_weights = nsa_compress_weights(nsa_cmp_pe[l], nsa_cmp_w1[l], nsa_cmp_w2[l])
        mixers_p, st_p = mix_prompt(in_projection(hp, w_in_pad, l), np_, tp_, w, sw, cmp_weights)
        mixers_s, st_s = mix_sample(in_projection(hs, w_in_pad, l), w, sw, l, pools,
                                    state_s5[l], state_gla[l], page_table)
        hp = outproj_ln(hp, mixers_p, w_out_b, l, ln1_g[l], ln1_b[l])
        hs = outproj_ln(hs, mixers_s, w_out_b, l, ln1_g[l], ln1_b[l])
        hp = ffn_ln(hp, wg_b, wu_b, wd_b, l, ln2_g[l], ln2_b[l])
        hs = ffn_ln(hs, wg_b, wu_b, wd_b, l, ln2_g[l], ln2_b[l])
        for lst, st in zip(outs_p, st_p):
            lst.append(st)
        for lst, st in zip(outs_s, st_s):
            lst.append(st)
    assert DEPTH == 2
    kv_tile = (KV_HEADS, 2, HEAD_DIM)
    dsa_kv_p, dsa_idx_p, nsa_cmp_p, nsa_slc_p = [
        jnp.moveaxis(pack_token_minor(a[0], a[1], np_, tp_, tile), -1, 2)
        for a, tile in zip(outs_p[:4], (kv_tile, (IDX_DIM,), kv_tile, kv_tile))]
    nsa_win_p, s5_p, gla_p = [jnp.stack(a) for a in outs_p[4:]]
    dsa_kv_s, dsa_idx_s, nsa_cmp_s, nsa_slc_s, win_rows_s, s5_s, gla_s = [jnp.stack(a) for a in outs_s]
    nsa_win_s = jnp.moveaxis(window_push(pools['nsa_win'], win_rows_s), -1, 2)
    return (hp.reshape(np_, tp_, D_MODEL), hs.reshape(ns_, ts_, D_MODEL),
            dsa_kv_p, dsa_kv_s, dsa_idx_p, dsa_idx_s, nsa_cmp_p, nsa_cmp_s,
            nsa_slc_p, nsa_slc_s, nsa_win_p, nsa_win_s, s5_p, s5_s, gla_p, gla_s)
```
